```python
import math
import jax
import jax.numpy as jnp
from jax import lax
import numpy as np

D_MODEL = 1024
BATCH = 8
SEQ = 2048
DEPTH = 1

HEAD_DIM = 64
ATTN_HEADS = D_MODEL // 128
ATTN_WIDTH = ATTN_HEADS * HEAD_DIM
MOBA_BLOCK = 256
MOBA_TOP_K = 3
QUERY_CHUNK = 32
ROPE_THETA = 10000.0
SSM_GROUP_DIM = 16
SSM_GROUPS = D_MODEL // 32
SSM_WIDTH = SSM_GROUPS * SSM_GROUP_DIM
SSM_STATE = 64
DT_MIN = 1e-3
DT_MAX = 1e-1
MIX_WIDTH = ATTN_WIDTH + SSM_WIDTH
IN_PROJ_WIDTH = 4 * ATTN_WIDTH + 2 * SSM_WIDTH
NORM_EPS = 1e-6
NEG_INF = -1e30

kernel_name = "hymba_moba_s5_hybrid_layer"


def rms_norm(x, g):
    xf = x.astype(jnp.float32)
    y = xf * lax.rsqrt(jnp.mean(xf * xf, axis=-1, keepdims=True) + NORM_EPS)
    return (y * g.astype(jnp.float32)).astype(x.dtype)


def rotary(t):
    L, dh = t.shape[2], t.shape[3]
    half = dh // 2
    inv_freq = 1.0 / (ROPE_THETA ** (jnp.arange(half, dtype=jnp.float32) / half))
    ang = jnp.arange(L, dtype=jnp.float32)[:, None] * inv_freq[None, :]
    cos, sin = jnp.cos(ang), jnp.sin(ang)
    tf = t.astype(jnp.float32)
    t1, t2 = tf[..., :half], tf[..., half:]
    out = jnp.concatenate([t1 * cos - t2 * sin, t2 * cos + t1 * sin], axis=-1)
    return out.astype(t.dtype)


def moba_attention(q, k, v):
    Bsz, H, L, dh = q.shape
    nb = -(-L // MOBA_BLOCK)
    pad = nb * MOBA_BLOCK - L
    kb = jnp.pad(k, ((0, 0), (0, 0), (0, pad), (0, 0))).reshape(Bsz, H, nb, MOBA_BLOCK, dh)
    vb = jnp.pad(v, ((0, 0), (0, 0), (0, pad), (0, 0))).reshape(Bsz, H, nb, MOBA_BLOCK, dh)
    k_mean = jnp.mean(kb.astype(jnp.float32), axis=3)
    n_sel = min(MOBA_TOP_K, nb - 1)
    scale = 1.0 / math.sqrt(dh)
    n_chunks = L // QUERY_CHUNK
    q_chunks = q.reshape(Bsz, H, n_chunks, QUERY_CHUNK, dh).transpose(2, 0, 1, 3, 4)
    b_idx = jnp.arange(Bsz)[:, None, None, None]
    h_idx = jnp.arange(H)[None, :, None, None]

    def chunk_fn(args):
        ci, qc = args
        start = ci * QUERY_CHUNK
        q_pos = start + jnp.arange(QUERY_CHUNK)
        own = start // MOBA_BLOCK
        k_own = lax.dynamic_index_in_dim(kb, own, axis=2, keepdims=False)
        v_own = lax.dynamic_index_in_dim(vb, own, axis=2, keepdims=False)
        key_pos = own * MOBA_BLOCK + jnp.arange(MOBA_BLOCK)
        s_own = jnp.einsum('bhqd,bhkd->bhqk', qc, k_own).astype(jnp.float32) * scale
        s_own = jnp.where(key_pos[None, :] <= q_pos[:, None], s_own, NEG_INF)
        if n_sel == 0:
            p_own = jax.nn.softmax(s_own, axis=-1).astype(v.dtype)
            return jnp.einsum('bhqk,bhkd->bhqd', p_own, v_own)
        gate = jnp.einsum('bhqd,bhnd->bhqn', qc.astype(jnp.float32), k_mean)
        gate = jnp.where(jnp.arange(nb) < own, gate, NEG_INF)
        _, idx = lax.top_k(gate, n_sel)
        valid = jnp.arange(n_sel) < own
        k_sel = kb[b_idx, h_idx, idx]
        v_sel = vb[b_idx, h_idx, idx]
        s_sel = jnp.einsum('bhqd,bhqnkd->bhqnk', qc, k_sel).astype(jnp.float32) * scale
        s_sel = jnp.where(valid[:, None], s_sel, NEG_INF)
        s_all = jnp.concatenate([s_sel.reshape(Bsz, H, QUERY_CHUNK, n_sel * MOBA_BLOCK), s_own], axis=-1)
        p = jax.nn.softmax(s_all, axis=-1).astype(v.dtype)
        p_sel = p[..., : n_sel * MOBA_BLOCK].reshape(Bsz, H, QUERY_CHUNK, n_sel, MOBA_BLOCK)
        p_own = p[..., n_sel * MOBA_BLOCK:]
        return (jnp.einsum('bhqnk,bhqnkd->bhqd', p_sel, v_sel)
                + jnp.einsum('bhqk,bhkd->bhqd', p_own, v_own))

    out = lax.map(chunk_fn, (jnp.arange(n_chunks), q_chunks))
    return out.transpose(1, 2, 0, 3, 4).reshape(Bsz, H, L, dh)


def s5_branch(u, lam_re, lam_im, b_re, b_im, c_re, c_im, d_skip, log_dt, w_glu, b_glu):
    Bsz, L, _ = u.shape
    uf = u.astype(jnp.float32).reshape(Bsz, L, SSM_GROUPS, SSM_GROUP_DIM)
    lam = lax.complex(lam_re.astype(jnp.float32), lam_im.astype(jnp.float32))
    dt = jnp.exp(log_dt.astype(jnp.float32))[:, None]
    lam_bar = jnp.exp(lam * dt)
    b_mat = lax.complex(b_re.astype(jnp.float32), b_im.astype(jnp.float32))
    b_bar = ((lam_bar - 1.0) / lam)[..., None] * b_mat
    bu = jnp.einsum('blgh,gph->blgp', uf.astype(jnp.complex64), b_bar)
    a = jnp.broadcast_to(lam_bar, (L, SSM_GROUPS, SSM_STATE))[None]

    def combine(e1, e2):
        a1, s1 = e1
        a2, s2 = e2
        return a1 * a2, a2 * s1 + s2

    _, xs = lax.associative_scan(combine, (a, bu), axis=1)
    y = (jnp.einsum('blgp,ghp->blgh', jnp.real(xs), c_re.astype(jnp.float32))
         - jnp.einsum('blgp,ghp->blgh', jnp.imag(xs), c_im.astype(jnp.float32))
         + d_skip.astype(jnp.float32) * uf)
    y = jax.nn.gelu(y.reshape(Bsz, L, SSM_WIDTH))
    y = y * jax.nn.sigmoid(y @ w_glu.astype(jnp.float32) + b_glu.astype(jnp.float32))
    return y.astype(u.dtype)


def setup_inputs(seed: int = 0) -> dict:
    key = jax.random.key(seed)
    ks = jax.random.split(key, 16)
    f32 = jnp.float32
    G, P, Hg = SSM_GROUPS, SSM_STATE, SSM_GROUP_DIM
    x = jax.random.normal(ks[0], (BATCH, SEQ, D_MODEL), f32)
    norm_gain = 1.0 + 0.01 * jax.random.normal(ks[1], (DEPTH, D_MODEL), f32)
    w_in = jax.random.normal(ks[2], (DEPTH, D_MODEL, IN_PROJ_WIDTH), f32) * D_MODEL ** -0.5
    w_out = jax.random.normal(ks[3], (DEPTH, MIX_WIDTH, D_MODEL), f32) * MIX_WIDTH ** -0.5
    lam_re = -0.5 + 0.01 * jax.random.normal(ks[4], (DEPTH, G, P), f32)
    lam_im = (jnp.pi * jnp.arange(P, dtype=f32))[None, None, :] + 0.01 * jax.random.normal(ks[5], (DEPTH, G, P), f32)
    b_re = jax.random.normal(ks[6], (DEPTH, G, P, Hg), f32) * (2.0 * Hg) ** -0.5
    b_im = jax.random.normal(ks[7], (DEPTH, G, P, Hg), f32) * (2.0 * Hg) ** -0.5
    c_re = jax.random.normal(ks[8], (DEPTH, G, Hg, P), f32) * (2.0 * P) ** -0.5
    c_im = jax.random.normal(ks[9], (DEPTH, G, Hg, P), f32) * (2.0 * P) ** -0.5
    d_skip = jax.random.normal(ks[10], (DEPTH, G, Hg), f32)
    log_dt = jax.random.uniform(ks[11], (DEPTH, G), f32, minval=math.log(DT_MIN), maxval=math.log(DT_MAX))
    w_glu = jax.random.normal(ks[12], (DEPTH, SSM_WIDTH, SSM_WIDTH), f32) * SSM_WIDTH ** -0.5
    b_glu = 0.01 * jax.random.normal(ks[13], (DEPTH, SSM_WIDTH), f32)
    final_gain = 1.0 + 0.01 * jax.random.normal(ks[14], (D_MODEL,), f32)
    return {"x": x, "norm_gain": norm_gain, "w_in": w_in, "w_out": w_out,
            "lam_re": lam_re, "lam_im": lam_im, "b_re": b_re, "b_im": b_im,
            "c_re": c_re, "c_im": c_im, "d_skip": d_skip, "log_dt": log_dt,
            "w_glu": w_glu, "b_glu": b_glu, "final_gain": final_gain}


def reference(x, norm_gain, w_in, w_out, lam_re, lam_im, b_re, b_im, c_re, c_im,
              d_skip, log_dt, w_glu, b_glu, final_gain):
    Bsz, L, _ = x.shape
    A, S = ATTN_WIDTH, SSM_WIDTH

    def to_heads(t):
        return t.reshape(Bsz, L, ATTN_HEADS, HEAD_DIM).transpose(0, 2, 1, 3)

    for layer in range(DEPTH):
        h = rms_norm(x, norm_gain[layer])
        proj = jnp.einsum('bld,de->ble', h, w_in[layer])
        q, k, v, z_attn, u_ssm, z_ssm = jnp.split(
            proj, [A, 2 * A, 3 * A, 4 * A, 4 * A + S], axis=-1)
        o_attn = moba_attention(rotary(to_heads(q)), rotary(to_heads(k)), to_heads(v))
        o_attn = o_attn.transpose(0, 2, 1, 3).reshape(Bsz, L, A) * jax.nn.silu(z_attn)
        o_ssm = s5_branch(u_ssm, lam_re[layer], lam_im[layer], b_re[layer], b_im[layer],
                          c_re[layer], c_im[layer], d_skip[layer], log_dt[layer],
                          w_glu[layer], b_glu[layer]) * jax.nn.silu(z_ssm)
        mixed = jnp.concatenate([o_attn, o_ssm], axis=-1)
        x = x + jnp.einsum('ble,ed->bld', mixed, w_out[layer])
    return rms_norm(x, final_gain)
```

```python
import functools
import math

import jax
import jax.numpy as jnp
from jax import lax
from jax.experimental import pallas as pl
from jax.experimental.pallas import tpu as pltpu

F32 = jnp.float32
BF16 = jnp.bfloat16

D_MODEL = 1024
HEAD_DIM = 64
ATTN_HEADS = 8
ATTN_WIDTH = ATTN_HEADS * HEAD_DIM
MOBA_BLOCK = 256
MOBA_TOP_K = 3
ROPE_THETA = 10000.0
SSM_GROUP_DIM = 16
SSM_GROUPS = 32
SSM_WIDTH = SSM_GROUPS * SSM_GROUP_DIM
SSM_STATE = 64
IN_PROJ_WIDTH = 4 * ATTN_WIDTH + 2 * SSM_WIDTH
NORM_EPS = 1e-6
NEG_INF = -1e30

LANES = 128
SSM_CHUNK = 8
GROUPS_PER_TILE = LANES // SSM_GROUP_DIM
SSM_TILES = SSM_WIDTH // LANES
TILE_COLS = SSM_CHUNK * LANES
STATE_COLS = GROUPS_PER_TILE * SSM_STATE
VMEM_LIMIT_BYTES = 56 * 1024 * 1024


def _resident(shape):
    zeros = (0,) * len(shape)
    return pl.BlockSpec(shape, lambda *_: zeros, pipeline_mode=pl.Buffered(1))


def _in_proj_kernel(x_ref, g_ref, w_ref, cos_ref, sin_hi_ref, sin_lo_ref,
                    q_ref, k_ref, v_ref, za_ref, u_ref, zs_ref):
    x = x_ref[...]
    ms = jnp.mean(x * x, axis=-1, keepdims=True)
    h = (x * lax.rsqrt(ms + NORM_EPS) * g_ref[...]).astype(BF16)

    def section(idx):
        return jnp.dot(h, w_ref[:, idx * ATTN_WIDTH:(idx + 1) * ATTN_WIDTH],
                       preferred_element_type=F32)

    cos = cos_ref[...]
    sin_hi = sin_hi_ref[...]
    sin_lo = sin_lo_ref[...]

    def rotary(t):
        return (t * cos + pltpu.roll(t, HEAD_DIM // 2, axis=1) * sin_hi
                + pltpu.roll(t, LANES - HEAD_DIM // 2, axis=1) * sin_lo)

    q = section(0)
    k = section(1)
    scale = 1.0 / math.sqrt(HEAD_DIM)
    for c in range(ATTN_WIDTH // LANES):
        sl = slice(c * LANES, (c + 1) * LANES)
        q_ref[:, sl] = (rotary(q[:, sl]) * scale).astype(BF16)
        k_ref[:, sl] = rotary(k[:, sl]).astype(BF16)
    v_ref[...] = section(2).astype(BF16)
    za_ref[...] = section(3)
    u_ref[...] = section(4)
    zs_ref[...] = section(5)


def _in_proj(x2, gain, w_in, cos, sin_hi, sin_lo, seq_len, tm):
    rows = x2.shape[0]
    pos_blocks = seq_len // tm
    row_spec = lambda width: pl.BlockSpec((tm, width), lambda i: (i, 0))
    tab_spec = pl.BlockSpec((tm, LANES), lambda i: (i % pos_blocks, 0))
    out_bf16 = jax.ShapeDtypeStruct((rows, ATTN_WIDTH), BF16)
    out_f32 = jax.ShapeDtypeStruct((rows, ATTN_WIDTH), F32)
    return pl.pallas_call(
        _in_proj_kernel,
        grid=(rows // tm,),
        in_specs=[row_spec(D_MODEL), _resident((1, D_MODEL)), _resident((D_MODEL, IN_PROJ_WIDTH)),
                  tab_spec, tab_spec, tab_spec],
        out_specs=[row_spec(ATTN_WIDTH)] * 6,
        out_shape=[out_bf16, out_bf16, out_bf16, out_f32, out_f32, out_f32],
        compiler_params=pltpu.CompilerParams(dimension_semantics=("arbitrary",),
                                             vmem_limit_bytes=VMEM_LIMIT_BYTES),
        name="in_proj",
    )(x2, gain, w_in, cos, sin_hi, sin_lo)


def _column(a, j):
    col = lax.broadcasted_iota(jnp.int32, a.shape, 1)
    return jnp.sum(jnp.where(col == j, a, 0.0), axis=1, keepdims=True)


def _attn_kernel(q_ref, k_ref, v_ref, z_ref, o_ref, kmean_ref, *, n_blocks):
    qi = pl.program_id(2)
    blk = MOBA_BLOCK

    @pl.when(qi == 0)
    def _():
        for j in range(n_blocks):
            kj = k_ref[0, j * blk:(j + 1) * blk, :].astype(F32)
            kmean_ref[j:j + 1, :] = jnp.mean(kj, axis=0, keepdims=True)

    q = q_ref[0]
    lane = lax.broadcasted_iota(jnp.int32, (blk, LANES), 1)
    first_head = lane < HEAD_DIM
    q_heads = (jnp.where(first_head, q, jnp.zeros_like(q)),
               jnp.where(first_head, jnp.zeros_like(q), q))
    kmean = kmean_ref[...].astype(BF16)

    nt_dims = (((1,), (1,)), ((), ()))
    blk_col = lax.broadcasted_iota(jnp.int32, (blk, n_blocks), 1)
    past = blk_col < qi

    def block_bias(qh):
        gate = lax.dot_general(qh, kmean, nt_dims, preferred_element_type=F32)
        gate = jnp.where(past, gate, NEG_INF)
        beaten_by = jnp.zeros((blk, n_blocks), jnp.int32)
        for j in range(n_blocks):
            gj = _column(gate, j)
            wins = (gj > gate) | ((gj == gate) & (j < blk_col))
            beaten_by = beaten_by + wins.astype(jnp.int32)
        keep = past & (beaten_by < MOBA_TOP_K)
        return jnp.where(keep, 0.0, NEG_INF)

    biases = tuple(block_bias(qh) for qh in q_heads)

    def attend(qh, kj, vj, s_mask, m, l, acc):
        s = s_mask(lax.dot_general(qh, kj, nt_dims, preferred_element_type=F32))
        m_new = jnp.maximum(m, jnp.max(s, axis=1, keepdims=True))
        alpha = jnp.exp(m - m_new)
        p = jnp.exp(s - m_new)
        l = alpha * l + jnp.sum(p, axis=1, keepdims=True)
        acc = alpha * acc + jnp.dot(p.astype(BF16), vj, preferred_element_type=F32)
        return m_new, l, acc

    own = pl.multiple_of(qi * blk, blk)
    k_own = k_ref[0, pl.ds(own, blk), :]
    v_own = v_ref[0, pl.ds(own, blk), :]
    row = lax.broadcasted_iota(jnp.int32, (blk, blk), 0)
    key = lax.broadcasted_iota(jnp.int32, (blk, blk), 1)
    causal = lambda s: jnp.where(key <= row, s, NEG_INF)
    init = (jnp.full((blk, 1), NEG_INF, F32), jnp.zeros((blk, 1), F32),
            jnp.zeros((blk, LANES), F32))
    state = tuple(attend(qh, k_own, v_own, causal, *init) for qh in q_heads)

    def past_block(j, state):
        start = pl.multiple_of(j * blk, blk)
        kj = k_ref[0, pl.ds(start, blk), :]
        vj = v_ref[0, pl.ds(start, blk), :]
        new = []
        for qh, bias, (m, l, acc) in zip(q_heads, biases, state):
            bj = _column(bias, j)
            new.append(attend(qh, kj, vj, lambda s: s + bj, m, l, acc))
        return tuple(new)

    state = lax.fori_loop(0, qi, past_block, state)
    (_, l0, acc0), (_, l1, acc1) = state
    o = jnp.where(first_head, acc0 / l0, acc1 / l1)
    z = z_ref[0]
    o_ref[0] = (o * (z / (1.0 + jnp.exp(-z)))).astype(BF16)


def _moba_attention(q, k, v, z_attn):
    bsz, seq_len, _ = q.shape
    n_blocks = seq_len // MOBA_BLOCK
    head_pairs = ATTN_WIDTH // LANES
    q_spec = pl.BlockSpec((1, MOBA_BLOCK, LANES), lambda b, h, i: (b, i, h))
    kv_spec = pl.BlockSpec((1, seq_len, LANES), lambda b, h, i: (b, 0, h))
    return pl.pallas_call(
        functools.partial(_attn_kernel, n_blocks=n_blocks),
        grid=(bsz, head_pairs, n_blocks),
        in_specs=[q_spec, kv_spec, kv_spec, q_spec],
        out_specs=q_spec,
        out_shape=jax.ShapeDtypeStruct((bsz, seq_len, ATTN_WIDTH), BF16),
        scratch_shapes=[pltpu.VMEM((n_blocks, LANES), F32)],
        compiler_params=pltpu.CompilerParams(
            dimension_semantics=("arbitrary", "arbitrary", "arbitrary"),
            vmem_limit_bytes=VMEM_LIMIT_BYTES),
        name="moba_attn",
    )(q, k, v, z_attn)


def _ssm_operators(lam_re, lam_im, b_re, b_im, c_re, c_im, log_dt):
    T, G, P, H = SSM_CHUNK, SSM_GROUPS, SSM_STATE, SSM_GROUP_DIM
    Q, GL = SSM_TILES, GROUPS_PER_TILE
    dt = jnp.exp(log_dt.astype(F32))[:, None]
    lam_r, lam_i = lam_re.astype(F32), lam_im.astype(F32)
    mag = jnp.exp(lam_r * dt)
    bar_r, bar_i = mag * jnp.cos(lam_i * dt), mag * jnp.sin(lam_i * dt)
    den = lam_r * lam_r + lam_i * lam_i
    f_r = ((bar_r - 1.0) * lam_r + bar_i * lam_i) / den
    f_i = (bar_i * lam_r - (bar_r - 1.0) * lam_i) / den
    bb_r = f_r[..., None] * b_re - f_i[..., None] * b_im
    bb_i = f_r[..., None] * b_im + f_i[..., None] * b_re

    pw_r, pw_i = [jnp.ones_like(bar_r)], [jnp.zeros_like(bar_r)]
    for _ in range(T):
        r, i = pw_r[-1], pw_i[-1]
        pw_r.append(r * bar_r - i * bar_i)
        pw_i.append(r * bar_i + i * bar_r)
    pw_r, pw_i = jnp.stack(pw_r), jnp.stack(pw_i)

    lb_r = pw_r[..., None] * bb_r - pw_i[..., None] * bb_i
    lb_i = pw_r[..., None] * bb_i + pw_i[..., None] * bb_r
    kd = (jnp.sum(c_re[None, :, :, :, None] * lb_r[:T, :, None, :, :], axis=3)
          - jnp.sum(c_im[None, :, :, :, None] * lb_i[:T, :, None, :, :], axis=3))

    eye = jnp.eye(GL, dtype=F32)
    steps = jnp.arange(T)
    lag = steps[None, :] - steps[:, None]
    toep = jnp.where((lag >= 0)[:, :, None, None, None], kd[jnp.clip(lag, 0)], 0.0)
    toep = toep.reshape(T, T, Q, GL, H, H)
    m_intra = jnp.einsum('jiqaoh,ac->qjahico', toep, eye).reshape(Q, T * LANES, T * LANES)

    sin_r = lb_r[T - 1 - steps].reshape(T, Q, GL, P, H)
    sin_i = lb_i[T - 1 - steps].reshape(T, Q, GL, P, H)
    sin = jnp.stack([sin_r, sin_i])
    w_state = jnp.einsum('rjqaph,ac->qjahrcp', sin, eye).reshape(Q, T * LANES, 2 * STATE_COLS)

    cl_r = c_re[None] * pw_r[1:, :, None, :] - c_im[None] * pw_i[1:, :, None, :]
    cl_i = c_re[None] * pw_i[1:, :, None, :] + c_im[None] * pw_r[1:, :, None, :]
    sout = jnp.stack([cl_r, -cl_i]).reshape(2, T, Q, GL, H, P)
    w_out = jnp.einsum('riqaop,ac->qrapico', sout, eye).reshape(Q, 2 * STATE_COLS, T * LANES)

    return m_intra, w_state, w_out, (pw_r[T], pw_i[T])


def _chunk_powers(a_r, a_i, n_steps):
    rs, is_ = [a_r], [a_i]
    for _ in range(n_steps - 1):
        r, i = rs[-1], is_[-1]
        rs.append(r * r - i * i)
        is_.append(2.0 * r * i)
    to_tiles = lambda a: a.reshape(SSM_TILES, STATE_COLS)
    return jnp.stack([jnp.stack([to_tiles(r), to_tiles(i)], axis=1) for r, i in zip(rs, is_)],
                     axis=1)


def _ssm_kernel(u_ref, zs_ref, m_ref, ws_ref, wo_ref, apow_ref, d_ref, wg_ref, bg_ref,
                o_ref, y_ref, *, n_chunks, scan_steps):
    row = lax.broadcasted_iota(jnp.int32, (n_chunks, STATE_COLS), 0)

    def shifted(a, sh):
        return jnp.where(row >= sh, pltpu.roll(a, sh, axis=0), 0.0)

    for q in range(SSM_TILES):
        cols = [slice(i * SSM_WIDTH + q * LANES, i * SSM_WIDTH + (q + 1) * LANES)
                for i in range(SSM_CHUNK)]
        xq = jnp.concatenate([u_ref[:, c] for c in cols], axis=1)
        xq_lo = xq.astype(BF16)
        y = jnp.dot(xq_lo, m_ref[q], preferred_element_type=F32)
        s = jnp.dot(xq_lo, ws_ref[q], preferred_element_type=F32)
        re, im = s[:, :STATE_COLS], s[:, STATE_COLS:]
        for k in range(scan_steps):
            a_re = apow_ref[q, k, 0:1, :]
            a_im = apow_ref[q, k, 1:2, :]
            sre, sim = shifted(re, 1 << k), shifted(im, 1 << k)
            re, im = re + a_re * sre - a_im * sim, im + a_re * sim + a_im * sre
        prev = jnp.concatenate([shifted(re, 1), shifted(im, 1)], axis=1).astype(BF16)
        y = y + jnp.dot(prev, wo_ref[q], preferred_element_type=F32)
        d_tile = d_ref[:, q * LANES:(q + 1) * LANES]
        y = y + xq * jnp.concatenate([d_tile] * SSM_CHUNK, axis=1)
        for i, c in enumerate(cols):
            y_ref[:, c] = y[:, i * LANES:(i + 1) * LANES]

    gelu_c = math.sqrt(2.0 / math.pi)
    for i in range(SSM_CHUNK):
        sl = slice(i * SSM_WIDTH, (i + 1) * SSM_WIDTH)
        y = y_ref[:, sl]
        y = 0.5 * y * (1.0 + jnp.tanh(gelu_c * (y + 0.044715 * (y * y * y))))
        g = jnp.dot(y.astype(BF16), wg_ref[...], preferred_element_type=F32) + bg_ref[...]
        z = zs_ref[:, sl]
        o = y / (1.0 + jnp.exp(-g)) * (z / (1.0 + jnp.exp(-z)))
        o_ref[:, sl] = o.astype(BF16)


def _s5_ssm(u, z_ssm, ops, d_skip, w_glu, b_glu, seq_len):
    m_intra, w_state, w_out, apow = ops
    rows = u.shape[0] // SSM_CHUNK
    n_chunks = seq_len // SSM_CHUNK
    scan_steps = apow.shape[1]
    wide = SSM_CHUNK * SSM_WIDTH
    u2 = u.reshape(rows, wide)
    z2 = z_ssm.reshape(rows, wide)
    blk = pl.BlockSpec((n_chunks, wide), lambda b: (b, 0))
    out = pl.pallas_call(
        functools.partial(_ssm_kernel, n_chunks=n_chunks, scan_steps=scan_steps),
        grid=(rows // n_chunks,),
        in_specs=[blk, blk, _resident(m_intra.shape), _resident(w_state.shape),
                  _resident(w_out.shape), _resident(apow.shape), _resident((1, SSM_WIDTH)),
                  _resident((SSM_WIDTH, SSM_WIDTH)), _resident((1, SSM_WIDTH))],
        out_specs=blk,
        out_shape=jax.ShapeDtypeStruct((rows, wide), BF16),
        scratch_shapes=[pltpu.VMEM((n_chunks, wide), F32)],
        compiler_params=pltpu.CompilerParams(dimension_semantics=("arbitrary",),
                                             vmem_limit_bytes=VMEM_LIMIT_BYTES),
        name="s5_ssm",
    )(u2, z2, m_intra, w_state, w_out, apow, d_skip, w_glu, b_glu)
    return out.reshape(rows * SSM_CHUNK, SSM_WIDTH)


def _out_proj_kernel(x_ref, ma_ref, ms_ref, w_ref, g_ref, o_ref, *, final_norm):
    y = (jnp.dot(ma_ref[...], w_ref[:ATTN_WIDTH, :], preferred_element_type=F32)
         + jnp.dot(ms_ref[...], w_ref[ATTN_WIDTH:, :], preferred_element_type=F32))
    r = x_ref[...] + y
    if final_norm:
        ms = jnp.mean(r * r, axis=-1, keepdims=True)
        r = r * lax.rsqrt(ms + NORM_EPS) * g_ref[...]
    o_ref[...] = r


def _out_proj(x2, mixed_attn, mixed_ssm, w_out, gain, tm, final_norm):
    rows = x2.shape[0]
    row_spec = lambda width: pl.BlockSpec((tm, width), lambda i: (i, 0))
    return pl.pallas_call(
        functools.partial(_out_proj_kernel, final_norm=final_norm),
        grid=(rows // tm,),
        in_specs=[row_spec(D_MODEL), row_spec(ATTN_WIDTH), row_spec(SSM_WIDTH),
                  _resident((ATTN_WIDTH + SSM_WIDTH, D_MODEL)), _resident((1, D_MODEL))],
        out_specs=row_spec(D_MODEL),
        out_shape=jax.ShapeDtypeStruct((rows, D_MODEL), F32),
        compiler_params=pltpu.CompilerParams(dimension_semantics=("arbitrary",),
                                             vmem_limit_bytes=VMEM_LIMIT_BYTES),
        name="out_proj",
    )(x2, mixed_attn, mixed_ssm, w_out, gain)


def _rotary_tables(seq_len):
    half = HEAD_DIM // 2
    inv_freq = 1.0 / (ROPE_THETA ** (jnp.arange(half, dtype=F32) / half))
    ang = jnp.arange(seq_len, dtype=F32)[:, None] * inv_freq[None, :]
    cos, sin, zero = jnp.cos(ang), jnp.sin(ang), jnp.zeros_like(ang)
    reps = LANES // HEAD_DIM
    cos_t = jnp.tile(jnp.concatenate([cos, cos], axis=1), (1, reps))
    sin_hi = jnp.tile(jnp.concatenate([zero, sin], axis=1), (1, reps))
    sin_lo = jnp.tile(jnp.concatenate([-sin, zero], axis=1), (1, reps))
    return cos_t, sin_hi, sin_lo


def kernel(x, norm_gain, w_in, w_out, lam_re, lam_im, b_re, b_im, c_re, c_im,
           d_skip, log_dt, w_glu, b_glu, final_gain):
    bsz, seq_len, _ = x.shape
    depth = norm_gain.shape[0]
    assert seq_len % MOBA_BLOCK == 0 and seq_len // MOBA_BLOCK > MOBA_TOP_K
    n_chunks = seq_len // SSM_CHUNK
    scan_steps = max(1, (n_chunks - 1).bit_length())
    tm = min(512, seq_len)
    cos, sin_hi, sin_lo = _rotary_tables(seq_len)

    x2 = x.reshape(bsz * seq_len, D_MODEL)
    for layer in range(depth):
        m_intra, w_state, w_sout, (a_r, a_i) = _ssm_operators(
            lam_re[layer], lam_im[layer], b_re[layer], b_im[layer], c_re[layer], c_im[layer],
            log_dt[layer])
        apow = _chunk_powers(a_r, a_i, scan_steps)
        ops = (m_intra.astype(BF16), w_state.astype(BF16), w_sout.astype(BF16), apow)

        q, k, v, z_attn, u, z_ssm = _in_proj(
            x2, norm_gain[layer].reshape(1, D_MODEL), w_in[layer].astype(BF16),
            cos, sin_hi, sin_lo, seq_len, tm)
        to_seq = lambda t: t.reshape(bsz, seq_len, ATTN_WIDTH)
        mixed_attn = _moba_attention(to_seq(q), to_seq(k), to_seq(v), to_seq(z_attn))
        mixed_ssm = _s5_ssm(u, z_ssm, ops, d_skip[layer].reshape(1, SSM_WIDTH),
                            w_glu[layer].astype(BF16), b_glu[layer].reshape(1, SSM_WIDTH), seq_len)
        x2 = _out_proj(x2, mixed_attn.reshape(bsz * seq_len, ATTN_WIDTH), mixed_ssm,
                       w_out[layer].astype(BF16), final_gain.reshape(1, D_MODEL), tm,
                       final_norm=layer == depth - 1)
    return x2.reshape(bsz, seq_len, D_MODEL)
```

```python
import functools
import math

import jax
import jax.numpy as jnp
from jax import lax
from jax.experimental import pallas as pl
from jax.experimental.pallas import tpu as pltpu

F32 = jnp.float32
BF16 = jnp.bfloat16

D_MODEL = 1024
HEAD_DIM = 64
ATTN_HEADS = 8
ATTN_WIDTH = ATTN_HEADS * HEAD_DIM
MOBA_BLOCK = 256
MOBA_TOP_K = 3
ROPE_THETA = 10000.0
SSM_GROUP_DIM = 16
SSM_GROUPS = 32
SSM_WIDTH = SSM_GROUPS * SSM_GROUP_DIM
SSM_STATE = 64
IN_PROJ_WIDTH = 4 * ATTN_WIDTH + 2 * SSM_WIDTH
NORM_EPS = 1e-6
NEG_INF = -1e30

LANES = 128
SSM_CHUNK = 8
GROUPS_PER_TILE = LANES // SSM_GROUP_DIM
SSM_TILES = SSM_WIDTH // LANES
TILE_COLS = SSM_CHUNK * LANES
STATE_COLS = GROUPS_PER_TILE * SSM_STATE
VMEM_LIMIT_BYTES = 56 * 1024 * 1024


def _resident(shape):
    zeros = (0,) * len(shape)
    return pl.BlockSpec(shape, lambda *_: zeros, pipeline_mode=pl.Buffered(1))


def _in_proj_kernel(x_ref, g_ref, w_ref, cos_ref, sin_hi_ref, sin_lo_ref,
                    q_ref, k_ref, v_ref, za_ref, u_ref, zs_ref):
    x = x_ref[...]
    ms = jnp.mean(x * x, axis=-1, keepdims=True)
    h = (x * lax.rsqrt(ms + NORM_EPS) * g_ref[...]).astype(BF16)

    def section(idx):
        return jnp.dot(h, w_ref[:, idx * ATTN_WIDTH:(idx + 1) * ATTN_WIDTH],
                       preferred_element_type=F32)

    cos = cos_ref[...]
    sin_hi = sin_hi_ref[...]
    sin_lo = sin_lo_ref[...]

    def rotary(t):
        return (t * cos + pltpu.roll(t, HEAD_DIM // 2, axis=1) * sin_hi
                + pltpu.roll(t, LANES - HEAD_DIM // 2, axis=1) * sin_lo)

    q = section(0)
    k = section(1)
    scale = 1.0 / math.sqrt(HEAD_DIM)
    for c in range(ATTN_WIDTH // LANES):
        sl = slice(c * LANES, (c + 1) * LANES)
        q_ref[:, sl] = (rotary(q[:, sl]) * scale).astype(BF16)
        k_ref[:, sl] = rotary(k[:, sl]).astype(BF16)
    v_ref[...] = section(2).astype(BF16)
    za_ref[...] = section(3)
    u_ref[...] = section(4)
    zs_ref[...] = section(5)


def _in_proj(x2, gain, w_in, cos, sin_hi, sin_lo, seq_len, tm):
    rows = x2.shape[0]
    pos_blocks = seq_len // tm
    row_spec = lambda width: pl.BlockSpec((tm, width), lambda i: (i, 0))
    tab_spec = pl.BlockSpec((tm, LANES), lambda i: (i % pos_blocks, 0))
    out_bf16 = jax.ShapeDtypeStruct((rows, ATTN_WIDTH), BF16)
    out_f32 = jax.ShapeDtypeStruct((rows, ATTN_WIDTH), F32)
    return pl.pallas_call(
        _in_proj_kernel,
        grid=(rows // tm,),
        in_specs=[row_spec(D_MODEL), _resident((1, D_MODEL)), _resident((D_MODEL, IN_PROJ_WIDTH)),
                  tab_spec, tab_spec, tab_spec],
        out_specs=[row_spec(ATTN_WIDTH)] * 6,
        out_shape=[out_bf16, out_bf16, out_bf16, out_f32, out_f32, out_f32],
        compiler_params=pltpu.CompilerParams(dimension_semantics=("arbitrary",),
                                             vmem_limit_bytes=VMEM_LIMIT_BYTES),
        name="in_proj",
    )(x2, gain, w_in, cos, sin_hi, sin_lo)


def _attn_kernel(q_ref, k_ref, v_ref, z_ref, o_ref, kmean_ref, vt_ref, bias_ref, s_ref, *,
                 n_blocks):
    qi = pl.program_id(2)
    blk = MOBA_BLOCK
    heads = LANES // HEAD_DIM

    @pl.when(qi == 0)
    def _():
        for j in range(n_blocks):
            rows = slice(j * blk, (j + 1) * blk)
            kmean_ref[j:j + 1, :] = jnp.mean(k_ref[0, rows, :].astype(F32), axis=0, keepdims=True)
            vt_ref[j] = v_ref[0, rows, :].astype(F32).T.astype(BF16)

    qt = q_ref[0].astype(F32).T
    dim = lax.broadcasted_iota(jnp.int32, (LANES, blk), 0)
    qt_heads = tuple(
        jnp.where((dim >= h * HEAD_DIM) & (dim < (h + 1) * HEAD_DIM), qt, 0.0).astype(BF16)
        for h in range(heads))
    kmean = kmean_ref[...].astype(BF16)

    blk_row = lax.broadcasted_iota(jnp.int32, (n_blocks, blk), 0)
    past = blk_row < qi
    for h in range(heads):
        gate = jnp.dot(kmean, qt_heads[h], preferred_element_type=F32)
        gate = jnp.where(past, gate, NEG_INF)
        beaten_by = jnp.zeros((n_blocks, blk), jnp.int32)
        for j in range(n_blocks):
            gj = gate[j:j + 1, :]
            wins = (gj > gate) | ((gj == gate) & (j < blk_row))
            beaten_by = beaten_by + wins.astype(jnp.int32)
        keep = past & (beaten_by < MOBA_TOP_K)
        bias_ref[h] = jnp.where(keep, 0.0, NEG_INF)

    def scores(kj, h):
        return jnp.dot(kj, qt_heads[h], preferred_element_type=F32)

    own = pl.multiple_of(qi * blk, blk)
    k_own = k_ref[0, pl.ds(own, blk), :]
    key = lax.broadcasted_iota(jnp.int32, (blk, blk), 0)
    query = lax.broadcasted_iota(jnp.int32, (blk, blk), 1)
    m_own = []
    for h in range(heads):
        s = jnp.where(key <= query, scores(k_own, h), NEG_INF)
        s_ref[h, qi] = s
        m_own.append(jnp.max(s, axis=0, keepdims=True))

    def stage_scores(j, m):
        kj = k_ref[0, j * blk:(j + 1) * blk, :]
        new = []
        for h in range(heads):
            s = scores(kj, h) + bias_ref[h, pl.ds(j, 1), :]
            s_ref[h, j] = s
            new.append(jnp.maximum(m[h], jnp.max(s, axis=0, keepdims=True)))
        return tuple(new)

    def accumulate(j, m, carry):
        vtj = vt_ref[j]
        new = []
        for h, (l, acc) in enumerate(carry):
            p = jnp.exp(s_ref[h, j] - m[h])
            l = l + jnp.sum(p, axis=0, keepdims=True)
            acc = acc + jnp.dot(vtj[h * HEAD_DIM:(h + 1) * HEAD_DIM, :], p.astype(BF16),
                                preferred_element_type=F32)
            new.append((l, acc))
        return tuple(new)

    init = (jnp.zeros((1, blk), F32), jnp.zeros((HEAD_DIM, blk), F32))

    def visible_blocks(n_past):
        def run(m):
            for j in range(n_past):
                m = stage_scores(j, m)
            state = (init,) * heads
            for j in range(n_past):
                state = accumulate(j, m, state)
            return accumulate(qi, m, state)
        return run

    state = lax.switch(qi, [visible_blocks(n) for n in range(n_blocks)], tuple(m_own))
    o_t = jnp.concatenate([acc / l for l, acc in state], axis=0)
    z = z_ref[0]
    o_ref[0] = (o_t.T * (z / (1.0 + jnp.exp(-z)))).astype(BF16)


def _moba_attention(q, k, v, z_attn):
    bsz, seq_len, _ = q.shape
    n_blocks = seq_len // MOBA_BLOCK
    head_pairs = ATTN_WIDTH // LANES
    q_spec = pl.BlockSpec((1, MOBA_BLOCK, LANES), lambda b, h, i: (b, i, h))
    kv_spec = pl.BlockSpec((1, seq_len, LANES), lambda b, h, i: (b, 0, h))
    return pl.pallas_call(
        functools.partial(_attn_kernel, n_blocks=n_blocks),
        grid=(bsz, head_pairs, n_blocks),
        in_specs=[q_spec, kv_spec, kv_spec, q_spec],
        out_specs=q_spec,
        out_shape=jax.ShapeDtypeStruct((bsz, seq_len, ATTN_WIDTH), BF16),
        scratch_shapes=[pltpu.VMEM((n_blocks, LANES), F32),
                        pltpu.VMEM((n_blocks, LANES, MOBA_BLOCK), BF16),
                        pltpu.VMEM((LANES // HEAD_DIM, n_blocks, MOBA_BLOCK), F32),
                        pltpu.VMEM((LANES // HEAD_DIM, n_blocks, MOBA_BLOCK, MOBA_BLOCK), F32)],
        compiler_params=pltpu.CompilerParams(
            dimension_semantics=("arbitrary", "arbitrary", "arbitrary"),
            vmem_limit_bytes=VMEM_LIMIT_BYTES),
        name="moba_attn",
    )(q, k, v, z_attn)


def _ssm_operators(lam_re, lam_im, b_re, b_im, c_re, c_im, log_dt):
    T, G, P, H = SSM_CHUNK, SSM_GROUPS, SSM_STATE, SSM_GROUP_DIM
    Q, GL = SSM_TILES, GROUPS_PER_TILE
    dt = jnp.exp(log_dt.astype(F32))[:, None]
    lam_r, lam_i = lam_re.astype(F32), lam_im.astype(F32)
    mag = jnp.exp(lam_r * dt)
    bar_r, bar_i = mag * jnp.cos(lam_i * dt), mag * jnp.sin(lam_i * dt)
    den = lam_r * lam_r + lam_i * lam_i
    f_r = ((bar_r - 1.0) * lam_r + bar_i * lam_i) / den
    f_i = (bar_i * lam_r - (bar_r - 1.0) * lam_i) / den
    bb_r = f_r[..., None] * b_re - f_i[..., None] * b_im
    bb_i = f_r[..., None] * b_im + f_i[..., None] * b_re

    pw_r, pw_i = [jnp.ones_like(bar_r)], [jnp.zeros_like(bar_r)]
    for _ in range(T):
        r, i = pw_r[-1], pw_i[-1]
        pw_r.append(r * bar_r - i * bar_i)
        pw_i.append(r * bar_i + i * bar_r)
    pw_r, pw_i = jnp.stack(pw_r), jnp.stack(pw_i)

    lb_r = pw_r[..., None] * bb_r - pw_i[..., None] * bb_i
    lb_i = pw_r[..., None] * bb_i + pw_i[..., None] * bb_r
    kd = (jnp.sum(c_re[None, :, :, :, None] * lb_r[:T, :, None, :, :], axis=3)
          - jnp.sum(c_im[None, :, :, :, None] * lb_i[:T, :, None, :, :], axis=3))

    eye = jnp.eye(GL, dtype=F32)
    steps = jnp.arange(T)
    lag = steps[None, :] - steps[:, None]
    toep = jnp.where((lag >= 0)[:, :, None, None, None], kd[jnp.clip(lag, 0)], 0.0)
    toep = toep.reshape(T, T, Q, GL, H, H)
    m_intra = jnp.einsum('jiqaoh,ac->qjahico', toep, eye).reshape(Q, T * LANES, T * LANES)

    sin_r = lb_r[T - 1 - steps].reshape(T, Q, GL, P, H)
    sin_i = lb_i[T - 1 - steps].reshape(T, Q, GL, P, H)
    sin = jnp.stack([sin_r, sin_i])
    w_state = jnp.einsum('rjqaph,ac->qjahrcp', sin, eye).reshape(Q, T * LANES, 2 * STATE_COLS)

    cl_r = c_re[None] * pw_r[1:, :, None, :] - c_im[None] * pw_i[1:, :, None, :]
    cl_i = c_re[None] * pw_i[1:, :, None, :] + c_im[None] * pw_r[1:, :, None, :]
    sout = jnp.stack([cl_r, -cl_i]).reshape(2, T, Q, GL, H, P)
    w_out = jnp.einsum('riqaop,ac->qrapico', sout, eye).reshape(Q, 2 * STATE_COLS, T * LANES)

    return m_intra, w_state, w_out, (pw_r[T], pw_i[T])


def _chunk_powers(a_r, a_i, n_steps):
    rs, is_ = [a_r], [a_i]
    for _ in range(n_steps - 1):
        r, i = rs[-1], is_[-1]
        rs.append(r * r - i * i)
        is_.append(2.0 * r * i)
    to_tiles = lambda a: a.reshape(SSM_TILES, STATE_COLS)
    return jnp.stack([jnp.stack([to_tiles(r), to_tiles(i)], axis=1) for r, i in zip(rs, is_)],
                     axis=1)


def _ssm_kernel(u_ref, zs_ref, m_ref, ws_ref, wo_ref, apow_ref, d_ref, wg_ref, bg_ref,
                o_ref, y_ref, *, n_chunks, scan_steps):
    row = lax.broadcasted_iota(jnp.int32, (n_chunks, STATE_COLS), 0)

    def shifted(a, sh):
        return jnp.where(row >= sh, pltpu.roll(a, sh, axis=0), 0.0)

    for q in range(SSM_TILES):
        cols = [slice(i * SSM_WIDTH + q * LANES, i * SSM_WIDTH + (q + 1) * LANES)
                for i in range(SSM_CHUNK)]
        xq = jnp.concatenate([u_ref[:, c] for c in cols], axis=1)
        xq_lo = xq.astype(BF16)
        y = jnp.dot(xq_lo, m_ref[q], preferred_element_type=F32)
        s = jnp.dot(xq_lo, ws_ref[q], preferred_element_type=F32)
        re, im = s[:, :STATE_COLS], s[:, STATE_COLS:]
        for k in range(scan_steps):
            a_re = apow_ref[q, k, 0:1, :]
            a_im = apow_ref[q, k, 1:2, :]
            sre, sim = shifted(re, 1 << k), shifted(im, 1 << k)
            re, im = re + a_re * sre - a_im * sim, im + a_re * sim + a_im * sre
        prev = jnp.concatenate([shifted(re, 1), shifted(im, 1)], axis=1).astype(BF16)
        y = y + jnp.dot(prev, wo_ref[q], preferred_element_type=F32)
        d_tile = d_ref[:, q * LANES:(q + 1) * LANES]
        y = y + xq * jnp.concatenate([d_tile] * SSM_CHUNK, axis=1)
        for i, c in enumerate(cols):
            y_ref[:, c] = y[:, i * LANES:(i + 1) * LANES]

    gelu_c = math.sqrt(2.0 / math.pi)
    for i in range(SSM_CHUNK):
        sl = slice(i * SSM_WIDTH, (i + 1) * SSM_WIDTH)
        y = y_ref[:, sl]
        y = 0.5 * y * (1.0 + jnp.tanh(gelu_c * (y + 0.044715 * (y * y * y))))
        g = jnp.dot(y.astype(BF16), wg_ref[...], preferred_element_type=F32) + bg_ref[...]
        z = zs_ref[:, sl]
        o = y / (1.0 + jnp.exp(-g)) * (z / (1.0 + jnp.exp(-z)))
        o_ref[:, sl] = o.astype(BF16)


def _s5_ssm(u, z_ssm, ops, d_skip, w_glu, b_glu, seq_len):
    m_intra, w_state, w_out, apow = ops
    rows = u.shape[0] // SSM_CHUNK
    n_chunks = seq_len // SSM_CHUNK
    scan_steps = apow.shape[1]
    wide = SSM_CHUNK * SSM_WIDTH
    u2 = u.reshape(rows, wide)
    z2 = z_ssm.reshape(rows, wide)
    blk = pl.BlockSpec((n_chunks, wide), lambda b: (b, 0))
    out = pl.pallas_call(
        functools.partial(_ssm_kernel, n_chunks=n_chunks, scan_steps=scan_steps),
        grid=(rows // n_chunks,),
        in_specs=[blk, blk, _resident(m_intra.shape), _resident(w_state.shape),
                  _resident(w_out.shape), _resident(apow.shape), _resident((1, SSM_WIDTH)),
                  _resident((SSM_WIDTH, SSM_WIDTH)), _resident((1, SSM_WIDTH))],
        out_specs=blk,
        out_shape=jax.ShapeDtypeStruct((rows, wide), BF16),
        scratch_shapes=[pltpu.VMEM((n_chunks, wide), F32)],
        compiler_params=pltpu.CompilerParams(dimension_semantics=("arbitrary",),
                                             vmem_limit_bytes=VMEM_LIMIT_BYTES),
        name="s5_ssm",
    )(u2, z2, m_intra, w_state, w_out, apow, d_skip, w_glu, b_glu)
    return out.reshape(rows * SSM_CHUNK, SSM_WIDTH)


def _out_proj_kernel(x_ref, ma_ref, ms_ref, w_ref, g_ref, o_ref, *, final_norm):
    y = (jnp.dot(ma_ref[...], w_ref[:ATTN_WIDTH, :], preferred_element_type=F32)
         + jnp.dot(ms_ref[...], w_ref[ATTN_WIDTH:, :], preferred_element_type=F32))
    r = x_ref[...] + y
    if final_norm:
        ms = jnp.mean(r * r, axis=-1, keepdims=True)
        r = r * lax.rsqrt(ms + NORM_EPS) * g_ref[...]
    o_ref[...] = r


def _out_proj(x2, mixed_attn, mixed_ssm, w_out, gain, tm, final_norm):
    rows = x2.shape[0]
    row_spec = lambda width: pl.BlockSpec((tm, width), lambda i: (i, 0))
    return pl.pallas_call(
        functools.partial(_out_proj_kernel, final_norm=final_norm),
        grid=(rows // tm,),
        in_specs=[row_spec(D_MODEL), row_spec(ATTN_WIDTH), row_spec(SSM_WIDTH),
                  _resident((ATTN_WIDTH + SSM_WIDTH, D_MODEL)), _resident((1, D_MODEL))],
        out_specs=row_spec(D_MODEL),
        out_shape=jax.ShapeDtypeStruct((rows, D_MODEL), F32),
        compiler_params=pltpu.CompilerParams(dimension_semantics=("arbitrary",),
                                             vmem_limit_bytes=VMEM_LIMIT_BYTES),
        name="out_proj",
    )(x2, mixed_attn, mixed_ssm, w_out, gain)


def _rotary_tables(seq_len):
    half = HEAD_DIM // 2
    inv_freq = 1.0 / (ROPE_THETA ** (jnp.arange(half, dtype=F32) / half))
    ang = jnp.arange(seq_len, dtype=F32)[:, None] * inv_freq[None, :]
    cos, sin, zero = jnp.cos(ang), jnp.sin(ang), jnp.zeros_like(ang)
    reps = LANES // HEAD_DIM
    cos_t = jnp.tile(jnp.concatenate([cos, cos], axis=1), (1, reps))
    sin_hi = jnp.tile(jnp.concatenate([zero, sin], axis=1), (1, reps))
    sin_lo = jnp.tile(jnp.concatenate([-sin, zero], axis=1), (1, reps))
    return cos_t, sin_hi, sin_lo


def kernel(x, norm_gain, w_in, w_out, lam_re, lam_im, b_re, b_im, c_re, c_im,
           d_skip, log_dt, w_glu, b_glu, final_gain):
    bsz, seq_len, _ = x.shape
    depth = norm_gain.shape[0]
    assert seq_len % MOBA_BLOCK == 0 and seq_len // MOBA_BLOCK > MOBA_TOP_K
    n_chunks = seq_len // SSM_CHUNK
    scan_steps = max(1, (n_chunks - 1).bit_length())
    tm = min(512, seq_len)
    cos, sin_hi, sin_lo = _rotary_tables(seq_len)

    x2 = x.reshape(bsz * seq_len, D_MODEL)
    for layer in range(depth):
        m_intra, w_state, w_sout, (a_r, a_i) = _ssm_operators(
            lam_re[layer], lam_im[layer], b_re[layer], b_im[layer], c_re[layer], c_im[layer],
            log_dt[layer])
        apow = _chunk_powers(a_r, a_i, scan_steps)
        ops = (m_intra.astype(BF16), w_state.astype(BF16), w_sout.astype(BF16), apow)

        q, k, v, z_attn, u, z_ssm = _in_proj(
            x2, norm_gain[layer].reshape(1, D_MODEL), w_in[layer].astype(BF16),
            cos, sin_hi, sin_lo, seq_len, tm)
        to_seq = lambda t: t.reshape(bsz, seq_len, ATTN_WIDTH)
        mixed_attn = _moba_attention(to_seq(q), to_seq(k), to_seq(v), to_seq(z_attn))
        mixed_ssm = _s5_ssm(u, z_ssm, ops, d_skip[layer].reshape(1, SSM_WIDTH),
                            w_glu[layer].astype(BF16), b_glu[layer].reshape(1, SSM_WIDTH), seq_len)
        x2 = _out_proj(x2, mixed_attn.reshape(bsz * seq_len, ATTN_WIDTH), mixed_ssm,
                       w_out[layer].astype(BF16), final_gain.reshape(1, D_MODEL), tm,
                       final_norm=layer == depth - 1)
    return x2.reshape(bsz, seq_len, D_MODEL)
```

```python
import functools
import math

import jax
import jax.numpy as jnp
from jax import lax
from jax.experimental import pallas as pl
from jax.experimental.pallas import tpu as pltpu

F32 = jnp.float32
BF16 = jnp.bfloat16

D_MODEL = 1024
HEAD_DIM = 64
ATTN_HEADS = 8
ATTN_WIDTH = ATTN_HEADS * HEAD_DIM
MOBA_BLOCK = 256
MOBA_TOP_K = 3
ROPE_THETA = 10000.0
SSM_GROUP_DIM = 16
SSM_GROUPS = 32
SSM_WIDTH = SSM_GROUPS * SSM_GROUP_DIM
SSM_STATE = 64
IN_PROJ_WIDTH = 4 * ATTN_WIDTH + 2 * SSM_WIDTH
NORM_EPS = 1e-6
NEG_INF = -1e30

LANES = 128
SSM_CHUNK = 8
GROUPS_PER_TILE = LANES // SSM_GROUP_DIM
SSM_TILES = SSM_WIDTH // LANES
TILE_COLS = SSM_CHUNK * LANES
STATE_COLS = GROUPS_PER_TILE * SSM_STATE
VMEM_LIMIT_BYTES = 56 * 1024 * 1024


def _resident(shape):
    zeros = (0,) * len(shape)
    return pl.BlockSpec(shape, lambda *_: zeros, pipeline_mode=pl.Buffered(1))


def _in_proj_kernel(x_ref, g_ref, w_ref, cos_ref, sin_hi_ref, sin_lo_ref,
                    q_ref, k_ref, v_ref, za_ref, u_ref, zs_ref):
    x = x_ref[...]
    ms = jnp.mean(x * x, axis=-1, keepdims=True)
    h = (x * lax.rsqrt(ms + NORM_EPS) * g_ref[...]).astype(BF16)

    def section(idx):
        return jnp.dot(h, w_ref[:, idx * ATTN_WIDTH:(idx + 1) * ATTN_WIDTH],
                       preferred_element_type=F32)

    cos = cos_ref[...]
    sin_hi = sin_hi_ref[...]
    sin_lo = sin_lo_ref[...]

    def rotary(t):
        return (t * cos + pltpu.roll(t, HEAD_DIM // 2, axis=1) * sin_hi
                + pltpu.roll(t, LANES - HEAD_DIM // 2, axis=1) * sin_lo)

    q = section(0)
    k = section(1)
    scale = 1.0 / math.sqrt(HEAD_DIM)
    for c in range(ATTN_WIDTH // LANES):
        sl = slice(c * LANES, (c + 1) * LANES)
        q_ref[:, sl] = (rotary(q[:, sl]) * scale).astype(BF16)
        k_ref[:, sl] = rotary(k[:, sl]).astype(BF16)
    v_ref[...] = section(2).astype(BF16)
    za_ref[...] = section(3).astype(BF16)
    u_ref[...] = section(4)
    zs_ref[...] = section(5).astype(BF16)


def _in_proj(x2, gain, w_in, cos, sin_hi, sin_lo, seq_len, tm):
    rows = x2.shape[0]
    pos_blocks = seq_len // tm
    row_spec = lambda width: pl.BlockSpec((tm, width), lambda i: (i, 0))
    tab_spec = pl.BlockSpec((tm, LANES), lambda i: (i % pos_blocks, 0))
    out_bf16 = jax.ShapeDtypeStruct((rows, ATTN_WIDTH), BF16)
    out_f32 = jax.ShapeDtypeStruct((rows, ATTN_WIDTH), F32)
    return pl.pallas_call(
        _in_proj_kernel,
        grid=(rows // tm,),
        in_specs=[row_spec(D_MODEL), _resident((1, D_MODEL)), _resident((D_MODEL, IN_PROJ_WIDTH)),
                  tab_spec, tab_spec, tab_spec],
        out_specs=[row_spec(ATTN_WIDTH)] * 6,
        out_shape=[out_bf16, out_bf16, out_bf16, out_bf16, out_f32, out_bf16],
        compiler_params=pltpu.CompilerParams(dimension_semantics=("arbitrary",),
                                             vmem_limit_bytes=VMEM_LIMIT_BYTES),
        name="in_proj",
    )(x2, gain, w_in, cos, sin_hi, sin_lo)


def _attn_kernel(q_ref, k_ref, v_ref, z_ref, o_ref, kmean_ref, vt_ref, bias_ref, s_ref, *,
                 n_blocks):
    qi = pl.program_id(2)
    blk = MOBA_BLOCK
    heads = LANES // HEAD_DIM

    @pl.when(qi == 0)
    def _():
        for j in range(n_blocks):
            rows = slice(j * blk, (j + 1) * blk)
            kmean_ref[j:j + 1, :] = jnp.mean(k_ref[0, rows, :].astype(F32), axis=0, keepdims=True)
            vt_ref[j] = v_ref[0, rows, :].astype(F32).T.astype(BF16)

    qt = q_ref[0].astype(F32).T
    dim = lax.broadcasted_iota(jnp.int32, (LANES, blk), 0)
    qt_heads = tuple(
        jnp.where((dim >= h * HEAD_DIM) & (dim < (h + 1) * HEAD_DIM), qt, 0.0).astype(BF16)
        for h in range(heads))
    kmean = kmean_ref[...].astype(BF16)

    blk_row = lax.broadcasted_iota(jnp.int32, (n_blocks, blk), 0)
    past = blk_row < qi
    for h in range(heads):
        gate = jnp.dot(kmean, qt_heads[h], preferred_element_type=F32)
        gate = jnp.where(past, gate, NEG_INF)
        beaten_by = jnp.zeros((n_blocks, blk), jnp.int32)
        for j in range(n_blocks):
            gj = gate[j:j + 1, :]
            wins = (gj > gate) | ((gj == gate) & (j < blk_row))
            beaten_by = beaten_by + wins.astype(jnp.int32)
        keep = past & (beaten_by < MOBA_TOP_K)
        bias_ref[h] = jnp.where(keep, 0.0, NEG_INF)

    def scores(kj, h):
        return jnp.dot(kj, qt_heads[h], preferred_element_type=F32)

    own = pl.multiple_of(qi * blk, blk)
    k_own = k_ref[0, pl.ds(own, blk), :]
    key = lax.broadcasted_iota(jnp.int32, (blk, blk), 0)
    query = lax.broadcasted_iota(jnp.int32, (blk, blk), 1)
    m_own = []
    for h in range(heads):
        s = jnp.where(key <= query, scores(k_own, h), NEG_INF)
        s_ref[h, qi] = s
        m_own.append(jnp.max(s, axis=0, keepdims=True))

    def stage_scores(j, m):
        kj = k_ref[0, j * blk:(j + 1) * blk, :]
        new = []
        for h in range(heads):
            s = scores(kj, h) + bias_ref[h, pl.ds(j, 1), :]
            s_ref[h, j] = s
            new.append(jnp.maximum(m[h], jnp.max(s, axis=0, keepdims=True)))
        return tuple(new)

    def accumulate(j, m, carry):
        vtj = vt_ref[j]
        new = []
        for h, (l, acc) in enumerate(carry):
            p = jnp.exp(s_ref[h, j] - m[h])
            l = l + jnp.sum(p, axis=0, keepdims=True)
            acc = acc + jnp.dot(vtj[h * HEAD_DIM:(h + 1) * HEAD_DIM, :], p.astype(BF16),
                                preferred_element_type=F32)
            new.append((l, acc))
        return tuple(new)

    init = (jnp.zeros((1, blk), F32), jnp.zeros((HEAD_DIM, blk), F32))

    def visible_blocks(n_past):
        def run(m):
            for j in range(n_past):
                m = stage_scores(j, m)
            state = (init,) * heads
            for j in range(n_past):
                state = accumulate(j, m, state)
            return accumulate(qi, m, state)
        return run

    state = lax.switch(qi, [visible_blocks(n) for n in range(n_blocks)], tuple(m_own))
    o_t = jnp.concatenate([acc / l for l, acc in state], axis=0)
    z = z_ref[0].astype(F32)
    o_ref[0] = (o_t.T * (z / (1.0 + jnp.exp(-z)))).astype(BF16)


def _moba_attention(q, k, v, z_attn):
    bsz, seq_len, _ = q.shape
    n_blocks = seq_len // MOBA_BLOCK
    head_pairs = ATTN_WIDTH // LANES
    q_spec = pl.BlockSpec((1, MOBA_BLOCK, LANES), lambda b, h, i: (b, i, h))
    kv_spec = pl.BlockSpec((1, seq_len, LANES), lambda b, h, i: (b, 0, h))
    return pl.pallas_call(
        functools.partial(_attn_kernel, n_blocks=n_blocks),
        grid=(bsz, head_pairs, n_blocks),
        in_specs=[q_spec, kv_spec, kv_spec, q_spec],
        out_specs=q_spec,
        out_shape=jax.ShapeDtypeStruct((bsz, seq_len, ATTN_WIDTH), BF16),
        scratch_shapes=[pltpu.VMEM((n_blocks, LANES), F32),
                        pltpu.VMEM((n_blocks, LANES, MOBA_BLOCK), BF16),
                        pltpu.VMEM((LANES // HEAD_DIM, n_blocks, MOBA_BLOCK), F32),
                        pltpu.VMEM((LANES // HEAD_DIM, n_blocks, MOBA_BLOCK, MOBA_BLOCK), F32)],
        compiler_params=pltpu.CompilerParams(
            dimension_semantics=("arbitrary", "arbitrary", "arbitrary"),
            vmem_limit_bytes=VMEM_LIMIT_BYTES),
        name="moba_attn",
    )(q, k, v, z_attn)


def _ssm_tables(lam_re, lam_im, b_re, b_im, c_re, c_im, log_dt, scan_steps):
    T, G, P, H = SSM_CHUNK, SSM_GROUPS, SSM_STATE, SSM_GROUP_DIM
    dt = jnp.exp(log_dt.astype(F32))[:, None]
    lam_r, lam_i = lam_re.astype(F32), lam_im.astype(F32)
    mag = jnp.exp(lam_r * dt)
    bar_r, bar_i = mag * jnp.cos(lam_i * dt), mag * jnp.sin(lam_i * dt)
    den = lam_r * lam_r + lam_i * lam_i
    f_r = ((bar_r - 1.0) * lam_r + bar_i * lam_i) / den
    f_i = (bar_i * lam_r - (bar_r - 1.0) * lam_i) / den
    bb_r = f_r[..., None] * b_re - f_i[..., None] * b_im
    bb_i = f_r[..., None] * b_im + f_i[..., None] * b_re

    pw_r, pw_i = [jnp.ones_like(bar_r)], [jnp.zeros_like(bar_r)]
    for _ in range(T):
        r, i = pw_r[-1], pw_i[-1]
        pw_r.append(r * bar_r - i * bar_i)
        pw_i.append(r * bar_i + i * bar_r)
    pw_r, pw_i = jnp.stack(pw_r), jnp.stack(pw_i)

    lb_r = pw_r[..., None] * bb_r - pw_i[..., None] * bb_i
    lb_i = pw_r[..., None] * bb_i + pw_i[..., None] * bb_r
    kd = (jnp.sum(c_re[None, :, :, :, None] * lb_r[:T, :, None, :, :], axis=3)
          - jnp.sum(c_im[None, :, :, :, None] * lb_i[:T, :, None, :, :], axis=3))

    cl_r = c_re[None] * pw_r[1:, :, None, :] - c_im[None] * pw_i[1:, :, None, :]
    cl_i = c_re[None] * pw_i[1:, :, None, :] + c_im[None] * pw_r[1:, :, None, :]

    kd_t = kd.transpose(0, 3, 1, 2).reshape(T, H, G * H)
    lb_t = jnp.stack([lb_r[:T], lb_i[:T]], axis=1).transpose(0, 1, 4, 2, 3).reshape(T, 2, H, G * P)
    cl_t = jnp.stack([cl_r, -cl_i], axis=1).transpose(0, 1, 4, 2, 3).reshape(T, 2, P, G * H)

    a_r, a_i = [pw_r[T]], [pw_i[T]]
    for _ in range(scan_steps - 1):
        r, i = a_r[-1], a_i[-1]
        a_r.append(r * r - i * i)
        a_i.append(2.0 * r * i)
    to_tiles = lambda a: jnp.stack(a).reshape(scan_steps, SSM_TILES, STATE_COLS)
    apow = jnp.stack([to_tiles(a_r), to_tiles(a_i)], axis=2).transpose(1, 0, 2, 3)
    return kd_t, lb_t, cl_t, apow


def _ssm_kernel(u_ref, kd_ref, lb_ref, cl_ref, apow_ref, d_ref, y_ref, m_ref, ws_ref, wo_ref, *,
                seq_len, n_seq, scan_steps):
    T, H, P = SSM_CHUNK, SSM_GROUP_DIM, SSM_STATE
    n_chunks = seq_len // T
    iota = lambda shape, d: lax.broadcasted_iota(jnp.int32, shape, d)
    h_bits, p_bits = H.bit_length() - 1, P.bit_length() - 1

    @pl.when(pl.program_id(1) == 0)
    def _():
        spread_h = ((iota((LANES, H), 0) & (H - 1)) == iota((LANES, H), 1)).astype(BF16)
        spread_p = ((iota((STATE_COLS, P), 0) & (P - 1)) == iota((STATE_COLS, P), 1)).astype(BF16)
        same_hh = (iota((LANES, LANES), 0) >> h_bits) == (iota((LANES, LANES), 1) >> h_bits)
        same_hp = (iota((LANES, STATE_COLS), 0) >> h_bits) == (iota((LANES, STATE_COLS), 1) >> p_bits)
        same_ph = (iota((STATE_COLS, LANES), 0) >> p_bits) == (iota((STATE_COLS, LANES), 1) >> h_bits)

        def expand(spread, coeff, same):
            full = jnp.dot(spread, coeff.astype(BF16), preferred_element_type=F32)
            return jnp.where(same, full, 0.0).astype(BF16)

        rows = lambda j: slice(j * LANES, (j + 1) * LANES)
        for d in range(T):
            block = expand(spread_h, kd_ref[d], same_hh)
            for j in range(T - d):
                m_ref[rows(j), rows(j + d)] = block
        for j in range(T):
            for i in range(j):
                m_ref[rows(j), rows(i)] = jnp.zeros((LANES, LANES), BF16)
        for j in range(T):
            for ri in range(2):
                ws_ref[rows(j), ri * STATE_COLS:(ri + 1) * STATE_COLS] = expand(
                    spread_h, lb_ref[T - 1 - j, ri], same_hp)
        for i in range(T):
            for ri in range(2):
                wo_ref[ri * STATE_COLS:(ri + 1) * STATE_COLS, rows(i)] = expand(
                    spread_p, cl_ref[i, ri], same_ph)

    def chunk_rows(ref_slice_fn):
        return [ref_slice_fn(pl.ds(b * seq_len + i, n_chunks, stride=T))
                for b in range(n_seq) for i in range(T)]

    pieces = chunk_rows(lambda sl: u_ref[sl, :])
    x = jnp.concatenate(
        [jnp.concatenate(pieces[b * T:(b + 1) * T], axis=1) for b in range(n_seq)], axis=0)
    x_lo = x.astype(BF16)
    y = jnp.dot(x_lo, m_ref[...], preferred_element_type=F32)
    s = jnp.dot(x_lo, ws_ref[...], preferred_element_type=F32)

    chunk = jnp.concatenate([iota((n_chunks, STATE_COLS), 0)] * n_seq, axis=0)

    def shifted(a, sh):
        return jnp.where(chunk >= sh, pltpu.roll(a, sh, axis=0), 0.0)

    re, im = s[:, :STATE_COLS], s[:, STATE_COLS:]
    for k in range(scan_steps):
        a_re = apow_ref[0, k, 0:1, :]
        a_im = apow_ref[0, k, 1:2, :]
        sre, sim = shifted(re, 1 << k), shifted(im, 1 << k)
        re, im = re + a_re * sre - a_im * sim, im + a_re * sim + a_im * sre
    prev = jnp.concatenate([shifted(re, 1), shifted(im, 1)], axis=1).astype(BF16)
    y = y + jnp.dot(prev, wo_ref[...], preferred_element_type=F32)
    y = y + x * jnp.concatenate([d_ref[...]] * T, axis=1)
    for b in range(n_seq):
        for i in range(T):
            y_ref[pl.ds(b * seq_len + i, n_chunks, stride=T), :] = (
                y[b * n_chunks:(b + 1) * n_chunks, i * LANES:(i + 1) * LANES])


def _s5_ssm(u, tables, d_skip, seq_len):
    kd_t, lb_t, cl_t, apow = tables
    T, H, P = SSM_CHUNK, SSM_GROUP_DIM, SSM_STATE
    bsz = u.shape[0] // seq_len
    n_seq = max(n for n in (4, 2, 1) if bsz % n == 0)
    scan_steps = apow.shape[1]
    io_spec = pl.BlockSpec((n_seq * seq_len, LANES), lambda q, b: (b, q))
    square = pltpu.VMEM((T * LANES, T * LANES), BF16)
    return pl.pallas_call(
        functools.partial(_ssm_kernel, seq_len=seq_len, n_seq=n_seq, scan_steps=scan_steps),
        grid=(SSM_TILES, bsz // n_seq),
        in_specs=[io_spec,
                  pl.BlockSpec((T, H, LANES), lambda q, b: (0, 0, q)),
                  pl.BlockSpec((T, 2, H, STATE_COLS), lambda q, b: (0, 0, 0, q)),
                  pl.BlockSpec((T, 2, P, LANES), lambda q, b: (0, 0, 0, q)),
                  pl.BlockSpec((1, scan_steps, 2, STATE_COLS), lambda q, b: (q, 0, 0, 0)),
                  pl.BlockSpec((1, LANES), lambda q, b: (0, q))],
        out_specs=io_spec,
        out_shape=jax.ShapeDtypeStruct(u.shape, F32),
        scratch_shapes=[square, square, square],
        compiler_params=pltpu.CompilerParams(dimension_semantics=("arbitrary", "arbitrary"),
                                             vmem_limit_bytes=VMEM_LIMIT_BYTES),
        name="s5_ssm",
    )(u, kd_t, lb_t, cl_t, apow, d_skip)


def _out_proj_kernel(x_ref, ma_ref, y_ref, zs_ref, wg_ref, bg_ref, w_ref, g_ref, o_ref, *,
                     final_norm):
    y = y_ref[...]
    y = 0.5 * y * (1.0 + jnp.tanh(math.sqrt(2.0 / math.pi) * (y + 0.044715 * (y * y * y))))
    gate = jnp.dot(y.astype(BF16), wg_ref[...], preferred_element_type=F32) + bg_ref[...]
    z = zs_ref[...].astype(F32)
    mixed_ssm = y / (1.0 + jnp.exp(-gate)) * (z / (1.0 + jnp.exp(-z)))
    r = (x_ref[...]
         + jnp.dot(ma_ref[...], w_ref[:ATTN_WIDTH, :], preferred_element_type=F32)
         + jnp.dot(mixed_ssm.astype(BF16), w_ref[ATTN_WIDTH:, :], preferred_element_type=F32))
    if final_norm:
        ms = jnp.mean(r * r, axis=-1, keepdims=True)
        r = r * lax.rsqrt(ms + NORM_EPS) * g_ref[...]
    o_ref[...] = r


def _out_proj(x2, mixed_attn, y_ssm, z_ssm, w_glu, b_glu, w_out, gain, tm, final_norm):
    rows = x2.shape[0]
    row_spec = lambda width: pl.BlockSpec((tm, width), lambda i: (i, 0))
    return pl.pallas_call(
        functools.partial(_out_proj_kernel, final_norm=final_norm),
        grid=(rows // tm,),
        in_specs=[row_spec(D_MODEL), row_spec(ATTN_WIDTH), row_spec(SSM_WIDTH), row_spec(SSM_WIDTH),
                  _resident((SSM_WIDTH, SSM_WIDTH)), _resident((1, SSM_WIDTH)),
                  _resident((ATTN_WIDTH + SSM_WIDTH, D_MODEL)), _resident((1, D_MODEL))],
        out_specs=row_spec(D_MODEL),
        out_shape=jax.ShapeDtypeStruct((rows, D_MODEL), F32),
        compiler_params=pltpu.CompilerParams(dimension_semantics=("arbitrary",),
                                             vmem_limit_bytes=VMEM_LIMIT_BYTES),
        name="out_proj",
    )(x2, mixed_attn, y_ssm, z_ssm, w_glu, b_glu, w_out, gain)


def _rotary_tables(seq_len):
    half = HEAD_DIM // 2
    inv_freq = 1.0 / (ROPE_THETA ** (jnp.arange(half, dtype=F32) / half))
    ang = jnp.arange(seq_len, dtype=F32)[:, None] * inv_freq[None, :]
    cos, sin, zero = jnp.cos(ang), jnp.sin(ang), jnp.zeros_like(ang)
    reps = LANES // HEAD_DIM
    cos_t = jnp.tile(jnp.concatenate([cos, cos], axis=1), (1, reps))
    sin_hi = jnp.tile(jnp.concatenate([zero, sin], axis=1), (1, reps))
    sin_lo = jnp.tile(jnp.concatenate([-sin, zero], axis=1), (1, reps))
    return cos_t, sin_hi, sin_lo


def kernel(x, norm_gain, w_in, w_out, lam_re, lam_im, b_re, b_im, c_re, c_im,
           d_skip, log_dt, w_glu, b_glu, final_gain):
    bsz, seq_len, _ = x.shape
    depth = norm_gain.shape[0]
    assert seq_len % MOBA_BLOCK == 0 and seq_len // MOBA_BLOCK > MOBA_TOP_K
    assert seq_len % SSM_CHUNK == 0
    n_chunks = seq_len // SSM_CHUNK
    scan_steps = max(1, (n_chunks - 1).bit_length())
    tm = min(512, seq_len)
    cos, sin_hi, sin_lo = _rotary_tables(seq_len)

    x2 = x.reshape(bsz * seq_len, D_MODEL)
    for layer in range(depth):
        tables = _ssm_tables(lam_re[layer], lam_im[layer], b_re[layer], b_im[layer],
                             c_re[layer], c_im[layer], log_dt[layer], scan_steps)
        q, k, v, z_attn, u, z_ssm = _in_proj(
            x2, norm_gain[layer].reshape(1, D_MODEL), w_in[layer].astype(BF16),
            cos, sin_hi, sin_lo, seq_len, tm)
        to_seq = lambda t: t.reshape(bsz, seq_len, ATTN_WIDTH)
        mixed_attn = _moba_attention(to_seq(q), to_seq(k), to_seq(v), to_seq(z_attn))
        y_ssm = _s5_ssm(u, tables, d_skip[layer].reshape(1, SSM_WIDTH), seq_len)
        x2 = _out_proj(x2, mixed_attn.reshape(bsz * seq_len, ATTN_WIDTH), y_ssm, z_ssm,
                       w_glu[layer].astype(BF16), b_glu[layer].reshape(1, SSM_WIDTH),
                       w_out[layer].astype(BF16), final_gain.reshape(1, D_MODEL), tm,
                       final_norm=layer == depth - 1)
    return x2.reshape(bsz, seq_len, D_MODEL)
```

```python
import functools
import math

import jax
import jax.numpy as jnp
from jax import lax
from jax.experimental import pallas as pl
from jax.experimental.pallas import tpu as pltpu

F32 = jnp.float32
BF16 = jnp.bfloat16

D_MODEL = 1024
HEAD_DIM = 64
ATTN_HEADS = 8
ATTN_WIDTH = ATTN_HEADS * HEAD_DIM
MOBA_BLOCK = 256
MOBA_TOP_K = 3
ROPE_THETA = 10000.0
SSM_GROUP_DIM = 16
SSM_GROUPS = 32
SSM_WIDTH = SSM_GROUPS * SSM_GROUP_DIM
SSM_STATE = 64
IN_PROJ_WIDTH = 4 * ATTN_WIDTH + 2 * SSM_WIDTH
NORM_EPS = 1e-6
NEG_INF = -1e30

LANES = 128
SSM_CHUNK = 8
GROUPS_PER_TILE = LANES // SSM_GROUP_DIM
SSM_TILES = SSM_WIDTH // LANES
TILE_COLS = SSM_CHUNK * LANES
STATE_COLS = GROUPS_PER_TILE * SSM_STATE
VMEM_LIMIT_BYTES = 56 * 1024 * 1024


def _resident(shape):
    zeros = (0,) * len(shape)
    return pl.BlockSpec(shape, lambda *_: zeros, pipeline_mode=pl.Buffered(1))


def _in_proj_kernel(x_ref, g_ref, w_ref, cos_ref, sin_hi_ref, sin_lo_ref,
                    q_ref, k_ref, v_ref, za_ref, u_ref, zs_ref):
    x = x_ref[...]
    ms = jnp.mean(x * x, axis=-1, keepdims=True)
    h = (x * lax.rsqrt(ms + NORM_EPS) * g_ref[...]).astype(BF16)

    def section(idx):
        return jnp.dot(h, w_ref[:, idx * ATTN_WIDTH:(idx + 1) * ATTN_WIDTH],
                       preferred_element_type=F32)

    cos = cos_ref[...]
    sin_hi = sin_hi_ref[...]
    sin_lo = sin_lo_ref[...]

    def rotary(t):
        return (t * cos + pltpu.roll(t, HEAD_DIM // 2, axis=1) * sin_hi
                + pltpu.roll(t, LANES - HEAD_DIM // 2, axis=1) * sin_lo)

    q = section(0)
    k = section(1)
    scale = 1.0 / math.sqrt(HEAD_DIM)
    for c in range(ATTN_WIDTH // LANES):
        sl = slice(c * LANES, (c + 1) * LANES)
        q_ref[:, sl] = (rotary(q[:, sl]) * scale).astype(BF16)
        k_ref[:, sl] = rotary(k[:, sl]).astype(BF16)
    v_ref[...] = section(2).astype(BF16)
    za_ref[...] = section(3).astype(BF16)
    u_ref[...] = section(4)
    zs_ref[...] = section(5).astype(BF16)


def _in_proj(x2, gain, w_in, cos, sin_hi, sin_lo, seq_len, tm):
    rows = x2.shape[0]
    pos_blocks = seq_len // tm
    row_spec = lambda width: pl.BlockSpec((tm, width), lambda i: (i, 0))
    tab_spec = pl.BlockSpec((tm, LANES), lambda i: (i % pos_blocks, 0))
    out_bf16 = jax.ShapeDtypeStruct((rows, ATTN_WIDTH), BF16)
    out_f32 = jax.ShapeDtypeStruct((rows, ATTN_WIDTH), F32)
    return pl.pallas_call(
        _in_proj_kernel,
        grid=(rows // tm,),
        in_specs=[row_spec(D_MODEL), _resident((1, D_MODEL)), _resident((D_MODEL, IN_PROJ_WIDTH)),
                  tab_spec, tab_spec, tab_spec],
        out_specs=[row_spec(ATTN_WIDTH)] * 6,
        out_shape=[out_bf16, out_bf16, out_bf16, out_bf16, out_f32, out_bf16],
        compiler_params=pltpu.CompilerParams(dimension_semantics=("arbitrary",),
                                             vmem_limit_bytes=VMEM_LIMIT_BYTES),
        name="in_proj",
    )(x2, gain, w_in, cos, sin_hi, sin_lo)


def _attn_kernel(q_ref, k_ref, v_ref, z_ref, o_ref, kmean_ref, vt_ref, qt_ref, bias_ref, s_ref,
                 acc_ref, *, n_blocks):
    pair = pl.program_id(2)
    blk = MOBA_BLOCK
    heads = LANES // HEAD_DIM
    tiles = (pair, n_blocks - 1 - pair)
    tile_rows = tuple(pl.ds(pl.multiple_of(qi * blk, blk), blk) for qi in tiles)

    @pl.when(pair == 0)
    def _():
        for j in range(n_blocks):
            rows = slice(j * blk, (j + 1) * blk)
            kmean_ref[j:j + 1, :] = jnp.mean(k_ref[0, rows, :].astype(F32), axis=0, keepdims=True)
            vt_ref[j] = v_ref[0, rows, :].astype(F32).T.astype(BF16)

    kmean = kmean_ref[...].astype(BF16)
    dim = lax.broadcasted_iota(jnp.int32, (LANES, blk), 0)
    blk_row = lax.broadcasted_iota(jnp.int32, (n_blocks, blk), 0)
    key = lax.broadcasted_iota(jnp.int32, (blk, blk), 0)
    query = lax.broadcasted_iota(jnp.int32, (blk, blk), 1)
    col_max = lambda s: jnp.max(s, axis=0, keepdims=True)
    col_sum = lambda s: jnp.sum(s, axis=0, keepdims=True)

    m = {}
    for t, qi in enumerate(tiles):
        qt = q_ref[0, tile_rows[t], :].astype(F32).T
        past = blk_row < qi
        k_own = k_ref[0, tile_rows[t], :]
        for h in range(heads):
            qt_h = jnp.where((dim >= h * HEAD_DIM) & (dim < (h + 1) * HEAD_DIM), qt, 0.0).astype(BF16)
            qt_ref[t, h] = qt_h
            gate = jnp.dot(kmean, qt_h, preferred_element_type=F32)
            gate = jnp.where(past, gate, NEG_INF)
            beaten_by = jnp.zeros((n_blocks, blk), jnp.int32)
            for j in range(n_blocks):
                gj = gate[j:j + 1, :]
                wins = (gj > gate) | ((gj == gate) & (j < blk_row))
                beaten_by = beaten_by + wins.astype(jnp.int32)
            keep = past & (beaten_by < MOBA_TOP_K)
            bias_ref[t, h] = jnp.where(keep, 0.0, NEG_INF)
            s = jnp.dot(k_own, qt_h, preferred_element_type=F32)
            s = jnp.where(key <= query, s, NEG_INF)
            s_ref[t, h, qi] = s
            m[t, h] = col_max(s)

    def past_item(it):
        first = it < pair
        return first, jnp.where(first, 0, 1), jnp.where(first, it, it - pair)

    for it in range(n_blocks - 1):
        first, t, j = past_item(it)
        kj = k_ref[0, pl.ds(pl.multiple_of(j * blk, blk), blk), :]
        for h in range(heads):
            s = (jnp.dot(kj, qt_ref[t, h], preferred_element_type=F32)
                 + bias_ref[t, h, pl.ds(j, 1), :])
            s_ref[t, h, j] = s
            cm = col_max(s)
            m[0, h] = jnp.where(first, jnp.maximum(m[0, h], cm), m[0, h])
            m[1, h] = jnp.where(first, m[1, h], jnp.maximum(m[1, h], cm))

    def weighted_values(vtj, h, p):
        return jnp.dot(vtj[h * HEAD_DIM:(h + 1) * HEAD_DIM, :], p.astype(BF16),
                       preferred_element_type=F32)

    l = {}
    for t, qi in enumerate(tiles):
        vt_own = vt_ref[qi]
        for h in range(heads):
            p = jnp.exp(s_ref[t, h, qi] - m[t, h])
            l[t, h] = col_sum(p)
            acc_ref[t, h] = weighted_values(vt_own, h, p)
    for it in range(n_blocks - 1):
        first, t, j = past_item(it)
        vtj = vt_ref[j]
        for h in range(heads):
            p = jnp.exp(s_ref[t, h, j] - jnp.where(first, m[0, h], m[1, h]))
            cs = col_sum(p)
            l[0, h] = jnp.where(first, l[0, h] + cs, l[0, h])
            l[1, h] = jnp.where(first, l[1, h], l[1, h] + cs)
            acc_ref[t, h] += weighted_values(vtj, h, p)

    for t in range(len(tiles)):
        o_t = jnp.concatenate([acc_ref[t, h] / l[t, h] for h in range(heads)], axis=0)
        z = z_ref[0, tile_rows[t], :].astype(F32)
        o_ref[0, tile_rows[t], :] = (o_t.T * (z / (1.0 + jnp.exp(-z)))).astype(BF16)


def _moba_attention(q, k, v, z_attn):
    bsz, seq_len, _ = q.shape
    n_blocks = seq_len // MOBA_BLOCK
    assert n_blocks % 2 == 0
    head_pairs = ATTN_WIDTH // LANES
    heads = LANES // HEAD_DIM
    seq_spec = pl.BlockSpec((1, seq_len, LANES), lambda b, h, i: (b, 0, h))
    return pl.pallas_call(
        functools.partial(_attn_kernel, n_blocks=n_blocks),
        grid=(bsz, head_pairs, n_blocks // 2),
        in_specs=[seq_spec] * 4,
        out_specs=seq_spec,
        out_shape=jax.ShapeDtypeStruct((bsz, seq_len, ATTN_WIDTH), BF16),
        scratch_shapes=[pltpu.VMEM((n_blocks, LANES), F32),
                        pltpu.VMEM((n_blocks, LANES, MOBA_BLOCK), BF16),
                        pltpu.VMEM((2, heads, LANES, MOBA_BLOCK), BF16),
                        pltpu.VMEM((2, heads, n_blocks, MOBA_BLOCK), F32),
                        pltpu.VMEM((2, heads, n_blocks, MOBA_BLOCK, MOBA_BLOCK), F32),
                        pltpu.VMEM((2, heads, HEAD_DIM, MOBA_BLOCK), F32)],
        compiler_params=pltpu.CompilerParams(
            dimension_semantics=("arbitrary", "arbitrary", "arbitrary"),
            vmem_limit_bytes=VMEM_LIMIT_BYTES),
        name="moba_attn",
    )(q, k, v, z_attn)


def _ssm_tables(lam_re, lam_im, b_re, b_im, c_re, c_im, log_dt, scan_steps):
    T, G, P, H = SSM_CHUNK, SSM_GROUPS, SSM_STATE, SSM_GROUP_DIM
    dt = jnp.exp(log_dt.astype(F32))[:, None]
    lam_r, lam_i = lam_re.astype(F32), lam_im.astype(F32)
    mag = jnp.exp(lam_r * dt)
    bar_r, bar_i = mag * jnp.cos(lam_i * dt), mag * jnp.sin(lam_i * dt)
    den = lam_r * lam_r + lam_i * lam_i
    f_r = ((bar_r - 1.0) * lam_r + bar_i * lam_i) / den
    f_i = (bar_i * lam_r - (bar_r - 1.0) * lam_i) / den
    bb_r = f_r[..., None] * b_re - f_i[..., None] * b_im
    bb_i = f_r[..., None] * b_im + f_i[..., None] * b_re

    pw_r, pw_i = [jnp.ones_like(bar_r)], [jnp.zeros_like(bar_r)]
    for _ in range(T):
        r, i = pw_r[-1], pw_i[-1]
        pw_r.append(r * bar_r - i * bar_i)
        pw_i.append(r * bar_i + i * bar_r)
    pw_r, pw_i = jnp.stack(pw_r), jnp.stack(pw_i)

    lb_r = pw_r[..., None] * bb_r - pw_i[..., None] * bb_i
    lb_i = pw_r[..., None] * bb_i + pw_i[..., None] * bb_r
    kd = (jnp.sum(c_re[None, :, :, :, None] * lb_r[:T, :, None, :, :], axis=3)
          - jnp.sum(c_im[None, :, :, :, None] * lb_i[:T, :, None, :, :], axis=3))

    cl_r = c_re[None] * pw_r[1:, :, None, :] - c_im[None] * pw_i[1:, :, None, :]
    cl_i = c_re[None] * pw_i[1:, :, None, :] + c_im[None] * pw_r[1:, :, None, :]

    kd_t = kd.transpose(0, 3, 1, 2).reshape(T, H, G * H)
    lb_t = jnp.stack([lb_r[:T], lb_i[:T]], axis=1).transpose(0, 1, 4, 2, 3).reshape(T, 2, H, G * P)
    cl_t = jnp.stack([cl_r, -cl_i], axis=1).transpose(0, 1, 4, 2, 3).reshape(T, 2, P, G * H)

    a_r, a_i = [pw_r[T]], [pw_i[T]]
    for _ in range(scan_steps - 1):
        r, i = a_r[-1], a_i[-1]
        a_r.append(r * r - i * i)
        a_i.append(2.0 * r * i)
    to_tiles = lambda a: jnp.stack(a).reshape(scan_steps, SSM_TILES, STATE_COLS)
    apow = jnp.stack([to_tiles(a_r), to_tiles(a_i)], axis=2).transpose(1, 0, 2, 3)
    return kd_t, lb_t, cl_t, apow


def _ssm_kernel(u_ref, kd_ref, lb_ref, cl_ref, apow_ref, d_ref, y_ref, m_ref, ws_ref, wo_ref, *,
                seq_len, n_seq, scan_steps):
    T, H, P = SSM_CHUNK, SSM_GROUP_DIM, SSM_STATE
    n_chunks = seq_len // T
    iota = lambda shape, d: lax.broadcasted_iota(jnp.int32, shape, d)
    h_bits, p_bits = H.bit_length() - 1, P.bit_length() - 1

    @pl.when(pl.program_id(1) == 0)
    def _():
        spread_h = ((iota((LANES, H), 0) & (H - 1)) == iota((LANES, H), 1)).astype(BF16)
        spread_p = ((iota((STATE_COLS, P), 0) & (P - 1)) == iota((STATE_COLS, P), 1)).astype(BF16)
        same_hh = (iota((LANES, LANES), 0) >> h_bits) == (iota((LANES, LANES), 1) >> h_bits)
        same_hp = (iota((LANES, STATE_COLS), 0) >> h_bits) == (iota((LANES, STATE_COLS), 1) >> p_bits)
        same_ph = (iota((STATE_COLS, LANES), 0) >> p_bits) == (iota((STATE_COLS, LANES), 1) >> h_bits)

        def expand(spread, coeff, same):
            full = jnp.dot(spread, coeff.astype(BF16), preferred_element_type=F32)
            return jnp.where(same, full, 0.0).astype(BF16)

        rows = lambda j: slice(j * LANES, (j + 1) * LANES)
        for d in range(T):
            block = expand(spread_h, kd_ref[d], same_hh)
            for j in range(T - d):
                m_ref[rows(j), rows(j + d)] = block
        for j in range(T):
            for i in range(j):
                m_ref[rows(j), rows(i)] = jnp.zeros((LANES, LANES), BF16)
        for j in range(T):
            for ri in range(2):
                ws_ref[rows(j), ri * STATE_COLS:(ri + 1) * STATE_COLS] = expand(
                    spread_h, lb_ref[T - 1 - j, ri], same_hp)
        for i in range(T):
            for ri in range(2):
                wo_ref[ri * STATE_COLS:(ri + 1) * STATE_COLS, rows(i)] = expand(
                    spread_p, cl_ref[i, ri], same_ph)

    def chunk_rows(ref_slice_fn):
        return [ref_slice_fn(pl.ds(b * seq_len + i, n_chunks, stride=T))
                for b in range(n_seq) for i in range(T)]

    pieces = chunk_rows(lambda sl: u_ref[sl, :])
    x = jnp.concatenate(
        [jnp.concatenate(pieces[b * T:(b + 1) * T], axis=1) for b in range(n_seq)], axis=0)
    x_lo = x.astype(BF16)
    y = jnp.dot(x_lo, m_ref[...], preferred_element_type=F32)
    s = jnp.dot(x_lo, ws_ref[...], preferred_element_type=F32)

    chunk = jnp.concatenate([iota((n_chunks, STATE_COLS), 0)] * n_seq, axis=0)

    def shifted(a, sh):
        return jnp.where(chunk >= sh, pltpu.roll(a, sh, axis=0), 0.0)

    re, im = s[:, :STATE_COLS], s[:, STATE_COLS:]
    for k in range(scan_steps):
        a_re = apow_ref[0, k, 0:1, :]
        a_im = apow_ref[0, k, 1:2, :]
        sre, sim = shifted(re, 1 << k), shifted(im, 1 << k)
        re, im = re + a_re * sre - a_im * sim, im + a_re * sim + a_im * sre
    prev = jnp.concatenate([shifted(re, 1), shifted(im, 1)], axis=1).astype(BF16)
    y = y + jnp.dot(prev, wo_ref[...], preferred_element_type=F32)
    y = y + x * jnp.concatenate([d_ref[...]] * T, axis=1)
    for b in range(n_seq):
        for i in range(T):
            y_ref[pl.ds(b * seq_len + i, n_chunks, stride=T), :] = (
                y[b * n_chunks:(b + 1) * n_chunks, i * LANES:(i + 1) * LANES])


def _s5_ssm(u, tables, d_skip, seq_len):
    kd_t, lb_t, cl_t, apow = tables
    T, H, P = SSM_CHUNK, SSM_GROUP_DIM, SSM_STATE
    bsz = u.shape[0] // seq_len
    n_seq = max(n for n in (4, 2, 1) if bsz % n == 0)
    scan_steps = apow.shape[1]
    io_spec = pl.BlockSpec((n_seq * seq_len, LANES), lambda q, b: (b, q))
    square = pltpu.VMEM((T * LANES, T * LANES), BF16)
    return pl.pallas_call(
        functools.partial(_ssm_kernel, seq_len=seq_len, n_seq=n_seq, scan_steps=scan_steps),
        grid=(SSM_TILES, bsz // n_seq),
        in_specs=[io_spec,
                  pl.BlockSpec((T, H, LANES), lambda q, b: (0, 0, q)),
                  pl.BlockSpec((T, 2, H, STATE_COLS), lambda q, b: (0, 0, 0, q)),
                  pl.BlockSpec((T, 2, P, LANES), lambda q, b: (0, 0, 0, q)),
                  pl.BlockSpec((1, scan_steps, 2, STATE_COLS), lambda q, b: (q, 0, 0, 0)),
                  pl.BlockSpec((1, LANES), lambda q, b: (0, q))],
        out_specs=io_spec,
        out_shape=jax.ShapeDtypeStruct(u.shape, F32),
        scratch_shapes=[square, square, square],
        compiler_params=pltpu.CompilerParams(dimension_semantics=("arbitrary", "arbitrary"),
                                             vmem_limit_bytes=VMEM_LIMIT_BYTES),
        name="s5_ssm",
    )(u, kd_t, lb_t, cl_t, apow, d_skip)


def _out_proj_kernel(x_ref, ma_ref, y_ref, zs_ref, wg_ref, bg_ref, w_ref, g_ref, o_ref, *,
                     final_norm):
    y = y_ref[...]
    y = 0.5 * y * (1.0 + jnp.tanh(math.sqrt(2.0 / math.pi) * (y + 0.044715 * (y * y * y))))
    gate = jnp.dot(y.astype(BF16), wg_ref[...], preferred_element_type=F32) + bg_ref[...]
    z = zs_ref[...].astype(F32)
    mixed_ssm = y / (1.0 + jnp.exp(-gate)) * (z / (1.0 + jnp.exp(-z)))
    r = (x_ref[...]
         + jnp.dot(ma_ref[...], w_ref[:ATTN_WIDTH, :], preferred_element_type=F32)
         + jnp.dot(mixed_ssm.astype(BF16), w_ref[ATTN_WIDTH:, :], preferred_element_type=F32))
    if final_norm:
        ms = jnp.mean(r * r, axis=-1, keepdims=True)
        r = r * lax.rsqrt(ms + NORM_EPS) * g_ref[...]
    o_ref[...] = r


def _out_proj(x2, mixed_attn, y_ssm, z_ssm, w_glu, b_glu, w_out, gain, tm, final_norm):
    rows = x2.shape[0]
    row_spec = lambda width: pl.BlockSpec((tm, width), lambda i: (i, 0))
    return pl.pallas_call(
        functools.partial(_out_proj_kernel, final_norm=final_norm),
        grid=(rows // tm,),
        in_specs=[row_spec(D_MODEL), row_spec(ATTN_WIDTH), row_spec(SSM_WIDTH), row_spec(SSM_WIDTH),
                  _resident((SSM_WIDTH, SSM_WIDTH)), _resident((1, SSM_WIDTH)),
                  _resident((ATTN_WIDTH + SSM_WIDTH, D_MODEL)), _resident((1, D_MODEL))],
        out_specs=row_spec(D_MODEL),
        out_shape=jax.ShapeDtypeStruct((rows, D_MODEL), F32),
        compiler_params=pltpu.CompilerParams(dimension_semantics=("arbitrary",),
                                             vmem_limit_bytes=VMEM_LIMIT_BYTES),
        name="out_proj",
    )(x2, mixed_attn, y_ssm, z_ssm, w_glu, b_glu, w_out, gain)


def _rotary_tables(seq_len):
    half = HEAD_DIM // 2
    inv_freq = 1.0 / (ROPE_THETA ** (jnp.arange(half, dtype=F32) / half))
    ang = jnp.arange(seq_len, dtype=F32)[:, None] * inv_freq[None, :]
    cos, sin, zero = jnp.cos(ang), jnp.sin(ang), jnp.zeros_like(ang)
    reps = LANES // HEAD_DIM
    cos_t = jnp.tile(jnp.concatenate([cos, cos], axis=1), (1, reps))
    sin_hi = jnp.tile(jnp.concatenate([zero, sin], axis=1), (1, reps))
    sin_lo = jnp.tile(jnp.concatenate([-sin, zero], axis=1), (1, reps))
    return cos_t, sin_hi, sin_lo


def kernel(x, norm_gain, w_in, w_out, lam_re, lam_im, b_re, b_im, c_re, c_im,
           d_skip, log_dt, w_glu, b_glu, final_gain):
    bsz, seq_len, _ = x.shape
    depth = norm_gain.shape[0]
    assert seq_len % MOBA_BLOCK == 0 and seq_len // MOBA_BLOCK > MOBA_TOP_K
    assert seq_len % SSM_CHUNK == 0
    n_chunks = seq_len // SSM_CHUNK
    scan_steps = max(1, (n_chunks - 1).bit_length())
    tm = min(512, seq_len)
    cos, sin_hi, sin_lo = _rotary_tables(seq_len)

    x2 = x.reshape(bsz * seq_len, D_MODEL)
    for layer in range(depth):
        tables = _ssm_tables(lam_re[layer], lam_im[layer], b_re[layer], b_im[layer],
                             c_re[layer], c_im[layer], log_dt[layer], scan_steps)
        q, k, v, z_attn, u, z_ssm = _in_proj(
            x2, norm_gain[layer].reshape(1, D_MODEL), w_in[layer].astype(BF16),
            cos, sin_hi, sin_lo, seq_len, tm)
        to_seq = lambda t: t.reshape(bsz, seq_len, ATTN_WIDTH)
        mixed_attn = _moba_attention(to_seq(q), to_seq(k), to_seq(v), to_seq(z_attn))
        y_ssm = _s5_ssm(u, tables, d_skip[layer].reshape(1, SSM_WIDTH), seq_len)
        x2 = _out_proj(x2, mixed_attn.reshape(bsz * seq_len, ATTN_WIDTH), y_ssm, z_ssm,
                       w_glu[layer].astype(BF16), b_glu[layer].reshape(1, SSM_WIDTH),
                       w_out[layer].astype(BF16), final_gain.reshape(1, D_MODEL), tm,
                       final_norm=layer == depth - 1)
    return x2.reshape(bsz, seq_len, D_MODEL)
```

```python
import functools
import math

import jax
import jax.numpy as jnp
from jax import lax
from jax.experimental import pallas as pl
from jax.experimental.pallas import tpu as pltpu

F32 = jnp.float32
BF16 = jnp.bfloat16

D_MODEL = 1024
HEAD_DIM = 64
ATTN_HEADS = 8
ATTN_WIDTH = ATTN_HEADS * HEAD_DIM
MOBA_BLOCK = 256
MOBA_TOP_K = 3
ROPE_THETA = 10000.0
SSM_GROUP_DIM = 16
SSM_GROUPS = 32
SSM_WIDTH = SSM_GROUPS * SSM_GROUP_DIM
SSM_STATE = 64
IN_PROJ_WIDTH = 4 * ATTN_WIDTH + 2 * SSM_WIDTH
NORM_EPS = 1e-6
NEG_INF = -1e30

LANES = 128
SSM_CHUNK = 8
GROUPS_PER_TILE = LANES // SSM_GROUP_DIM
SSM_TILES = SSM_WIDTH // LANES
TILE_COLS = SSM_CHUNK * LANES
STATE_COLS = GROUPS_PER_TILE * SSM_STATE
VMEM_LIMIT_BYTES = 56 * 1024 * 1024
BF16_SUBLANES = 16
VALUE_ROWS = HEAD_DIM + BF16_SUBLANES
QK_SCALE = math.log2(math.e) / math.sqrt(HEAD_DIM)


def _resident(shape):
    zeros = (0,) * len(shape)
    return pl.BlockSpec(shape, lambda *_: zeros, pipeline_mode=pl.Buffered(1))


def _in_proj_kernel(x_ref, g_ref, w_ref, cos_ref, sin_hi_ref, sin_lo_ref,
                    q_ref, k_ref, v_ref, za_ref, u_ref, zs_ref):
    x = x_ref[...]
    ms = jnp.mean(x * x, axis=-1, keepdims=True)
    h = (x * lax.rsqrt(ms + NORM_EPS) * g_ref[...]).astype(BF16)

    def section(idx):
        return jnp.dot(h, w_ref[:, idx * ATTN_WIDTH:(idx + 1) * ATTN_WIDTH],
                       preferred_element_type=F32)

    cos = cos_ref[...]
    sin_hi = sin_hi_ref[...]
    sin_lo = sin_lo_ref[...]

    def rotary(t):
        return (t * cos + pltpu.roll(t, HEAD_DIM // 2, axis=1) * sin_hi
                + pltpu.roll(t, LANES - HEAD_DIM // 2, axis=1) * sin_lo)

    q = section(0)
    k = section(1)
    for c in range(ATTN_WIDTH // LANES):
        sl = slice(c * LANES, (c + 1) * LANES)
        q_ref[:, sl] = (rotary(q[:, sl]) * QK_SCALE).astype(BF16)
        k_ref[:, sl] = rotary(k[:, sl]).astype(BF16)
    v_ref[...] = section(2).astype(BF16)
    za_ref[...] = section(3).astype(BF16)
    u_ref[...] = section(4)
    zs_ref[...] = section(5).astype(BF16)


def _in_proj(x2, gain, w_in, cos, sin_hi, sin_lo, seq_len, tm):
    rows = x2.shape[0]
    pos_blocks = seq_len // tm
    row_spec = lambda width: pl.BlockSpec((tm, width), lambda i: (i, 0))
    tab_spec = pl.BlockSpec((tm, LANES), lambda i: (i % pos_blocks, 0))
    out_bf16 = jax.ShapeDtypeStruct((rows, ATTN_WIDTH), BF16)
    out_f32 = jax.ShapeDtypeStruct((rows, ATTN_WIDTH), F32)
    return pl.pallas_call(
        _in_proj_kernel,
        grid=(rows // tm,),
        in_specs=[row_spec(D_MODEL), _resident((1, D_MODEL)), _resident((D_MODEL, IN_PROJ_WIDTH)),
                  tab_spec, tab_spec, tab_spec],
        out_specs=[row_spec(ATTN_WIDTH)] * 6,
        out_shape=[out_bf16, out_bf16, out_bf16, out_bf16, out_f32, out_bf16],
        compiler_params=pltpu.CompilerParams(dimension_semantics=("arbitrary",),
                                             vmem_limit_bytes=VMEM_LIMIT_BYTES),
        name="in_proj",
    )(x2, gain, w_in, cos, sin_hi, sin_lo)


def _attn_kernel(q_ref, k_ref, v_ref, z_ref, o_ref, kx_ref, vx_ref, qx_ref, s_ref, acc_ref, *,
                 n_blocks):
    pair = pl.program_id(2)
    blk = MOBA_BLOCK
    heads = LANES // HEAD_DIM
    tiles = (pair, n_blocks - 1 - pair)
    tile_rows = tuple(pl.ds(pl.multiple_of(qi * blk, blk), blk) for qi in tiles)

    head_rows = lambda h: slice(h * HEAD_DIM, (h + 1) * HEAD_DIM)
    spare_base = lambda h: ((h + 1) % heads) * HEAD_DIM
    iota = lambda shape, d: lax.broadcasted_iota(jnp.int32, shape, d)

    @pl.when(pair == 0)
    def _():
        lane = iota((blk, LANES), 1)
        sum_rows = (iota((VALUE_ROWS - HEAD_DIM, blk), 0) == 0).astype(F32)
        kmean = []
        for j in range(n_blocks):
            rows = slice(j * blk, (j + 1) * blk)
            kj = k_ref[0, rows, :]
            kmean.append(jnp.mean(kj.astype(F32), axis=0, keepdims=True))
            vt = v_ref[0, rows, :].astype(F32).T
            for h in range(heads):
                in_head = (lane >= h * HEAD_DIM) & (lane < (h + 1) * HEAD_DIM)
                tag = jnp.where(lane == spare_base(h) + j, 1.0, 0.0).astype(BF16)
                kx_ref[h, j] = jnp.where(in_head, kj, tag)
                vx_ref[j, h] = jnp.concatenate([vt[head_rows(h)], sum_rows], axis=0).astype(BF16)
        kmean = jnp.concatenate(kmean, axis=0).astype(BF16)

        blk_row = iota((n_blocks, blk), 0)
        zero_rows = lambda n: jnp.zeros((n, blk), F32)
        in_head_order = lambda h, own, other: [own, other] if h == 0 else [other, own]
        for qi in range(n_blocks):
            qt = q_ref[0, qi * blk:(qi + 1) * blk, :].astype(F32).T
            past = blk_row < qi
            for h in range(heads):
                q_rows = qt[head_rows(h)]
                q_only = jnp.concatenate(in_head_order(h, q_rows, zero_rows(HEAD_DIM)), axis=0)
                gate = jnp.dot(kmean, q_only.astype(BF16), preferred_element_type=F32)
                gate = jnp.where(past, gate, NEG_INF)
                beaten_by = jnp.zeros((n_blocks, blk), jnp.int32)
                for j in range(n_blocks):
                    gj = gate[j:j + 1, :]
                    wins = (gj > gate) | ((gj == gate) & (j < blk_row))
                    beaten_by = beaten_by + wins.astype(jnp.int32)
                keep = (past & (beaten_by < MOBA_TOP_K)) | (blk_row == qi)
                bias = jnp.where(keep, 0.0, NEG_INF)
                spare = jnp.concatenate([bias, zero_rows(HEAD_DIM - n_blocks)], axis=0)
                qx_ref[qi, h] = jnp.concatenate(in_head_order(h, q_rows, spare),
                                                axis=0).astype(BF16)

    key = iota((blk, blk), 0)
    query = iota((blk, blk), 1)
    col_max = lambda s: jnp.max(s, axis=0, keepdims=True)

    m = {}
    for t, qi in enumerate(tiles):
        for h in range(heads):
            s = jnp.dot(kx_ref[h, qi], qx_ref[qi, h], preferred_element_type=F32)
            s = jnp.where(key <= query, s, NEG_INF)
            s_ref[t, h, qi] = s
            m[t, h] = col_max(s)

    def past_item(it):
        first = it < pair
        return first, jnp.where(first, 0, 1), jnp.where(first, it, it - pair)

    for it in range(n_blocks - 1):
        first, t, j = past_item(it)
        for h in range(heads):
            s = jnp.dot(kx_ref[h, j], qx_ref[jnp.where(first, tiles[0], tiles[1]), h],
                        preferred_element_type=F32)
            s_ref[t, h, j] = s
            cm = col_max(s)
            m[0, h] = jnp.where(first, jnp.maximum(m[0, h], cm), m[0, h])
            m[1, h] = jnp.where(first, m[1, h], jnp.maximum(m[1, h], cm))

    def weighted_values(j, h, s, m_h):
        p = jnp.exp2(s - m_h).astype(BF16)
        return jnp.dot(vx_ref[j, h], p, preferred_element_type=F32)

    for t, qi in enumerate(tiles):
        for h in range(heads):
            acc_ref[t, h] = weighted_values(qi, h, s_ref[t, h, qi], m[t, h])
    for it in range(n_blocks - 1):
        first, t, j = past_item(it)
        for h in range(heads):
            acc_ref[t, h] += weighted_values(j, h, s_ref[t, h, j],
                                             jnp.where(first, m[0, h], m[1, h]))

    for t in range(len(tiles)):
        acc = [acc_ref[t, h] for h in range(heads)]
        o_t = jnp.concatenate([a[:HEAD_DIM] / a[HEAD_DIM:HEAD_DIM + 1] for a in acc], axis=0)
        z = z_ref[0, tile_rows[t], :].astype(F32)
        o_ref[0, tile_rows[t], :] = (o_t.T * (z / (1.0 + jnp.exp(-z)))).astype(BF16)


def _moba_attention(q, k, v, z_attn):
    bsz, seq_len, _ = q.shape
    n_blocks = seq_len // MOBA_BLOCK
    assert n_blocks % 2 == 0 and n_blocks <= HEAD_DIM
    head_pairs = ATTN_WIDTH // LANES
    heads = LANES // HEAD_DIM
    assert heads == 2, "the block bias rides in the other head's half of the 128 lanes"
    seq_spec = pl.BlockSpec((1, seq_len, LANES), lambda b, h, i: (b, 0, h))
    return pl.pallas_call(
        functools.partial(_attn_kernel, n_blocks=n_blocks),
        grid=(bsz, head_pairs, n_blocks // 2),
        in_specs=[seq_spec] * 4,
        out_specs=seq_spec,
        out_shape=jax.ShapeDtypeStruct((bsz, seq_len, ATTN_WIDTH), BF16),
        scratch_shapes=[pltpu.VMEM((heads, n_blocks, MOBA_BLOCK, LANES), BF16),
                        pltpu.VMEM((n_blocks, heads, VALUE_ROWS, MOBA_BLOCK), BF16),
                        pltpu.VMEM((n_blocks, heads, LANES, MOBA_BLOCK), BF16),
                        pltpu.VMEM((2, heads, n_blocks, MOBA_BLOCK, MOBA_BLOCK), F32),
                        pltpu.VMEM((2, heads, VALUE_ROWS, MOBA_BLOCK), F32)],
        compiler_params=pltpu.CompilerParams(
            dimension_semantics=("arbitrary", "arbitrary", "arbitrary"),
            vmem_limit_bytes=VMEM_LIMIT_BYTES),
        name="moba_attn",
    )(q, k, v, z_attn)


def _ssm_tables(lam_re, lam_im, b_re, b_im, c_re, c_im, log_dt, scan_steps):
    T, G, P, H = SSM_CHUNK, SSM_GROUPS, SSM_STATE, SSM_GROUP_DIM
    dt = jnp.exp(log_dt.astype(F32))[:, None]
    lam_r, lam_i = lam_re.astype(F32), lam_im.astype(F32)
    mag = jnp.exp(lam_r * dt)
    bar_r, bar_i = mag * jnp.cos(lam_i * dt), mag * jnp.sin(lam_i * dt)
    den = lam_r * lam_r + lam_i * lam_i
    f_r = ((bar_r - 1.0) * lam_r + bar_i * lam_i) / den
    f_i = (bar_i * lam_r - (bar_r - 1.0) * lam_i) / den
    bb_r = f_r[..., None] * b_re - f_i[..., None] * b_im
    bb_i = f_r[..., None] * b_im + f_i[..., None] * b_re

    pw_r, pw_i = [jnp.ones_like(bar_r)], [jnp.zeros_like(bar_r)]
    for _ in range(T):
        r, i = pw_r[-1], pw_i[-1]
        pw_r.append(r * bar_r - i * bar_i)
        pw_i.append(r * bar_i + i * bar_r)
    pw_r, pw_i = jnp.stack(pw_r), jnp.stack(pw_i)

    lb_r = pw_r[..., None] * bb_r - pw_i[..., None] * bb_i
    lb_i = pw_r[..., None] * bb_i + pw_i[..., None] * bb_r
    kd = (jnp.sum(c_re[None, :, :, :, None] * lb_r[:T, :, None, :, :], axis=3)
          - jnp.sum(c_im[None, :, :, :, None] * lb_i[:T, :, None, :, :], axis=3))

    cl_r = c_re[None] * pw_r[1:, :, None, :] - c_im[None] * pw_i[1:, :, None, :]
    cl_i = c_re[None] * pw_i[1:, :, None, :] + c_im[None] * pw_r[1:, :, None, :]

    kd_t = kd.transpose(0, 3, 1, 2).reshape(T, H, G * H)
    lb_t = jnp.stack([lb_r[:T], lb_i[:T]], axis=1).transpose(0, 1, 4, 2, 3).reshape(T, 2, H, G * P)
    cl_t = jnp.stack([cl_r, -cl_i], axis=1).transpose(0, 1, 4, 2, 3).reshape(T, 2, P, G * H)

    a_r, a_i = [pw_r[T]], [pw_i[T]]
    for _ in range(scan_steps - 1):
        r, i = a_r[-1], a_i[-1]
        a_r.append(r * r - i * i)
        a_i.append(2.0 * r * i)
    to_tiles = lambda a: jnp.stack(a).reshape(scan_steps, SSM_TILES, STATE_COLS)
    apow = jnp.stack([to_tiles(a_r), to_tiles(a_i)], axis=2).transpose(1, 0, 2, 3)
    return kd_t, lb_t, cl_t, apow


def _ssm_kernel(u_ref, kd_ref, lb_ref, cl_ref, apow_ref, d_ref, y_ref, m_ref, ws_ref, wo_ref, *,
                seq_len, n_seq, scan_steps):
    T, H, P = SSM_CHUNK, SSM_GROUP_DIM, SSM_STATE
    n_chunks = seq_len // T
    iota = lambda shape, d: lax.broadcasted_iota(jnp.int32, shape, d)
    h_bits, p_bits = H.bit_length() - 1, P.bit_length() - 1

    @pl.when(pl.program_id(1) == 0)
    def _():
        spread_h = ((iota((LANES, H), 0) & (H - 1)) == iota((LANES, H), 1)).astype(BF16)
        spread_p = ((iota((STATE_COLS, P), 0) & (P - 1)) == iota((STATE_COLS, P), 1)).astype(BF16)
        same_hh = (iota((LANES, LANES), 0) >> h_bits) == (iota((LANES, LANES), 1) >> h_bits)
        same_hp = (iota((LANES, STATE_COLS), 0) >> h_bits) == (iota((LANES, STATE_COLS), 1) >> p_bits)
        same_ph = (iota((STATE_COLS, LANES), 0) >> p_bits) == (iota((STATE_COLS, LANES), 1) >> h_bits)

        def expand(spread, coeff, same):
            full = jnp.dot(spread, coeff.astype(BF16), preferred_element_type=F32)
            return jnp.where(same, full, 0.0).astype(BF16)

        rows = lambda j: slice(j * LANES, (j + 1) * LANES)
        for d in range(T):
            block = expand(spread_h, kd_ref[d], same_hh)
            for j in range(T - d):
                m_ref[rows(j), rows(j + d)] = block
        for j in range(T):
            for i in range(j):
                m_ref[rows(j), rows(i)] = jnp.zeros((LANES, LANES), BF16)
        for j in range(T):
            for ri in range(2):
                ws_ref[rows(j), ri * STATE_COLS:(ri + 1) * STATE_COLS] = expand(
                    spread_h, lb_ref[T - 1 - j, ri], same_hp)
        for i in range(T):
            for ri in range(2):
                wo_ref[ri * STATE_COLS:(ri + 1) * STATE_COLS, rows(i)] = expand(
                    spread_p, cl_ref[i, ri], same_ph)

    def chunk_rows(ref_slice_fn):
        return [ref_slice_fn(pl.ds(b * seq_len + i, n_chunks, stride=T))
                for b in range(n_seq) for i in range(T)]

    pieces = chunk_rows(lambda sl: u_ref[sl, :])
    x = jnp.concatenate(
        [jnp.concatenate(pieces[b * T:(b + 1) * T], axis=1) for b in range(n_seq)], axis=0)
    x_lo = x.astype(BF16)
    y = jnp.dot(x_lo, m_ref[...], preferred_element_type=F32)
    s = jnp.dot(x_lo, ws_ref[...], preferred_element_type=F32)

    chunk = jnp.concatenate([iota((n_chunks, STATE_COLS), 0)] * n_seq, axis=0)

    def shifted(a, sh):
        return jnp.where(chunk >= sh, pltpu.roll(a, sh, axis=0), 0.0)

    re, im = s[:, :STATE_COLS], s[:, STATE_COLS:]
    for k in range(scan_steps):
        a_re = apow_ref[0, k, 0:1, :]
        a_im = apow_ref[0, k, 1:2, :]
        sre, sim = shifted(re, 1 << k), shifted(im, 1 << k)
        re, im = re + a_re * sre - a_im * sim, im + a_re * sim + a_im * sre
    prev = jnp.concatenate([shifted(re, 1), shifted(im, 1)], axis=1).astype(BF16)
    y = y + jnp.dot(prev, wo_ref[...], preferred_element_type=F32)
    y = y + x * jnp.concatenate([d_ref[...]] * T, axis=1)
    for b in range(n_seq):
        for i in range(T):
            y_ref[pl.ds(b * seq_len + i, n_chunks, stride=T), :] = (
                y[b * n_chunks:(b + 1) * n_chunks, i * LANES:(i + 1) * LANES])


def _s5_ssm(u, tables, d_skip, seq_len):
    kd_t, lb_t, cl_t, apow = tables
    T, H, P = SSM_CHUNK, SSM_GROUP_DIM, SSM_STATE
    bsz = u.shape[0] // seq_len
    n_seq = max(n for n in (4, 2, 1) if bsz % n == 0)
    scan_steps = apow.shape[1]
    io_spec = pl.BlockSpec((n_seq * seq_len, LANES), lambda q, b: (b, q))
    square = pltpu.VMEM((T * LANES, T * LANES), BF16)
    return pl.pallas_call(
        functools.partial(_ssm_kernel, seq_len=seq_len, n_seq=n_seq, scan_steps=scan_steps),
        grid=(SSM_TILES, bsz // n_seq),
        in_specs=[io_spec,
                  pl.BlockSpec((T, H, LANES), lambda q, b: (0, 0, q)),
                  pl.BlockSpec((T, 2, H, STATE_COLS), lambda q, b: (0, 0, 0, q)),
                  pl.BlockSpec((T, 2, P, LANES), lambda q, b: (0, 0, 0, q)),
                  pl.BlockSpec((1, scan_steps, 2, STATE_COLS), lambda q, b: (q, 0, 0, 0)),
                  pl.BlockSpec((1, LANES), lambda q, b: (0, q))],
        out_specs=io_spec,
        out_shape=jax.ShapeDtypeStruct(u.shape, F32),
        scratch_shapes=[square, square, square],
        compiler_params=pltpu.CompilerParams(dimension_semantics=("arbitrary", "arbitrary"),
                                             vmem_limit_bytes=VMEM_LIMIT_BYTES),
        name="s5_ssm",
    )(u, kd_t, lb_t, cl_t, apow, d_skip)


def _out_proj_kernel(x_ref, ma_ref, y_ref, zs_ref, wg_ref, bg_ref, w_ref, g_ref, o_ref, *,
                     final_norm):
    y = y_ref[...]
    y = 0.5 * y * (1.0 + jnp.tanh(math.sqrt(2.0 / math.pi) * (y + 0.044715 * (y * y * y))))
    gate = jnp.dot(y.astype(BF16), wg_ref[...], preferred_element_type=F32) + bg_ref[...]
    z = zs_ref[...].astype(F32)
    mixed_ssm = y / (1.0 + jnp.exp(-gate)) * (z / (1.0 + jnp.exp(-z)))
    r = (x_ref[...]
         + jnp.dot(ma_ref[...], w_ref[:ATTN_WIDTH, :], preferred_element_type=F32)
         + jnp.dot(mixed_ssm.astype(BF16), w_ref[ATTN_WIDTH:, :], preferred_element_type=F32))
    if final_norm:
        ms = jnp.mean(r * r, axis=-1, keepdims=True)
        r = r * lax.rsqrt(ms + NORM_EPS) * g_ref[...]
    o_ref[...] = r


def _out_proj(x2, mixed_attn, y_ssm, z_ssm, w_glu, b_glu, w_out, gain, tm, final_norm):
    rows = x2.shape[0]
    row_spec = lambda width: pl.BlockSpec((tm, width), lambda i: (i, 0))
    return pl.pallas_call(
        functools.partial(_out_proj_kernel, final_norm=final_norm),
        grid=(rows // tm,),
        in_specs=[row_spec(D_MODEL), row_spec(ATTN_WIDTH), row_spec(SSM_WIDTH), row_spec(SSM_WIDTH),
                  _resident((SSM_WIDTH, SSM_WIDTH)), _resident((1, SSM_WIDTH)),
                  _resident((ATTN_WIDTH + SSM_WIDTH, D_MODEL)), _resident((1, D_MODEL))],
        out_specs=row_spec(D_MODEL),
        out_shape=jax.ShapeDtypeStruct((rows, D_MODEL), F32),
        compiler_params=pltpu.CompilerParams(dimension_semantics=("arbitrary",),
                                             vmem_limit_bytes=VMEM_LIMIT_BYTES),
        name="out_proj",
    )(x2, mixed_attn, y_ssm, z_ssm, w_glu, b_glu, w_out, gain)


def _rotary_tables(seq_len):
    half = HEAD_DIM // 2
    inv_freq = 1.0 / (ROPE_THETA ** (jnp.arange(half, dtype=F32) / half))
    ang = jnp.arange(seq_len, dtype=F32)[:, None] * inv_freq[None, :]
    cos, sin, zero = jnp.cos(ang), jnp.sin(ang), jnp.zeros_like(ang)
    reps = LANES // HEAD_DIM
    cos_t = jnp.tile(jnp.concatenate([cos, cos], axis=1), (1, reps))
    sin_hi = jnp.tile(jnp.concatenate([zero, sin], axis=1), (1, reps))
    sin_lo = jnp.tile(jnp.concatenate([-sin, zero], axis=1), (1, reps))
    return cos_t, sin_hi, sin_lo


def kernel(x, norm_gain, w_in, w_out, lam_re, lam_im, b_re, b_im, c_re, c_im,
           d_skip, log_dt, w_glu, b_glu, final_gain):
    bsz, seq_len, _ = x.shape
    depth = norm_gain.shape[0]
    assert seq_len % MOBA_BLOCK == 0 and seq_len // MOBA_BLOCK > MOBA_TOP_K
    assert seq_len % SSM_CHUNK == 0
    n_chunks = seq_len // SSM_CHUNK
    scan_steps = max(1, (n_chunks - 1).bit_length())
    tm = min(512, seq_len)
    cos, sin_hi, sin_lo = _rotary_tables(seq_len)

    x2 = x.reshape(bsz * seq_len, D_MODEL)
    for layer in range(depth):
        tables = _ssm_tables(lam_re[layer], lam_im[layer], b_re[layer], b_im[layer],
                             c_re[layer], c_im[layer], log_dt[layer], scan_steps)
        q, k, v, z_attn, u, z_ssm = _in_proj(
            x2, norm_gain[layer].reshape(1, D_MODEL), w_in[layer].astype(BF16),
            cos, sin_hi, sin_lo, seq_len, tm)
        to_seq = lambda t: t.reshape(bsz, seq_len, ATTN_WIDTH)
        mixed_attn = _moba_attention(to_seq(q), to_seq(k), to_seq(v), to_seq(z_attn))
        y_ssm = _s5_ssm(u, tables, d_skip[layer].reshape(1, SSM_WIDTH), seq_len)
        x2 = _out_proj(x2, mixed_attn.reshape(bsz * seq_len, ATTN_WIDTH), y_ssm, z_ssm,
                       w_glu[layer].astype(BF16), b_glu[layer].reshape(1, SSM_WIDTH),
                       w_out[layer].astype(BF16), final_gain.reshape(1, D_MODEL), tm,
                       final_norm=layer == depth - 1)
    return x2.reshape(bsz, seq_len, D_MODEL)
```

```python
import functools
import math

import jax
import jax.numpy as jnp
from jax import lax
from jax.experimental import pallas as pl
from jax.experimental.pallas import tpu as pltpu

F32 = jnp.float32
BF16 = jnp.bfloat16

D_MODEL = 1024
HEAD_DIM = 64
ATTN_HEADS = 8
ATTN_WIDTH = ATTN_HEADS * HEAD_DIM
MOBA_BLOCK = 256
MOBA_TOP_K = 3
ROPE_THETA = 10000.0
SSM_GROUP_DIM = 16
SSM_GROUPS = 32
SSM_WIDTH = SSM_GROUPS * SSM_GROUP_DIM
SSM_STATE = 64
IN_PROJ_WIDTH = 4 * ATTN_WIDTH + 2 * SSM_WIDTH
NORM_EPS = 1e-6
NEG_INF = -1e30

LANES = 128
SSM_CHUNK = 8
GROUPS_PER_TILE = LANES // SSM_GROUP_DIM
SSM_TILES = SSM_WIDTH // LANES
TILE_COLS = SSM_CHUNK * LANES
STATE_COLS = GROUPS_PER_TILE * SSM_STATE
VMEM_LIMIT_BYTES = 56 * 1024 * 1024
BF16_SUBLANES = 16
VALUE_ROWS = HEAD_DIM + BF16_SUBLANES
QK_SCALE = math.log2(math.e) / math.sqrt(HEAD_DIM)


def _resident(shape):
    zeros = (0,) * len(shape)
    return pl.BlockSpec(shape, lambda *_: zeros, pipeline_mode=pl.Buffered(1))


def _in_proj_kernel(x_ref, g_ref, w_ref, cos_ref, sin_hi_ref, sin_lo_ref,
                    q_ref, k_ref, v_ref, za_ref, u_ref, zs_ref):
    x = x_ref[...]
    ms = jnp.mean(x * x, axis=-1, keepdims=True)
    h = (x * lax.rsqrt(ms + NORM_EPS) * g_ref[...]).astype(BF16)

    def section(idx):
        return jnp.dot(h, w_ref[:, idx * ATTN_WIDTH:(idx + 1) * ATTN_WIDTH],
                       preferred_element_type=F32)

    cos = cos_ref[...]
    sin_hi = sin_hi_ref[...]
    sin_lo = sin_lo_ref[...]

    def rotary(t):
        return (t * cos + pltpu.roll(t, HEAD_DIM // 2, axis=1) * sin_hi
                + pltpu.roll(t, LANES - HEAD_DIM // 2, axis=1) * sin_lo)

    q = section(0)
    k = section(1)
    for c in range(ATTN_WIDTH // LANES):
        sl = slice(c * LANES, (c + 1) * LANES)
        q_ref[:, sl] = (rotary(q[:, sl]) * QK_SCALE).astype(BF16)
        k_ref[:, sl] = rotary(k[:, sl]).astype(BF16)
    v_ref[...] = section(2).astype(BF16)
    za_ref[...] = section(3).astype(BF16)
    u_ref[...] = section(4)
    zs_ref[...] = section(5).astype(BF16)


def _in_proj(x2, gain, w_in, cos, sin_hi, sin_lo, seq_len, tm):
    rows = x2.shape[0]
    pos_blocks = seq_len // tm
    row_spec = lambda width: pl.BlockSpec((tm, width), lambda i: (i, 0))
    tab_spec = pl.BlockSpec((tm, LANES), lambda i: (i % pos_blocks, 0))
    out_bf16 = jax.ShapeDtypeStruct((rows, ATTN_WIDTH), BF16)
    out_f32 = jax.ShapeDtypeStruct((rows, ATTN_WIDTH), F32)
    return pl.pallas_call(
        _in_proj_kernel,
        grid=(rows // tm,),
        in_specs=[row_spec(D_MODEL), _resident((1, D_MODEL)), _resident((D_MODEL, IN_PROJ_WIDTH)),
                  tab_spec, tab_spec, tab_spec],
        out_specs=[row_spec(ATTN_WIDTH)] * 6,
        out_shape=[out_bf16, out_bf16, out_bf16, out_bf16, out_f32, out_bf16],
        compiler_params=pltpu.CompilerParams(dimension_semantics=("arbitrary",),
                                             vmem_limit_bytes=VMEM_LIMIT_BYTES),
        name="in_proj",
    )(x2, gain, w_in, cos, sin_hi, sin_lo)


def _attn_kernel(q_ref, k_ref, v_ref, z_ref, o_ref, kx_ref, vx_ref, qx_ref, sa_ref, sb_ref,
                 m_ref, acc_ref, *, n_blocks):
    step = pl.program_id(2)
    blk = MOBA_BLOCK
    heads = LANES // HEAD_DIM

    head_rows = lambda h: slice(h * HEAD_DIM, (h + 1) * HEAD_DIM)
    spare_base = lambda h: ((h + 1) % heads) * HEAD_DIM
    iota = lambda shape, d: lax.broadcasted_iota(jnp.int32, shape, d)

    def prepare_operands():
        lane = iota((blk, LANES), 1)
        sum_rows = (iota((VALUE_ROWS - HEAD_DIM, blk), 0) == 0).astype(F32)
        kmean = []
        for j in range(n_blocks):
            rows = slice(j * blk, (j + 1) * blk)
            kj = k_ref[0, rows, :]
            kmean.append(jnp.mean(kj.astype(F32), axis=0, keepdims=True))
            vt = v_ref[0, rows, :].astype(F32).T
            for h in range(heads):
                in_head = (lane >= h * HEAD_DIM) & (lane < (h + 1) * HEAD_DIM)
                tag = jnp.where(lane == spare_base(h) + j, 1.0, 0.0).astype(BF16)
                kx_ref[h, j] = jnp.where(in_head, kj, tag)
                vx_ref[j, h] = jnp.concatenate([vt[head_rows(h)], sum_rows], axis=0).astype(BF16)
        kmean = jnp.concatenate(kmean, axis=0).astype(BF16)

        blk_row = iota((n_blocks, blk), 0)
        zero_rows = lambda n: jnp.zeros((n, blk), F32)
        in_head_order = lambda h, own, other: [own, other] if h == 0 else [other, own]
        for qi in range(n_blocks):
            qt = q_ref[0, qi * blk:(qi + 1) * blk, :].astype(F32).T
            past = blk_row < qi
            for h in range(heads):
                q_rows = qt[head_rows(h)]
                q_only = jnp.concatenate(in_head_order(h, q_rows, zero_rows(HEAD_DIM)), axis=0)
                gate = jnp.dot(kmean, q_only.astype(BF16), preferred_element_type=F32)
                gate = jnp.where(past, gate, NEG_INF)
                beaten_by = jnp.zeros((n_blocks, blk), jnp.int32)
                for j in range(n_blocks):
                    gj = gate[j:j + 1, :]
                    wins = (gj > gate) | ((gj == gate) & (j < blk_row))
                    beaten_by = beaten_by + wins.astype(jnp.int32)
                keep = (past & (beaten_by < MOBA_TOP_K)) | (blk_row == qi)
                bias = jnp.where(keep, 0.0, NEG_INF)
                spare = jnp.concatenate([bias, zero_rows(HEAD_DIM - n_blocks)], axis=0)
                qx_ref[qi, h] = jnp.concatenate(in_head_order(h, q_rows, spare),
                                                axis=0).astype(BF16)

    key = iota((blk, blk), 0)
    query = iota((blk, blk), 1)
    col_max = lambda s: jnp.max(s, axis=0, keepdims=True)

    def past_item(pair, it):
        first = it < pair
        return first, jnp.where(first, 0, 1), jnp.where(first, it, it - pair)

    def stage(pair, s_ref):
        tiles = (pair, n_blocks - 1 - pair)
        m = {}
        for t, qi in enumerate(tiles):
            for h in range(heads):
                s = jnp.dot(kx_ref[h, qi], qx_ref[qi, h], preferred_element_type=F32)
                s = jnp.where(key <= query, s, NEG_INF)
                s_ref[t, h, qi] = s
                m[t, h] = col_max(s)
        for it in range(n_blocks - 1):
            first, t, j = past_item(pair, it)
            for h in range(heads):
                s = jnp.dot(kx_ref[h, j], qx_ref[jnp.where(first, tiles[0], tiles[1]), h],
                            preferred_element_type=F32)
                s_ref[t, h, j] = s
                cm = col_max(s)
                m[0, h] = jnp.where(first, jnp.maximum(m[0, h], cm), m[0, h])
                m[1, h] = jnp.where(first, m[1, h], jnp.maximum(m[1, h], cm))
        return m

    def weighted_values(j, h, s, m_h):
        p = jnp.exp2(s - m_h).astype(BF16)
        return jnp.dot(vx_ref[j, h], p, preferred_element_type=F32)

    def finish(pair, s_ref, m, acc_ref):
        tiles = (pair, n_blocks - 1 - pair)
        for t, qi in enumerate(tiles):
            for h in range(heads):
                acc_ref[t, h] = weighted_values(qi, h, s_ref[t, h, qi], m[t, h])
        for it in range(n_blocks - 1):
            first, t, j = past_item(pair, it)
            for h in range(heads):
                acc_ref[t, h] += weighted_values(j, h, s_ref[t, h, j],
                                                 jnp.where(first, m[0, h], m[1, h]))
        for t, qi in enumerate(tiles):
            rows = pl.ds(pl.multiple_of(qi * blk, blk), blk)
            acc = [acc_ref[t, h] for h in range(heads)]
            o_t = jnp.concatenate([a[:HEAD_DIM] / a[HEAD_DIM:HEAD_DIM + 1] for a in acc], axis=0)
            z = z_ref[0, rows, :].astype(F32)
            o_ref[0, rows, :] = (o_t.T * (z / (1.0 + jnp.exp(-z)))).astype(BF16)

    m_keys = [(t, h) for t in range(2) for h in range(heads)]

    @pl.when(step == 0)
    def _():
        prepare_operands()
        m0 = stage(0, sa_ref)
        for t, h in m_keys:
            m_ref[t, h] = m0[t, h]

    pair_a = 2 * step
    m_a = {key_: m_ref[key_] for key_ in m_keys}
    m_b = stage(pair_a + 1, sb_ref)
    finish(pair_a, sa_ref, m_a, acc_ref.at[0])
    m_c = stage(lax.rem(pair_a + 2, n_blocks // 2), sa_ref)
    finish(pair_a + 1, sb_ref, m_b, acc_ref.at[1])
    for t, h in m_keys:
        m_ref[t, h] = m_c[t, h]


def _moba_attention(q, k, v, z_attn):
    bsz, seq_len, _ = q.shape
    n_blocks = seq_len // MOBA_BLOCK
    assert n_blocks % 4 == 0 and n_blocks <= HEAD_DIM
    head_pairs = ATTN_WIDTH // LANES
    heads = LANES // HEAD_DIM
    assert heads == 2, "the block bias rides in the other head's half of the 128 lanes"
    seq_spec = pl.BlockSpec((1, seq_len, LANES), lambda b, h, i: (b, 0, h))
    scores = pltpu.VMEM((2, heads, n_blocks, MOBA_BLOCK, MOBA_BLOCK), F32)
    return pl.pallas_call(
        functools.partial(_attn_kernel, n_blocks=n_blocks),
        grid=(bsz, head_pairs, n_blocks // 4),
        in_specs=[seq_spec] * 4,
        out_specs=seq_spec,
        out_shape=jax.ShapeDtypeStruct((bsz, seq_len, ATTN_WIDTH), BF16),
        scratch_shapes=[pltpu.VMEM((heads, n_blocks, MOBA_BLOCK, LANES), BF16),
                        pltpu.VMEM((n_blocks, heads, VALUE_ROWS, MOBA_BLOCK), BF16),
                        pltpu.VMEM((n_blocks, heads, LANES, MOBA_BLOCK), BF16),
                        scores, scores,
                        pltpu.VMEM((2, heads, 1, MOBA_BLOCK), F32),
                        pltpu.VMEM((2, 2, heads, VALUE_ROWS, MOBA_BLOCK), F32)],
        compiler_params=pltpu.CompilerParams(
            dimension_semantics=("arbitrary", "arbitrary", "arbitrary"),
            vmem_limit_bytes=VMEM_LIMIT_BYTES),
        name="moba_attn",
    )(q, k, v, z_attn)


def _ssm_tables(lam_re, lam_im, b_re, b_im, c_re, c_im, log_dt, scan_steps):
    T, G, P, H = SSM_CHUNK, SSM_GROUPS, SSM_STATE, SSM_GROUP_DIM
    dt = jnp.exp(log_dt.astype(F32))[:, None]
    lam_r, lam_i = lam_re.astype(F32), lam_im.astype(F32)
    mag = jnp.exp(lam_r * dt)
    bar_r, bar_i = mag * jnp.cos(lam_i * dt), mag * jnp.sin(lam_i * dt)
    den = lam_r * lam_r + lam_i * lam_i
    f_r = ((bar_r - 1.0) * lam_r + bar_i * lam_i) / den
    f_i = (bar_i * lam_r - (bar_r - 1.0) * lam_i) / den
    bb_r = f_r[..., None] * b_re - f_i[..., None] * b_im
    bb_i = f_r[..., None] * b_im + f_i[..., None] * b_re

    pw_r, pw_i = [jnp.ones_like(bar_r)], [jnp.zeros_like(bar_r)]
    for _ in range(T):
        r, i = pw_r[-1], pw_i[-1]
        pw_r.append(r * bar_r - i * bar_i)
        pw_i.append(r * bar_i + i * bar_r)
    pw_r, pw_i = jnp.stack(pw_r), jnp.stack(pw_i)

    lb_r = pw_r[..., None] * bb_r - pw_i[..., None] * bb_i
    lb_i = pw_r[..., None] * bb_i + pw_i[..., None] * bb_r
    kd = (jnp.sum(c_re[None, :, :, :, None] * lb_r[:T, :, None, :, :], axis=3)
          - jnp.sum(c_im[None, :, :, :, None] * lb_i[:T, :, None, :, :], axis=3))

    cl_r = c_re[None] * pw_r[1:, :, None, :] - c_im[None] * pw_i[1:, :, None, :]
    cl_i = c_re[None] * pw_i[1:, :, None, :] + c_im[None] * pw_r[1:, :, None, :]

    kd_t = kd.transpose(0, 3, 1, 2).reshape(T, H, G * H)
    lb_t = jnp.stack([lb_r[:T], lb_i[:T]], axis=1).transpose(0, 1, 4, 2, 3).reshape(T, 2, H, G * P)
    cl_t = jnp.stack([cl_r, -cl_i], axis=1).transpose(0, 1, 4, 2, 3).reshape(T, 2, P, G * H)

    a_r, a_i = [pw_r[T]], [pw_i[T]]
    for _ in range(scan_steps - 1):
        r, i = a_r[-1], a_i[-1]
        a_r.append(r * r - i * i)
        a_i.append(2.0 * r * i)
    to_tiles = lambda a: jnp.stack(a).reshape(scan_steps, SSM_TILES, STATE_COLS)
    apow = jnp.stack([to_tiles(a_r), to_tiles(a_i)], axis=2).transpose(1, 0, 2, 3)
    return kd_t, lb_t, cl_t, apow


def _ssm_kernel(u_ref, kd_ref, lb_ref, cl_ref, apow_ref, d_ref, y_ref, m_ref, ws_ref, wo_ref, *,
                seq_len, n_seq, scan_steps):
    T, H, P = SSM_CHUNK, SSM_GROUP_DIM, SSM_STATE
    n_chunks = seq_len // T
    iota = lambda shape, d: lax.broadcasted_iota(jnp.int32, shape, d)
    h_bits, p_bits = H.bit_length() - 1, P.bit_length() - 1

    @pl.when(pl.program_id(1) == 0)
    def _():
        spread_h = ((iota((LANES, H), 0) & (H - 1)) == iota((LANES, H), 1)).astype(BF16)
        spread_p = ((iota((STATE_COLS, P), 0) & (P - 1)) == iota((STATE_COLS, P), 1)).astype(BF16)
        same_hh = (iota((LANES, LANES), 0) >> h_bits) == (iota((LANES, LANES), 1) >> h_bits)
        same_hp = (iota((LANES, STATE_COLS), 0) >> h_bits) == (iota((LANES, STATE_COLS), 1) >> p_bits)
        same_ph = (iota((STATE_COLS, LANES), 0) >> p_bits) == (iota((STATE_COLS, LANES), 1) >> h_bits)

        def expand(spread, coeff, same):
            full = jnp.dot(spread, coeff.astype(BF16), preferred_element_type=F32)
            return jnp.where(same, full, 0.0).astype(BF16)

        rows = lambda j: slice(j * LANES, (j + 1) * LANES)
        for d in range(T):
            block = expand(spread_h, kd_ref[d], same_hh)
            for j in range(T - d):
                m_ref[rows(j), rows(j + d)] = block
        for j in range(T):
            for i in range(j):
                m_ref[rows(j), rows(i)] = jnp.zeros((LANES, LANES), BF16)
        for j in range(T):
            for ri in range(2):
                ws_ref[rows(j), ri * STATE_COLS:(ri + 1) * STATE_COLS] = expand(
                    spread_h, lb_ref[T - 1 - j, ri], same_hp)
        for i in range(T):
            for ri in range(2):
                wo_ref[ri * STATE_COLS:(ri + 1) * STATE_COLS, rows(i)] = expand(
                    spread_p, cl_ref[i, ri], same_ph)

    def chunk_rows(ref_slice_fn):
        return [ref_slice_fn(pl.ds(b * seq_len + i, n_chunks, stride=T))
                for b in range(n_seq) for i in range(T)]

    pieces = chunk_rows(lambda sl: u_ref[sl, :])
    x = jnp.concatenate(
        [jnp.concatenate(pieces[b * T:(b + 1) * T], axis=1) for b in range(n_seq)], axis=0)
    x_lo = x.astype(BF16)
    y = jnp.dot(x_lo, m_ref[...], preferred_element_type=F32)
    s = jnp.dot(x_lo, ws_ref[...], preferred_element_type=F32)

    chunk = jnp.concatenate([iota((n_chunks, STATE_COLS), 0)] * n_seq, axis=0)

    def shifted(a, sh):
        return jnp.where(chunk >= sh, pltpu.roll(a, sh, axis=0), 0.0)

    re, im = s[:, :STATE_COLS], s[:, STATE_COLS:]
    for k in range(scan_steps):
        a_re = apow_ref[0, k, 0:1, :]
        a_im = apow_ref[0, k, 1:2, :]
        sre, sim = shifted(re, 1 << k), shifted(im, 1 << k)
        re, im = re + a_re * sre - a_im * sim, im + a_re * sim + a_im * sre
    prev = jnp.concatenate([shifted(re, 1), shifted(im, 1)], axis=1).astype(BF16)
    y = y + jnp.dot(prev, wo_ref[...], preferred_element_type=F32)
    y = y + x * jnp.concatenate([d_ref[...]] * T, axis=1)
    for b in range(n_seq):
        for i in range(T):
            y_ref[pl.ds(b * seq_len + i, n_chunks, stride=T), :] = (
                y[b * n_chunks:(b + 1) * n_chunks, i * LANES:(i + 1) * LANES])


def _s5_ssm(u, tables, d_skip, seq_len):
    kd_t, lb_t, cl_t, apow = tables
    T, H, P = SSM_CHUNK, SSM_GROUP_DIM, SSM_STATE
    bsz = u.shape[0] // seq_len
    n_seq = max(n for n in (4, 2, 1) if bsz % n == 0)
    scan_steps = apow.shape[1]
    io_spec = pl.BlockSpec((n_seq * seq_len, LANES), lambda q, b: (b, q))
    square = pltpu.VMEM((T * LANES, T * LANES), BF16)
    return pl.pallas_call(
        functools.partial(_ssm_kernel, seq_len=seq_len, n_seq=n_seq, scan_steps=scan_steps),
        grid=(SSM_TILES, bsz // n_seq),
        in_specs=[io_spec,
                  pl.BlockSpec((T, H, LANES), lambda q, b: (0, 0, q)),
                  pl.BlockSpec((T, 2, H, STATE_COLS), lambda q, b: (0, 0, 0, q)),
                  pl.BlockSpec((T, 2, P, LANES), lambda q, b: (0, 0, 0, q)),
                  pl.BlockSpec((1, scan_steps, 2, STATE_COLS), lambda q, b: (q, 0, 0, 0)),
                  pl.BlockSpec((1, LANES), lambda q, b: (0, q))],
        out_specs=io_spec,
        out_shape=jax.ShapeDtypeStruct(u.shape, F32),
        scratch_shapes=[square, square, square],
        compiler_params=pltpu.CompilerParams(dimension_semantics=("arbitrary", "arbitrary"),
                                             vmem_limit_bytes=VMEM_LIMIT_BYTES),
        name="s5_ssm",
    )(u, kd_t, lb_t, cl_t, apow, d_skip)


def _out_proj_kernel(x_ref, ma_ref, y_ref, zs_ref, wg_ref, bg_ref, w_ref, g_ref, o_ref, *,
                     final_norm):
    y = y_ref[...]
    y = 0.5 * y * (1.0 + jnp.tanh(math.sqrt(2.0 / math.pi) * (y + 0.044715 * (y * y * y))))
    gate = jnp.dot(y.astype(BF16), wg_ref[...], preferred_element_type=F32) + bg_ref[...]
    z = zs_ref[...].astype(F32)
    mixed_ssm = y / (1.0 + jnp.exp(-gate)) * (z / (1.0 + jnp.exp(-z)))
    r = (x_ref[...]
         + jnp.dot(ma_ref[...], w_ref[:ATTN_WIDTH, :], preferred_element_type=F32)
         + jnp.dot(mixed_ssm.astype(BF16), w_ref[ATTN_WIDTH:, :], preferred_element_type=F32))
    if final_norm:
        ms = jnp.mean(r * r, axis=-1, keepdims=True)
        r = r * lax.rsqrt(ms + NORM_EPS) * g_ref[...]
    o_ref[...] = r


def _out_proj(x2, mixed_attn, y_ssm, z_ssm, w_glu, b_glu, w_out, gain, tm, final_norm):
    rows = x2.shape[0]
    row_spec = lambda width: pl.BlockSpec((tm, width), lambda i: (i, 0))
    return pl.pallas_call(
        functools.partial(_out_proj_kernel, final_norm=final_norm),
        grid=(rows // tm,),
        in_specs=[row_spec(D_MODEL), row_spec(ATTN_WIDTH), row_spec(SSM_WIDTH), row_spec(SSM_WIDTH),
                  _resident((SSM_WIDTH, SSM_WIDTH)), _resident((1, SSM_WIDTH)),
                  _resident((ATTN_WIDTH + SSM_WIDTH, D_MODEL)), _resident((1, D_MODEL))],
        out_specs=row_spec(D_MODEL),
        out_shape=jax.ShapeDtypeStruct((rows, D_MODEL), F32),
        compiler_params=pltpu.CompilerParams(dimension_semantics=("arbitrary",),
                                             vmem_limit_bytes=VMEM_LIMIT_BYTES),
        name="out_proj",
    )(x2, mixed_attn, y_ssm, z_ssm, w_glu, b_glu, w_out, gain)


def _rotary_tables(seq_len):
    half = HEAD_DIM // 2
    inv_freq = 1.0 / (ROPE_THETA ** (jnp.arange(half, dtype=F32) / half))
    ang = jnp.arange(seq_len, dtype=F32)[:, None] * inv_freq[None, :]
    cos, sin, zero = jnp.cos(ang), jnp.sin(ang), jnp.zeros_like(ang)
    reps = LANES // HEAD_DIM
    cos_t = jnp.tile(jnp.concatenate([cos, cos], axis=1), (1, reps))
    sin_hi = jnp.tile(jnp.concatenate([zero, sin], axis=1), (1, reps))
    sin_lo = jnp.tile(jnp.concatenate([-sin, zero], axis=1), (1, reps))
    return cos_t, sin_hi, sin_lo


def kernel(x, norm_gain, w_in, w_out, lam_re, lam_im, b_re, b_im, c_re, c_im,
           d_skip, log_dt, w_glu, b_glu, final_gain):
    bsz, seq_len, _ = x.shape
    depth = norm_gain.shape[0]
    assert seq_len % MOBA_BLOCK == 0 and seq_len // MOBA_BLOCK > MOBA_TOP_K
    assert seq_len % SSM_CHUNK == 0
    n_chunks = seq_len // SSM_CHUNK
    scan_steps = max(1, (n_chunks - 1).bit_length())
    tm = min(512, seq_len)
    cos, sin_hi, sin_lo = _rotary_tables(seq_len)

    x2 = x.reshape(bsz * seq_len, D_MODEL)
    for layer in range(depth):
        tables = _ssm_tables(lam_re[layer], lam_im[layer], b_re[layer], b_im[layer],
                             c_re[layer], c_im[layer], log_dt[layer], scan_steps)
        q, k, v, z_attn, u, z_ssm = _in_proj(
            x2, norm_gain[layer].reshape(1, D_MODEL), w_in[layer].astype(BF16),
            cos, sin_hi, sin_lo, seq_len, tm)
        to_seq = lambda t: t.reshape(bsz, seq_len, ATTN_WIDTH)
        mixed_attn = _moba_attention(to_seq(q), to_seq(k), to_seq(v), to_seq(z_attn))
        y_ssm = _s5_ssm(u, tables, d_skip[layer].reshape(1, SSM_WIDTH), seq_len)
        x2 = _out_proj(x2, mixed_attn.reshape(bsz * seq_len, ATTN_WIDTH), y_ssm, z_ssm,
                       w_glu[layer].astype(BF16), b_glu[layer].reshape(1, SSM_WIDTH),
                       w_out[layer].astype(BF16), final_gain.reshape(1, D_MODEL), tm,
                       final_norm=layer == depth - 1)
    return x2.reshape(bsz, seq_len, D_MODEL)
```

```python
import functools
import math

import jax
import jax.numpy as jnp
from jax import lax
from jax.experimental import pallas as pl
from jax.experimental.pallas import tpu as pltpu

F32 = jnp.float32
BF16 = jnp.bfloat16

D_MODEL = 1024
HEAD_DIM = 64
ATTN_HEADS = 8
ATTN_WIDTH = ATTN_HEADS * HEAD_DIM
MOBA_BLOCK = 256
MOBA_TOP_K = 3
ROPE_THETA = 10000.0
SSM_GROUP_DIM = 16
SSM_GROUPS = 32
SSM_WIDTH = SSM_GROUPS * SSM_GROUP_DIM
SSM_STATE = 64
IN_PROJ_WIDTH = 4 * ATTN_WIDTH + 2 * SSM_WIDTH
NORM_EPS = 1e-6
NEG_INF = -1e30

LANES = 128
SSM_CHUNK = 8
GROUPS_PER_TILE = LANES // SSM_GROUP_DIM
SSM_TILES = SSM_WIDTH // LANES
TILE_COLS = SSM_CHUNK * LANES
STATE_COLS = GROUPS_PER_TILE * SSM_STATE
VMEM_LIMIT_BYTES = 56 * 1024 * 1024
F32_SUBLANES = 8
BF16_SUBLANES = 16
VALUE_ROWS = HEAD_DIM + BF16_SUBLANES
QK_SCALE = math.log2(math.e) / math.sqrt(HEAD_DIM)


def _resident(shape):
    zeros = (0,) * len(shape)
    return pl.BlockSpec(shape, lambda *_: zeros, pipeline_mode=pl.Buffered(1))


def _in_proj_kernel(x_ref, g_ref, w_ref, cos_ref, sin_hi_ref, sin_lo_ref,
                    q_ref, k_ref, v_ref, za_ref, u_ref, zs_ref):
    x = x_ref[...]
    ms = jnp.mean(x * x, axis=-1, keepdims=True)
    h = (x * lax.rsqrt(ms + NORM_EPS) * g_ref[...]).astype(BF16)

    def section(idx):
        return jnp.dot(h, w_ref[:, idx * ATTN_WIDTH:(idx + 1) * ATTN_WIDTH],
                       preferred_element_type=F32)

    cos = cos_ref[...]
    sin_hi = sin_hi_ref[...]
    sin_lo = sin_lo_ref[...]

    def rotary(t):
        return (t * cos + pltpu.roll(t, HEAD_DIM // 2, axis=1) * sin_hi
                + pltpu.roll(t, LANES - HEAD_DIM // 2, axis=1) * sin_lo)

    q = section(0)
    k = section(1)
    for c in range(ATTN_WIDTH // LANES):
        sl = slice(c * LANES, (c + 1) * LANES)
        q_ref[:, sl] = (rotary(q[:, sl]) * QK_SCALE).astype(BF16)
        k_ref[:, sl] = rotary(k[:, sl]).astype(BF16)
    v_ref[...] = section(2).astype(BF16)
    za_ref[...] = section(3).astype(BF16)
    u_ref[...] = section(4)
    zs_ref[...] = section(5).astype(BF16)


def _in_proj(x2, gain, w_in, cos, sin_hi, sin_lo, seq_len, tm):
    rows = x2.shape[0]
    pos_blocks = seq_len // tm
    row_spec = lambda width: pl.BlockSpec((tm, width), lambda i: (i, 0))
    tab_spec = pl.BlockSpec((tm, LANES), lambda i: (i % pos_blocks, 0))
    out_bf16 = jax.ShapeDtypeStruct((rows, ATTN_WIDTH), BF16)
    out_f32 = jax.ShapeDtypeStruct((rows, ATTN_WIDTH), F32)
    return pl.pallas_call(
        _in_proj_kernel,
        grid=(rows // tm,),
        in_specs=[row_spec(D_MODEL), _resident((1, D_MODEL)), _resident((D_MODEL, IN_PROJ_WIDTH)),
                  tab_spec, tab_spec, tab_spec],
        out_specs=[row_spec(ATTN_WIDTH)] * 6,
        out_shape=[out_bf16, out_bf16, out_bf16, out_bf16, out_f32, out_bf16],
        compiler_params=pltpu.CompilerParams(dimension_semantics=("arbitrary",),
                                             vmem_limit_bytes=VMEM_LIMIT_BYTES),
        name="in_proj",
    )(x2, gain, w_in, cos, sin_hi, sin_lo)


def _attn_kernel(q_ref, k_ref, v_ref, z_ref, o_ref, kx_ref, vx_ref, qx_ref, sa_ref, sb_ref,
                 m_ref, acc_ref, *, n_blocks):
    step = pl.program_id(2)
    blk = MOBA_BLOCK
    heads = LANES // HEAD_DIM

    head_rows = lambda h: slice(h * HEAD_DIM, (h + 1) * HEAD_DIM)
    spare_base = lambda h: ((h + 1) % heads) * HEAD_DIM
    iota = lambda shape, d: lax.broadcasted_iota(jnp.int32, shape, d)

    def prepare_operands():
        lane = iota((blk, LANES), 1)
        sum_rows = (iota((VALUE_ROWS - HEAD_DIM, blk), 0) == 0).astype(F32)
        kmean = []
        for j in range(n_blocks):
            rows = slice(j * blk, (j + 1) * blk)
            kj = k_ref[0, rows, :]
            kmean.append(jnp.mean(kj.astype(F32), axis=0, keepdims=True))
            vt = v_ref[0, rows, :].astype(F32).T
            for h in range(heads):
                in_head = (lane >= h * HEAD_DIM) & (lane < (h + 1) * HEAD_DIM)
                tag = jnp.where(lane == spare_base(h) + j, 1.0, 0.0).astype(BF16)
                kx_ref[h, j] = jnp.where(in_head, kj, tag)
                vx_ref[j, h] = jnp.concatenate([vt[head_rows(h)], sum_rows], axis=0).astype(BF16)
        kmean = jnp.concatenate(kmean, axis=0).astype(BF16)

        blk_row = iota((n_blocks, blk), 0)
        zero_rows = lambda n: jnp.zeros((n, blk), F32)
        in_head_order = lambda h, own, other: [own, other] if h == 0 else [other, own]
        for qi in range(n_blocks):
            qt = q_ref[0, qi * blk:(qi + 1) * blk, :].astype(F32).T
            past = blk_row < qi
            for h in range(heads):
                q_rows = qt[head_rows(h)]
                q_only = jnp.concatenate(in_head_order(h, q_rows, zero_rows(HEAD_DIM)), axis=0)
                gate = jnp.dot(kmean, q_only.astype(BF16), preferred_element_type=F32)
                gate = jnp.where(past, gate, NEG_INF)
                beaten_by = jnp.zeros((n_blocks, blk), jnp.int32)
                for j in range(n_blocks):
                    gj = gate[j:j + 1, :]
                    wins = (gj > gate) | ((gj == gate) & (j < blk_row))
                    beaten_by = beaten_by + wins.astype(jnp.int32)
                keep = (past & (beaten_by < MOBA_TOP_K)) | (blk_row == qi)
                bias = jnp.where(keep, 0.0, NEG_INF)
                spare = jnp.concatenate([bias, zero_rows(HEAD_DIM - n_blocks)], axis=0)
                qx_ref[qi, h] = jnp.concatenate(in_head_order(h, q_rows, spare),
                                                axis=0).astype(BF16)

    key = iota((blk, blk), 0)
    query = iota((blk, blk), 1)
    col_max = lambda s: jnp.max(s, axis=0, keepdims=True)

    def past_item(pair, it):
        first = it < pair
        return first, jnp.where(first, 0, 1), jnp.where(first, it, it - pair)

    def stage(pair, s_ref):
        tiles = (pair, n_blocks - 1 - pair)
        m = {}
        for t, qi in enumerate(tiles):
            for h in range(heads):
                s = jnp.dot(kx_ref[h, qi], qx_ref[qi, h], preferred_element_type=F32)
                s = jnp.where(key <= query, s, NEG_INF)
                s_ref[t, h, qi] = s
                m[t, h] = col_max(s)
        for it in range(n_blocks - 1):
            first, t, j = past_item(pair, it)
            for h in range(heads):
                s = jnp.dot(kx_ref[h, j], qx_ref[jnp.where(first, tiles[0], tiles[1]), h],
                            preferred_element_type=F32)
                s_ref[t, h, j] = s
                cm = col_max(s)
                m[0, h] = jnp.where(first, jnp.maximum(m[0, h], cm), m[0, h])
                m[1, h] = jnp.where(first, m[1, h], jnp.maximum(m[1, h], cm))
        return m

    def weighted_values(j, h, s, m_h):
        p = jnp.exp2(s - m_h).astype(BF16)
        return jnp.dot(vx_ref[j, h], p, preferred_element_type=F32)

    def finish(pair, s_ref, m, acc_ref):
        tiles = (pair, n_blocks - 1 - pair)
        for t, qi in enumerate(tiles):
            for h in range(heads):
                acc_ref[t, h] = weighted_values(qi, h, s_ref[t, h, qi], m[t, h])
        for it in range(n_blocks - 1):
            first, t, j = past_item(pair, it)
            for h in range(heads):
                acc_ref[t, h] += weighted_values(j, h, s_ref[t, h, j],
                                                 jnp.where(first, m[0, h], m[1, h]))
        for t, qi in enumerate(tiles):
            rows = pl.ds(pl.multiple_of(qi * blk, blk), blk)
            acc = [acc_ref[t, h] for h in range(heads)]
            o_t = jnp.concatenate([a[:HEAD_DIM] / a[HEAD_DIM:HEAD_DIM + 1] for a in acc], axis=0)
            z = z_ref[0, rows, :].astype(F32)
            o_ref[0, rows, :] = (o_t.T * (z / (1.0 + jnp.exp(-z)))).astype(BF16)

    m_keys = [(t, h) for t in range(2) for h in range(heads)]

    @pl.when(step == 0)
    def _():
        prepare_operands()
        m0 = stage(0, sa_ref)
        for t, h in m_keys:
            m_ref[t, h] = m0[t, h]

    pair_a = 2 * step
    m_a = {key_: m_ref[key_] for key_ in m_keys}
    m_b = stage(pair_a + 1, sb_ref)
    finish(pair_a, sa_ref, m_a, acc_ref.at[0])
    m_c = stage(lax.rem(pair_a + 2, n_blocks // 2), sa_ref)
    finish(pair_a + 1, sb_ref, m_b, acc_ref.at[1])
    for t, h in m_keys:
        m_ref[t, h] = m_c[t, h]


def _moba_attention(q, k, v, z_attn):
    bsz, seq_len, _ = q.shape
    n_blocks = seq_len // MOBA_BLOCK
    assert n_blocks % 4 == 0 and n_blocks <= HEAD_DIM
    head_pairs = ATTN_WIDTH // LANES
    heads = LANES // HEAD_DIM
    assert heads == 2, "the block bias rides in the other head's half of the 128 lanes"
    seq_spec = pl.BlockSpec((1, seq_len, LANES), lambda b, h, i: (b, 0, h))
    scores = pltpu.VMEM((2, heads, n_blocks, MOBA_BLOCK, MOBA_BLOCK), F32)
    return pl.pallas_call(
        functools.partial(_attn_kernel, n_blocks=n_blocks),
        grid=(bsz, head_pairs, n_blocks // 4),
        in_specs=[seq_spec] * 4,
        out_specs=seq_spec,
        out_shape=jax.ShapeDtypeStruct((bsz, seq_len, ATTN_WIDTH), BF16),
        scratch_shapes=[pltpu.VMEM((heads, n_blocks, MOBA_BLOCK, LANES), BF16),
                        pltpu.VMEM((n_blocks, heads, VALUE_ROWS, MOBA_BLOCK), BF16),
                        pltpu.VMEM((n_blocks, heads, LANES, MOBA_BLOCK), BF16),
                        scores, scores,
                        pltpu.VMEM((2, heads, 1, MOBA_BLOCK), F32),
                        pltpu.VMEM((2, 2, heads, VALUE_ROWS, MOBA_BLOCK), F32)],
        compiler_params=pltpu.CompilerParams(
            dimension_semantics=("arbitrary", "arbitrary", "arbitrary"),
            vmem_limit_bytes=VMEM_LIMIT_BYTES),
        name="moba_attn",
    )(q, k, v, z_attn)


def _ssm_tables(lam_re, lam_im, b_re, b_im, c_re, c_im, log_dt):
    T, G, P, H = SSM_CHUNK, SSM_GROUPS, SSM_STATE, SSM_GROUP_DIM
    dt = jnp.exp(log_dt.astype(F32))[:, None]
    lam_r, lam_i = lam_re.astype(F32), lam_im.astype(F32)
    mag = jnp.exp(lam_r * dt)
    bar_r, bar_i = mag * jnp.cos(lam_i * dt), mag * jnp.sin(lam_i * dt)
    den = lam_r * lam_r + lam_i * lam_i
    f_r = ((bar_r - 1.0) * lam_r + bar_i * lam_i) / den
    f_i = (bar_i * lam_r - (bar_r - 1.0) * lam_i) / den
    bb_r = f_r[..., None] * b_re - f_i[..., None] * b_im
    bb_i = f_r[..., None] * b_im + f_i[..., None] * b_re

    def powers(base_r, base_i, count):
        out_r, out_i = [base_r], [base_i]
        for _ in range(count - 1):
            r, i = out_r[-1], out_i[-1]
            out_r.append(r * base_r - i * base_i)
            out_i.append(r * base_i + i * base_r)
        return out_r, out_i

    pw_r, pw_i = powers(bar_r, bar_i, T)
    pw_r = jnp.stack([jnp.ones_like(bar_r)] + pw_r)
    pw_i = jnp.stack([jnp.zeros_like(bar_i)] + pw_i)

    lb_r = pw_r[:T, :, :, None] * bb_r - pw_i[:T, :, :, None] * bb_i
    lb_i = pw_r[:T, :, :, None] * bb_i + pw_i[:T, :, :, None] * bb_r
    cl_r = c_re[None] * pw_r[:, :, None, :] - c_im[None] * pw_i[:, :, None, :]
    cl_i = c_re[None] * pw_i[:, :, None, :] + c_im[None] * pw_r[:, :, None, :]
    lb_t = jnp.stack([lb_r, lb_i], axis=1).transpose(0, 1, 4, 2, 3).reshape(T, 2, H, G * P)
    cl_t = jnp.stack([cl_r, -cl_i], axis=1).transpose(0, 1, 4, 2, 3).reshape(T + 1, 2, P, G * H)

    a_r, a_i = powers(pw_r[T], pw_i[T], F32_SUBLANES)
    tiled = lambda a: jnp.stack(a).reshape(len(a), SSM_TILES, STATE_COLS)
    a_lin = jnp.stack([tiled(a_r), tiled(a_i)]).transpose(2, 0, 1, 3)
    log_steps = [(1 << k) - 1 for k in range(F32_SUBLANES.bit_length() - 1)]
    a_log = jnp.stack([tiled([a_r[s] for s in log_steps]),
                       tiled([a_i[s] for s in log_steps])]).transpose(2, 1, 0, 3)
    return lb_t, cl_t, a_log, a_lin


def _ssm_kernel(u_ref, lb_ref, cl_ref, alog_ref, alin_ref, d_ref, y_ref, m_ref, ws_ref, wo_ref, *,
                seq_len, n_seq):
    T, H, P = SSM_CHUNK, SSM_GROUP_DIM, SSM_STATE
    n_chunks = seq_len // T
    iota = lambda shape, d: lax.broadcasted_iota(jnp.int32, shape, d)
    h_bits, p_bits = H.bit_length() - 1, P.bit_length() - 1

    @pl.when(pl.program_id(1) == 0)
    def _():
        spread_h = ((iota((LANES, H), 0) & (H - 1)) == iota((LANES, H), 1)).astype(BF16)
        spread_p = ((iota((STATE_COLS, P), 0) & (P - 1)) == iota((STATE_COLS, P), 1)).astype(BF16)
        same_hp = (iota((LANES, STATE_COLS), 0) >> h_bits) == (iota((LANES, STATE_COLS), 1) >> p_bits)
        same_ph = (iota((STATE_COLS, LANES), 0) >> p_bits) == (iota((STATE_COLS, LANES), 1) >> h_bits)

        def expand(spread, coeff, same):
            full = jnp.dot(spread, coeff.astype(BF16), preferred_element_type=F32)
            return jnp.where(same, full, 0.0).astype(BF16)

        rows = lambda j: slice(j * LANES, (j + 1) * LANES)
        state_out = lambda d: jnp.concatenate(
            [expand(spread_p, cl_ref[d, ri], same_ph) for ri in range(2)], axis=0)
        for j in range(T):
            ws_ref[rows(j), :] = jnp.concatenate(
                [expand(spread_h, lb_ref[T - 1 - j, ri], same_hp) for ri in range(2)], axis=1)
        for i in range(T):
            wo_ref[:, rows(i)] = state_out(i + 1)
        c_out = state_out(0)
        for d in range(T):
            block = jnp.dot(ws_ref[rows(T - 1 - d), :], c_out,
                            preferred_element_type=F32).astype(BF16)
            for j in range(T - d):
                m_ref[rows(j), rows(j + d)] = block
        for j in range(T):
            for i in range(j):
                m_ref[rows(j), rows(i)] = jnp.zeros((LANES, LANES), BF16)

    pieces = [u_ref[pl.ds(b * seq_len + i, n_chunks, stride=T), :]
              for b in range(n_seq) for i in range(T)]
    x = jnp.concatenate(
        [jnp.concatenate(pieces[b * T:(b + 1) * T], axis=1) for b in range(n_seq)], axis=0)
    x_lo = x.astype(BF16)
    y = jnp.dot(x_lo, m_ref[...], preferred_element_type=F32)
    s = jnp.dot(x_lo, ws_ref[...], preferred_element_type=F32)

    sub = F32_SUBLANES
    chunk = jnp.concatenate([iota((n_chunks, STATE_COLS), 0)] * n_seq, axis=0)
    in_tile = chunk & (sub - 1)
    cmul = lambda a_re, a_im, b_re, b_im: (a_re * b_re - a_im * b_im, a_re * b_im + a_im * b_re)

    re, im = s[:, :STATE_COLS], s[:, STATE_COLS:]
    for k in range(sub.bit_length() - 1):
        sh = 1 << k
        sre = jnp.where(in_tile >= sh, pltpu.roll(re, sh, axis=0), 0.0)
        sim = jnp.where(in_tile >= sh, pltpu.roll(im, sh, axis=0), 0.0)
        dre, dim_ = cmul(alog_ref[0, k, 0:1, :], alog_ref[0, k, 1:2, :], sre, sim)
        re, im = re + dre, im + dim_
    lin_re, lin_im = alin_ref[0, 0], alin_ref[0, 1]
    tiles_per_seq = n_chunks // sub
    done_re, done_im = [], []
    for b in range(n_seq):
        carry = None
        for g in range(tiles_per_seq):
            rows = slice((b * tiles_per_seq + g) * sub, (b * tiles_per_seq + g + 1) * sub)
            t_re, t_im = re[rows], im[rows]
            if carry is not None:
                d_re, d_im = cmul(lin_re, lin_im, *carry)
                t_re, t_im = t_re + d_re, t_im + d_im
            carry = (t_re[sub - 1:sub], t_im[sub - 1:sub])
            done_re.append(t_re)
            done_im.append(t_im)
    re, im = jnp.concatenate(done_re, axis=0), jnp.concatenate(done_im, axis=0)

    prev = jnp.concatenate([jnp.where(chunk >= 1, pltpu.roll(a, 1, axis=0), 0.0)
                            for a in (re, im)], axis=1).astype(BF16)
    y = y + jnp.dot(prev, wo_ref[...], preferred_element_type=F32)
    y = y + x * jnp.concatenate([d_ref[...]] * T, axis=1)
    for b in range(n_seq):
        for i in range(T):
            y_ref[pl.ds(b * seq_len + i, n_chunks, stride=T), :] = (
                y[b * n_chunks:(b + 1) * n_chunks, i * LANES:(i + 1) * LANES])


def _s5_ssm(u, tables, d_skip, seq_len):
    lb_t, cl_t, a_log, a_lin = tables
    T, H, P = SSM_CHUNK, SSM_GROUP_DIM, SSM_STATE
    assert seq_len % (T * F32_SUBLANES) == 0
    bsz = u.shape[0] // seq_len
    n_seq = max(n for n in (4, 2, 1) if bsz % n == 0)
    io_spec = pl.BlockSpec((n_seq * seq_len, LANES), lambda q, b: (b, q))
    square = pltpu.VMEM((T * LANES, T * LANES), BF16)
    return pl.pallas_call(
        functools.partial(_ssm_kernel, seq_len=seq_len, n_seq=n_seq),
        grid=(SSM_TILES, bsz // n_seq),
        in_specs=[io_spec,
                  pl.BlockSpec((T, 2, H, STATE_COLS), lambda q, b: (0, 0, 0, q)),
                  pl.BlockSpec((T + 1, 2, P, LANES), lambda q, b: (0, 0, 0, q)),
                  pl.BlockSpec((1,) + a_log.shape[1:], lambda q, b: (q, 0, 0, 0)),
                  pl.BlockSpec((1,) + a_lin.shape[1:], lambda q, b: (q, 0, 0, 0)),
                  pl.BlockSpec((1, LANES), lambda q, b: (0, q))],
        out_specs=io_spec,
        out_shape=jax.ShapeDtypeStruct(u.shape, F32),
        scratch_shapes=[square, square, square],
        compiler_params=pltpu.CompilerParams(dimension_semantics=("arbitrary", "arbitrary"),
                                             vmem_limit_bytes=VMEM_LIMIT_BYTES),
        name="s5_ssm",
    )(u, lb_t, cl_t, a_log, a_lin, d_skip)


def _out_proj_kernel(x_ref, ma_ref, y_ref, zs_ref, wg_ref, bg_ref, w_ref, g_ref, o_ref, *,
                     final_norm):
    y = y_ref[...]
    y = 0.5 * y * (1.0 + jnp.tanh(math.sqrt(2.0 / math.pi) * (y + 0.044715 * (y * y * y))))
    gate = jnp.dot(y.astype(BF16), wg_ref[...], preferred_element_type=F32) + bg_ref[...]
    z = zs_ref[...].astype(F32)
    mixed_ssm = y / (1.0 + jnp.exp(-gate)) * (z / (1.0 + jnp.exp(-z)))
    r = (x_ref[...]
         + jnp.dot(ma_ref[...], w_ref[:ATTN_WIDTH, :], preferred_element_type=F32)
         + jnp.dot(mixed_ssm.astype(BF16), w_ref[ATTN_WIDTH:, :], preferred_element_type=F32))
    if final_norm:
        ms = jnp.mean(r * r, axis=-1, keepdims=True)
        r = r * lax.rsqrt(ms + NORM_EPS) * g_ref[...]
    o_ref[...] = r


def _out_proj(x2, mixed_attn, y_ssm, z_ssm, w_glu, b_glu, w_out, gain, tm, final_norm):
    rows = x2.shape[0]
    row_spec = lambda width: pl.BlockSpec((tm, width), lambda i: (i, 0))
    return pl.pallas_call(
        functools.partial(_out_proj_kernel, final_norm=final_norm),
        grid=(rows // tm,),
        in_specs=[row_spec(D_MODEL), row_spec(ATTN_WIDTH), row_spec(SSM_WIDTH), row_spec(SSM_WIDTH),
                  _resident((SSM_WIDTH, SSM_WIDTH)), _resident((1, SSM_WIDTH)),
                  _resident((ATTN_WIDTH + SSM_WIDTH, D_MODEL)), _resident((1, D_MODEL))],
        out_specs=row_spec(D_MODEL),
        out_shape=jax.ShapeDtypeStruct((rows, D_MODEL), F32),
        compiler_params=pltpu.CompilerParams(dimension_semantics=("arbitrary",),
                                             vmem_limit_bytes=VMEM_LIMIT_BYTES),
        name="out_proj",
    )(x2, mixed_attn, y_ssm, z_ssm, w_glu, b_glu, w_out, gain)


def _rotary_tables(seq_len):
    half = HEAD_DIM // 2
    inv_freq = 1.0 / (ROPE_THETA ** (jnp.arange(half, dtype=F32) / half))
    ang = jnp.arange(seq_len, dtype=F32)[:, None] * inv_freq[None, :]
    cos, sin, zero = jnp.cos(ang), jnp.sin(ang), jnp.zeros_like(ang)
    reps = LANES // HEAD_DIM
    cos_t = jnp.tile(jnp.concatenate([cos, cos], axis=1), (1, reps))
    sin_hi = jnp.tile(jnp.concatenate([zero, sin], axis=1), (1, reps))
    sin_lo = jnp.tile(jnp.concatenate([-sin, zero], axis=1), (1, reps))
    return cos_t, sin_hi, sin_lo


def kernel(x, norm_gain, w_in, w_out, lam_re, lam_im, b_re, b_im, c_re, c_im,
           d_skip, log_dt, w_glu, b_glu, final_gain):
    bsz, seq_len, _ = x.shape
    depth = norm_gain.shape[0]
    assert seq_len % MOBA_BLOCK == 0 and seq_len // MOBA_BLOCK > MOBA_TOP_K
    assert seq_len % SSM_CHUNK == 0
    tm = min(512, seq_len)
    cos, sin_hi, sin_lo = _rotary_tables(seq_len)

    x2 = x.reshape(bsz * seq_len, D_MODEL)
    for layer in range(depth):
        tables = _ssm_tables(lam_re[layer], lam_im[layer], b_re[layer], b_im[layer],
                             c_re[layer], c_im[layer], log_dt[layer])
        q, k, v, z_attn, u, z_ssm = _in_proj(
            x2, norm_gain[layer].reshape(1, D_MODEL), w_in[layer].astype(BF16),
            cos, sin_hi, sin_lo, seq_len, tm)
        to_seq = lambda t: t.reshape(bsz, seq_len, ATTN_WIDTH)
        mixed_attn = _moba_attention(to_seq(q), to_seq(k), to_seq(v), to_seq(z_attn))
        y_ssm = _s5_ssm(u, tables, d_skip[layer].reshape(1, SSM_WIDTH), seq_len)
        x2 = _out_proj(x2, mixed_attn.reshape(bsz * seq_len, ATTN_WIDTH), y_ssm, z_ssm,
                       w_glu[layer].astype(BF16), b_glu[layer].reshape(1, SSM_WIDTH),
                       w_out[layer].astype(BF16), final_gain.reshape(1, D_MODEL), tm,
                       final_norm=layer == depth - 1)
    return x2.reshape(bsz, seq_len, D_MODEL)
```

```python
import functools
import math

import jax
import jax.numpy as jnp
from jax import lax
from jax.experimental import pallas as pl
from jax.experimental.pallas import tpu as pltpu

F32 = jnp.float32
BF16 = jnp.bfloat16

D_MODEL = 1024
HEAD_DIM = 64
ATTN_HEADS = 8
ATTN_WIDTH = ATTN_HEADS * HEAD_DIM
MOBA_BLOCK = 256
MOBA_TOP_K = 3
ROPE_THETA = 10000.0
SSM_GROUP_DIM = 16
SSM_GROUPS = 32
SSM_WIDTH = SSM_GROUPS * SSM_GROUP_DIM
SSM_STATE = 64
IN_PROJ_WIDTH = 4 * ATTN_WIDTH + 2 * SSM_WIDTH
NORM_EPS = 1e-6
NEG_INF = -1e30

LANES = 128
SSM_CHUNK = 8
GROUPS_PER_TILE = LANES // SSM_GROUP_DIM
SSM_TILES = SSM_WIDTH // LANES
TILE_COLS = SSM_CHUNK * LANES
STATE_COLS = GROUPS_PER_TILE * SSM_STATE
VMEM_LIMIT_BYTES = 56 * 1024 * 1024
F32_SUBLANES = 8
BF16_SUBLANES = 16
VALUE_ROWS = HEAD_DIM + BF16_SUBLANES
QK_SCALE = math.log2(math.e) / math.sqrt(HEAD_DIM)


def _resident(shape):
    zeros = (0,) * len(shape)
    return pl.BlockSpec(shape, lambda *_: zeros, pipeline_mode=pl.Buffered(1))


def _in_proj_kernel(x_ref, g_ref, w_ref, cos_ref, sin_hi_ref, sin_lo_ref,
                    q_ref, k_ref, v_ref, za_ref, u_ref, zs_ref):
    x = x_ref[...]
    ms = jnp.mean(x * x, axis=-1, keepdims=True)
    h = (x * lax.rsqrt(ms + NORM_EPS) * g_ref[...]).astype(BF16)

    def section(idx):
        return jnp.dot(h, w_ref[:, idx * ATTN_WIDTH:(idx + 1) * ATTN_WIDTH],
                       preferred_element_type=F32)

    cos = cos_ref[...]
    sin_hi = sin_hi_ref[...]
    sin_lo = sin_lo_ref[...]

    def rotary(t):
        return (t * cos + pltpu.roll(t, HEAD_DIM // 2, axis=1) * sin_hi
                + pltpu.roll(t, LANES - HEAD_DIM // 2, axis=1) * sin_lo)

    q = section(0)
    k = section(1)
    for c in range(ATTN_WIDTH // LANES):
        sl = slice(c * LANES, (c + 1) * LANES)
        q_ref[:, sl] = (rotary(q[:, sl]) * QK_SCALE).astype(BF16)
        k_ref[:, sl] = rotary(k[:, sl]).astype(BF16)
    v_ref[...] = section(2).astype(BF16)
    za_ref[...] = section(3).astype(BF16)
    u_ref[...] = section(4)
    zs_ref[...] = section(5).astype(BF16)


def _in_proj(x2, gain, w_in, cos, sin_hi, sin_lo, seq_len, tm):
    rows = x2.shape[0]
    pos_blocks = seq_len // tm
    row_spec = lambda width: pl.BlockSpec((tm, width), lambda i: (i, 0))
    tab_spec = pl.BlockSpec((tm, LANES), lambda i: (i % pos_blocks, 0))
    out_bf16 = jax.ShapeDtypeStruct((rows, ATTN_WIDTH), BF16)
    out_f32 = jax.ShapeDtypeStruct((rows, ATTN_WIDTH), F32)
    return pl.pallas_call(
        _in_proj_kernel,
        grid=(rows // tm,),
        in_specs=[row_spec(D_MODEL), _resident((1, D_MODEL)), _resident((D_MODEL, IN_PROJ_WIDTH)),
                  tab_spec, tab_spec, tab_spec],
        out_specs=[row_spec(ATTN_WIDTH)] * 6,
        out_shape=[out_bf16, out_bf16, out_bf16, out_bf16, out_f32, out_bf16],
        compiler_params=pltpu.CompilerParams(dimension_semantics=("arbitrary",),
                                             vmem_limit_bytes=VMEM_LIMIT_BYTES),
        name="in_proj",
    )(x2, gain, w_in, cos, sin_hi, sin_lo)


def _attn_kernel(q_ref, k_ref, v_ref, z_ref, o_ref, kx_ref, vx_ref, qx_ref, sa_ref, sb_ref,
                 m_ref, acc_ref, *, n_blocks):
    step = pl.program_id(2)
    blk = MOBA_BLOCK
    heads = LANES // HEAD_DIM

    head_rows = lambda h: slice(h * HEAD_DIM, (h + 1) * HEAD_DIM)
    spare_base = lambda h: ((h + 1) % heads) * HEAD_DIM
    iota = lambda shape, d: lax.broadcasted_iota(jnp.int32, shape, d)

    def prepare_operands():
        lane = iota((blk, LANES), 1)
        sum_rows = (iota((VALUE_ROWS - HEAD_DIM, blk), 0) == 0).astype(F32)
        kmean = []
        for j in range(n_blocks):
            rows = slice(j * blk, (j + 1) * blk)
            kj = k_ref[0, rows, :]
            kmean.append(jnp.mean(kj.astype(F32), axis=0, keepdims=True))
            vt = v_ref[0, rows, :].astype(F32).T
            for h in range(heads):
                in_head = (lane >= h * HEAD_DIM) & (lane < (h + 1) * HEAD_DIM)
                tag = jnp.where(lane == spare_base(h) + j, 1.0, 0.0).astype(BF16)
                kx_ref[h, j] = jnp.where(in_head, kj, tag)
                vx_ref[j, h] = jnp.concatenate([vt[head_rows(h)], sum_rows], axis=0).astype(BF16)
        kmean = jnp.concatenate(kmean, axis=0).astype(BF16)

        blk_row = iota((n_blocks, blk), 0)
        zero_rows = lambda n: jnp.zeros((n, blk), F32)
        in_head_order = lambda h, own, other: [own, other] if h == 0 else [other, own]
        for qi in range(n_blocks):
            qt = q_ref[0, qi * blk:(qi + 1) * blk, :].astype(F32).T
            past = blk_row < qi
            for h in range(heads):
                q_rows = qt[head_rows(h)]
                q_only = jnp.concatenate(in_head_order(h, q_rows, zero_rows(HEAD_DIM)), axis=0)
                gate = jnp.dot(kmean, q_only.astype(BF16), preferred_element_type=F32)
                gate = jnp.where(past, gate, NEG_INF)
                beaten_by = jnp.zeros((n_blocks, blk), jnp.int32)
                for j in range(n_blocks):
                    gj = gate[j:j + 1, :]
                    wins = (gj > gate) | ((gj == gate) & (j < blk_row))
                    beaten_by = beaten_by + wins.astype(jnp.int32)
                keep = (past & (beaten_by < MOBA_TOP_K)) | (blk_row == qi)
                bias = jnp.where(keep, 0.0, NEG_INF)
                spare = jnp.concatenate([bias, zero_rows(HEAD_DIM - n_blocks)], axis=0)
                qx_ref[qi, h] = jnp.concatenate(in_head_order(h, q_rows, spare),
                                                axis=0).astype(BF16)

    key = iota((blk, blk), 0)
    query = iota((blk, blk), 1)
    col_max = lambda s: jnp.max(s, axis=0, keepdims=True)

    def past_item(pair, it):
        first = it < pair
        return first, jnp.where(first, 0, 1), jnp.where(first, it, it - pair)

    def stage_items(pair, s_ref, m):
        tiles = (pair, n_blocks - 1 - pair)

        def own(t):
            qi = tiles[t]
            for h in range(heads):
                s = jnp.dot(kx_ref[h, qi], qx_ref[qi, h], preferred_element_type=F32)
                s = jnp.where(key <= query, s, NEG_INF)
                s_ref[t, h, qi] = s
                m[t, h] = col_max(s)

        def past(it):
            first, t, j = past_item(pair, it)
            for h in range(heads):
                s = jnp.dot(kx_ref[h, j], qx_ref[jnp.where(first, tiles[0], tiles[1]), h],
                            preferred_element_type=F32)
                s_ref[t, h, j] = s
                cm = col_max(s)
                m[0, h] = jnp.where(first, jnp.maximum(m[0, h], cm), m[0, h])
                m[1, h] = jnp.where(first, m[1, h], jnp.maximum(m[1, h], cm))

        return ([functools.partial(own, t) for t in range(2)]
                + [functools.partial(past, it) for it in range(n_blocks - 1)])

    def weighted_values(j, h, s, m_h):
        p = jnp.exp2(s - m_h).astype(BF16)
        return jnp.dot(vx_ref[j, h], p, preferred_element_type=F32)

    def finish_items(pair, s_ref, m, acc_ref):
        tiles = (pair, n_blocks - 1 - pair)

        def own(t):
            qi = tiles[t]
            for h in range(heads):
                acc_ref[t, h] = weighted_values(qi, h, s_ref[t, h, qi], m[t, h])

        def past(it):
            first, t, j = past_item(pair, it)
            for h in range(heads):
                acc_ref[t, h] += weighted_values(j, h, s_ref[t, h, j],
                                                 jnp.where(first, m[0, h], m[1, h]))

        def write_rows():
            for t, qi in enumerate(tiles):
                rows = pl.ds(pl.multiple_of(qi * blk, blk), blk)
                acc = [acc_ref[t, h] for h in range(heads)]
                o_t = jnp.concatenate([a[:HEAD_DIM] / a[HEAD_DIM:HEAD_DIM + 1] for a in acc],
                                      axis=0)
                z = z_ref[0, rows, :].astype(F32)
                o_ref[0, rows, :] = (o_t.T * (z / (1.0 + jnp.exp(-z)))).astype(BF16)

        return ([functools.partial(own, t) for t in range(2)]
                + [functools.partial(past, it) for it in range(n_blocks - 1)] + [write_rows])

    def alternate(*item_lists):
        for group in zip(*item_lists):
            for item in group:
                item()
        for items in item_lists:
            for item in items[min(map(len, item_lists)):]:
                item()

    m_keys = [(t, h) for t in range(2) for h in range(heads)]

    @pl.when(step == 0)
    def _():
        prepare_operands()
        m0 = {}
        alternate(stage_items(0, sa_ref, m0))
        for t, h in m_keys:
            m_ref[t, h] = m0[t, h]

    pair_a = 2 * step
    m_a = {key_: m_ref[key_] for key_ in m_keys}
    m_b, m_c = {}, {}
    alternate(finish_items(pair_a, sa_ref, m_a, acc_ref.at[0]),
              stage_items(pair_a + 1, sb_ref, m_b))
    alternate(finish_items(pair_a + 1, sb_ref, m_b, acc_ref.at[1]),
              stage_items(lax.rem(pair_a + 2, n_blocks // 2), sa_ref, m_c))
    for t, h in m_keys:
        m_ref[t, h] = m_c[t, h]


def _moba_attention(q, k, v, z_attn):
    bsz, seq_len, _ = q.shape
    n_blocks = seq_len // MOBA_BLOCK
    assert n_blocks % 4 == 0 and n_blocks <= HEAD_DIM
    head_pairs = ATTN_WIDTH // LANES
    heads = LANES // HEAD_DIM
    assert heads == 2, "the block bias rides in the other head's half of the 128 lanes"
    seq_spec = pl.BlockSpec((1, seq_len, LANES), lambda b, h, i: (b, 0, h))
    scores = pltpu.VMEM((2, heads, n_blocks, MOBA_BLOCK, MOBA_BLOCK), F32)
    return pl.pallas_call(
        functools.partial(_attn_kernel, n_blocks=n_blocks),
        grid=(bsz, head_pairs, n_blocks // 4),
        in_specs=[seq_spec] * 4,
        out_specs=seq_spec,
        out_shape=jax.ShapeDtypeStruct((bsz, seq_len, ATTN_WIDTH), BF16),
        scratch_shapes=[pltpu.VMEM((heads, n_blocks, MOBA_BLOCK, LANES), BF16),
                        pltpu.VMEM((n_blocks, heads, VALUE_ROWS, MOBA_BLOCK), BF16),
                        pltpu.VMEM((n_blocks, heads, LANES, MOBA_BLOCK), BF16),
                        scores, scores,
                        pltpu.VMEM((2, heads, 1, MOBA_BLOCK), F32),
                        pltpu.VMEM((2, 2, heads, VALUE_ROWS, MOBA_BLOCK), F32)],
        compiler_params=pltpu.CompilerParams(
            dimension_semantics=("arbitrary", "arbitrary", "arbitrary"),
            vmem_limit_bytes=VMEM_LIMIT_BYTES),
        name="moba_attn",
    )(q, k, v, z_attn)


def _ssm_tables(lam_re, lam_im, b_re, b_im, c_re, c_im, log_dt):
    T, G, P, H = SSM_CHUNK, SSM_GROUPS, SSM_STATE, SSM_GROUP_DIM
    dt = jnp.exp(log_dt.astype(F32))[:, None]
    lam_r, lam_i = lam_re.astype(F32), lam_im.astype(F32)
    mag = jnp.exp(lam_r * dt)
    bar_r, bar_i = mag * jnp.cos(lam_i * dt), mag * jnp.sin(lam_i * dt)
    den = lam_r * lam_r + lam_i * lam_i
    f_r = ((bar_r - 1.0) * lam_r + bar_i * lam_i) / den
    f_i = (bar_i * lam_r - (bar_r - 1.0) * lam_i) / den
    bb_r = f_r[..., None] * b_re - f_i[..., None] * b_im
    bb_i = f_r[..., None] * b_im + f_i[..., None] * b_re

    def powers(base_r, base_i, count):
        out_r, out_i = [base_r], [base_i]
        for _ in range(count - 1):
            r, i = out_r[-1], out_i[-1]
            out_r.append(r * base_r - i * base_i)
            out_i.append(r * base_i + i * base_r)
        return out_r, out_i

    pw_r, pw_i = powers(bar_r, bar_i, T)
    pw_r = jnp.stack([jnp.ones_like(bar_r)] + pw_r)
    pw_i = jnp.stack([jnp.zeros_like(bar_i)] + pw_i)

    lb_r = pw_r[:T, :, :, None] * bb_r - pw_i[:T, :, :, None] * bb_i
    lb_i = pw_r[:T, :, :, None] * bb_i + pw_i[:T, :, :, None] * bb_r
    cl_r = c_re[None] * pw_r[:, :, None, :] - c_im[None] * pw_i[:, :, None, :]
    cl_i = c_re[None] * pw_i[:, :, None, :] + c_im[None] * pw_r[:, :, None, :]
    lb_t = jnp.stack([lb_r, lb_i], axis=1).transpose(0, 1, 4, 2, 3).reshape(T, 2, H, G * P)
    cl_t = jnp.stack([cl_r, -cl_i], axis=1).transpose(0, 1, 4, 2, 3).reshape(T + 1, 2, P, G * H)

    a_r, a_i = powers(pw_r[T], pw_i[T], F32_SUBLANES)
    tiled = lambda a: jnp.stack(a).reshape(len(a), SSM_TILES, STATE_COLS)
    a_lin = jnp.stack([tiled(a_r), tiled(a_i)]).transpose(2, 0, 1, 3)
    log_steps = [(1 << k) - 1 for k in range(F32_SUBLANES.bit_length() - 1)]
    a_log = jnp.stack([tiled([a_r[s] for s in log_steps]),
                       tiled([a_i[s] for s in log_steps])]).transpose(2, 1, 0, 3)
    return lb_t, cl_t, a_log, a_lin


def _ssm_kernel(u_ref, lb_ref, cl_ref, alog_ref, alin_ref, d_ref, y_ref, m_ref, ws_ref, wo_ref, *,
                seq_len, n_seq):
    T, H, P = SSM_CHUNK, SSM_GROUP_DIM, SSM_STATE
    n_chunks = seq_len // T
    iota = lambda shape, d: lax.broadcasted_iota(jnp.int32, shape, d)
    h_bits, p_bits = H.bit_length() - 1, P.bit_length() - 1

    @pl.when(pl.program_id(1) == 0)
    def _():
        spread_h = ((iota((LANES, H), 0) & (H - 1)) == iota((LANES, H), 1)).astype(BF16)
        spread_p = ((iota((STATE_COLS, P), 0) & (P - 1)) == iota((STATE_COLS, P), 1)).astype(BF16)
        same_hp = (iota((LANES, STATE_COLS), 0) >> h_bits) == (iota((LANES, STATE_COLS), 1) >> p_bits)
        same_ph = (iota((STATE_COLS, LANES), 0) >> p_bits) == (iota((STATE_COLS, LANES), 1) >> h_bits)

        def expand(spread, coeff, same):
            full = jnp.dot(spread, coeff.astype(BF16), preferred_element_type=F32)
            return jnp.where(same, full, 0.0).astype(BF16)

        rows = lambda j: slice(j * LANES, (j + 1) * LANES)
        state_out = lambda d: jnp.concatenate(
            [expand(spread_p, cl_ref[d, ri], same_ph) for ri in range(2)], axis=0)
        for j in range(T):
            ws_ref[rows(j), :] = jnp.concatenate(
                [expand(spread_h, lb_ref[T - 1 - j, ri], same_hp) for ri in range(2)], axis=1)
        for i in range(T):
            wo_ref[:, rows(i)] = state_out(i + 1)
        c_out = state_out(0)
        for d in range(T):
            block = jnp.dot(ws_ref[rows(T - 1 - d), :], c_out,
                            preferred_element_type=F32).astype(BF16)
            for j in range(T - d):
                m_ref[rows(j), rows(j + d)] = block
        for j in range(T):
            for i in range(j):
                m_ref[rows(j), rows(i)] = jnp.zeros((LANES, LANES), BF16)

    pieces = [u_ref[pl.ds(b * seq_len + i, n_chunks, stride=T), :]
              for b in range(n_seq) for i in range(T)]
    x = jnp.concatenate(
        [jnp.concatenate(pieces[b * T:(b + 1) * T], axis=1) for b in range(n_seq)], axis=0)
    x_lo = x.astype(BF16)
    y = jnp.dot(x_lo, m_ref[...], preferred_element_type=F32)
    s = jnp.dot(x_lo, ws_ref[...], preferred_element_type=F32)

    sub = F32_SUBLANES
    chunk = jnp.concatenate([iota((n_chunks, STATE_COLS), 0)] * n_seq, axis=0)
    in_tile = chunk & (sub - 1)
    cmul = lambda a_re, a_im, b_re, b_im: (a_re * b_re - a_im * b_im, a_re * b_im + a_im * b_re)

    re, im = s[:, :STATE_COLS], s[:, STATE_COLS:]
    for k in range(sub.bit_length() - 1):
        sh = 1 << k
        sre = jnp.where(in_tile >= sh, pltpu.roll(re, sh, axis=0), 0.0)
        sim = jnp.where(in_tile >= sh, pltpu.roll(im, sh, axis=0), 0.0)
        dre, dim_ = cmul(alog_ref[0, k, 0:1, :], alog_ref[0, k, 1:2, :], sre, sim)
        re, im = re + dre, im + dim_
    lin_re, lin_im = alin_ref[0, 0], alin_ref[0, 1]
    tiles_per_seq = n_chunks // sub
    done_re, done_im = [], []
    for b in range(n_seq):
        carry = None
        for g in range(tiles_per_seq):
            rows = slice((b * tiles_per_seq + g) * sub, (b * tiles_per_seq + g + 1) * sub)
            t_re, t_im = re[rows], im[rows]
            if carry is not None:
                d_re, d_im = cmul(lin_re, lin_im, *carry)
                t_re, t_im = t_re + d_re, t_im + d_im
            carry = (t_re[sub - 1:sub], t_im[sub - 1:sub])
            done_re.append(t_re)
            done_im.append(t_im)
    re, im = jnp.concatenate(done_re, axis=0), jnp.concatenate(done_im, axis=0)

    prev = jnp.concatenate([jnp.where(chunk >= 1, pltpu.roll(a, 1, axis=0), 0.0)
                            for a in (re, im)], axis=1).astype(BF16)
    y = y + jnp.dot(prev, wo_ref[...], preferred_element_type=F32)
    y = y + x * jnp.concatenate([d_ref[...]] * T, axis=1)
    for b in range(n_seq):
        for i in range(T):
            y_ref[pl.ds(b * seq_len + i, n_chunks, stride=T), :] = (
                y[b * n_chunks:(b + 1) * n_chunks, i * LANES:(i + 1) * LANES])


def _s5_ssm(u, tables, d_skip, seq_len):
    lb_t, cl_t, a_log, a_lin = tables
    T, H, P = SSM_CHUNK, SSM_GROUP_DIM, SSM_STATE
    assert seq_len % (T * F32_SUBLANES) == 0
    bsz = u.shape[0] // seq_len
    n_seq = max(n for n in (4, 2, 1) if bsz % n == 0)
    io_spec = pl.BlockSpec((n_seq * seq_len, LANES), lambda q, b: (b, q))
    square = pltpu.VMEM((T * LANES, T * LANES), BF16)
    return pl.pallas_call(
        functools.partial(_ssm_kernel, seq_len=seq_len, n_seq=n_seq),
        grid=(SSM_TILES, bsz // n_seq),
        in_specs=[io_spec,
                  pl.BlockSpec((T, 2, H, STATE_COLS), lambda q, b: (0, 0, 0, q)),
                  pl.BlockSpec((T + 1, 2, P, LANES), lambda q, b: (0, 0, 0, q)),
                  pl.BlockSpec((1,) + a_log.shape[1:], lambda q, b: (q, 0, 0, 0)),
                  pl.BlockSpec((1,) + a_lin.shape[1:], lambda q, b: (q, 0, 0, 0)),
                  pl.BlockSpec((1, LANES), lambda q, b: (0, q))],
        out_specs=io_spec,
        out_shape=jax.ShapeDtypeStruct(u.shape, F32),
        scratch_shapes=[square, square, square],
        compiler_params=pltpu.CompilerParams(dimension_semantics=("arbitrary", "arbitrary"),
                                             vmem_limit_bytes=VMEM_LIMIT_BYTES),
        name="s5_ssm",
    )(u, lb_t, cl_t, a_log, a_lin, d_skip)


def _out_proj_kernel(x_ref, ma_ref, y_ref, zs_ref, wg_ref, bg_ref, w_ref, g_ref, o_ref, *,
                     final_norm):
    y = y_ref[...]
    y = 0.5 * y * (1.0 + jnp.tanh(math.sqrt(2.0 / math.pi) * (y + 0.044715 * (y * y * y))))
    gate = jnp.dot(y.astype(BF16), wg_ref[...], preferred_element_type=F32) + bg_ref[...]
    z = zs_ref[...].astype(F32)
    mixed_ssm = y / (1.0 + jnp.exp(-gate)) * (z / (1.0 + jnp.exp(-z)))
    r = (x_ref[...]
         + jnp.dot(ma_ref[...], w_ref[:ATTN_WIDTH, :], preferred_element_type=F32)
         + jnp.dot(mixed_ssm.astype(BF16), w_ref[ATTN_WIDTH:, :], preferred_element_type=F32))
    if final_norm:
        ms = jnp.mean(r * r, axis=-1, keepdims=True)
        r = r * lax.rsqrt(ms + NORM_EPS) * g_ref[...]
    o_ref[...] = r


def _out_proj(x2, mixed_attn, y_ssm, z_ssm, w_glu, b_glu, w_out, gain, tm, final_norm):
    rows = x2.shape[0]
    row_spec = lambda width: pl.BlockSpec((tm, width), lambda i: (i, 0))
    return pl.pallas_call(
        functools.partial(_out_proj_kernel, final_norm=final_norm),
        grid=(rows // tm,),
        in_specs=[row_spec(D_MODEL), row_spec(ATTN_WIDTH), row_spec(SSM_WIDTH), row_spec(SSM_WIDTH),
                  _resident((SSM_WIDTH, SSM_WIDTH)), _resident((1, SSM_WIDTH)),
                  _resident((ATTN_WIDTH + SSM_WIDTH, D_MODEL)), _resident((1, D_MODEL))],
        out_specs=row_spec(D_MODEL),
        out_shape=jax.ShapeDtypeStruct((rows, D_MODEL), F32),
        compiler_params=pltpu.CompilerParams(dimension_semantics=("arbitrary",),
                                             vmem_limit_bytes=VMEM_LIMIT_BYTES),
        name="out_proj",
    )(x2, mixed_attn, y_ssm, z_ssm, w_glu, b_glu, w_out, gain)


def _rotary_tables(seq_len):
    half = HEAD_DIM // 2
    inv_freq = 1.0 / (ROPE_THETA ** (jnp.arange(half, dtype=F32) / half))
    ang = jnp.arange(seq_len, dtype=F32)[:, None] * inv_freq[None, :]
    cos, sin, zero = jnp.cos(ang), jnp.sin(ang), jnp.zeros_like(ang)
    reps = LANES // HEAD_DIM
    cos_t = jnp.tile(jnp.concatenate([cos, cos], axis=1), (1, reps))
    sin_hi = jnp.tile(jnp.concatenate([zero, sin], axis=1), (1, reps))
    sin_lo = jnp.tile(jnp.concatenate([-sin, zero], axis=1), (1, reps))
    return cos_t, sin_hi, sin_lo


def kernel(x, norm_gain, w_in, w_out, lam_re, lam_im, b_re, b_im, c_re, c_im,
           d_skip, log_dt, w_glu, b_glu, final_gain):
    bsz, seq_len, _ = x.shape
    depth = norm_gain.shape[0]
    assert seq_len % MOBA_BLOCK == 0 and seq_len // MOBA_BLOCK > MOBA_TOP_K
    assert seq_len % SSM_CHUNK == 0
    tm = min(512, seq_len)
    cos, sin_hi, sin_lo = _rotary_tables(seq_len)

    x2 = x.reshape(bsz * seq_len, D_MODEL)
    for layer in range(depth):
        tables = _ssm_tables(lam_re[layer], lam_im[layer], b_re[layer], b_im[layer],
                             c_re[layer], c_im[layer], log_dt[layer])
        q, k, v, z_attn, u, z_ssm = _in_proj(
            x2, norm_gain[layer].reshape(1, D_MODEL), w_in[layer].astype(BF16),
            cos, sin_hi, sin_lo, seq_len, tm)
        to_seq = lambda t: t.reshape(bsz, seq_len, ATTN_WIDTH)
        mixed_attn = _moba_attention(to_seq(q), to_seq(k), to_seq(v), to_seq(z_attn))
        y_ssm = _s5_ssm(u, tables, d_skip[layer].reshape(1, SSM_WIDTH), seq_len)
        x2 = _out_proj(x2, mixed_attn.reshape(bsz * seq_len, ATTN_WIDTH), y_ssm, z_ssm,
                       w_glu[layer].astype(BF16), b_glu[layer].reshape(1, SSM_WIDTH),
                       w_out[layer].astype(BF16), final_gain.reshape(1, D_MODEL), tm,
                       final_norm=layer == depth - 1)
    return x2.reshape(bsz, seq_len, D_MODEL)
```

```python
import functools
import math

import jax
import jax.numpy as jnp
from jax import lax
from jax.experimental import pallas as pl
from jax.experimental.pallas import tpu as pltpu

F32 = jnp.float32
BF16 = jnp.bfloat16

D_MODEL = 1024
HEAD_DIM = 64
ATTN_HEADS = 8
ATTN_WIDTH = ATTN_HEADS * HEAD_DIM
MOBA_BLOCK = 256
MOBA_TOP_K = 3
ROPE_THETA = 10000.0
SSM_GROUP_DIM = 16
SSM_GROUPS = 32
SSM_WIDTH = SSM_GROUPS * SSM_GROUP_DIM
SSM_STATE = 64
IN_PROJ_WIDTH = 4 * ATTN_WIDTH + 2 * SSM_WIDTH
NORM_EPS = 1e-6
NEG_INF = -1e30

LANES = 128
SSM_CHUNK = 8
GROUPS_PER_TILE = LANES // SSM_GROUP_DIM
SSM_TILES = SSM_WIDTH // LANES
TILE_COLS = SSM_CHUNK * LANES
STATE_COLS = GROUPS_PER_TILE * SSM_STATE
VMEM_LIMIT_BYTES = 56 * 1024 * 1024
MXU_COLS = 256
F32_SUBLANES = 8
BF16_SUBLANES = 16
VALUE_ROWS = HEAD_DIM + BF16_SUBLANES
QK_SCALE = math.log2(math.e) / math.sqrt(HEAD_DIM)


def _resident(shape):
    zeros = (0,) * len(shape)
    return pl.BlockSpec(shape, lambda *_: zeros, pipeline_mode=pl.Buffered(1))


def _in_proj_kernel(x_ref, g_ref, w32_ref, cos_ref, sin_hi_ref, sin_lo_ref,
                    q_ref, k_ref, v_ref, za_ref, u_ref, zs_ref, w_ref):
    @pl.when(pl.program_id(0) == 0)
    def _():
        for c in range(0, IN_PROJ_WIDTH, ATTN_WIDTH):
            w_ref[:, c:c + ATTN_WIDTH] = w32_ref[:, c:c + ATTN_WIDTH].astype(BF16)

    x = x_ref[...]
    ms = jnp.mean(x * x, axis=-1, keepdims=True)
    h = (x * lax.rsqrt(ms + NORM_EPS) * g_ref[...]).astype(BF16)

    def section(idx):
        return jnp.dot(h, w_ref[:, idx * ATTN_WIDTH:(idx + 1) * ATTN_WIDTH],
                       preferred_element_type=F32)

    cos = cos_ref[...]
    sin_hi = sin_hi_ref[...]
    sin_lo = sin_lo_ref[...]

    def rotary(t):
        return (t * cos + pltpu.roll(t, HEAD_DIM // 2, axis=1) * sin_hi
                + pltpu.roll(t, LANES - HEAD_DIM // 2, axis=1) * sin_lo)

    q = section(0)
    k = section(1)
    for c in range(ATTN_WIDTH // LANES):
        sl = slice(c * LANES, (c + 1) * LANES)
        q_ref[:, sl] = (rotary(q[:, sl]) * QK_SCALE).astype(BF16)
        k_ref[:, sl] = rotary(k[:, sl]).astype(BF16)
    v_ref[...] = section(2).astype(BF16)
    za_ref[...] = section(3).astype(BF16)
    u_ref[...] = section(4)
    zs_ref[...] = section(5).astype(BF16)


def _in_proj(x2, gain, w_in, cos, sin_hi, sin_lo, seq_len, tm):
    rows = x2.shape[0]
    pos_blocks = seq_len // tm
    row_spec = lambda width: pl.BlockSpec((tm, width), lambda i: (i, 0))
    tab_spec = pl.BlockSpec((tm, LANES), lambda i: (i % pos_blocks, 0))
    out_bf16 = jax.ShapeDtypeStruct((rows, ATTN_WIDTH), BF16)
    out_f32 = jax.ShapeDtypeStruct((rows, ATTN_WIDTH), F32)
    return pl.pallas_call(
        _in_proj_kernel,
        grid=(rows // tm,),
        in_specs=[row_spec(D_MODEL), _resident((1, D_MODEL)), _resident((D_MODEL, IN_PROJ_WIDTH)),
                  tab_spec, tab_spec, tab_spec],
        out_specs=[row_spec(ATTN_WIDTH)] * 6,
        out_shape=[out_bf16, out_bf16, out_bf16, out_bf16, out_f32, out_bf16],
        scratch_shapes=[pltpu.VMEM((D_MODEL, IN_PROJ_WIDTH), BF16)],
        compiler_params=pltpu.CompilerParams(dimension_semantics=("arbitrary",),
                                             vmem_limit_bytes=VMEM_LIMIT_BYTES),
        name="in_proj",
    )(x2, gain, w_in, cos, sin_hi, sin_lo)


def _attn_kernel(q_ref, k_ref, v_ref, z_ref, o_ref, kx_ref, vx_ref, qx_ref, sa_ref, sb_ref,
                 m_ref, acc_ref, *, n_blocks):
    step = pl.program_id(2)
    blk = MOBA_BLOCK
    heads = LANES // HEAD_DIM

    head_rows = lambda h: slice(h * HEAD_DIM, (h + 1) * HEAD_DIM)
    spare_base = lambda h: ((h + 1) % heads) * HEAD_DIM
    iota = lambda shape, d: lax.broadcasted_iota(jnp.int32, shape, d)

    def prepare_operands():
        lane = iota((blk, LANES), 1)
        sum_rows = (iota((VALUE_ROWS - HEAD_DIM, blk), 0) == 0).astype(F32)
        kmean = []
        for j in range(n_blocks):
            rows = slice(j * blk, (j + 1) * blk)
            kj = k_ref[0, rows, :]
            kmean.append(jnp.mean(kj.astype(F32), axis=0, keepdims=True))
            vt = v_ref[0, rows, :].astype(F32).T
            for h in range(heads):
                in_head = (lane >= h * HEAD_DIM) & (lane < (h + 1) * HEAD_DIM)
                tag = jnp.where(lane == spare_base(h) + j, 1.0, 0.0).astype(BF16)
                kx_ref[h, j] = jnp.where(in_head, kj, tag)
                vx_ref[j, h] = jnp.concatenate([vt[head_rows(h)], sum_rows], axis=0).astype(BF16)
        kmean = jnp.concatenate(kmean, axis=0).astype(BF16)

        blk_row = iota((n_blocks, blk), 0)
        zero_rows = lambda n: jnp.zeros((n, blk), F32)
        in_head_order = lambda h, own, other: [own, other] if h == 0 else [other, own]
        for qi in range(n_blocks):
            qt = q_ref[0, qi * blk:(qi + 1) * blk, :].astype(F32).T
            past = blk_row < qi
            for h in range(heads):
                q_rows = qt[head_rows(h)]
                q_only = jnp.concatenate(in_head_order(h, q_rows, zero_rows(HEAD_DIM)), axis=0)
                gate = jnp.dot(kmean, q_only.astype(BF16), preferred_element_type=F32)
                gate = jnp.where(past, gate, NEG_INF)
                beaten_by = jnp.zeros((n_blocks, blk), jnp.int32)
                for j in range(n_blocks):
                    gj = gate[j:j + 1, :]
                    wins = (gj > gate) | ((gj == gate) & (j < blk_row))
                    beaten_by = beaten_by + wins.astype(jnp.int32)
                keep = (past & (beaten_by < MOBA_TOP_K)) | (blk_row == qi)
                bias = jnp.where(keep, 0.0, NEG_INF)
                spare = jnp.concatenate([bias, zero_rows(HEAD_DIM - n_blocks)], axis=0)
                qx_ref[qi, h] = jnp.concatenate(in_head_order(h, q_rows, spare),
                                                axis=0).astype(BF16)

    key = iota((blk, blk), 0)
    query = iota((blk, blk), 1)
    col_max = lambda s: jnp.max(s, axis=0, keepdims=True)

    def past_item(pair, it):
        first = it < pair
        return first, jnp.where(first, 0, 1), jnp.where(first, it, it - pair)

    def stage_items(pair, s_ref, m):
        tiles = (pair, n_blocks - 1 - pair)

        def own(t):
            qi = tiles[t]
            for h in range(heads):
                s = jnp.dot(kx_ref[h, qi], qx_ref[qi, h], preferred_element_type=F32)
                s = jnp.where(key <= query, s, NEG_INF)
                s_ref[t, h, qi] = s
                m[t, h] = col_max(s)

        def past(it):
            first, t, j = past_item(pair, it)
            for h in range(heads):
                s = jnp.dot(kx_ref[h, j], qx_ref[jnp.where(first, tiles[0], tiles[1]), h],
                            preferred_element_type=F32)
                s_ref[t, h, j] = s
                cm = col_max(s)
                m[0, h] = jnp.where(first, jnp.maximum(m[0, h], cm), m[0, h])
                m[1, h] = jnp.where(first, m[1, h], jnp.maximum(m[1, h], cm))

        return ([functools.partial(own, t) for t in range(2)]
                + [functools.partial(past, it) for it in range(n_blocks - 1)])

    def weighted_values(j, h, s, m_h):
        p = jnp.exp2(s - m_h).astype(BF16)
        return jnp.dot(vx_ref[j, h], p, preferred_element_type=F32)

    def finish_items(pair, s_ref, m, acc_ref):
        tiles = (pair, n_blocks - 1 - pair)

        def own(t):
            qi = tiles[t]
            for h in range(heads):
                acc_ref[t, h] = weighted_values(qi, h, s_ref[t, h, qi], m[t, h])

        def past(it):
            first, t, j = past_item(pair, it)
            for h in range(heads):
                acc_ref[t, h] += weighted_values(j, h, s_ref[t, h, j],
                                                 jnp.where(first, m[0, h], m[1, h]))

        def write_rows():
            for t, qi in enumerate(tiles):
                rows = pl.ds(pl.multiple_of(qi * blk, blk), blk)
                acc = [acc_ref[t, h] for h in range(heads)]
                o_t = jnp.concatenate([a[:HEAD_DIM] / a[HEAD_DIM:HEAD_DIM + 1] for a in acc],
                                      axis=0)
                z = z_ref[0, rows, :].astype(F32)
                o_ref[0, rows, :] = (o_t.T * (z / (1.0 + jnp.exp(-z)))).astype(BF16)

        return ([functools.partial(own, t) for t in range(2)]
                + [functools.partial(past, it) for it in range(n_blocks - 1)] + [write_rows])

    def alternate(*item_lists):
        for group in zip(*item_lists):
            for item in group:
                item()
        for items in item_lists:
            for item in items[min(map(len, item_lists)):]:
                item()

    m_keys = [(t, h) for t in range(2) for h in range(heads)]

    @pl.when(step == 0)
    def _():
        prepare_operands()
        m0 = {}
        alternate(stage_items(0, sa_ref, m0))
        for t, h in m_keys:
            m_ref[t, h] = m0[t, h]

    pair_a = 2 * step
    m_a = {key_: m_ref[key_] for key_ in m_keys}
    m_b, m_c = {}, {}
    alternate(finish_items(pair_a, sa_ref, m_a, acc_ref.at[0]),
              stage_items(pair_a + 1, sb_ref, m_b))
    alternate(finish_items(pair_a + 1, sb_ref, m_b, acc_ref.at[1]),
              stage_items(lax.rem(pair_a + 2, n_blocks // 2), sa_ref, m_c))
    for t, h in m_keys:
        m_ref[t, h] = m_c[t, h]


def _moba_attention(q, k, v, z_attn):
    bsz, seq_len, _ = q.shape
    n_blocks = seq_len // MOBA_BLOCK
    assert n_blocks % 4 == 0 and n_blocks <= HEAD_DIM
    head_pairs = ATTN_WIDTH // LANES
    heads = LANES // HEAD_DIM
    assert heads == 2, "the block bias rides in the other head's half of the 128 lanes"
    seq_spec = pl.BlockSpec((1, seq_len, LANES), lambda b, h, i: (b, 0, h))
    scores = pltpu.VMEM((2, heads, n_blocks, MOBA_BLOCK, MOBA_BLOCK), F32)
    return pl.pallas_call(
        functools.partial(_attn_kernel, n_blocks=n_blocks),
        grid=(bsz, head_pairs, n_blocks // 4),
        in_specs=[seq_spec] * 4,
        out_specs=seq_spec,
        out_shape=jax.ShapeDtypeStruct((bsz, seq_len, ATTN_WIDTH), BF16),
        scratch_shapes=[pltpu.VMEM((heads, n_blocks, MOBA_BLOCK, LANES), BF16),
                        pltpu.VMEM((n_blocks, heads, VALUE_ROWS, MOBA_BLOCK), BF16),
                        pltpu.VMEM((n_blocks, heads, LANES, MOBA_BLOCK), BF16),
                        scores, scores,
                        pltpu.VMEM((2, heads, 1, MOBA_BLOCK), F32),
                        pltpu.VMEM((2, 2, heads, VALUE_ROWS, MOBA_BLOCK), F32)],
        compiler_params=pltpu.CompilerParams(
            dimension_semantics=("arbitrary", "arbitrary", "arbitrary"),
            vmem_limit_bytes=VMEM_LIMIT_BYTES),
        name="moba_attn",
    )(q, k, v, z_attn)


def _ssm_tables(lam_re, lam_im, b_re, b_im, c_re, c_im, log_dt):
    T, G, P, H = SSM_CHUNK, SSM_GROUPS, SSM_STATE, SSM_GROUP_DIM
    dt = jnp.exp(log_dt.astype(F32))[:, None]
    lam_r, lam_i = lam_re.astype(F32), lam_im.astype(F32)
    mag = jnp.exp(lam_r * dt)
    bar_r, bar_i = mag * jnp.cos(lam_i * dt), mag * jnp.sin(lam_i * dt)
    den = lam_r * lam_r + lam_i * lam_i
    f_r = ((bar_r - 1.0) * lam_r + bar_i * lam_i) / den
    f_i = (bar_i * lam_r - (bar_r - 1.0) * lam_i) / den
    bb_r = f_r[..., None] * b_re - f_i[..., None] * b_im
    bb_i = f_r[..., None] * b_im + f_i[..., None] * b_re

    def powers(base_r, base_i, count):
        out_r, out_i = [base_r], [base_i]
        for _ in range(count - 1):
            r, i = out_r[-1], out_i[-1]
            out_r.append(r * base_r - i * base_i)
            out_i.append(r * base_i + i * base_r)
        return out_r, out_i

    pw_r, pw_i = powers(bar_r, bar_i, T)
    pw_r = jnp.stack([jnp.ones_like(bar_r)] + pw_r)
    pw_i = jnp.stack([jnp.zeros_like(bar_i)] + pw_i)

    lb_r = pw_r[:T, :, :, None] * bb_r - pw_i[:T, :, :, None] * bb_i
    lb_i = pw_r[:T, :, :, None] * bb_i + pw_i[:T, :, :, None] * bb_r
    cl_r = c_re[None] * pw_r[:, :, None, :] - c_im[None] * pw_i[:, :, None, :]
    cl_i = c_re[None] * pw_i[:, :, None, :] + c_im[None] * pw_r[:, :, None, :]
    lb_t = jnp.stack([lb_r, lb_i], axis=1).transpose(0, 1, 4, 2, 3).reshape(T, 2, H, G * P)
    cl_t = jnp.stack([cl_r, -cl_i], axis=1).transpose(0, 1, 4, 2, 3).reshape(T + 1, 2, P, G * H)

    a_r, a_i = powers(pw_r[T], pw_i[T], F32_SUBLANES)
    tiled = lambda a: jnp.stack(a).reshape(len(a), SSM_TILES, STATE_COLS)
    a_lin = jnp.stack([tiled(a_r), tiled(a_i)]).transpose(2, 0, 1, 3)
    log_steps = [(1 << k) - 1 for k in range(F32_SUBLANES.bit_length() - 1)]
    a_log = jnp.stack([tiled([a_r[s] for s in log_steps]),
                       tiled([a_i[s] for s in log_steps])]).transpose(2, 1, 0, 3)
    return lb_t, cl_t, a_log, a_lin


def _ssm_kernel(u_ref, lb_ref, cl_ref, alog_ref, alin_ref, d_ref, y_ref, m_ref, ws_ref, wo_ref, *,
                seq_len, n_seq):
    T, H, P = SSM_CHUNK, SSM_GROUP_DIM, SSM_STATE
    n_chunks = seq_len // T
    iota = lambda shape, d: lax.broadcasted_iota(jnp.int32, shape, d)
    h_bits, p_bits = H.bit_length() - 1, P.bit_length() - 1

    @pl.when(pl.program_id(1) == 0)
    def _():
        spread_h = ((iota((LANES, H), 0) & (H - 1)) == iota((LANES, H), 1)).astype(BF16)
        spread_p = ((iota((STATE_COLS, P), 0) & (P - 1)) == iota((STATE_COLS, P), 1)).astype(BF16)
        same_hp = (iota((LANES, STATE_COLS), 0) >> h_bits) == (iota((LANES, STATE_COLS), 1) >> p_bits)
        same_ph = (iota((STATE_COLS, LANES), 0) >> p_bits) == (iota((STATE_COLS, LANES), 1) >> h_bits)

        def expand(spread, coeff, same):
            full = jnp.dot(spread, coeff.astype(BF16), preferred_element_type=F32)
            return jnp.where(same, full, 0.0).astype(BF16)

        rows = lambda j: slice(j * LANES, (j + 1) * LANES)
        state_out = lambda d: jnp.concatenate(
            [expand(spread_p, cl_ref[d, ri], same_ph) for ri in range(2)], axis=0)
        for j in range(T):
            ws_ref[rows(j), :] = jnp.concatenate(
                [expand(spread_h, lb_ref[T - 1 - j, ri], same_hp) for ri in range(2)], axis=1)
        for i in range(T):
            wo_ref[:, rows(i)] = state_out(i + 1)
        c_out = state_out(0)
        for d in range(T):
            block = jnp.dot(ws_ref[rows(T - 1 - d), :], c_out,
                            preferred_element_type=F32).astype(BF16)
            for j in range(T - d):
                m_ref[rows(j), rows(j + d)] = block
        for j in range(T):
            for i in range(j):
                m_ref[rows(j), rows(i)] = jnp.zeros((LANES, LANES), BF16)

    pieces = [u_ref[pl.ds(b * seq_len + i, n_chunks, stride=T), :]
              for b in range(n_seq) for i in range(T)]
    x = jnp.concatenate(
        [jnp.concatenate(pieces[b * T:(b + 1) * T], axis=1) for b in range(n_seq)], axis=0)
    x_lo = x.astype(BF16)
    y = jnp.concatenate(
        [jnp.dot(x_lo[:, :hi], m_ref[:hi, hi - MXU_COLS:hi], preferred_element_type=F32)
         for hi in range(MXU_COLS, T * LANES + 1, MXU_COLS)], axis=1)
    s = jnp.dot(x_lo, ws_ref[...], preferred_element_type=F32)

    sub = F32_SUBLANES
    chunk = jnp.concatenate([iota((n_chunks, STATE_COLS), 0)] * n_seq, axis=0)
    in_tile = chunk & (sub - 1)
    cmul = lambda a_re, a_im, b_re, b_im: (a_re * b_re - a_im * b_im, a_re * b_im + a_im * b_re)

    re, im = s[:, :STATE_COLS], s[:, STATE_COLS:]
    for k in range(sub.bit_length() - 1):
        sh = 1 << k
        sre = jnp.where(in_tile >= sh, pltpu.roll(re, sh, axis=0), 0.0)
        sim = jnp.where(in_tile >= sh, pltpu.roll(im, sh, axis=0), 0.0)
        dre, dim_ = cmul(alog_ref[0, k, 0:1, :], alog_ref[0, k, 1:2, :], sre, sim)
        re, im = re + dre, im + dim_
    lin_re, lin_im = alin_ref[0, 0], alin_ref[0, 1]
    tiles_per_seq = n_chunks // sub
    done_re, done_im = [], []
    for b in range(n_seq):
        carry = None
        for g in range(tiles_per_seq):
            rows = slice((b * tiles_per_seq + g) * sub, (b * tiles_per_seq + g + 1) * sub)
            t_re, t_im = re[rows], im[rows]
            if carry is not None:
                d_re, d_im = cmul(lin_re, lin_im, *carry)
                t_re, t_im = t_re + d_re, t_im + d_im
            carry = (t_re[sub - 1:sub], t_im[sub - 1:sub])
            done_re.append(t_re)
            done_im.append(t_im)
    re, im = jnp.concatenate(done_re, axis=0), jnp.concatenate(done_im, axis=0)

    prev = jnp.concatenate([jnp.where(chunk >= 1, pltpu.roll(a, 1, axis=0), 0.0)
                            for a in (re, im)], axis=1).astype(BF16)
    y = y + jnp.dot(prev, wo_ref[...], preferred_element_type=F32)
    y = y + x * jnp.concatenate([d_ref[...]] * T, axis=1)
    for b in range(n_seq):
        for i in range(T):
            y_ref[pl.ds(b * seq_len + i, n_chunks, stride=T), :] = (
                y[b * n_chunks:(b + 1) * n_chunks, i * LANES:(i + 1) * LANES])


def _s5_ssm(u, tables, d_skip, seq_len):
    lb_t, cl_t, a_log, a_lin = tables
    T, H, P = SSM_CHUNK, SSM_GROUP_DIM, SSM_STATE
    assert seq_len % (T * F32_SUBLANES) == 0
    bsz = u.shape[0] // seq_len
    n_seq = max(n for n in (4, 2, 1) if bsz % n == 0)
    io_spec = pl.BlockSpec((n_seq * seq_len, LANES), lambda q, b: (b, q))
    square = pltpu.VMEM((T * LANES, T * LANES), BF16)
    return pl.pallas_call(
        functools.partial(_ssm_kernel, seq_len=seq_len, n_seq=n_seq),
        grid=(SSM_TILES, bsz // n_seq),
        in_specs=[io_spec,
                  pl.BlockSpec((T, 2, H, STATE_COLS), lambda q, b: (0, 0, 0, q)),
                  pl.BlockSpec((T + 1, 2, P, LANES), lambda q, b: (0, 0, 0, q)),
                  pl.BlockSpec((1,) + a_log.shape[1:], lambda q, b: (q, 0, 0, 0)),
                  pl.BlockSpec((1,) + a_lin.shape[1:], lambda q, b: (q, 0, 0, 0)),
                  pl.BlockSpec((1, LANES), lambda q, b: (0, q))],
        out_specs=io_spec,
        out_shape=jax.ShapeDtypeStruct(u.shape, F32),
        scratch_shapes=[square, square, square],
        compiler_params=pltpu.CompilerParams(dimension_semantics=("arbitrary", "arbitrary"),
                                             vmem_limit_bytes=VMEM_LIMIT_BYTES),
        name="s5_ssm",
    )(u, lb_t, cl_t, a_log, a_lin, d_skip)


def _out_proj_kernel(x_ref, ma_ref, y_ref, zs_ref, wg32_ref, bg_ref, w32_ref, g_ref, o_ref,
                     wg_ref, w_ref, *, final_norm):
    @pl.when(pl.program_id(0) == 0)
    def _():
        wg_ref[...] = wg32_ref[...].astype(BF16)
        w_ref[...] = w32_ref[...].astype(BF16)

    y = y_ref[...]
    y = 0.5 * y * (1.0 + jnp.tanh(math.sqrt(2.0 / math.pi) * (y + 0.044715 * (y * y * y))))
    gate = jnp.dot(y.astype(BF16), wg_ref[...], preferred_element_type=F32) + bg_ref[...]
    z = zs_ref[...].astype(F32)
    mixed_ssm = y / (1.0 + jnp.exp(-gate)) * (z / (1.0 + jnp.exp(-z)))
    r = (x_ref[...]
         + jnp.dot(ma_ref[...], w_ref[:ATTN_WIDTH, :], preferred_element_type=F32)
         + jnp.dot(mixed_ssm.astype(BF16), w_ref[ATTN_WIDTH:, :], preferred_element_type=F32))
    if final_norm:
        ms = jnp.mean(r * r, axis=-1, keepdims=True)
        r = r * lax.rsqrt(ms + NORM_EPS) * g_ref[...]
    o_ref[...] = r


def _out_proj(x2, mixed_attn, y_ssm, z_ssm, w_glu, b_glu, w_out, gain, tm, final_norm):
    rows = x2.shape[0]
    row_spec = lambda width: pl.BlockSpec((tm, width), lambda i: (i, 0))
    return pl.pallas_call(
        functools.partial(_out_proj_kernel, final_norm=final_norm),
        grid=(rows // tm,),
        in_specs=[row_spec(D_MODEL), row_spec(ATTN_WIDTH), row_spec(SSM_WIDTH), row_spec(SSM_WIDTH),
                  _resident((SSM_WIDTH, SSM_WIDTH)), _resident((1, SSM_WIDTH)),
                  _resident((ATTN_WIDTH + SSM_WIDTH, D_MODEL)), _resident((1, D_MODEL))],
        out_specs=row_spec(D_MODEL),
        out_shape=jax.ShapeDtypeStruct((rows, D_MODEL), F32),
        scratch_shapes=[pltpu.VMEM((SSM_WIDTH, SSM_WIDTH), BF16),
                        pltpu.VMEM((ATTN_WIDTH + SSM_WIDTH, D_MODEL), BF16)],
        compiler_params=pltpu.CompilerParams(dimension_semantics=("arbitrary",),
                                             vmem_limit_bytes=VMEM_LIMIT_BYTES),
        name="out_proj",
    )(x2, mixed_attn, y_ssm, z_ssm, w_glu, b_glu, w_out, gain)


def _rotary_tables(seq_len):
    half = HEAD_DIM // 2
    inv_freq = 1.0 / (ROPE_THETA ** (jnp.arange(half, dtype=F32) / half))
    ang = jnp.arange(seq_len, dtype=F32)[:, None] * inv_freq[None, :]
    cos, sin, zero = jnp.cos(ang), jnp.sin(ang), jnp.zeros_like(ang)
    reps = LANES // HEAD_DIM
    cos_t = jnp.tile(jnp.concatenate([cos, cos], axis=1), (1, reps))
    sin_hi = jnp.tile(jnp.concatenate([zero, sin], axis=1), (1, reps))
    sin_lo = jnp.tile(jnp.concatenate([-sin, zero], axis=1), (1, reps))
    return cos_t, sin_hi, sin_lo


def kernel(x, norm_gain, w_in, w_out, lam_re, lam_im, b_re, b_im, c_re, c_im,
           d_skip, log_dt, w_glu, b_glu, final_gain):
    bsz, seq_len, _ = x.shape
    depth = norm_gain.shape[0]
    assert seq_len % MOBA_BLOCK == 0 and seq_len // MOBA_BLOCK > MOBA_TOP_K
    assert seq_len % SSM_CHUNK == 0
    tm = min(1024, seq_len)
    cos, sin_hi, sin_lo = _rotary_tables(seq_len)

    x2 = x.reshape(bsz * seq_len, D_MODEL)
    for layer in range(depth):
        tables = _ssm_tables(lam_re[layer], lam_im[layer], b_re[layer], b_im[layer],
                             c_re[layer], c_im[layer], log_dt[layer])
        q, k, v, z_attn, u, z_ssm = _in_proj(
            x2, norm_gain[layer].reshape(1, D_MODEL), w_in[layer], cos, sin_hi, sin_lo, seq_len, tm)
        to_seq = lambda t: t.reshape(bsz, seq_len, ATTN_WIDTH)
        mixed_attn = _moba_attention(to_seq(q), to_seq(k), to_seq(v), to_seq(z_attn))
        y_ssm = _s5_ssm(u, tables, d_skip[layer].reshape(1, SSM_WIDTH), seq_len)
        x2 = _out_proj(x2, mixed_attn.reshape(bsz * seq_len, ATTN_WIDTH), y_ssm, z_ssm,
                       w_glu[layer], b_glu[layer].reshape(1, SSM_WIDTH),
                       w_out[layer], final_gain.reshape(1, D_MODEL), tm,
                       final_norm=layer == depth - 1)
    return x2.reshape(bsz, seq_len, D_MODEL)
```

```python
import functools
import math

import jax
import jax.numpy as jnp
from jax import lax
from jax.experimental import pallas as pl
from jax.experimental.pallas import tpu as pltpu

F32 = jnp.float32
BF16 = jnp.bfloat16

D_MODEL = 1024
HEAD_DIM = 64
ATTN_HEADS = 8
ATTN_WIDTH = ATTN_HEADS * HEAD_DIM
MOBA_BLOCK = 256
MOBA_TOP_K = 3
ROPE_THETA = 10000.0
SSM_GROUP_DIM = 16
SSM_GROUPS = 32
SSM_WIDTH = SSM_GROUPS * SSM_GROUP_DIM
SSM_STATE = 64
IN_PROJ_WIDTH = 4 * ATTN_WIDTH + 2 * SSM_WIDTH
NORM_EPS = 1e-6
NEG_INF = -1e30

LANES = 128
SSM_CHUNK = 8
GROUPS_PER_TILE = LANES // SSM_GROUP_DIM
SSM_TILES = SSM_WIDTH // LANES
TILE_COLS = SSM_CHUNK * LANES
STATE_COLS = GROUPS_PER_TILE * SSM_STATE
VMEM_LIMIT_BYTES = 56 * 1024 * 1024
MXU_COLS = 256
F32_SUBLANES = 8
BF16_SUBLANES = 16
VALUE_ROWS = HEAD_DIM + BF16_SUBLANES
QK_SCALE = math.log2(math.e) / math.sqrt(HEAD_DIM)


def _resident(shape):
    zeros = (0,) * len(shape)
    return pl.BlockSpec(shape, lambda *_: zeros, pipeline_mode=pl.Buffered(1))


def _in_proj_kernel(x_ref, g_ref, w32_ref, cos_ref, sin_hi_ref, sin_lo_ref,
                    q_ref, k_ref, v_ref, za_ref, u_ref, zs_ref, w_ref):
    @pl.when(pl.program_id(0) == 0)
    def _():
        for c in range(0, IN_PROJ_WIDTH, ATTN_WIDTH):
            w_ref[:, c:c + ATTN_WIDTH] = w32_ref[:, c:c + ATTN_WIDTH].astype(BF16)

    x = x_ref[...]
    ms = jnp.mean(x * x, axis=-1, keepdims=True)
    h = (x * lax.rsqrt(ms + NORM_EPS) * g_ref[...]).astype(BF16)

    def section(idx):
        return jnp.dot(h, w_ref[:, idx * ATTN_WIDTH:(idx + 1) * ATTN_WIDTH],
                       preferred_element_type=F32)

    cos = cos_ref[...]
    sin_hi = sin_hi_ref[...]
    sin_lo = sin_lo_ref[...]

    def rotary(t):
        return (t * cos + pltpu.roll(t, HEAD_DIM // 2, axis=1) * sin_hi
                + pltpu.roll(t, LANES - HEAD_DIM // 2, axis=1) * sin_lo)

    q = section(0)
    k = section(1)
    for c in range(ATTN_WIDTH // LANES):
        sl = slice(c * LANES, (c + 1) * LANES)
        q_ref[:, sl] = (rotary(q[:, sl]) * QK_SCALE).astype(BF16)
        k_ref[:, sl] = rotary(k[:, sl]).astype(BF16)
    v_ref[...] = section(2).astype(BF16)
    za_ref[...] = section(3).astype(BF16)
    u_ref[...] = section(4)
    zs_ref[...] = section(5).astype(BF16)


def _in_proj(x2, gain, w_in, cos, sin_hi, sin_lo, seq_len, tm):
    rows = x2.shape[0]
    pos_blocks = seq_len // tm
    row_spec = lambda width: pl.BlockSpec((tm, width), lambda i: (i, 0))
    tab_spec = pl.BlockSpec((tm, LANES), lambda i: (i % pos_blocks, 0))
    out_bf16 = jax.ShapeDtypeStruct((rows, ATTN_WIDTH), BF16)
    out_f32 = jax.ShapeDtypeStruct((rows, ATTN_WIDTH), F32)
    return pl.pallas_call(
        _in_proj_kernel,
        grid=(rows // tm,),
        in_specs=[row_spec(D_MODEL), _resident((1, D_MODEL)), _resident((D_MODEL, IN_PROJ_WIDTH)),
                  tab_spec, tab_spec, tab_spec],
        out_specs=[row_spec(ATTN_WIDTH)] * 6,
        out_shape=[out_bf16, out_bf16, out_bf16, out_bf16, out_f32, out_bf16],
        scratch_shapes=[pltpu.VMEM((D_MODEL, IN_PROJ_WIDTH), BF16)],
        compiler_params=pltpu.CompilerParams(dimension_semantics=("arbitrary",),
                                             vmem_limit_bytes=VMEM_LIMIT_BYTES),
        name="in_proj",
    )(x2, gain, w_in, cos, sin_hi, sin_lo)


def _attn_kernel(q_ref, k_ref, v_ref, z_ref, o_ref, kx_ref, vx_ref, qx_ref, sa_ref, sb_ref,
                 m_ref, acc_ref, *, n_blocks):
    step = pl.program_id(2)
    blk = MOBA_BLOCK
    heads = LANES // HEAD_DIM

    head_rows = lambda h: slice(h * HEAD_DIM, (h + 1) * HEAD_DIM)
    spare_base = lambda h: ((h + 1) % heads) * HEAD_DIM
    iota = lambda shape, d: lax.broadcasted_iota(jnp.int32, shape, d)

    def prepare_operands():
        lane = iota((blk, LANES), 1)
        sum_rows = (iota((VALUE_ROWS - HEAD_DIM, blk), 0) == 0).astype(F32)
        kmean = []
        for j in range(n_blocks):
            rows = slice(j * blk, (j + 1) * blk)
            kj = k_ref[0, rows, :]
            kmean.append(jnp.mean(kj.astype(F32), axis=0, keepdims=True))
            vt = v_ref[0, rows, :].astype(F32).T
            for h in range(heads):
                in_head = (lane >= h * HEAD_DIM) & (lane < (h + 1) * HEAD_DIM)
                tag = jnp.where(lane == spare_base(h) + j, 1.0, 0.0).astype(BF16)
                kx_ref[h, j] = jnp.where(in_head, kj, tag)
                vx_ref[j, h] = jnp.concatenate([vt[head_rows(h)], sum_rows], axis=0).astype(BF16)
        kmean = jnp.concatenate(kmean, axis=0).astype(BF16)

        blk_row = iota((n_blocks, blk), 0)
        zero_rows = lambda n: jnp.zeros((n, blk), F32)
        in_head_order = lambda h, own, other: [own, other] if h == 0 else [other, own]
        for qi in range(n_blocks):
            qt = q_ref[0, qi * blk:(qi + 1) * blk, :].astype(F32).T
            past = blk_row < qi
            for h in range(heads):
                q_rows = qt[head_rows(h)]
                q_only = jnp.concatenate(in_head_order(h, q_rows, zero_rows(HEAD_DIM)), axis=0)
                gate = jnp.dot(kmean, q_only.astype(BF16), preferred_element_type=F32)
                gate = jnp.where(past, gate, NEG_INF)
                beaten_by = jnp.zeros((n_blocks, blk), jnp.int32)
                for j in range(n_blocks):
                    gj = gate[j:j + 1, :]
                    wins = (gj > gate) | ((gj == gate) & (j < blk_row))
                    beaten_by = beaten_by + wins.astype(jnp.int32)
                keep = (past & (beaten_by < MOBA_TOP_K)) | (blk_row == qi)
                bias = jnp.where(keep, 0.0, NEG_INF)
                spare = jnp.concatenate([bias, zero_rows(HEAD_DIM - n_blocks)], axis=0)
                qx_ref[qi, h] = jnp.concatenate(in_head_order(h, q_rows, spare),
                                                axis=0).astype(BF16)

    key = iota((blk, blk), 0)
    query = iota((blk, blk), 1)
    col_max = lambda s: jnp.max(s, axis=0, keepdims=True)

    def past_item(pair, it):
        first = it < pair
        return first, jnp.where(first, 0, 1), jnp.where(first, it, it - pair)

    def stage_items(pair, s_ref, m):
        tiles = (pair, n_blocks - 1 - pair)

        def own(t):
            qi = tiles[t]
            for h in range(heads):
                s = jnp.dot(kx_ref[h, qi], qx_ref[qi, h], preferred_element_type=F32)
                s = jnp.where(key <= query, s, NEG_INF)
                s_ref[t, h, qi] = s
                m[t, h] = col_max(s)

        def past(it):
            first, t, j = past_item(pair, it)
            for h in range(heads):
                s = jnp.dot(kx_ref[h, j], qx_ref[jnp.where(first, tiles[0], tiles[1]), h],
                            preferred_element_type=F32)
                s_ref[t, h, j] = s
                cm = col_max(s)
                m[0, h] = jnp.where(first, jnp.maximum(m[0, h], cm), m[0, h])
                m[1, h] = jnp.where(first, m[1, h], jnp.maximum(m[1, h], cm))

        return ([functools.partial(own, t) for t in range(2)]
                + [functools.partial(past, it) for it in range(n_blocks - 1)])

    def weighted_values(j, h, s, m_h):
        p = jnp.exp2(s - m_h).astype(BF16)
        return jnp.dot(vx_ref[j, h], p, preferred_element_type=F32)

    def finish_items(pair, s_ref, m, acc_ref):
        tiles = (pair, n_blocks - 1 - pair)

        def own(t):
            qi = tiles[t]
            for h in range(heads):
                acc_ref[t, h] = weighted_values(qi, h, s_ref[t, h, qi], m[t, h])

        def past(it):
            first, t, j = past_item(pair, it)
            for h in range(heads):
                acc_ref[t, h] += weighted_values(j, h, s_ref[t, h, j],
                                                 jnp.where(first, m[0, h], m[1, h]))

        def write_rows():
            for t, qi in enumerate(tiles):
                rows = pl.ds(pl.multiple_of(qi * blk, blk), blk)
                acc = [acc_ref[t, h] for h in range(heads)]
                o_t = jnp.concatenate([a[:HEAD_DIM] / a[HEAD_DIM:HEAD_DIM + 1] for a in acc],
                                      axis=0)
                z = z_ref[0, rows, :].astype(F32)
                o_ref[0, rows, :] = (o_t.T * (z / (1.0 + jnp.exp(-z)))).astype(BF16)

        return ([functools.partial(own, t) for t in range(2)]
                + [functools.partial(past, it) for it in range(n_blocks - 1)] + [write_rows])

    def alternate(*item_lists):
        for group in zip(*item_lists):
            for item in group:
                item()
        for items in item_lists:
            for item in items[min(map(len, item_lists)):]:
                item()

    m_keys = [(t, h) for t in range(2) for h in range(heads)]

    @pl.when(step == 0)
    def _():
        prepare_operands()
        m0 = {}
        alternate(stage_items(0, sa_ref, m0))
        for t, h in m_keys:
            m_ref[t, h] = m0[t, h]

    pair_a = 2 * step
    m_a = {key_: m_ref[key_] for key_ in m_keys}
    m_b, m_c = {}, {}
    alternate(finish_items(pair_a, sa_ref, m_a, acc_ref.at[0]),
              stage_items(pair_a + 1, sb_ref, m_b))
    alternate(finish_items(pair_a + 1, sb_ref, m_b, acc_ref.at[1]),
              stage_items(lax.rem(pair_a + 2, n_blocks // 2), sa_ref, m_c))
    for t, h in m_keys:
        m_ref[t, h] = m_c[t, h]


def _moba_attention(q, k, v, z_attn):
    bsz, seq_len, _ = q.shape
    n_blocks = seq_len // MOBA_BLOCK
    assert n_blocks % 4 == 0 and n_blocks <= HEAD_DIM
    head_pairs = ATTN_WIDTH // LANES
    heads = LANES // HEAD_DIM
    assert heads == 2, "the block bias rides in the other head's half of the 128 lanes"
    seq_spec = pl.BlockSpec((1, seq_len, LANES), lambda b, h, i: (b, 0, h))
    scores = pltpu.VMEM((2, heads, n_blocks, MOBA_BLOCK, MOBA_BLOCK), F32)
    return pl.pallas_call(
        functools.partial(_attn_kernel, n_blocks=n_blocks),
        grid=(bsz, head_pairs, n_blocks // 4),
        in_specs=[seq_spec] * 4,
        out_specs=seq_spec,
        out_shape=jax.ShapeDtypeStruct((bsz, seq_len, ATTN_WIDTH), BF16),
        scratch_shapes=[pltpu.VMEM((heads, n_blocks, MOBA_BLOCK, LANES), BF16),
                        pltpu.VMEM((n_blocks, heads, VALUE_ROWS, MOBA_BLOCK), BF16),
                        pltpu.VMEM((n_blocks, heads, LANES, MOBA_BLOCK), BF16),
                        scores, scores,
                        pltpu.VMEM((2, heads, 1, MOBA_BLOCK), F32),
                        pltpu.VMEM((2, 2, heads, VALUE_ROWS, MOBA_BLOCK), F32)],
        compiler_params=pltpu.CompilerParams(
            dimension_semantics=("arbitrary", "arbitrary", "arbitrary"),
            vmem_limit_bytes=VMEM_LIMIT_BYTES),
        name="moba_attn",
    )(q, k, v, z_attn)


def _ssm_tables(lam_re, lam_im, b_re, b_im, c_re, c_im, log_dt):
    T, G, P, H = SSM_CHUNK, SSM_GROUPS, SSM_STATE, SSM_GROUP_DIM
    dt = jnp.exp(log_dt.astype(F32))[:, None]
    lam_r, lam_i = lam_re.astype(F32), lam_im.astype(F32)
    mag = jnp.exp(lam_r * dt)
    bar_r, bar_i = mag * jnp.cos(lam_i * dt), mag * jnp.sin(lam_i * dt)
    den = lam_r * lam_r + lam_i * lam_i
    f_r = ((bar_r - 1.0) * lam_r + bar_i * lam_i) / den
    f_i = (bar_i * lam_r - (bar_r - 1.0) * lam_i) / den
    bb_r = f_r[..., None] * b_re - f_i[..., None] * b_im
    bb_i = f_r[..., None] * b_im + f_i[..., None] * b_re

    def powers(base_r, base_i, count):
        out_r, out_i = [base_r], [base_i]
        for _ in range(count - 1):
            r, i = out_r[-1], out_i[-1]
            out_r.append(r * base_r - i * base_i)
            out_i.append(r * base_i + i * base_r)
        return out_r, out_i

    pw_r, pw_i = powers(bar_r, bar_i, T)
    pw_r = jnp.stack([jnp.ones_like(bar_r)] + pw_r)
    pw_i = jnp.stack([jnp.zeros_like(bar_i)] + pw_i)

    lb_r = pw_r[:T, :, :, None] * bb_r - pw_i[:T, :, :, None] * bb_i
    lb_i = pw_r[:T, :, :, None] * bb_i + pw_i[:T, :, :, None] * bb_r
    cl_r = c_re[None] * pw_r[:, :, None, :] - c_im[None] * pw_i[:, :, None, :]
    cl_i = c_re[None] * pw_i[:, :, None, :] + c_im[None] * pw_r[:, :, None, :]
    lb_t = jnp.stack([lb_r, lb_i], axis=1).transpose(0, 1, 4, 2, 3).reshape(T, 2, H, G * P)
    cl_t = jnp.stack([cl_r, -cl_i], axis=1).transpose(0, 1, 4, 2, 3).reshape(T + 1, 2, P, G * H)

    a_r, a_i = powers(pw_r[T], pw_i[T], F32_SUBLANES)
    tiled = lambda a: jnp.stack(a).reshape(len(a), SSM_TILES, STATE_COLS)
    a_lin = jnp.stack([tiled(a_r), tiled(a_i)]).transpose(2, 0, 1, 3)
    shifts = [1 << k for k in range(F32_SUBLANES.bit_length() - 1)]
    a_log = jnp.stack([tiled([a_r[s - 1] for s in shifts]),
                       tiled([a_i[s - 1] for s in shifts])]).transpose(2, 1, 0, 3)
    row = jnp.arange(F32_SUBLANES)[None, None, None, :, None]
    keep = row >= jnp.asarray(shifts)[None, :, None, None, None]
    a_log = jnp.where(keep, a_log[:, :, :, None, :], 0.0)
    return lb_t, cl_t, a_log, a_lin


def _ssm_kernel(u_ref, lb_ref, cl_ref, alog_ref, alin_ref, d_ref, y_ref, m_ref, ws_ref, wo_ref, *,
                seq_len, n_seq):
    T, H, P = SSM_CHUNK, SSM_GROUP_DIM, SSM_STATE
    n_chunks = seq_len // T
    iota = lambda shape, d: lax.broadcasted_iota(jnp.int32, shape, d)
    h_bits, p_bits = H.bit_length() - 1, P.bit_length() - 1

    @pl.when(pl.program_id(1) == 0)
    def _():
        spread_h = ((iota((LANES, H), 0) & (H - 1)) == iota((LANES, H), 1)).astype(BF16)
        spread_p = ((iota((STATE_COLS, P), 0) & (P - 1)) == iota((STATE_COLS, P), 1)).astype(BF16)
        same_hp = (iota((LANES, STATE_COLS), 0) >> h_bits) == (iota((LANES, STATE_COLS), 1) >> p_bits)
        same_ph = (iota((STATE_COLS, LANES), 0) >> p_bits) == (iota((STATE_COLS, LANES), 1) >> h_bits)

        def expand(spread, coeff, same):
            full = jnp.dot(spread, coeff.astype(BF16), preferred_element_type=F32)
            return jnp.where(same, full, 0.0).astype(BF16)

        rows = lambda j: slice(j * LANES, (j + 1) * LANES)
        state_out = lambda d: jnp.concatenate(
            [expand(spread_p, cl_ref[d, ri], same_ph) for ri in range(2)], axis=0)
        for j in range(T):
            ws_ref[rows(j), :] = jnp.concatenate(
                [expand(spread_h, lb_ref[T - 1 - j, ri], same_hp) for ri in range(2)], axis=1)
        for i in range(T):
            wo_ref[:, rows(i)] = state_out(i + 1)
        c_out = state_out(0)
        for d in range(T):
            block = jnp.dot(ws_ref[rows(T - 1 - d), :], c_out,
                            preferred_element_type=F32).astype(BF16)
            for j in range(T - d):
                m_ref[rows(j), rows(j + d)] = block
        for j in range(T):
            for i in range(j):
                m_ref[rows(j), rows(i)] = jnp.zeros((LANES, LANES), BF16)

    pieces = [u_ref[pl.ds(b * seq_len + i, n_chunks, stride=T), :]
              for b in range(n_seq) for i in range(T)]
    x = jnp.concatenate(
        [jnp.concatenate(pieces[b * T:(b + 1) * T], axis=1) for b in range(n_seq)], axis=0)
    x_lo = x.astype(BF16)
    groups = [slice(c, c + MXU_COLS) for c in range(0, STATE_COLS, MXU_COLS)]
    im_of = lambda cols: slice(STATE_COLS + cols.start, STATE_COLS + cols.stop)
    increments = [(jnp.dot(x_lo, ws_ref[:, cols], preferred_element_type=F32),
                   jnp.dot(x_lo, ws_ref[:, im_of(cols)], preferred_element_type=F32))
                  for cols in groups]
    y = jnp.concatenate(
        [jnp.dot(x_lo[:, :hi], m_ref[:hi, hi - MXU_COLS:hi], preferred_element_type=F32)
         for hi in range(MXU_COLS, T * LANES + 1, MXU_COLS)], axis=1)
    y = y + x * jnp.concatenate([d_ref[...]] * T, axis=1)

    sub = F32_SUBLANES
    first_row = iota((sub, MXU_COLS), 0) == 0
    cmul = lambda a_re, a_im, b_re, b_im: (a_re * b_re - a_im * b_im, a_re * b_im + a_im * b_re)
    tiles_per_seq = n_chunks // sub

    for cols, (re, im) in zip(groups, increments):
        log_mul = [(alog_ref[0, k, 0, :, cols], alog_ref[0, k, 1, :, cols])
                   for k in range(sub.bit_length() - 1)]
        lin_mul = (alin_ref[0, 0, :, cols], alin_ref[0, 1, :, cols])
        prev_re, prev_im = [], []
        for b in range(n_seq):
            carry = (jnp.zeros((1, MXU_COLS), F32),) * 2
            for g in range(tiles_per_seq):
                rows = slice((b * tiles_per_seq + g) * sub, (b * tiles_per_seq + g + 1) * sub)
                t_re, t_im = re[rows], im[rows]
                for k, mul in enumerate(log_mul):
                    d_re, d_im = cmul(*mul, pltpu.roll(t_re, 1 << k, axis=0),
                                      pltpu.roll(t_im, 1 << k, axis=0))
                    t_re, t_im = t_re + d_re, t_im + d_im
                if g > 0:
                    d_re, d_im = cmul(*lin_mul, *carry)
                    t_re, t_im = t_re + d_re, t_im + d_im
                prev_re.append(jnp.where(first_row, carry[0], pltpu.roll(t_re, 1, axis=0)))
                prev_im.append(jnp.where(first_row, carry[1], pltpu.roll(t_im, 1, axis=0)))
                carry = (t_re[sub - 1:sub], t_im[sub - 1:sub])
        prev_re, prev_im = (jnp.concatenate(a, axis=0).astype(BF16) for a in (prev_re, prev_im))
        y = (y + jnp.dot(prev_re, wo_ref[cols, :], preferred_element_type=F32)
             + jnp.dot(prev_im, wo_ref[im_of(cols), :], preferred_element_type=F32))
    for b in range(n_seq):
        for i in range(T):
            y_ref[pl.ds(b * seq_len + i, n_chunks, stride=T), :] = (
                y[b * n_chunks:(b + 1) * n_chunks, i * LANES:(i + 1) * LANES])


def _s5_ssm(u, tables, d_skip, seq_len):
    lb_t, cl_t, a_log, a_lin = tables
    T, H, P = SSM_CHUNK, SSM_GROUP_DIM, SSM_STATE
    assert seq_len % (T * F32_SUBLANES) == 0
    bsz = u.shape[0] // seq_len
    n_seq = max(n for n in (4, 2, 1) if bsz % n == 0)
    io_spec = pl.BlockSpec((n_seq * seq_len, LANES), lambda q, b: (b, q))
    square = pltpu.VMEM((T * LANES, T * LANES), BF16)
    return pl.pallas_call(
        functools.partial(_ssm_kernel, seq_len=seq_len, n_seq=n_seq),
        grid=(SSM_TILES, bsz // n_seq),
        in_specs=[io_spec,
                  pl.BlockSpec((T, 2, H, STATE_COLS), lambda q, b: (0, 0, 0, q)),
                  pl.BlockSpec((T + 1, 2, P, LANES), lambda q, b: (0, 0, 0, q)),
                  pl.BlockSpec((1,) + a_log.shape[1:], lambda q, b: (q, 0, 0, 0, 0)),
                  pl.BlockSpec((1,) + a_lin.shape[1:], lambda q, b: (q, 0, 0, 0)),
                  pl.BlockSpec((1, LANES), lambda q, b: (0, q))],
        out_specs=io_spec,
        out_shape=jax.ShapeDtypeStruct(u.shape, F32),
        scratch_shapes=[square, square, square],
        compiler_params=pltpu.CompilerParams(dimension_semantics=("arbitrary", "arbitrary"),
                                             vmem_limit_bytes=VMEM_LIMIT_BYTES),
        name="s5_ssm",
    )(u, lb_t, cl_t, a_log, a_lin, d_skip)


def _out_proj_kernel(x_ref, ma_ref, y_ref, zs_ref, wg32_ref, bg_ref, w32_ref, g_ref, o_ref,
                     wg_ref, w_ref, *, final_norm):
    @pl.when(pl.program_id(0) == 0)
    def _():
        wg_ref[...] = wg32_ref[...].astype(BF16)
        w_ref[...] = w32_ref[...].astype(BF16)

    y = y_ref[...]
    y = 0.5 * y * (1.0 + jnp.tanh(math.sqrt(2.0 / math.pi) * (y + 0.044715 * (y * y * y))))
    gate = jnp.dot(y.astype(BF16), wg_ref[...], preferred_element_type=F32) + bg_ref[...]
    z = zs_ref[...].astype(F32)
    mixed_ssm = y / (1.0 + jnp.exp(-gate)) * (z / (1.0 + jnp.exp(-z)))
    r = (x_ref[...]
         + jnp.dot(ma_ref[...], w_ref[:ATTN_WIDTH, :], preferred_element_type=F32)
         + jnp.dot(mixed_ssm.astype(BF16), w_ref[ATTN_WIDTH:, :], preferred_element_type=F32))
    if final_norm:
        ms = jnp.mean(r * r, axis=-1, keepdims=True)
        r = r * lax.rsqrt(ms + NORM_EPS) * g_ref[...]
    o_ref[...] = r


def _out_proj(x2, mixed_attn, y_ssm, z_ssm, w_glu, b_glu, w_out, gain, tm, final_norm):
    rows = x2.shape[0]
    row_spec = lambda width: pl.BlockSpec((tm, width), lambda i: (i, 0))
    return pl.pallas_call(
        functools.partial(_out_proj_kernel, final_norm=final_norm),
        grid=(rows // tm,),
        in_specs=[row_spec(D_MODEL), row_spec(ATTN_WIDTH), row_spec(SSM_WIDTH), row_spec(SSM_WIDTH),
                  _resident((SSM_WIDTH, SSM_WIDTH)), _resident((1, SSM_WIDTH)),
                  _resident((ATTN_WIDTH + SSM_WIDTH, D_MODEL)), _resident((1, D_MODEL))],
        out_specs=row_spec(D_MODEL),
        out_shape=jax.ShapeDtypeStruct((rows, D_MODEL), F32),
        scratch_shapes=[pltpu.VMEM((SSM_WIDTH, SSM_WIDTH), BF16),
                        pltpu.VMEM((ATTN_WIDTH + SSM_WIDTH, D_MODEL), BF16)],
        compiler_params=pltpu.CompilerParams(dimension_semantics=("arbitrary",),
                                             vmem_limit_bytes=VMEM_LIMIT_BYTES),
        name="out_proj",
    )(x2, mixed_attn, y_ssm, z_ssm, w_glu, b_glu, w_out, gain)


def _rotary_tables(seq_len):
    half = HEAD_DIM // 2
    inv_freq = 1.0 / (ROPE_THETA ** (jnp.arange(half, dtype=F32) / half))
    ang = jnp.arange(seq_len, dtype=F32)[:, None] * inv_freq[None, :]
    cos, sin, zero = jnp.cos(ang), jnp.sin(ang), jnp.zeros_like(ang)
    reps = LANES // HEAD_DIM
    cos_t = jnp.tile(jnp.concatenate([cos, cos], axis=1), (1, reps))
    sin_hi = jnp.tile(jnp.concatenate([zero, sin], axis=1), (1, reps))
    sin_lo = jnp.tile(jnp.concatenate([-sin, zero], axis=1), (1, reps))
    return cos_t, sin_hi, sin_lo


def kernel(x, norm_gain, w_in, w_out, lam_re, lam_im, b_re, b_im, c_re, c_im,
           d_skip, log_dt, w_glu, b_glu, final_gain):
    bsz, seq_len, _ = x.shape
    depth = norm_gain.shape[0]
    assert seq_len % MOBA_BLOCK == 0 and seq_len // MOBA_BLOCK > MOBA_TOP_K
    assert seq_len % SSM_CHUNK == 0
    tm = min(1024, seq_len)
    cos, sin_hi, sin_lo = _rotary_tables(seq_len)

    x2 = x.reshape(bsz * seq_len, D_MODEL)
    for layer in range(depth):
        tables = _ssm_tables(lam_re[layer], lam_im[layer], b_re[layer], b_im[layer],
                             c_re[layer], c_im[layer], log_dt[layer])
        q, k, v, z_attn, u, z_ssm = _in_proj(
            x2, norm_gain[layer].reshape(1, D_MODEL), w_in[layer], cos, sin_hi, sin_lo, seq_len, tm)
        to_seq = lambda t: t.reshape(bsz, seq_len, ATTN_WIDTH)
        mixed_attn = _moba_attention(to_seq(q), to_seq(k), to_seq(v), to_seq(z_attn))
        y_ssm = _s5_ssm(u, tables, d_skip[layer].reshape(1, SSM_WIDTH), seq_len)
        x2 = _out_proj(x2, mixed_attn.reshape(bsz * seq_len, ATTN_WIDTH), y_ssm, z_ssm,
                       w_glu[layer], b_glu[layer].reshape(1, SSM_WIDTH),
                       w_out[layer], final_gain.reshape(1, D_MODEL), tm,
                       final_norm=layer == depth - 1)
    return x2.reshape(bsz, seq_len, D_MODEL)
```

```python
import functools
import math

import jax
import jax.numpy as jnp
from jax import lax
from jax.experimental import pallas as pl
from jax.experimental.pallas import tpu as pltpu

F32 = jnp.float32
BF16 = jnp.bfloat16

D_MODEL = 1024
HEAD_DIM = 64
ATTN_HEADS = 8
ATTN_WIDTH = ATTN_HEADS * HEAD_DIM
MOBA_BLOCK = 256
MOBA_TOP_K = 3
ROPE_THETA = 10000.0
SSM_GROUP_DIM = 16
SSM_GROUPS = 32
SSM_WIDTH = SSM_GROUPS * SSM_GROUP_DIM
SSM_STATE = 64
IN_PROJ_WIDTH = 4 * ATTN_WIDTH + 2 * SSM_WIDTH
NORM_EPS = 1e-6
NEG_INF = -1e30

LANES = 128
SSM_CHUNK = 8
GROUPS_PER_TILE = LANES // SSM_GROUP_DIM
SSM_TILES = SSM_WIDTH // LANES
TILE_COLS = SSM_CHUNK * LANES
STATE_COLS = GROUPS_PER_TILE * SSM_STATE
VMEM_LIMIT_BYTES = 56 * 1024 * 1024
MXU_COLS = 256
F32_SUBLANES = 8
BF16_SUBLANES = 16
VALUE_ROWS = HEAD_DIM + BF16_SUBLANES
QK_SCALE = math.log2(math.e) / math.sqrt(HEAD_DIM)


def _resident(shape):
    zeros = (0,) * len(shape)
    return pl.BlockSpec(shape, lambda *_: zeros, pipeline_mode=pl.Buffered(1))


def _in_proj_kernel(x_ref, g_ref, w32_ref, cos_ref, sin_hi_ref, sin_lo_ref,
                    q_ref, k_ref, v_ref, za_ref, u_ref, zs_ref, w_ref):
    @pl.when(pl.program_id(0) == 0)
    def _():
        for c in range(0, IN_PROJ_WIDTH, ATTN_WIDTH):
            w_ref[:, c:c + ATTN_WIDTH] = w32_ref[:, c:c + ATTN_WIDTH].astype(BF16)

    x = x_ref[...]
    ms = jnp.mean(x * x, axis=-1, keepdims=True)
    h = (x * lax.rsqrt(ms + NORM_EPS) * g_ref[...]).astype(BF16)

    def section(idx):
        return jnp.dot(h, w_ref[:, idx * ATTN_WIDTH:(idx + 1) * ATTN_WIDTH],
                       preferred_element_type=F32)

    cos = cos_ref[...]
    sin_hi = sin_hi_ref[...]
    sin_lo = sin_lo_ref[...]

    def rotary(t):
        return (t * cos + pltpu.roll(t, HEAD_DIM // 2, axis=1) * sin_hi
                + pltpu.roll(t, LANES - HEAD_DIM // 2, axis=1) * sin_lo)

    q = section(0)
    k = section(1)
    for c in range(ATTN_WIDTH // LANES):
        sl = slice(c * LANES, (c + 1) * LANES)
        q_ref[:, sl] = (rotary(q[:, sl]) * QK_SCALE).astype(BF16)
        k_ref[:, sl] = rotary(k[:, sl]).astype(BF16)
    v_ref[...] = section(2).astype(BF16)
    za_ref[...] = section(3).astype(BF16)
    u_ref[...] = section(4)
    zs_ref[...] = section(5).astype(BF16)


def _in_proj(x2, gain, w_in, cos, sin_hi, sin_lo, seq_len, tm):
    rows = x2.shape[0]
    pos_blocks = seq_len // tm
    row_spec = lambda width: pl.BlockSpec((tm, width), lambda i: (i, 0))
    tab_spec = pl.BlockSpec((tm, LANES), lambda i: (i % pos_blocks, 0))
    out_bf16 = jax.ShapeDtypeStruct((rows, ATTN_WIDTH), BF16)
    out_f32 = jax.ShapeDtypeStruct((rows, ATTN_WIDTH), F32)
    return pl.pallas_call(
        _in_proj_kernel,
        grid=(rows // tm,),
        in_specs=[row_spec(D_MODEL), _resident((1, D_MODEL)), _resident((D_MODEL, IN_PROJ_WIDTH)),
                  tab_spec, tab_spec, tab_spec],
        out_specs=[row_spec(ATTN_WIDTH)] * 6,
        out_shape=[out_bf16, out_bf16, out_bf16, out_bf16, out_f32, out_bf16],
        scratch_shapes=[pltpu.VMEM((D_MODEL, IN_PROJ_WIDTH), BF16)],
        compiler_params=pltpu.CompilerParams(dimension_semantics=("arbitrary",),
                                             vmem_limit_bytes=VMEM_LIMIT_BYTES),
        name="in_proj",
    )(x2, gain, w_in, cos, sin_hi, sin_lo)


def _attn_kernel(q0_ref, k0_ref, v0_ref, qn_ref, kn_ref, vn_ref, z_ref, o_ref,
                 kx_ref, vx_ref, qx_ref, sa_ref, sb_ref, m_ref, acc_ref, *, n_blocks):
    unit = pl.program_id(0)
    blk = MOBA_BLOCK
    heads = LANES // HEAD_DIM
    n_pairs = n_blocks // 2
    cur = lax.rem(unit, 2)
    nxt = 1 - cur

    head_rows = lambda h: slice(h * HEAD_DIM, (h + 1) * HEAD_DIM)
    spare_base = lambda h: ((h + 1) % heads) * HEAD_DIM
    iota = lambda shape, d: lax.broadcasted_iota(jnp.int32, shape, d)
    key = iota((blk, blk), 0)
    query = iota((blk, blk), 1)
    col_max = lambda s: jnp.max(s, axis=0, keepdims=True)

    def prepare_items(q_ref, k_ref, v_ref, slot):
        kmean = {}

        def key_block(j):
            lane = iota((blk, LANES), 1)
            sum_rows = (iota((VALUE_ROWS - HEAD_DIM, blk), 0) == 0).astype(F32)
            rows = slice(j * blk, (j + 1) * blk)
            kj = k_ref[0, rows, :]
            kmean[j] = jnp.mean(kj.astype(F32), axis=0, keepdims=True)
            vt = v_ref[0, rows, :].astype(F32).T
            for h in range(heads):
                in_head = (lane >= h * HEAD_DIM) & (lane < (h + 1) * HEAD_DIM)
                tag = jnp.where(lane == spare_base(h) + j, 1.0, 0.0).astype(BF16)
                kx_ref[slot, h, j] = jnp.where(in_head, kj, tag)
                vx_ref[slot, j, h] = jnp.concatenate([vt[head_rows(h)], sum_rows],
                                                     axis=0).astype(BF16)

        def query_block(qi):
            if "all" not in kmean:
                kmean["all"] = jnp.concatenate([kmean[j] for j in range(n_blocks)],
                                               axis=0).astype(BF16)
            blk_row = iota((n_blocks, blk), 0)
            zero_rows = lambda n: jnp.zeros((n, blk), F32)
            in_head_order = lambda h, own, other: [own, other] if h == 0 else [other, own]
            qt = q_ref[0, qi * blk:(qi + 1) * blk, :].astype(F32).T
            past = blk_row < qi
            for h in range(heads):
                q_rows = qt[head_rows(h)]
                q_only = jnp.concatenate(in_head_order(h, q_rows, zero_rows(HEAD_DIM)), axis=0)
                gate = jnp.dot(kmean["all"], q_only.astype(BF16), preferred_element_type=F32)
                gate = jnp.where(past, gate, NEG_INF)
                beaten_by = jnp.zeros((n_blocks, blk), jnp.int32)
                for j in range(n_blocks):
                    gj = gate[j:j + 1, :]
                    wins = (gj > gate) | ((gj == gate) & (j < blk_row))
                    beaten_by = beaten_by + wins.astype(jnp.int32)
                keep = (past & (beaten_by < MOBA_TOP_K)) | (blk_row == qi)
                bias = jnp.where(keep, 0.0, NEG_INF)
                spare = jnp.concatenate([bias, zero_rows(HEAD_DIM - n_blocks)], axis=0)
                qx_ref[slot, qi, h] = jnp.concatenate(in_head_order(h, q_rows, spare),
                                                      axis=0).astype(BF16)

        return ([functools.partial(key_block, j) for j in range(n_blocks)]
                + [functools.partial(query_block, qi) for qi in range(n_blocks)])

    def visible_blocks(pair):
        tiles = (pair, n_blocks - 1 - pair)
        return ([(t, tiles[t], True) for t in range(2)]
                + [(t, j, False) for t in range(2) for j in range(tiles[t])])

    traced = lambda t: t + jnp.minimum(unit, 0)

    def stage_items(pair, s_ref, m, slot):
        def block(t, j, own):
            qi = (pair, n_blocks - 1 - pair)[t]
            for h in range(heads):
                s = jnp.dot(kx_ref[slot, h, j], qx_ref[slot, qi, h], preferred_element_type=F32)
                if own:
                    s = jnp.where(key <= query, s, NEG_INF)
                s_ref[traced(t), h, j] = s
                m[t, h] = col_max(s) if own else jnp.maximum(m[t, h], col_max(s))
        return [functools.partial(block, *b) for b in visible_blocks(pair)]

    def finish_items(pair, s_ref, m, acc_ref, slot):
        def block(t, j, own):
            for h in range(heads):
                p = jnp.exp2(s_ref[traced(t), h, j] - m[t, h]).astype(BF16)
                pv = jnp.dot(vx_ref[slot, j, h], p, preferred_element_type=F32)
                if own:
                    acc_ref[traced(t), h] = pv
                else:
                    acc_ref[traced(t), h] += pv

        def write_rows():
            for t, qi in enumerate((pair, n_blocks - 1 - pair)):
                rows = slice(qi * blk, (qi + 1) * blk)
                acc = [acc_ref[t, h] for h in range(heads)]
                o_t = jnp.concatenate([a[:HEAD_DIM] / a[HEAD_DIM:HEAD_DIM + 1] for a in acc],
                                      axis=0)
                z = z_ref[0, rows, :].astype(F32)
                o_ref[0, rows, :] = (o_t.T * (z / (1.0 + jnp.exp(-z)))).astype(BF16)

        return [functools.partial(block, *b) for b in visible_blocks(pair)] + [write_rows]

    def alternate(*item_lists):
        for group in zip(*item_lists):
            for item in group:
                item()
        for items in item_lists:
            for item in items[min(map(len, item_lists)):]:
                item()

    m_keys = [(t, h) for t in range(2) for h in range(heads)]

    @pl.when(unit == 0)
    def _():
        m0 = {}
        alternate(prepare_items(q0_ref, k0_ref, v0_ref, 0))
        alternate(stage_items(0, sa_ref, m0, 0))
        for t, h in m_keys:
            m_ref[t, h] = m0[t, h]

    m = {key_: m_ref[key_] for key_ in m_keys}
    for pair in range(n_pairs):
        s_ref, s_next = (sa_ref, sb_ref) if pair % 2 == 0 else (sb_ref, sa_ref)
        m_next = {}
        lists = [finish_items(pair, s_ref, m, acc_ref.at[pair % 2], cur)]
        if pair + 1 < n_pairs:
            lists.append(stage_items(pair + 1, s_next, m_next, cur))
        else:
            lists.append(stage_items(0, s_next, m_next, nxt))
        if pair == n_pairs - 2:
            lists.append(prepare_items(qn_ref, kn_ref, vn_ref, nxt))
        alternate(*lists)
        m = m_next
    for t, h in m_keys:
        m_ref[t, h] = m[t, h]


def _moba_attention(q, k, v, z_attn):
    bsz, seq_len, _ = q.shape
    n_blocks = seq_len // MOBA_BLOCK
    assert n_blocks % 4 == 0 and n_blocks <= HEAD_DIM
    head_pairs = ATTN_WIDTH // LANES
    heads = LANES // HEAD_DIM
    assert heads == 2, "the block bias rides in the other head's half of the 128 lanes"
    n_units = bsz * head_pairs
    block = (1, seq_len, LANES)
    unit_index = lambda u: (u // head_pairs, 0, u % head_pairs)
    first_spec = pl.BlockSpec(block, lambda u: (0, 0, 0))
    next_spec = pl.BlockSpec(block, lambda u: unit_index(jnp.minimum(u + 1, n_units - 1)))
    unit_spec = pl.BlockSpec(block, unit_index)
    scores = pltpu.VMEM((2, heads, n_blocks, MOBA_BLOCK, MOBA_BLOCK), F32)
    return pl.pallas_call(
        functools.partial(_attn_kernel, n_blocks=n_blocks),
        grid=(n_units,),
        in_specs=[first_spec] * 3 + [next_spec] * 3 + [unit_spec],
        out_specs=unit_spec,
        out_shape=jax.ShapeDtypeStruct((bsz, seq_len, ATTN_WIDTH), BF16),
        scratch_shapes=[pltpu.VMEM((2, heads, n_blocks, MOBA_BLOCK, LANES), BF16),
                        pltpu.VMEM((2, n_blocks, heads, VALUE_ROWS, MOBA_BLOCK), BF16),
                        pltpu.VMEM((2, n_blocks, heads, LANES, MOBA_BLOCK), BF16),
                        scores, scores,
                        pltpu.VMEM((2, heads, 1, MOBA_BLOCK), F32),
                        pltpu.VMEM((2, 2, heads, VALUE_ROWS, MOBA_BLOCK), F32)],
        compiler_params=pltpu.CompilerParams(dimension_semantics=("arbitrary",),
                                             vmem_limit_bytes=VMEM_LIMIT_BYTES),
        name="moba_attn",
    )(q, k, v, q, k, v, z_attn)


def _ssm_tables(lam_re, lam_im, b_re, b_im, c_re, c_im, log_dt):
    T, G, P, H = SSM_CHUNK, SSM_GROUPS, SSM_STATE, SSM_GROUP_DIM
    dt = jnp.exp(log_dt.astype(F32))[:, None]
    lam_r, lam_i = lam_re.astype(F32), lam_im.astype(F32)
    mag = jnp.exp(lam_r * dt)
    bar_r, bar_i = mag * jnp.cos(lam_i * dt), mag * jnp.sin(lam_i * dt)
    den = lam_r * lam_r + lam_i * lam_i
    f_r = ((bar_r - 1.0) * lam_r + bar_i * lam_i) / den
    f_i = (bar_i * lam_r - (bar_r - 1.0) * lam_i) / den
    bb_r = f_r[..., None] * b_re - f_i[..., None] * b_im
    bb_i = f_r[..., None] * b_im + f_i[..., None] * b_re

    def powers(base_r, base_i, count):
        out_r, out_i = [base_r], [base_i]
        for _ in range(count - 1):
            r, i = out_r[-1], out_i[-1]
            out_r.append(r * base_r - i * base_i)
            out_i.append(r * base_i + i * base_r)
        return out_r, out_i

    pw_r, pw_i = powers(bar_r, bar_i, T)
    pw_r = jnp.stack([jnp.ones_like(bar_r)] + pw_r)
    pw_i = jnp.stack([jnp.zeros_like(bar_i)] + pw_i)

    lb_r = pw_r[:T, :, :, None] * bb_r - pw_i[:T, :, :, None] * bb_i
    lb_i = pw_r[:T, :, :, None] * bb_i + pw_i[:T, :, :, None] * bb_r
    cl_r = c_re[None] * pw_r[:, :, None, :] - c_im[None] * pw_i[:, :, None, :]
    cl_i = c_re[None] * pw_i[:, :, None, :] + c_im[None] * pw_r[:, :, None, :]
    lb_t = jnp.stack([lb_r, lb_i], axis=1).transpose(0, 1, 4, 2, 3).reshape(T, 2, H, G * P)
    cl_t = jnp.stack([cl_r, -cl_i], axis=1).transpose(0, 1, 4, 2, 3).reshape(T + 1, 2, P, G * H)

    a_r, a_i = powers(pw_r[T], pw_i[T], F32_SUBLANES)
    tiled = lambda a: jnp.stack(a).reshape(len(a), SSM_TILES, STATE_COLS)
    a_lin = jnp.stack([tiled(a_r), tiled(a_i)]).transpose(2, 0, 1, 3)
    shifts = [1 << k for k in range(F32_SUBLANES.bit_length() - 1)]
    a_log = jnp.stack([tiled([a_r[s - 1] for s in shifts]),
                       tiled([a_i[s - 1] for s in shifts])]).transpose(2, 1, 0, 3)
    row = jnp.arange(F32_SUBLANES)[None, None, None, :, None]
    keep = row >= jnp.asarray(shifts)[None, :, None, None, None]
    a_log = jnp.where(keep, a_log[:, :, :, None, :], 0.0)
    return lb_t, cl_t, a_log, a_lin


def _ssm_kernel(u_ref, lb_ref, cl_ref, alog_ref, alin_ref, d_ref, y_ref, m_ref, ws_ref, wo_ref, *,
                seq_len, n_seq):
    T, H, P = SSM_CHUNK, SSM_GROUP_DIM, SSM_STATE
    n_chunks = seq_len // T
    iota = lambda shape, d: lax.broadcasted_iota(jnp.int32, shape, d)
    h_bits, p_bits = H.bit_length() - 1, P.bit_length() - 1

    @pl.when(pl.program_id(1) == 0)
    def _():
        spread_h = ((iota((LANES, H), 0) & (H - 1)) == iota((LANES, H), 1)).astype(BF16)
        spread_p = ((iota((STATE_COLS, P), 0) & (P - 1)) == iota((STATE_COLS, P), 1)).astype(BF16)
        same_hp = (iota((LANES, STATE_COLS), 0) >> h_bits) == (iota((LANES, STATE_COLS), 1) >> p_bits)
        same_ph = (iota((STATE_COLS, LANES), 0) >> p_bits) == (iota((STATE_COLS, LANES), 1) >> h_bits)

        def expand(spread, coeff, same):
            full = jnp.dot(spread, coeff.astype(BF16), preferred_element_type=F32)
            return jnp.where(same, full, 0.0).astype(BF16)

        rows = lambda j: slice(j * LANES, (j + 1) * LANES)
        state_out = lambda d: jnp.concatenate(
            [expand(spread_p, cl_ref[d, ri], same_ph) for ri in range(2)], axis=0)
        for j in range(T):
            ws_ref[rows(j), :] = jnp.concatenate(
                [expand(spread_h, lb_ref[T - 1 - j, ri], same_hp) for ri in range(2)], axis=1)
        for i in range(T):
            wo_ref[:, rows(i)] = state_out(i + 1)
        c_out = state_out(0)
        for d in range(T):
            block = jnp.dot(ws_ref[rows(T - 1 - d), :], c_out,
                            preferred_element_type=F32).astype(BF16)
            for j in range(T - d):
                m_ref[rows(j), rows(j + d)] = block
        for j in range(T):
            for i in range(j):
                m_ref[rows(j), rows(i)] = jnp.zeros((LANES, LANES), BF16)

    pieces = [u_ref[pl.ds(b * seq_len + i, n_chunks, stride=T), :]
              for b in range(n_seq) for i in range(T)]
    x = jnp.concatenate(
        [jnp.concatenate(pieces[b * T:(b + 1) * T], axis=1) for b in range(n_seq)], axis=0)
    x_lo = x.astype(BF16)
    groups = [slice(c, c + MXU_COLS) for c in range(0, STATE_COLS, MXU_COLS)]
    im_of = lambda cols: slice(STATE_COLS + cols.start, STATE_COLS + cols.stop)
    increments = [(jnp.dot(x_lo, ws_ref[:, cols], preferred_element_type=F32),
                   jnp.dot(x_lo, ws_ref[:, im_of(cols)], preferred_element_type=F32))
                  for cols in groups]
    y = jnp.concatenate(
        [jnp.dot(x_lo[:, :hi], m_ref[:hi, hi - MXU_COLS:hi], preferred_element_type=F32)
         for hi in range(MXU_COLS, T * LANES + 1, MXU_COLS)], axis=1)
    y = y + x * jnp.concatenate([d_ref[...]] * T, axis=1)

    sub = F32_SUBLANES
    first_row = iota((sub, MXU_COLS), 0) == 0
    cmul = lambda a_re, a_im, b_re, b_im: (a_re * b_re - a_im * b_im, a_re * b_im + a_im * b_re)
    tiles_per_seq = n_chunks // sub

    for cols, (re, im) in zip(groups, increments):
        log_mul = [(alog_ref[0, k, 0, :, cols], alog_ref[0, k, 1, :, cols])
                   for k in range(sub.bit_length() - 1)]
        lin_mul = (alin_ref[0, 0, :, cols], alin_ref[0, 1, :, cols])
        prev_re, prev_im = [], []
        for b in range(n_seq):
            carry = (jnp.zeros((1, MXU_COLS), F32),) * 2
            for g in range(tiles_per_seq):
                rows = slice((b * tiles_per_seq + g) * sub, (b * tiles_per_seq + g + 1) * sub)
                t_re, t_im = re[rows], im[rows]
                for k, mul in enumerate(log_mul):
                    d_re, d_im = cmul(*mul, pltpu.roll(t_re, 1 << k, axis=0),
                                      pltpu.roll(t_im, 1 << k, axis=0))
                    t_re, t_im = t_re + d_re, t_im + d_im
                if g > 0:
                    d_re, d_im = cmul(*lin_mul, *carry)
                    t_re, t_im = t_re + d_re, t_im + d_im
                prev_re.append(jnp.where(first_row, carry[0], pltpu.roll(t_re, 1, axis=0)))
                prev_im.append(jnp.where(first_row, carry[1], pltpu.roll(t_im, 1, axis=0)))
                carry = (t_re[sub - 1:sub], t_im[sub - 1:sub])
        prev_re, prev_im = (jnp.concatenate(a, axis=0).astype(BF16) for a in (prev_re, prev_im))
        y = (y + jnp.dot(prev_re, wo_ref[cols, :], preferred_element_type=F32)
             + jnp.dot(prev_im, wo_ref[im_of(cols), :], preferred_element_type=F32))
    for b in range(n_seq):
        for i in range(T):
            y_ref[pl.ds(b * seq_len + i, n_chunks, stride=T), :] = (
                y[b * n_chunks:(b + 1) * n_chunks, i * LANES:(i + 1) * LANES])


def _s5_ssm(u, tables, d_skip, seq_len):
    lb_t, cl_t, a_log, a_lin = tables
    T, H, P = SSM_CHUNK, SSM_GROUP_DIM, SSM_STATE
    assert seq_len % (T * F32_SUBLANES) == 0
    bsz = u.shape[0] // seq_len
    n_seq = max(n for n in (4, 2, 1) if bsz % n == 0)
    io_spec = pl.BlockSpec((n_seq * seq_len, LANES), lambda q, b: (b, q))
    square = pltpu.VMEM((T * LANES, T * LANES), BF16)
    return pl.pallas_call(
        functools.partial(_ssm_kernel, seq_len=seq_len, n_seq=n_seq),
        grid=(SSM_TILES, bsz // n_seq),
        in_specs=[io_spec,
                  pl.BlockSpec((T, 2, H, STATE_COLS), lambda q, b: (0, 0, 0, q)),
                  pl.BlockSpec((T + 1, 2, P, LANES), lambda q, b: (0, 0, 0, q)),
                  pl.BlockSpec((1,) + a_log.shape[1:], lambda q, b: (q, 0, 0, 0, 0)),
                  pl.BlockSpec((1,) + a_lin.shape[1:], lambda q, b: (q, 0, 0, 0)),
                  pl.BlockSpec((1, LANES), lambda q, b: (0, q))],
        out_specs=io_spec,
        out_shape=jax.ShapeDtypeStruct(u.shape, F32),
        scratch_shapes=[square, square, square],
        compiler_params=pltpu.CompilerParams(dimension_semantics=("arbitrary", "arbitrary"),
                                             vmem_limit_bytes=VMEM_LIMIT_BYTES),
        name="s5_ssm",
    )(u, lb_t, cl_t, a_log, a_lin, d_skip)


def _out_proj_kernel(x_ref, ma_ref, y_ref, zs_ref, wg32_ref, bg_ref, w32_ref, g_ref, o_ref,
                     wg_ref, w_ref, *, final_norm):
    @pl.when(pl.program_id(0) == 0)
    def _():
        wg_ref[...] = wg32_ref[...].astype(BF16)
        w_ref[...] = w32_ref[...].astype(BF16)

    y = y_ref[...]
    y = 0.5 * y * (1.0 + jnp.tanh(math.sqrt(2.0 / math.pi) * (y + 0.044715 * (y * y * y))))
    gate = jnp.dot(y.astype(BF16), wg_ref[...], preferred_element_type=F32) + bg_ref[...]
    z = zs_ref[...].astype(F32)
    mixed_ssm = y / (1.0 + jnp.exp(-gate)) * (z / (1.0 + jnp.exp(-z)))
    r = (x_ref[...]
         + jnp.dot(ma_ref[...], w_ref[:ATTN_WIDTH, :], preferred_element_type=F32)
         + jnp.dot(mixed_ssm.astype(BF16), w_ref[ATTN_WIDTH:, :], preferred_element_type=F32))
    if final_norm:
        ms = jnp.mean(r * r, axis=-1, keepdims=True)
        r = r * lax.rsqrt(ms + NORM_EPS) * g_ref[...]
    o_ref[...] = r


def _out_proj(x2, mixed_attn, y_ssm, z_ssm, w_glu, b_glu, w_out, gain, tm, final_norm):
    rows = x2.shape[0]
    row_spec = lambda width: pl.BlockSpec((tm, width), lambda i: (i, 0))
    return pl.pallas_call(
        functools.partial(_out_proj_kernel, final_norm=final_norm),
        grid=(rows // tm,),
        in_specs=[row_spec(D_MODEL), row_spec(ATTN_WIDTH), row_spec(SSM_WIDTH), row_spec(SSM_WIDTH),
                  _resident((SSM_WIDTH, SSM_WIDTH)), _resident((1, SSM_WIDTH)),
                  _resident((ATTN_WIDTH + SSM_WIDTH, D_MODEL)), _resident((1, D_MODEL))],
        out_specs=row_spec(D_MODEL),
        out_shape=jax.ShapeDtypeStruct((rows, D_MODEL), F32),
        scratch_shapes=[pltpu.VMEM((SSM_WIDTH, SSM_WIDTH), BF16),
                        pltpu.VMEM((ATTN_WIDTH + SSM_WIDTH, D_MODEL), BF16)],
        compiler_params=pltpu.CompilerParams(dimension_semantics=("arbitrary",),
                                             vmem_limit_bytes=VMEM_LIMIT_BYTES),
        name="out_proj",
    )(x2, mixed_attn, y_ssm, z_ssm, w_glu, b_glu, w_out, gain)


def _rotary_tables(seq_len):
    half = HEAD_DIM // 2
    inv_freq = 1.0 / (ROPE_THETA ** (jnp.arange(half, dtype=F32) / half))
    ang = jnp.arange(seq_len, dtype=F32)[:, None] * inv_freq[None, :]
    cos, sin, zero = jnp.cos(ang), jnp.sin(ang), jnp.zeros_like(ang)
    reps = LANES // HEAD_DIM
    cos_t = jnp.tile(jnp.concatenate([cos, cos], axis=1), (1, reps))
    sin_hi = jnp.tile(jnp.concatenate([zero, sin], axis=1), (1, reps))
    sin_lo = jnp.tile(jnp.concatenate([-sin, zero], axis=1), (1, reps))
    return cos_t, sin_hi, sin_lo


def kernel(x, norm_gain, w_in, w_out, lam_re, lam_im, b_re, b_im, c_re, c_im,
           d_skip, log_dt, w_glu, b_glu, final_gain):
    bsz, seq_len, _ = x.shape
    depth = norm_gain.shape[0]
    assert seq_len % MOBA_BLOCK == 0 and seq_len // MOBA_BLOCK > MOBA_TOP_K
    assert seq_len % SSM_CHUNK == 0
    tm = min(1024, seq_len)
    cos, sin_hi, sin_lo = _rotary_tables(seq_len)

    x2 = x.reshape(bsz * seq_len, D_MODEL)
    for layer in range(depth):
        tables = _ssm_tables(lam_re[layer], lam_im[layer], b_re[layer], b_im[layer],
                             c_re[layer], c_im[layer], log_dt[layer])
        q, k, v, z_attn, u, z_ssm = _in_proj(
            x2, norm_gain[layer].reshape(1, D_MODEL), w_in[layer], cos, sin_hi, sin_lo, seq_len, tm)
        to_seq = lambda t: t.reshape(bsz, seq_len, ATTN_WIDTH)
        mixed_attn = _moba_attention(to_seq(q), to_seq(k), to_seq(v), to_seq(z_attn))
        y_ssm = _s5_ssm(u, tables, d_skip[layer].reshape(1, SSM_WIDTH), seq_len)
        x2 = _out_proj(x2, mixed_attn.reshape(bsz * seq_len, ATTN_WIDTH), y_ssm, z_ssm,
                       w_glu[layer], b_glu[layer].reshape(1, SSM_WIDTH),
                       w_out[layer], final_gain.reshape(1, D_MODEL), tm,
                       final_norm=layer == depth - 1)
    return x2.reshape(bsz, seq_len, D_MODEL)
```

```python
import functools
import math

import jax
import jax.numpy as jnp
from jax import lax
from jax.experimental import pallas as pl
from jax.experimental.pallas import tpu as pltpu

F32 = jnp.float32
BF16 = jnp.bfloat16

D_MODEL = 1024
HEAD_DIM = 64
ATTN_HEADS = 8
ATTN_WIDTH = ATTN_HEADS * HEAD_DIM
MOBA_BLOCK = 256
MOBA_TOP_K = 3
ROPE_THETA = 10000.0
SSM_GROUP_DIM = 16
SSM_GROUPS = 32
SSM_WIDTH = SSM_GROUPS * SSM_GROUP_DIM
SSM_STATE = 64
IN_PROJ_WIDTH = 4 * ATTN_WIDTH + 2 * SSM_WIDTH
NORM_EPS = 1e-6
NEG_INF = -1e30

LANES = 128
SSM_CHUNK = 8
GROUPS_PER_TILE = LANES // SSM_GROUP_DIM
SSM_TILES = SSM_WIDTH // LANES
TILE_COLS = SSM_CHUNK * LANES
STATE_COLS = GROUPS_PER_TILE * SSM_STATE
VMEM_LIMIT_BYTES = 56 * 1024 * 1024
MXU_COLS = 256
F32_SUBLANES = 8
BF16_SUBLANES = 16
VALUE_ROWS = HEAD_DIM + BF16_SUBLANES
QK_SCALE = math.log2(math.e) / math.sqrt(HEAD_DIM)


def _resident(shape):
    zeros = (0,) * len(shape)
    return pl.BlockSpec(shape, lambda *_: zeros, pipeline_mode=pl.Buffered(1))


def _in_proj_kernel(x_ref, g_ref, w32_ref, cos_ref, sin_hi_ref, sin_lo_ref,
                    q_ref, k_ref, v_ref, za_ref, u_ref, zs_ref, w_ref):
    @pl.when(pl.program_id(0) == 0)
    def _():
        for c in range(0, IN_PROJ_WIDTH, ATTN_WIDTH):
            w_ref[:, c:c + ATTN_WIDTH] = w32_ref[:, c:c + ATTN_WIDTH].astype(BF16)

    x = x_ref[...]
    ms = jnp.mean(x * x, axis=-1, keepdims=True)
    h = (x * lax.rsqrt(ms + NORM_EPS) * g_ref[...]).astype(BF16)

    def section(idx):
        return jnp.dot(h, w_ref[:, idx * ATTN_WIDTH:(idx + 1) * ATTN_WIDTH],
                       preferred_element_type=F32)

    cos = cos_ref[...]
    sin_hi = sin_hi_ref[...]
    sin_lo = sin_lo_ref[...]

    def rotary(t):
        return (t * cos + pltpu.roll(t, HEAD_DIM // 2, axis=1) * sin_hi
                + pltpu.roll(t, LANES - HEAD_DIM // 2, axis=1) * sin_lo)

    q = section(0)
    k = section(1)
    for c in range(ATTN_WIDTH // LANES):
        sl = slice(c * LANES, (c + 1) * LANES)
        q_ref[:, sl] = (rotary(q[:, sl]) * QK_SCALE).astype(BF16)
        k_ref[:, sl] = rotary(k[:, sl]).astype(BF16)
    v_ref[...] = section(2).astype(BF16)
    za_ref[...] = section(3).astype(BF16)
    u_ref[...] = section(4)
    zs_ref[...] = section(5).astype(BF16)


def _in_proj(x2, gain, w_in, cos, sin_hi, sin_lo, seq_len, tm):
    rows = x2.shape[0]
    pos_blocks = seq_len // tm
    row_spec = lambda width: pl.BlockSpec((tm, width), lambda i: (i, 0))
    tab_spec = pl.BlockSpec((tm, LANES), lambda i: (i % pos_blocks, 0))
    out_bf16 = jax.ShapeDtypeStruct((rows, ATTN_WIDTH), BF16)
    out_f32 = jax.ShapeDtypeStruct((rows, ATTN_WIDTH), F32)
    return pl.pallas_call(
        _in_proj_kernel,
        grid=(rows // tm,),
        in_specs=[row_spec(D_MODEL), _resident((1, D_MODEL)), _resident((D_MODEL, IN_PROJ_WIDTH)),
                  tab_spec, tab_spec, tab_spec],
        out_specs=[row_spec(ATTN_WIDTH)] * 6,
        out_shape=[out_bf16, out_bf16, out_bf16, out_bf16, out_f32, out_bf16],
        scratch_shapes=[pltpu.VMEM((D_MODEL, IN_PROJ_WIDTH), BF16)],
        compiler_params=pltpu.CompilerParams(dimension_semantics=("arbitrary",),
                                             vmem_limit_bytes=VMEM_LIMIT_BYTES),
        name="in_proj",
    )(x2, gain, w_in, cos, sin_hi, sin_lo)


def _attn_kernel(q0_ref, k0_ref, v0_ref, qn_ref, kn_ref, vn_ref, z_ref, o_ref,
                 kx_ref, vx_ref, qx_ref, sa_ref, sb_ref, m_ref, acc_ref, *, n_blocks):
    unit = pl.program_id(0)
    blk = MOBA_BLOCK
    heads = LANES // HEAD_DIM
    n_pairs = n_blocks // 2
    cur = lax.rem(unit, 2)
    nxt = 1 - cur

    head_rows = lambda h: slice(h * HEAD_DIM, (h + 1) * HEAD_DIM)
    spare_base = lambda h: ((h + 1) % heads) * HEAD_DIM
    iota = lambda shape, d: lax.broadcasted_iota(jnp.int32, shape, d)
    key = iota((blk, blk), 0)
    query = iota((blk, blk), 1)
    col_max = lambda s: jnp.max(s, axis=0, keepdims=True)

    def prepare_items(q_ref, k_ref, v_ref, slot):
        kmean = {}

        def key_block(j):
            lane = iota((blk, LANES), 1)
            sum_rows = (iota((VALUE_ROWS - HEAD_DIM, blk), 0) == 0).astype(F32)
            rows = slice(j * blk, (j + 1) * blk)
            kj = k_ref[0, rows, :]
            kmean[j] = jnp.mean(kj.astype(F32), axis=0, keepdims=True)
            vt = v_ref[0, rows, :].astype(F32).T
            for h in range(heads):
                in_head = (lane >= h * HEAD_DIM) & (lane < (h + 1) * HEAD_DIM)
                tag = jnp.where(lane == spare_base(h) + j, 1.0, 0.0).astype(BF16)
                kx_ref[slot, h, j] = jnp.where(in_head, kj, tag)
                vx_ref[slot, j, h] = jnp.concatenate([vt[head_rows(h)], sum_rows],
                                                     axis=0).astype(BF16)

        def query_block(qi):
            if "all" not in kmean:
                kmean["all"] = jnp.concatenate([kmean[j] for j in range(n_blocks)],
                                               axis=0).astype(BF16)
            blk_row = iota((n_blocks, blk), 0)
            zero_rows = lambda n: jnp.zeros((n, blk), F32)
            in_head_order = lambda h, own, other: [own, other] if h == 0 else [other, own]
            qt = q_ref[0, qi * blk:(qi + 1) * blk, :].astype(F32).T
            past = blk_row < qi
            for h in range(heads):
                q_rows = qt[head_rows(h)]
                q_only = jnp.concatenate(in_head_order(h, q_rows, zero_rows(HEAD_DIM)), axis=0)
                gate = jnp.dot(kmean["all"], q_only.astype(BF16), preferred_element_type=F32)
                gate = jnp.where(past, gate, NEG_INF)
                beaten_by = jnp.zeros((n_blocks, blk), jnp.int32)
                for j in range(n_blocks):
                    gj = gate[j:j + 1, :]
                    wins = (gj > gate) | ((gj == gate) & (j < blk_row))
                    beaten_by = beaten_by + wins.astype(jnp.int32)
                keep = (past & (beaten_by < MOBA_TOP_K)) | (blk_row == qi)
                bias = jnp.where(keep, 0.0, NEG_INF)
                spare = jnp.concatenate([bias, zero_rows(HEAD_DIM - n_blocks)], axis=0)
                qx_ref[slot, qi, h] = jnp.concatenate(in_head_order(h, q_rows, spare),
                                                      axis=0).astype(BF16)

        return ([functools.partial(key_block, j) for j in range(n_blocks)]
                + [functools.partial(query_block, qi) for qi in range(n_blocks)])

    def visible_blocks(pair):
        tiles = (pair, n_blocks - 1 - pair)
        return ([(t, tiles[t], True) for t in range(2)]
                + [(t, j, False) for t in range(2) for j in range(tiles[t])])

    traced = lambda t: t + jnp.minimum(unit, 0)

    def stage_items(pair, s_ref, m, slot):
        def block(t, j, own):
            qi = (pair, n_blocks - 1 - pair)[t]
            for h in range(heads):
                s = jnp.dot(kx_ref[slot, h, j], qx_ref[slot, qi, h], preferred_element_type=F32)
                if own:
                    s = jnp.where(key <= query, s, NEG_INF)
                s_ref[traced(t), h, j] = s
                m[t, h] = col_max(s) if own else jnp.maximum(m[t, h], col_max(s))
        return [functools.partial(block, *b) for b in visible_blocks(pair)]

    def finish_items(pair, s_ref, m, acc_ref, slot):
        def block(t, j, own):
            for h in range(heads):
                p = jnp.exp2(s_ref[traced(t), h, j] - m[t, h]).astype(BF16)
                pv = jnp.dot(vx_ref[slot, j, h], p, preferred_element_type=F32)
                if own:
                    acc_ref[traced(t), h] = pv
                else:
                    acc_ref[traced(t), h] += pv

        def write_rows():
            for t, qi in enumerate((pair, n_blocks - 1 - pair)):
                rows = slice(qi * blk, (qi + 1) * blk)
                acc = [acc_ref[t, h] for h in range(heads)]
                o_t = jnp.concatenate([a[:HEAD_DIM] / a[HEAD_DIM:HEAD_DIM + 1] for a in acc],
                                      axis=0)
                z = z_ref[0, rows, :].astype(F32)
                o_ref[0, rows, :] = (o_t.T * (z / (1.0 + jnp.exp(-z)))).astype(BF16)

        return [functools.partial(block, *b) for b in visible_blocks(pair)] + [write_rows]

    def alternate(*item_lists):
        for group in zip(*item_lists):
            for item in group:
                item()
        for items in item_lists:
            for item in items[min(map(len, item_lists)):]:
                item()

    m_keys = [(t, h) for t in range(2) for h in range(heads)]

    @pl.when(unit == 0)
    def _():
        m0 = {}
        alternate(prepare_items(q0_ref, k0_ref, v0_ref, 0))
        alternate(stage_items(0, sa_ref, m0, 0))
        for t, h in m_keys:
            m_ref[t, h] = m0[t, h]

    m = {key_: m_ref[key_] for key_ in m_keys}
    for pair in range(n_pairs):
        s_ref, s_next = (sa_ref, sb_ref) if pair % 2 == 0 else (sb_ref, sa_ref)
        m_next = {}
        lists = [finish_items(pair, s_ref, m, acc_ref.at[pair % 2], cur)]
        if pair + 1 < n_pairs:
            lists.append(stage_items(pair + 1, s_next, m_next, cur))
        else:
            lists.append(stage_items(0, s_next, m_next, nxt))
        if pair == n_pairs - 2:
            lists.append(prepare_items(qn_ref, kn_ref, vn_ref, nxt))
        alternate(*lists)
        m = m_next
    for t, h in m_keys:
        m_ref[t, h] = m[t, h]


def _moba_attention(q, k, v, z_attn):
    bsz, seq_len, _ = q.shape
    n_blocks = seq_len // MOBA_BLOCK
    assert n_blocks % 4 == 0 and n_blocks <= HEAD_DIM
    head_pairs = ATTN_WIDTH // LANES
    heads = LANES // HEAD_DIM
    assert heads == 2, "the block bias rides in the other head's half of the 128 lanes"
    n_units = bsz * head_pairs
    block = (1, seq_len, LANES)
    unit_index = lambda u: (u // head_pairs, 0, u % head_pairs)
    first_spec = pl.BlockSpec(block, lambda u: (0, 0, 0))
    next_spec = pl.BlockSpec(block, lambda u: unit_index(jnp.minimum(u + 1, n_units - 1)))
    unit_spec = pl.BlockSpec(block, unit_index)
    scores = pltpu.VMEM((2, heads, n_blocks, MOBA_BLOCK, MOBA_BLOCK), F32)
    return pl.pallas_call(
        functools.partial(_attn_kernel, n_blocks=n_blocks),
        grid=(n_units,),
        in_specs=[first_spec] * 3 + [next_spec] * 3 + [unit_spec],
        out_specs=unit_spec,
        out_shape=jax.ShapeDtypeStruct((bsz, seq_len, ATTN_WIDTH), BF16),
        scratch_shapes=[pltpu.VMEM((2, heads, n_blocks, MOBA_BLOCK, LANES), BF16),
                        pltpu.VMEM((2, n_blocks, heads, VALUE_ROWS, MOBA_BLOCK), BF16),
                        pltpu.VMEM((2, n_blocks, heads, LANES, MOBA_BLOCK), BF16),
                        scores, scores,
                        pltpu.VMEM((2, heads, 1, MOBA_BLOCK), F32),
                        pltpu.VMEM((2, 2, heads, VALUE_ROWS, MOBA_BLOCK), F32)],
        compiler_params=pltpu.CompilerParams(dimension_semantics=("arbitrary",),
                                             vmem_limit_bytes=VMEM_LIMIT_BYTES),
        name="moba_attn",
    )(q, k, v, q, k, v, z_attn)


def _ssm_tables(lam_re, lam_im, b_re, b_im, c_re, c_im, log_dt):
    T, G, P, H = SSM_CHUNK, SSM_GROUPS, SSM_STATE, SSM_GROUP_DIM
    dt = jnp.exp(log_dt.astype(F32))[:, None]
    lam_r, lam_i = lam_re.astype(F32), lam_im.astype(F32)
    mag = jnp.exp(lam_r * dt)
    bar_r, bar_i = mag * jnp.cos(lam_i * dt), mag * jnp.sin(lam_i * dt)
    den = lam_r * lam_r + lam_i * lam_i
    f_r = ((bar_r - 1.0) * lam_r + bar_i * lam_i) / den
    f_i = (bar_i * lam_r - (bar_r - 1.0) * lam_i) / den
    bb_r = f_r[..., None] * b_re - f_i[..., None] * b_im
    bb_i = f_r[..., None] * b_im + f_i[..., None] * b_re

    def powers(base_r, base_i, count):
        out_r, out_i = [base_r], [base_i]
        for _ in range(count - 1):
            r, i = out_r[-1], out_i[-1]
            out_r.append(r * base_r - i * base_i)
            out_i.append(r * base_i + i * base_r)
        return out_r, out_i

    pw_r, pw_i = powers(bar_r, bar_i, T)
    pw_r = jnp.stack([jnp.ones_like(bar_r)] + pw_r)
    pw_i = jnp.stack([jnp.zeros_like(bar_i)] + pw_i)

    lb_r = pw_r[:T, :, :, None] * bb_r - pw_i[:T, :, :, None] * bb_i
    lb_i = pw_r[:T, :, :, None] * bb_i + pw_i[:T, :, :, None] * bb_r
    cl_r = c_re[None] * pw_r[:, :, None, :] - c_im[None] * pw_i[:, :, None, :]
    cl_i = c_re[None] * pw_i[:, :, None, :] + c_im[None] * pw_r[:, :, None, :]
    lb_t = jnp.stack([lb_r, lb_i], axis=1).transpose(0, 1, 4, 2, 3).reshape(T, 2, H, G * P)
    cl_t = jnp.stack([cl_r, -cl_i], axis=1).transpose(0, 1, 4, 2, 3).reshape(T + 1, 2, P, G * H)

    a_r, a_i = powers(pw_r[T], pw_i[T], F32_SUBLANES)
    tiled = lambda a: jnp.stack(a).reshape(len(a), SSM_TILES, STATE_COLS)
    a_lin = jnp.stack([tiled(a_r), tiled(a_i)]).transpose(2, 0, 1, 3)
    shifts = [1 << k for k in range(F32_SUBLANES.bit_length() - 1)]
    a_log = jnp.stack([tiled([a_r[s - 1] for s in shifts]),
                       tiled([a_i[s - 1] for s in shifts])]).transpose(2, 1, 0, 3)
    row = jnp.arange(F32_SUBLANES)[None, None, None, :, None]
    keep = row >= jnp.asarray(shifts)[None, :, None, None, None]
    a_log = jnp.where(keep, a_log[:, :, :, None, :], 0.0)
    return lb_t, cl_t, a_log, a_lin


def _ssm_kernel(u_ref, lb_ref, cl_ref, alog_ref, alin_ref, d_ref, y_ref, m_ref, ws_ref, wo_ref,
                rows_ref, *, seq_len, n_seq):
    T, H, P = SSM_CHUNK, SSM_GROUP_DIM, SSM_STATE
    n_chunks = seq_len // T
    iota = lambda shape, d: lax.broadcasted_iota(jnp.int32, shape, d)
    h_bits, p_bits = H.bit_length() - 1, P.bit_length() - 1

    @pl.when(pl.program_id(1) == 0)
    def _():
        spread_h = ((iota((LANES, H), 0) & (H - 1)) == iota((LANES, H), 1)).astype(BF16)
        spread_p = ((iota((STATE_COLS, P), 0) & (P - 1)) == iota((STATE_COLS, P), 1)).astype(BF16)
        same_hp = (iota((LANES, STATE_COLS), 0) >> h_bits) == (iota((LANES, STATE_COLS), 1) >> p_bits)
        same_ph = (iota((STATE_COLS, LANES), 0) >> p_bits) == (iota((STATE_COLS, LANES), 1) >> h_bits)

        def expand(spread, coeff, same):
            full = jnp.dot(spread, coeff.astype(BF16), preferred_element_type=F32)
            return jnp.where(same, full, 0.0).astype(BF16)

        rows = lambda j: slice(j * LANES, (j + 1) * LANES)
        state_out = lambda d: jnp.concatenate(
            [expand(spread_p, cl_ref[d, ri], same_ph) for ri in range(2)], axis=0)
        for j in range(T):
            ws_ref[rows(j), :] = jnp.concatenate(
                [expand(spread_h, lb_ref[T - 1 - j, ri], same_hp) for ri in range(2)], axis=1)
        for i in range(T):
            wo_ref[:, rows(i)] = state_out(i + 1)
        c_out = state_out(0)
        for d in range(T):
            block = jnp.dot(ws_ref[rows(T - 1 - d), :], c_out,
                            preferred_element_type=F32).astype(BF16)
            for j in range(T - d):
                m_ref[rows(j), rows(j + d)] = block
        for j in range(T):
            for i in range(j):
                m_ref[rows(j), rows(i)] = jnp.zeros((LANES, LANES), BF16)

    pieces = [u_ref[pl.ds(b * seq_len + i, n_chunks, stride=T), :]
              for b in range(n_seq) for i in range(T)]
    x = jnp.concatenate(
        [jnp.concatenate(pieces[b * T:(b + 1) * T], axis=1) for b in range(n_seq)], axis=0)
    x_lo = x.astype(BF16)
    groups = [slice(c, c + MXU_COLS) for c in range(0, STATE_COLS, MXU_COLS)]
    im_of = lambda cols: slice(STATE_COLS + cols.start, STATE_COLS + cols.stop)
    increments = [(jnp.dot(x_lo, ws_ref[:, cols], preferred_element_type=F32),
                   jnp.dot(x_lo, ws_ref[:, im_of(cols)], preferred_element_type=F32))
                  for cols in groups]
    y = jnp.concatenate(
        [jnp.dot(x_lo[:, :hi], m_ref[:hi, hi - MXU_COLS:hi], preferred_element_type=F32)
         for hi in range(MXU_COLS, T * LANES + 1, MXU_COLS)], axis=1)
    y = y + x * jnp.concatenate([d_ref[...]] * T, axis=1)

    sub = F32_SUBLANES
    first_row = iota((sub, MXU_COLS), 0) == 0
    cmul = lambda a_re, a_im, b_re, b_im: (a_re * b_re - a_im * b_im, a_re * b_im + a_im * b_re)
    tiles_per_seq = n_chunks // sub

    for cols, (re, im) in zip(groups, increments):
        log_mul = [(alog_ref[0, k, 0, :, cols], alog_ref[0, k, 1, :, cols])
                   for k in range(sub.bit_length() - 1)]
        lin_mul = (alin_ref[0, 0, :, cols], alin_ref[0, 1, :, cols])
        prev_re, prev_im = [], []
        for b in range(n_seq):
            carry = (jnp.zeros((1, MXU_COLS), F32),) * 2
            for g in range(tiles_per_seq):
                rows = slice((b * tiles_per_seq + g) * sub, (b * tiles_per_seq + g + 1) * sub)
                t_re, t_im = re[rows], im[rows]
                for k, mul in enumerate(log_mul):
                    d_re, d_im = cmul(*mul, pltpu.roll(t_re, 1 << k, axis=0),
                                      pltpu.roll(t_im, 1 << k, axis=0))
                    t_re, t_im = t_re + d_re, t_im + d_im
                if g > 0:
                    d_re, d_im = cmul(*lin_mul, *carry)
                    t_re, t_im = t_re + d_re, t_im + d_im
                prev_re.append(jnp.where(first_row, carry[0], pltpu.roll(t_re, 1, axis=0)))
                prev_im.append(jnp.where(first_row, carry[1], pltpu.roll(t_im, 1, axis=0)))
                carry = (t_re[sub - 1:sub], t_im[sub - 1:sub])
        prev_re, prev_im = (jnp.concatenate(a, axis=0).astype(BF16) for a in (prev_re, prev_im))
        y = (y + jnp.dot(prev_re, wo_ref[cols, :], preferred_element_type=F32)
             + jnp.dot(prev_im, wo_ref[im_of(cols), :], preferred_element_type=F32))
    for b in range(n_seq):
        for i in range(T):
            rows_ref[pl.ds(b * seq_len + i, n_chunks, stride=T), :] = (
                y[b * n_chunks:(b + 1) * n_chunks, i * LANES:(i + 1) * LANES])
    y_ref[...] = rows_ref[...].astype(BF16)


def _s5_ssm(u, tables, d_skip, seq_len):
    lb_t, cl_t, a_log, a_lin = tables
    T, H, P = SSM_CHUNK, SSM_GROUP_DIM, SSM_STATE
    assert seq_len % (T * F32_SUBLANES) == 0
    bsz = u.shape[0] // seq_len
    n_seq = max(n for n in (4, 2, 1) if bsz % n == 0)
    io_spec = pl.BlockSpec((n_seq * seq_len, LANES), lambda q, b: (b, q))
    square = pltpu.VMEM((T * LANES, T * LANES), BF16)
    return pl.pallas_call(
        functools.partial(_ssm_kernel, seq_len=seq_len, n_seq=n_seq),
        grid=(SSM_TILES, bsz // n_seq),
        in_specs=[io_spec,
                  pl.BlockSpec((T, 2, H, STATE_COLS), lambda q, b: (0, 0, 0, q)),
                  pl.BlockSpec((T + 1, 2, P, LANES), lambda q, b: (0, 0, 0, q)),
                  pl.BlockSpec((1,) + a_log.shape[1:], lambda q, b: (q, 0, 0, 0, 0)),
                  pl.BlockSpec((1,) + a_lin.shape[1:], lambda q, b: (q, 0, 0, 0)),
                  pl.BlockSpec((1, LANES), lambda q, b: (0, q))],
        out_specs=io_spec,
        out_shape=jax.ShapeDtypeStruct(u.shape, BF16),
        scratch_shapes=[square, square, square, pltpu.VMEM((n_seq * seq_len, LANES), F32)],
        compiler_params=pltpu.CompilerParams(dimension_semantics=("arbitrary", "arbitrary"),
                                             vmem_limit_bytes=VMEM_LIMIT_BYTES),
        name="s5_ssm",
    )(u, lb_t, cl_t, a_log, a_lin, d_skip)


def _out_proj_kernel(x_ref, ma_ref, y_ref, zs_ref, wg32_ref, bg_ref, w32_ref, g_ref, o_ref,
                     wg_ref, w_ref, *, final_norm):
    @pl.when(pl.program_id(0) == 0)
    def _():
        wg_ref[...] = wg32_ref[...].astype(BF16)
        w_ref[...] = w32_ref[...].astype(BF16)

    y = y_ref[...].astype(F32)
    y = 0.5 * y * (1.0 + jnp.tanh(math.sqrt(2.0 / math.pi) * (y + 0.044715 * (y * y * y))))
    gate = jnp.dot(y.astype(BF16), wg_ref[...], preferred_element_type=F32) + bg_ref[...]
    z = zs_ref[...].astype(F32)
    mixed_ssm = y / (1.0 + jnp.exp(-gate)) * (z / (1.0 + jnp.exp(-z)))
    r = (x_ref[...]
         + jnp.dot(ma_ref[...], w_ref[:ATTN_WIDTH, :], preferred_element_type=F32)
         + jnp.dot(mixed_ssm.astype(BF16), w_ref[ATTN_WIDTH:, :], preferred_element_type=F32))
    if final_norm:
        ms = jnp.mean(r * r, axis=-1, keepdims=True)
        r = r * lax.rsqrt(ms + NORM_EPS) * g_ref[...]
    o_ref[...] = r


def _out_proj(x2, mixed_attn, y_ssm, z_ssm, w_glu, b_glu, w_out, gain, tm, final_norm):
    rows = x2.shape[0]
    row_spec = lambda width: pl.BlockSpec((tm, width), lambda i: (i, 0))
    return pl.pallas_call(
        functools.partial(_out_proj_kernel, final_norm=final_norm),
        grid=(rows // tm,),
        in_specs=[row_spec(D_MODEL), row_spec(ATTN_WIDTH), row_spec(SSM_WIDTH), row_spec(SSM_WIDTH),
                  _resident((SSM_WIDTH, SSM_WIDTH)), _resident((1, SSM_WIDTH)),
                  _resident((ATTN_WIDTH + SSM_WIDTH, D_MODEL)), _resident((1, D_MODEL))],
        out_specs=row_spec(D_MODEL),
        out_shape=jax.ShapeDtypeStruct((rows, D_MODEL), F32),
        scratch_shapes=[pltpu.VMEM((SSM_WIDTH, SSM_WIDTH), BF16),
                        pltpu.VMEM((ATTN_WIDTH + SSM_WIDTH, D_MODEL), BF16)],
        compiler_params=pltpu.CompilerParams(dimension_semantics=("arbitrary",),
                                             vmem_limit_bytes=VMEM_LIMIT_BYTES),
        name="out_proj",
    )(x2, mixed_attn, y_ssm, z_ssm, w_glu, b_glu, w_out, gain)


def _rotary_tables(seq_len):
    half = HEAD_DIM // 2
    inv_freq = 1.0 / (ROPE_THETA ** (jnp.arange(half, dtype=F32) / half))
    ang = jnp.arange(seq_len, dtype=F32)[:, None] * inv_freq[None, :]
    cos, sin, zero = jnp.cos(ang), jnp.sin(ang), jnp.zeros_like(ang)
    reps = LANES // HEAD_DIM
    cos_t = jnp.tile(jnp.concatenate([cos, cos], axis=1), (1, reps))
    sin_hi = jnp.tile(jnp.concatenate([zero, sin], axis=1), (1, reps))
    sin_lo = jnp.tile(jnp.concatenate([-sin, zero], axis=1), (1, reps))
    return cos_t, sin_hi, sin_lo


def kernel(x, norm_gain, w_in, w_out, lam_re, lam_im, b_re, b_im, c_re, c_im,
           d_skip, log_dt, w_glu, b_glu, final_gain):
    bsz, seq_len, _ = x.shape
    depth = norm_gain.shape[0]
    assert seq_len % MOBA_BLOCK == 0 and seq_len // MOBA_BLOCK > MOBA_TOP_K
    assert seq_len % SSM_CHUNK == 0
    tm = min(1024, seq_len)
    cos, sin_hi, sin_lo = _rotary_tables(seq_len)

    x2 = x.reshape(bsz * seq_len, D_MODEL)
    for layer in range(depth):
        tables = _ssm_tables(lam_re[layer], lam_im[layer], b_re[layer], b_im[layer],
                             c_re[layer], c_im[layer], log_dt[layer])
        q, k, v, z_attn, u, z_ssm = _in_proj(
            x2, norm_gain[layer].reshape(1, D_MODEL), w_in[layer], cos, sin_hi, sin_lo, seq_len, tm)
        to_seq = lambda t: t.reshape(bsz, seq_len, ATTN_WIDTH)
        mixed_attn = _moba_attention(to_seq(q), to_seq(k), to_seq(v), to_seq(z_attn))
        y_ssm = _s5_ssm(u, tables, d_skip[layer].reshape(1, SSM_WIDTH), seq_len)
        x2 = _out_proj(x2, mixed_attn.reshape(bsz * seq_len, ATTN_WIDTH), y_ssm, z_ssm,
                       w_glu[layer], b_glu[layer].reshape(1, SSM_WIDTH),
                       w_out[layer], final_gain.reshape(1, D_MODEL), tm,
                       final_norm=layer == depth - 1)
    return x2.reshape(bsz, seq_len, D_MODEL)
```

```python
import functools
import math

import jax
import jax.numpy as jnp
from jax import lax
from jax.experimental import pallas as pl
from jax.experimental.pallas import tpu as pltpu

F32 = jnp.float32
BF16 = jnp.bfloat16

D_MODEL = 1024
HEAD_DIM = 64
ATTN_HEADS = 8
ATTN_WIDTH = ATTN_HEADS * HEAD_DIM
MOBA_BLOCK = 256
MOBA_TOP_K = 3
ROPE_THETA = 10000.0
SSM_GROUP_DIM = 16
SSM_GROUPS = 32
SSM_WIDTH = SSM_GROUPS * SSM_GROUP_DIM
SSM_STATE = 64
IN_PROJ_WIDTH = 4 * ATTN_WIDTH + 2 * SSM_WIDTH
NORM_EPS = 1e-6
NEG_INF = -1e30

LANES = 128
SSM_CHUNK = 8
GROUPS_PER_TILE = LANES // SSM_GROUP_DIM
SSM_TILES = SSM_WIDTH // LANES
TILE_COLS = SSM_CHUNK * LANES
STATE_COLS = GROUPS_PER_TILE * SSM_STATE
VMEM_LIMIT_BYTES = 56 * 1024 * 1024
MXU_COLS = 256
OUT_PROJ_ROWS = 256
F32_SUBLANES = 8
BF16_SUBLANES = 16
VALUE_ROWS = HEAD_DIM + BF16_SUBLANES
QK_SCALE = math.log2(math.e) / math.sqrt(HEAD_DIM)


def _resident(shape):
    zeros = (0,) * len(shape)
    return pl.BlockSpec(shape, lambda *_: zeros, pipeline_mode=pl.Buffered(1))


def _in_proj_kernel(x_ref, g_ref, w32_ref, cos_ref, sin_hi_ref, sin_lo_ref,
                    q_ref, k_ref, v_ref, za_ref, u_ref, zs_ref, w_ref):
    @pl.when(pl.program_id(0) == 0)
    def _():
        for c in range(0, IN_PROJ_WIDTH, ATTN_WIDTH):
            w_ref[:, c:c + ATTN_WIDTH] = w32_ref[:, c:c + ATTN_WIDTH].astype(BF16)

    x = x_ref[...]
    ms = jnp.mean(x * x, axis=-1, keepdims=True)
    h = (x * lax.rsqrt(ms + NORM_EPS) * g_ref[...]).astype(BF16)

    def section(idx):
        return jnp.dot(h, w_ref[:, idx * ATTN_WIDTH:(idx + 1) * ATTN_WIDTH],
                       preferred_element_type=F32)

    cos = cos_ref[...]
    sin_hi = sin_hi_ref[...]
    sin_lo = sin_lo_ref[...]

    def rotary(t):
        return (t * cos + pltpu.roll(t, HEAD_DIM // 2, axis=1) * sin_hi
                + pltpu.roll(t, LANES - HEAD_DIM // 2, axis=1) * sin_lo)

    q = section(0)
    k = section(1)
    for c in range(ATTN_WIDTH // LANES):
        sl = slice(c * LANES, (c + 1) * LANES)
        q_ref[:, sl] = (rotary(q[:, sl]) * QK_SCALE).astype(BF16)
        k_ref[:, sl] = rotary(k[:, sl]).astype(BF16)
    v_ref[...] = section(2).astype(BF16)
    za_ref[...] = section(3).astype(BF16)
    u_ref[...] = section(4)
    zs_ref[...] = section(5).astype(BF16)


def _in_proj(x2, gain, w_in, cos, sin_hi, sin_lo, seq_len, tm):
    rows = x2.shape[0]
    pos_blocks = seq_len // tm
    row_spec = lambda width: pl.BlockSpec((tm, width), lambda i: (i, 0))
    tab_spec = pl.BlockSpec((tm, LANES), lambda i: (i % pos_blocks, 0))
    out_bf16 = jax.ShapeDtypeStruct((rows, ATTN_WIDTH), BF16)
    out_f32 = jax.ShapeDtypeStruct((rows, ATTN_WIDTH), F32)
    return pl.pallas_call(
        _in_proj_kernel,
        grid=(rows // tm,),
        in_specs=[row_spec(D_MODEL), _resident((1, D_MODEL)), _resident((D_MODEL, IN_PROJ_WIDTH)),
                  tab_spec, tab_spec, tab_spec],
        out_specs=[row_spec(ATTN_WIDTH)] * 6,
        out_shape=[out_bf16, out_bf16, out_bf16, out_bf16, out_f32, out_bf16],
        scratch_shapes=[pltpu.VMEM((D_MODEL, IN_PROJ_WIDTH), BF16)],
        compiler_params=pltpu.CompilerParams(dimension_semantics=("arbitrary",),
                                             vmem_limit_bytes=VMEM_LIMIT_BYTES),
        name="in_proj",
    )(x2, gain, w_in, cos, sin_hi, sin_lo)


def _attn_kernel(q0_ref, k0_ref, v0_ref, qn_ref, kn_ref, vn_ref, z_ref, o_ref,
                 kx_ref, vx_ref, qx_ref, sa_ref, sb_ref, m_ref, acc_ref, *, n_blocks):
    unit = pl.program_id(0)
    blk = MOBA_BLOCK
    heads = LANES // HEAD_DIM
    n_pairs = n_blocks // 2
    cur = lax.rem(unit, 2)
    nxt = 1 - cur

    head_rows = lambda h: slice(h * HEAD_DIM, (h + 1) * HEAD_DIM)
    spare_base = lambda h: ((h + 1) % heads) * HEAD_DIM
    iota = lambda shape, d: lax.broadcasted_iota(jnp.int32, shape, d)
    key = iota((blk, blk), 0)
    query = iota((blk, blk), 1)
    col_max = lambda s: jnp.max(s, axis=0, keepdims=True)

    def prepare_items(q_ref, k_ref, v_ref, slot):
        kmean = {}

        def key_block(j):
            lane = iota((blk, LANES), 1)
            sum_rows = (iota((VALUE_ROWS - HEAD_DIM, blk), 0) == 0).astype(F32)
            rows = slice(j * blk, (j + 1) * blk)
            kj = k_ref[0, rows, :]
            kmean[j] = jnp.mean(kj.astype(F32), axis=0, keepdims=True)
            vt = v_ref[0, rows, :].astype(F32).T
            for h in range(heads):
                in_head = (lane >= h * HEAD_DIM) & (lane < (h + 1) * HEAD_DIM)
                tag = jnp.where(lane == spare_base(h) + j, 1.0, 0.0).astype(BF16)
                kx_ref[slot, h, j] = jnp.where(in_head, kj, tag)
                vx_ref[slot, j, h] = jnp.concatenate([vt[head_rows(h)], sum_rows],
                                                     axis=0).astype(BF16)

        def query_block(qi):
            if "all" not in kmean:
                kmean["all"] = jnp.concatenate([kmean[j] for j in range(n_blocks)],
                                               axis=0).astype(BF16)
            blk_row = iota((n_blocks, blk), 0)
            zero_rows = lambda n: jnp.zeros((n, blk), F32)
            in_head_order = lambda h, own, other: [own, other] if h == 0 else [other, own]
            qt = q_ref[0, qi * blk:(qi + 1) * blk, :].astype(F32).T
            past = blk_row < qi
            for h in range(heads):
                q_rows = qt[head_rows(h)]
                q_only = jnp.concatenate(in_head_order(h, q_rows, zero_rows(HEAD_DIM)), axis=0)
                gate = jnp.dot(kmean["all"], q_only.astype(BF16), preferred_element_type=F32)
                gate = jnp.where(past, gate, NEG_INF)
                beaten_by = jnp.zeros((n_blocks, blk), jnp.int32)
                for j in range(n_blocks):
                    gj = gate[j:j + 1, :]
                    wins = (gj > gate) | ((gj == gate) & (j < blk_row))
                    beaten_by = beaten_by + wins.astype(jnp.int32)
                keep = (past & (beaten_by < MOBA_TOP_K)) | (blk_row == qi)
                bias = jnp.where(keep, 0.0, NEG_INF)
                spare = jnp.concatenate([bias, zero_rows(HEAD_DIM - n_blocks)], axis=0)
                qx_ref[slot, qi, h] = jnp.concatenate(in_head_order(h, q_rows, spare),
                                                      axis=0).astype(BF16)

        return ([functools.partial(key_block, j) for j in range(n_blocks)]
                + [functools.partial(query_block, qi) for qi in range(n_blocks)])

    def visible_blocks(pair):
        tiles = (pair, n_blocks - 1 - pair)
        return ([(t, tiles[t], True) for t in range(2)]
                + [(t, j, False) for t in range(2) for j in range(tiles[t])])

    traced = lambda t: t + jnp.minimum(unit, 0)

    def stage_items(pair, s_ref, m, slot):
        def block(t, j, own):
            qi = (pair, n_blocks - 1 - pair)[t]
            for h in range(heads):
                s = jnp.dot(kx_ref[slot, h, j], qx_ref[slot, qi, h], preferred_element_type=F32)
                if own:
                    s = jnp.where(key <= query, s, NEG_INF)
                s_ref[traced(t), h, j] = s
                m[t, h] = col_max(s) if own else jnp.maximum(m[t, h], col_max(s))
        return [functools.partial(block, *b) for b in visible_blocks(pair)]

    def finish_items(pair, s_ref, m, acc_ref, slot):
        def block(t, j, own):
            for h in range(heads):
                p = jnp.exp2(s_ref[traced(t), h, j] - m[t, h]).astype(BF16)
                pv = jnp.dot(vx_ref[slot, j, h], p, preferred_element_type=F32)
                if own:
                    acc_ref[traced(t), h] = pv
                else:
                    acc_ref[traced(t), h] += pv

        def write_rows():
            for t, qi in enumerate((pair, n_blocks - 1 - pair)):
                rows = slice(qi * blk, (qi + 1) * blk)
                acc = [acc_ref[t, h] for h in range(heads)]
                o_t = jnp.concatenate([a[:HEAD_DIM] / a[HEAD_DIM:HEAD_DIM + 1] for a in acc],
                                      axis=0)
                z = z_ref[0, rows, :].astype(F32)
                silu = 0.5 * z * (1.0 + jnp.tanh(0.5 * z))
                o_ref[0, rows, :] = (o_t.T * silu).astype(BF16)

        return [functools.partial(block, *b) for b in visible_blocks(pair)] + [write_rows]

    def alternate(*item_lists):
        for group in zip(*item_lists):
            for item in group:
                item()
        for items in item_lists:
            for item in items[min(map(len, item_lists)):]:
                item()

    m_keys = [(t, h) for t in range(2) for h in range(heads)]

    @pl.when(unit == 0)
    def _():
        m0 = {}
        alternate(prepare_items(q0_ref, k0_ref, v0_ref, 0))
        alternate(stage_items(0, sa_ref, m0, 0))
        for t, h in m_keys:
            m_ref[t, h] = m0[t, h]

    m = {key_: m_ref[key_] for key_ in m_keys}
    for pair in range(n_pairs):
        s_ref, s_next = (sa_ref, sb_ref) if pair % 2 == 0 else (sb_ref, sa_ref)
        m_next = {}
        lists = [finish_items(pair, s_ref, m, acc_ref.at[pair % 2], cur)]
        if pair + 1 < n_pairs:
            lists.append(stage_items(pair + 1, s_next, m_next, cur))
        else:
            lists.append(stage_items(0, s_next, m_next, nxt))
        if pair == n_pairs - 2:
            lists.append(prepare_items(qn_ref, kn_ref, vn_ref, nxt))
        alternate(*lists)
        m = m_next
    for t, h in m_keys:
        m_ref[t, h] = m[t, h]


def _moba_attention(q, k, v, z_attn):
    bsz, seq_len, _ = q.shape
    n_blocks = seq_len // MOBA_BLOCK
    assert n_blocks % 4 == 0 and n_blocks <= HEAD_DIM
    head_pairs = ATTN_WIDTH // LANES
    heads = LANES // HEAD_DIM
    assert heads == 2, "the block bias rides in the other head's half of the 128 lanes"
    n_units = bsz * head_pairs
    block = (1, seq_len, LANES)
    unit_index = lambda u: (u // head_pairs, 0, u % head_pairs)
    first_spec = pl.BlockSpec(block, lambda u: (0, 0, 0))
    next_spec = pl.BlockSpec(block, lambda u: unit_index(jnp.minimum(u + 1, n_units - 1)))
    unit_spec = pl.BlockSpec(block, unit_index)
    scores = pltpu.VMEM((2, heads, n_blocks, MOBA_BLOCK, MOBA_BLOCK), F32)
    return pl.pallas_call(
        functools.partial(_attn_kernel, n_blocks=n_blocks),
        grid=(n_units,),
        in_specs=[first_spec] * 3 + [next_spec] * 3 + [unit_spec],
        out_specs=unit_spec,
        out_shape=jax.ShapeDtypeStruct((bsz, seq_len, ATTN_WIDTH), BF16),
        scratch_shapes=[pltpu.VMEM((2, heads, n_blocks, MOBA_BLOCK, LANES), BF16),
                        pltpu.VMEM((2, n_blocks, heads, VALUE_ROWS, MOBA_BLOCK), BF16),
                        pltpu.VMEM((2, n_blocks, heads, LANES, MOBA_BLOCK), BF16),
                        scores, scores,
                        pltpu.VMEM((2, heads, 1, MOBA_BLOCK), F32),
                        pltpu.VMEM((2, 2, heads, VALUE_ROWS, MOBA_BLOCK), F32)],
        compiler_params=pltpu.CompilerParams(dimension_semantics=("arbitrary",),
                                             vmem_limit_bytes=VMEM_LIMIT_BYTES),
        name="moba_attn",
    )(q, k, v, q, k, v, z_attn)


def _ssm_tables(lam_re, lam_im, b_re, b_im, c_re, c_im, log_dt):
    T, G, P, H = SSM_CHUNK, SSM_GROUPS, SSM_STATE, SSM_GROUP_DIM
    dt = jnp.exp(log_dt.astype(F32))[:, None]
    lam_r, lam_i = lam_re.astype(F32), lam_im.astype(F32)
    mag = jnp.exp(lam_r * dt)
    bar_r, bar_i = mag * jnp.cos(lam_i * dt), mag * jnp.sin(lam_i * dt)
    den = lam_r * lam_r + lam_i * lam_i
    f_r = ((bar_r - 1.0) * lam_r + bar_i * lam_i) / den
    f_i = (bar_i * lam_r - (bar_r - 1.0) * lam_i) / den
    bb_r = f_r[..., None] * b_re - f_i[..., None] * b_im
    bb_i = f_r[..., None] * b_im + f_i[..., None] * b_re

    def powers(base_r, base_i, count):
        out_r, out_i = [base_r], [base_i]
        for _ in range(count - 1):
            r, i = out_r[-1], out_i[-1]
            out_r.append(r * base_r - i * base_i)
            out_i.append(r * base_i + i * base_r)
        return out_r, out_i

    pw_r, pw_i = powers(bar_r, bar_i, T)
    pw_r = jnp.stack([jnp.ones_like(bar_r)] + pw_r)
    pw_i = jnp.stack([jnp.zeros_like(bar_i)] + pw_i)

    lb_r = pw_r[:T, :, :, None] * bb_r - pw_i[:T, :, :, None] * bb_i
    lb_i = pw_r[:T, :, :, None] * bb_i + pw_i[:T, :, :, None] * bb_r
    cl_r = c_re[None] * pw_r[:, :, None, :] - c_im[None] * pw_i[:, :, None, :]
    cl_i = c_re[None] * pw_i[:, :, None, :] + c_im[None] * pw_r[:, :, None, :]
    lb_t = jnp.stack([lb_r, lb_i], axis=1).transpose(0, 1, 4, 2, 3).reshape(T, 2, H, G * P)
    cl_t = jnp.stack([cl_r, -cl_i], axis=1).transpose(0, 1, 4, 2, 3).reshape(T + 1, 2, P, G * H)

    a_r, a_i = powers(pw_r[T], pw_i[T], F32_SUBLANES)
    tiled = lambda a: jnp.stack(a).reshape(len(a), SSM_TILES, STATE_COLS)
    a_lin = jnp.stack([tiled(a_r), tiled(a_i)]).transpose(2, 0, 1, 3)
    shifts = [1 << k for k in range(F32_SUBLANES.bit_length() - 1)]
    a_log = jnp.stack([tiled([a_r[s - 1] for s in shifts]),
                       tiled([a_i[s - 1] for s in shifts])]).transpose(2, 1, 0, 3)
    row = jnp.arange(F32_SUBLANES)[None, None, None, :, None]
    keep = row >= jnp.asarray(shifts)[None, :, None, None, None]
    a_log = jnp.where(keep, a_log[:, :, :, None, :], 0.0)
    return lb_t, cl_t, a_log, a_lin


def _ssm_kernel(u_ref, lb_ref, cl_ref, alog_ref, alin_ref, d_ref, y_ref, m_ref, ws_ref, wo_ref, *,
                seq_len, n_seq):
    T, H, P = SSM_CHUNK, SSM_GROUP_DIM, SSM_STATE
    n_chunks = seq_len // T
    iota = lambda shape, d: lax.broadcasted_iota(jnp.int32, shape, d)
    h_bits, p_bits = H.bit_length() - 1, P.bit_length() - 1

    @pl.when(pl.program_id(1) == 0)
    def _():
        spread_h = ((iota((LANES, H), 0) & (H - 1)) == iota((LANES, H), 1)).astype(BF16)
        spread_p = ((iota((STATE_COLS, P), 0) & (P - 1)) == iota((STATE_COLS, P), 1)).astype(BF16)
        same_hp = (iota((LANES, STATE_COLS), 0) >> h_bits) == (iota((LANES, STATE_COLS), 1) >> p_bits)
        same_ph = (iota((STATE_COLS, LANES), 0) >> p_bits) == (iota((STATE_COLS, LANES), 1) >> h_bits)

        def expand(spread, coeff, same):
            full = jnp.dot(spread, coeff.astype(BF16), preferred_element_type=F32)
            return jnp.where(same, full, 0.0).astype(BF16)

        rows = lambda j: slice(j * LANES, (j + 1) * LANES)
        state_out = lambda d: jnp.concatenate(
            [expand(spread_p, cl_ref[d, ri], same_ph) for ri in range(2)], axis=0)
        for j in range(T):
            ws_ref[rows(j), :] = jnp.concatenate(
                [expand(spread_h, lb_ref[T - 1 - j, ri], same_hp) for ri in range(2)], axis=1)
        for i in range(T):
            wo_ref[:, rows(i)] = state_out(i + 1)
        c_out = state_out(0)
        for d in range(T):
            block = jnp.dot(ws_ref[rows(T - 1 - d), :], c_out,
                            preferred_element_type=F32).astype(BF16)
            for j in range(T - d):
                m_ref[rows(j), rows(j + d)] = block
        for j in range(T):
            for i in range(j):
                m_ref[rows(j), rows(i)] = jnp.zeros((LANES, LANES), BF16)

    pieces = [u_ref[pl.ds(b * seq_len + i, n_chunks, stride=T), :]
              for b in range(n_seq) for i in range(T)]
    x = jnp.concatenate(
        [jnp.concatenate(pieces[b * T:(b + 1) * T], axis=1) for b in range(n_seq)], axis=0)
    x_lo = x.astype(BF16)
    groups = [slice(c, c + MXU_COLS) for c in range(0, STATE_COLS, MXU_COLS)]
    im_of = lambda cols: slice(STATE_COLS + cols.start, STATE_COLS + cols.stop)
    increments = [(jnp.dot(x_lo, ws_ref[:, cols], preferred_element_type=F32),
                   jnp.dot(x_lo, ws_ref[:, im_of(cols)], preferred_element_type=F32))
                  for cols in groups]
    y = jnp.concatenate(
        [jnp.dot(x_lo[:, :hi], m_ref[:hi, hi - MXU_COLS:hi], preferred_element_type=F32)
         for hi in range(MXU_COLS, T * LANES + 1, MXU_COLS)], axis=1)
    y = y + x * jnp.concatenate([d_ref[...]] * T, axis=1)

    sub = F32_SUBLANES
    first_row = iota((sub, MXU_COLS), 0) == 0
    cmul = lambda a_re, a_im, b_re, b_im: (a_re * b_re - a_im * b_im, a_re * b_im + a_im * b_re)
    tiles_per_seq = n_chunks // sub

    for cols, (re, im) in zip(groups, increments):
        log_mul = [(alog_ref[0, k, 0, :, cols], alog_ref[0, k, 1, :, cols])
                   for k in range(sub.bit_length() - 1)]
        lin_mul = (alin_ref[0, 0, :, cols], alin_ref[0, 1, :, cols])
        prev_re, prev_im = [], []
        for b in range(n_seq):
            carry = (jnp.zeros((1, MXU_COLS), F32),) * 2
            for g in range(tiles_per_seq):
                rows = slice((b * tiles_per_seq + g) * sub, (b * tiles_per_seq + g + 1) * sub)
                t_re, t_im = re[rows], im[rows]
                for k, mul in enumerate(log_mul):
                    d_re, d_im = cmul(*mul, pltpu.roll(t_re, 1 << k, axis=0),
                                      pltpu.roll(t_im, 1 << k, axis=0))
                    t_re, t_im = t_re + d_re, t_im + d_im
                if g > 0:
                    d_re, d_im = cmul(*lin_mul, *carry)
                    t_re, t_im = t_re + d_re, t_im + d_im
                prev_re.append(jnp.where(first_row, carry[0], pltpu.roll(t_re, 1, axis=0)))
                prev_im.append(jnp.where(first_row, carry[1], pltpu.roll(t_im, 1, axis=0)))
                carry = (t_re[sub - 1:sub], t_im[sub - 1:sub])
        prev_re, prev_im = (jnp.concatenate(a, axis=0).astype(BF16) for a in (prev_re, prev_im))
        y = (y + jnp.dot(prev_re, wo_ref[cols, :], preferred_element_type=F32)
             + jnp.dot(prev_im, wo_ref[im_of(cols), :], preferred_element_type=F32))
    for b in range(n_seq):
        for i in range(T):
            y_ref[pl.ds(b * seq_len + i, n_chunks, stride=T), :] = (
                y[b * n_chunks:(b + 1) * n_chunks, i * LANES:(i + 1) * LANES])


def _s5_ssm(u, tables, d_skip, seq_len):
    lb_t, cl_t, a_log, a_lin = tables
    T, H, P = SSM_CHUNK, SSM_GROUP_DIM, SSM_STATE
    assert seq_len % (T * F32_SUBLANES) == 0
    bsz = u.shape[0] // seq_len
    n_seq = max(n for n in (4, 2, 1) if bsz % n == 0)
    io_spec = pl.BlockSpec((n_seq * seq_len, LANES), lambda q, b: (b, q))
    square = pltpu.VMEM((T * LANES, T * LANES), BF16)
    return pl.pallas_call(
        functools.partial(_ssm_kernel, seq_len=seq_len, n_seq=n_seq),
        grid=(SSM_TILES, bsz // n_seq),
        in_specs=[io_spec,
                  pl.BlockSpec((T, 2, H, STATE_COLS), lambda q, b: (0, 0, 0, q)),
                  pl.BlockSpec((T + 1, 2, P, LANES), lambda q, b: (0, 0, 0, q)),
                  pl.BlockSpec((1,) + a_log.shape[1:], lambda q, b: (q, 0, 0, 0, 0)),
                  pl.BlockSpec((1,) + a_lin.shape[1:], lambda q, b: (q, 0, 0, 0)),
                  pl.BlockSpec((1, LANES), lambda q, b: (0, q))],
        out_specs=io_spec,
        out_shape=jax.ShapeDtypeStruct(u.shape, F32),
        scratch_shapes=[square, square, square],
        compiler_params=pltpu.CompilerParams(dimension_semantics=("arbitrary", "arbitrary"),
                                             vmem_limit_bytes=VMEM_LIMIT_BYTES),
        name="s5_ssm",
    )(u, lb_t, cl_t, a_log, a_lin, d_skip)


def _out_proj_kernel(x_ref, ma_ref, y_ref, zs_ref, wg32_ref, bg_ref, w32_ref, g_ref, o_ref,
                     wg_ref, w_ref, *, final_norm):
    @pl.when(pl.program_id(0) == 0)
    def _():
        wg_ref[...] = wg32_ref[...].astype(BF16)
        w_ref[...] = w32_ref[...].astype(BF16)

    half_plus = lambda t: 1.0 + jnp.tanh(0.5 * t)
    for c in range(0, x_ref.shape[0], OUT_PROJ_ROWS):
        rows = slice(c, c + OUT_PROJ_ROWS)
        y = y_ref[rows, :]
        y = y * (1.0 + jnp.tanh(math.sqrt(2.0 / math.pi) * (y + 0.044715 * (y * y * y))))
        gate = jnp.dot((0.5 * y).astype(BF16), wg_ref[...], preferred_element_type=F32) + bg_ref[...]
        z = zs_ref[rows, :].astype(F32)
        mixed_ssm = (0.125 * y) * half_plus(gate) * (z * half_plus(z))
        mixed = jnp.concatenate([ma_ref[rows, :], mixed_ssm.astype(BF16)], axis=1)
        r = x_ref[rows, :] + jnp.dot(mixed, w_ref[...], preferred_element_type=F32)
        if final_norm:
            ms = jnp.mean(r * r, axis=-1, keepdims=True)
            r = r * lax.rsqrt(ms + NORM_EPS) * g_ref[...]
        o_ref[rows, :] = r


def _out_proj(x2, mixed_attn, y_ssm, z_ssm, w_glu, b_glu, w_out, gain, tm, final_norm):
    rows = x2.shape[0]
    row_spec = lambda width: pl.BlockSpec((tm, width), lambda i: (i, 0))
    return pl.pallas_call(
        functools.partial(_out_proj_kernel, final_norm=final_norm),
        grid=(rows // tm,),
        in_specs=[row_spec(D_MODEL), row_spec(ATTN_WIDTH), row_spec(SSM_WIDTH), row_spec(SSM_WIDTH),
                  _resident((SSM_WIDTH, SSM_WIDTH)), _resident((1, SSM_WIDTH)),
                  _resident((ATTN_WIDTH + SSM_WIDTH, D_MODEL)), _resident((1, D_MODEL))],
        out_specs=row_spec(D_MODEL),
        out_shape=jax.ShapeDtypeStruct((rows, D_MODEL), F32),
        scratch_shapes=[pltpu.VMEM((SSM_WIDTH, SSM_WIDTH), BF16),
                        pltpu.VMEM((ATTN_WIDTH + SSM_WIDTH, D_MODEL), BF16)],
        compiler_params=pltpu.CompilerParams(dimension_semantics=("arbitrary",),
                                             vmem_limit_bytes=VMEM_LIMIT_BYTES),
        name="out_proj",
    )(x2, mixed_attn, y_ssm, z_ssm, w_glu, b_glu, w_out, gain)


def _rotary_tables(seq_len):
    half = HEAD_DIM // 2
    inv_freq = 1.0 / (ROPE_THETA ** (jnp.arange(half, dtype=F32) / half))
    ang = jnp.arange(seq_len, dtype=F32)[:, None] * inv_freq[None, :]
    cos, sin, zero = jnp.cos(ang), jnp.sin(ang), jnp.zeros_like(ang)
    reps = LANES // HEAD_DIM
    cos_t = jnp.tile(jnp.concatenate([cos, cos], axis=1), (1, reps))
    sin_hi = jnp.tile(jnp.concatenate([zero, sin], axis=1), (1, reps))
    sin_lo = jnp.tile(jnp.concatenate([-sin, zero], axis=1), (1, reps))
    return cos_t, sin_hi, sin_lo


def kernel(x, norm_gain, w_in, w_out, lam_re, lam_im, b_re, b_im, c_re, c_im,
           d_skip, log_dt, w_glu, b_glu, final_gain):
    bsz, seq_len, _ = x.shape
    depth = norm_gain.shape[0]
    assert seq_len % MOBA_BLOCK == 0 and seq_len // MOBA_BLOCK > MOBA_TOP_K
    assert seq_len % SSM_CHUNK == 0
    tm = min(1024, seq_len)
    cos, sin_hi, sin_lo = _rotary_tables(seq_len)

    x2 = x.reshape(bsz * seq_len, D_MODEL)
    for layer in range(depth):
        tables = _ssm_tables(lam_re[layer], lam_im[layer], b_re[layer], b_im[layer],
                             c_re[layer], c_im[layer], log_dt[layer])
        q, k, v, z_attn, u, z_ssm = _in_proj(
            x2, norm_gain[layer].reshape(1, D_MODEL), w_in[layer], cos, sin_hi, sin_lo, seq_len, tm)
        to_seq = lambda t: t.reshape(bsz, seq_len, ATTN_WIDTH)
        mixed_attn = _moba_attention(to_seq(q), to_seq(k), to_seq(v), to_seq(z_attn))
        y_ssm = _s5_ssm(u, tables, d_skip[layer].reshape(1, SSM_WIDTH), seq_len)
        x2 = _out_proj(x2, mixed_attn.reshape(bsz * seq_len, ATTN_WIDTH), y_ssm, z_ssm,
                       w_glu[layer], b_glu[layer].reshape(1, SSM_WIDTH),
                       w_out[layer], final_gain.reshape(1, D_MODEL), tm,
                       final_norm=layer == depth - 1)
    return x2.reshape(bsz, seq_len, D_MODEL)
```

```python
import functools
import math

import jax
import jax.numpy as jnp
from jax import lax
from jax.experimental import pallas as pl
from jax.experimental.pallas import tpu as pltpu

F32 = jnp.float32
BF16 = jnp.bfloat16

D_MODEL = 1024
HEAD_DIM = 64
ATTN_HEADS = 8
ATTN_WIDTH = ATTN_HEADS * HEAD_DIM
MOBA_BLOCK = 256
MOBA_TOP_K = 3
ROPE_THETA = 10000.0
SSM_GROUP_DIM = 16
SSM_GROUPS = 32
SSM_WIDTH = SSM_GROUPS * SSM_GROUP_DIM
SSM_STATE = 64
IN_PROJ_WIDTH = 4 * ATTN_WIDTH + 2 * SSM_WIDTH
NORM_EPS = 1e-6
NEG_INF = -1e30

LANES = 128
SSM_CHUNK = 8
GROUPS_PER_TILE = LANES // SSM_GROUP_DIM
SSM_TILES = SSM_WIDTH // LANES
TILE_COLS = SSM_CHUNK * LANES
STATE_COLS = GROUPS_PER_TILE * SSM_STATE
VMEM_LIMIT_BYTES = 56 * 1024 * 1024
MXU_COLS = 256
OUT_PROJ_ROWS = 256
F32_SUBLANES = 8
BF16_SUBLANES = 16
VALUE_ROWS = HEAD_DIM + BF16_SUBLANES
QK_SCALE = math.log2(math.e) / math.sqrt(HEAD_DIM)


def _resident(shape):
    zeros = (0,) * len(shape)
    return pl.BlockSpec(shape, lambda *_: zeros, pipeline_mode=pl.Buffered(1))


def _in_proj_kernel(x_ref, g_ref, w32_ref, cos_ref, sin_hi_ref, sin_lo_ref,
                    q_ref, k_ref, v_ref, za_ref, u_ref, zs_ref, w_ref):
    @pl.when(pl.program_id(0) == 0)
    def _():
        for c in range(0, IN_PROJ_WIDTH, ATTN_WIDTH):
            w_ref[:, c:c + ATTN_WIDTH] = w32_ref[:, c:c + ATTN_WIDTH].astype(BF16)

    x = x_ref[...]
    ms = jnp.mean(x * x, axis=-1, keepdims=True)
    h = (x * lax.rsqrt(ms + NORM_EPS) * g_ref[...]).astype(BF16)

    def section(idx):
        return jnp.dot(h, w_ref[:, idx * ATTN_WIDTH:(idx + 1) * ATTN_WIDTH],
                       preferred_element_type=F32)

    cos = cos_ref[...]
    sin_hi = sin_hi_ref[...]
    sin_lo = sin_lo_ref[...]

    def rotary(t):
        return (t * cos + pltpu.roll(t, HEAD_DIM // 2, axis=1) * sin_hi
                + pltpu.roll(t, LANES - HEAD_DIM // 2, axis=1) * sin_lo)

    q = section(0)
    k = section(1)
    for c in range(ATTN_WIDTH // LANES):
        sl = slice(c * LANES, (c + 1) * LANES)
        q_ref[:, sl] = (rotary(q[:, sl]) * QK_SCALE).astype(BF16)
        k_ref[:, sl] = rotary(k[:, sl]).astype(BF16)
    v_ref[...] = section(2).astype(BF16)
    za_ref[...] = section(3).astype(BF16)
    u_ref[...] = section(4)
    zs_ref[...] = section(5).astype(BF16)


def _in_proj(x2, gain, w_in, cos, sin_hi, sin_lo, seq_len, tm):
    rows = x2.shape[0]
    pos_blocks = seq_len // tm
    row_spec = lambda width: pl.BlockSpec((tm, width), lambda i: (i, 0))
    tab_spec = pl.BlockSpec((tm, LANES), lambda i: (i % pos_blocks, 0))
    out_bf16 = jax.ShapeDtypeStruct((rows, ATTN_WIDTH), BF16)
    out_f32 = jax.ShapeDtypeStruct((rows, ATTN_WIDTH), F32)
    return pl.pallas_call(
        _in_proj_kernel,
        grid=(rows // tm,),
        in_specs=[row_spec(D_MODEL), _resident((1, D_MODEL)), _resident((D_MODEL, IN_PROJ_WIDTH)),
                  tab_spec, tab_spec, tab_spec],
        out_specs=[row_spec(ATTN_WIDTH)] * 6,
        out_shape=[out_bf16, out_bf16, out_bf16, out_bf16, out_f32, out_bf16],
        scratch_shapes=[pltpu.VMEM((D_MODEL, IN_PROJ_WIDTH), BF16)],
        compiler_params=pltpu.CompilerParams(dimension_semantics=("arbitrary",),
                                             vmem_limit_bytes=VMEM_LIMIT_BYTES),
        name="in_proj",
    )(x2, gain, w_in, cos, sin_hi, sin_lo)


def _attn_kernel(q0_ref, k0_ref, v0_ref, qn_ref, kn_ref, vn_ref, z_ref, o_ref,
                 kx_ref, vx_ref, qx_ref, sa_ref, sb_ref, m_ref, acc_ref, *, n_blocks):
    unit = pl.program_id(0)
    blk = MOBA_BLOCK
    heads = LANES // HEAD_DIM
    n_pairs = n_blocks // 2
    cur = lax.rem(unit, 2)
    nxt = 1 - cur

    head_rows = lambda h: slice(h * HEAD_DIM, (h + 1) * HEAD_DIM)
    spare_base = lambda h: ((h + 1) % heads) * HEAD_DIM
    iota = lambda shape, d: lax.broadcasted_iota(jnp.int32, shape, d)
    key = iota((blk, blk), 0)
    query = iota((blk, blk), 1)
    col_max = lambda s: jnp.max(s, axis=0, keepdims=True)

    def prepare_items(q_ref, k_ref, v_ref, slot):
        kmean = {}

        def key_block(j):
            lane = iota((blk, LANES), 1)
            sum_rows = (iota((VALUE_ROWS - HEAD_DIM, blk), 0) == 0).astype(F32)
            rows = slice(j * blk, (j + 1) * blk)
            kj = k_ref[0, rows, :]
            kmean[j] = jnp.mean(kj.astype(F32), axis=0, keepdims=True)
            vt = v_ref[0, rows, :].astype(F32).T
            for h in range(heads):
                in_head = (lane >= h * HEAD_DIM) & (lane < (h + 1) * HEAD_DIM)
                tag = jnp.where(lane == spare_base(h) + j, 1.0, 0.0).astype(BF16)
                kx_ref[slot, h, j] = jnp.where(in_head, kj, tag)
                vx_ref[slot, j, h] = jnp.concatenate([vt[head_rows(h)], sum_rows],
                                                     axis=0).astype(BF16)

        def query_block(qi):
            if "all" not in kmean:
                kmean["all"] = jnp.concatenate([kmean[j] for j in range(n_blocks)],
                                               axis=0).astype(BF16)
            blk_row = iota((n_blocks, blk), 0)
            zero_rows = lambda n: jnp.zeros((n, blk), F32)
            in_head_order = lambda h, own, other: [own, other] if h == 0 else [other, own]
            qt = q_ref[0, qi * blk:(qi + 1) * blk, :].astype(F32).T
            past = blk_row < qi
            for h in range(heads):
                q_rows = qt[head_rows(h)]
                q_only = jnp.concatenate(in_head_order(h, q_rows, zero_rows(HEAD_DIM)), axis=0)
                gate = jnp.dot(kmean["all"], q_only.astype(BF16), preferred_element_type=F32)
                gate = jnp.where(past, gate, NEG_INF)
                beaten_by = jnp.zeros((n_blocks, blk), jnp.int32)
                for j in range(n_blocks):
                    gj = gate[j:j + 1, :]
                    wins = (gj > gate) | ((gj == gate) & (j < blk_row))
                    beaten_by = beaten_by + wins.astype(jnp.int32)
                keep = (past & (beaten_by < MOBA_TOP_K)) | (blk_row == qi)
                bias = jnp.where(keep, 0.0, NEG_INF)
                spare = jnp.concatenate([bias, zero_rows(HEAD_DIM - n_blocks)], axis=0)
                qx_ref[slot, qi, h] = jnp.concatenate(in_head_order(h, q_rows, spare),
                                                      axis=0).astype(BF16)

        return ([functools.partial(key_block, j) for j in range(n_blocks)]
                + [functools.partial(query_block, qi) for qi in range(n_blocks)])

    def visible_blocks(pair):
        tiles = (pair, n_blocks - 1 - pair)
        return ([(t, tiles[t], True) for t in range(2)]
                + [(t, j, False) for t in range(2) for j in range(tiles[t])])

    traced = lambda t: t + jnp.minimum(unit, 0)

    def stage_items(pair, s_ref, m, slot):
        def block(t, j, own, h):
            qi = (pair, n_blocks - 1 - pair)[t]
            s = jnp.dot(kx_ref[slot, h, j], qx_ref[slot, qi, h], preferred_element_type=F32)
            if own:
                s = jnp.where(key <= query, s, NEG_INF)
            s_ref[traced(t), h, j] = s
            m[t, h] = col_max(s) if own else jnp.maximum(m[t, h], col_max(s))
        return [functools.partial(block, *b, h) for b in visible_blocks(pair)
                for h in range(heads)]

    def finish_items(pair, s_ref, m, acc_ref, slot):
        def block(t, j, own, h):
            p = jnp.exp2(s_ref[traced(t), h, j] - m[t, h]).astype(BF16)
            pv = jnp.dot(vx_ref[slot, j, h], p, preferred_element_type=F32)
            if own:
                acc_ref[traced(t), h] = pv
            else:
                acc_ref[traced(t), h] += pv

        def write_rows():
            for t, qi in enumerate((pair, n_blocks - 1 - pair)):
                rows = slice(qi * blk, (qi + 1) * blk)
                acc = [acc_ref[t, h] for h in range(heads)]
                o_t = jnp.concatenate([a[:HEAD_DIM] / a[HEAD_DIM:HEAD_DIM + 1] for a in acc],
                                      axis=0)
                z = z_ref[0, rows, :].astype(F32)
                silu = 0.5 * z * (1.0 + jnp.tanh(0.5 * z))
                o_ref[0, rows, :] = (o_t.T * silu).astype(BF16)

        return ([functools.partial(block, *b, h) for b in visible_blocks(pair)
                 for h in range(heads)] + [write_rows])

    def alternate(*item_lists):
        for group in zip(*item_lists):
            for item in group:
                item()
        for items in item_lists:
            for item in items[min(map(len, item_lists)):]:
                item()

    m_keys = [(t, h) for t in range(2) for h in range(heads)]

    @pl.when(unit == 0)
    def _():
        m0 = {}
        alternate(prepare_items(q0_ref, k0_ref, v0_ref, 0))
        alternate(stage_items(0, sa_ref, m0, 0))
        for t, h in m_keys:
            m_ref[t, h] = m0[t, h]

    m = {key_: m_ref[key_] for key_ in m_keys}
    for pair in range(n_pairs):
        s_ref, s_next = (sa_ref, sb_ref) if pair % 2 == 0 else (sb_ref, sa_ref)
        m_next = {}
        lists = [finish_items(pair, s_ref, m, acc_ref.at[pair % 2], cur)]
        if pair + 1 < n_pairs:
            lists.append(stage_items(pair + 1, s_next, m_next, cur))
        else:
            lists.append(stage_items(0, s_next, m_next, nxt))
        if pair == n_pairs - 2:
            lists.append(prepare_items(qn_ref, kn_ref, vn_ref, nxt))
        alternate(lists[1], lists[0], *lists[2:])
        m = m_next
    for t, h in m_keys:
        m_ref[t, h] = m[t, h]


def _moba_attention(q, k, v, z_attn):
    bsz, seq_len, _ = q.shape
    n_blocks = seq_len // MOBA_BLOCK
    assert n_blocks % 4 == 0 and n_blocks <= HEAD_DIM
    head_pairs = ATTN_WIDTH // LANES
    heads = LANES // HEAD_DIM
    assert heads == 2, "the block bias rides in the other head's half of the 128 lanes"
    n_units = bsz * head_pairs
    block = (1, seq_len, LANES)
    unit_index = lambda u: (u // head_pairs, 0, u % head_pairs)
    first_spec = pl.BlockSpec(block, lambda u: (0, 0, 0))
    next_spec = pl.BlockSpec(block, lambda u: unit_index(jnp.minimum(u + 1, n_units - 1)))
    unit_spec = pl.BlockSpec(block, unit_index)
    scores = pltpu.VMEM((2, heads, n_blocks, MOBA_BLOCK, MOBA_BLOCK), F32)
    return pl.pallas_call(
        functools.partial(_attn_kernel, n_blocks=n_blocks),
        grid=(n_units,),
        in_specs=[first_spec] * 3 + [next_spec] * 3 + [unit_spec],
        out_specs=unit_spec,
        out_shape=jax.ShapeDtypeStruct((bsz, seq_len, ATTN_WIDTH), BF16),
        scratch_shapes=[pltpu.VMEM((2, heads, n_blocks, MOBA_BLOCK, LANES), BF16),
                        pltpu.VMEM((2, n_blocks, heads, VALUE_ROWS, MOBA_BLOCK), BF16),
                        pltpu.VMEM((2, n_blocks, heads, LANES, MOBA_BLOCK), BF16),
                        scores, scores,
                        pltpu.VMEM((2, heads, 1, MOBA_BLOCK), F32),
                        pltpu.VMEM((2, 2, heads, VALUE_ROWS, MOBA_BLOCK), F32)],
        compiler_params=pltpu.CompilerParams(dimension_semantics=("arbitrary",),
                                             vmem_limit_bytes=VMEM_LIMIT_BYTES),
        name="moba_attn",
    )(q, k, v, q, k, v, z_attn)


def _ssm_tables(lam_re, lam_im, b_re, b_im, c_re, c_im, log_dt):
    T, G, P, H = SSM_CHUNK, SSM_GROUPS, SSM_STATE, SSM_GROUP_DIM
    dt = jnp.exp(log_dt.astype(F32))[:, None]
    lam_r, lam_i = lam_re.astype(F32), lam_im.astype(F32)
    mag = jnp.exp(lam_r * dt)
    bar_r, bar_i = mag * jnp.cos(lam_i * dt), mag * jnp.sin(lam_i * dt)
    den = lam_r * lam_r + lam_i * lam_i
    f_r = ((bar_r - 1.0) * lam_r + bar_i * lam_i) / den
    f_i = (bar_i * lam_r - (bar_r - 1.0) * lam_i) / den
    bb_r = f_r[..., None] * b_re - f_i[..., None] * b_im
    bb_i = f_r[..., None] * b_im + f_i[..., None] * b_re

    def powers(base_r, base_i, count):
        out_r, out_i = [base_r], [base_i]
        for _ in range(count - 1):
            r, i = out_r[-1], out_i[-1]
            out_r.append(r * base_r - i * base_i)
            out_i.append(r * base_i + i * base_r)
        return out_r, out_i

    pw_r, pw_i = powers(bar_r, bar_i, T)
    pw_r = jnp.stack([jnp.ones_like(bar_r)] + pw_r)
    pw_i = jnp.stack([jnp.zeros_like(bar_i)] + pw_i)

    lb_r = pw_r[:T, :, :, None] * bb_r - pw_i[:T, :, :, None] * bb_i
    lb_i = pw_r[:T, :, :, None] * bb_i + pw_i[:T, :, :, None] * bb_r
    cl_r = c_re[None] * pw_r[:, :, None, :] - c_im[None] * pw_i[:, :, None, :]
    cl_i = c_re[None] * pw_i[:, :, None, :] + c_im[None] * pw_r[:, :, None, :]
    lb_t = jnp.stack([lb_r, lb_i], axis=1).transpose(0, 1, 4, 2, 3).reshape(T, 2, H, G * P)
    cl_t = jnp.stack([cl_r, -cl_i], axis=1).transpose(0, 1, 4, 2, 3).reshape(T + 1, 2, P, G * H)

    a_r, a_i = powers(pw_r[T], pw_i[T], F32_SUBLANES)
    tiled = lambda a: jnp.stack(a).reshape(len(a), SSM_TILES, STATE_COLS)
    a_lin = jnp.stack([tiled(a_r), tiled(a_i)]).transpose(2, 0, 1, 3)
    shifts = [1 << k for k in range(F32_SUBLANES.bit_length() - 1)]
    a_log = jnp.stack([tiled([a_r[s - 1] for s in shifts]),
                       tiled([a_i[s - 1] for s in shifts])]).transpose(2, 1, 0, 3)
    row = jnp.arange(F32_SUBLANES)[None, None, None, :, None]
    keep = row >= jnp.asarray(shifts)[None, :, None, None, None]
    a_log = jnp.where(keep, a_log[:, :, :, None, :], 0.0)
    return lb_t, cl_t, a_log, a_lin


def _ssm_kernel(u_ref, lb_ref, cl_ref, alog_ref, alin_ref, d_ref, y_ref, m_ref, ws_ref, wo_ref, *,
                seq_len, n_seq):
    T, H, P = SSM_CHUNK, SSM_GROUP_DIM, SSM_STATE
    n_chunks = seq_len // T
    iota = lambda shape, d: lax.broadcasted_iota(jnp.int32, shape, d)
    h_bits, p_bits = H.bit_length() - 1, P.bit_length() - 1

    @pl.when(pl.program_id(1) == 0)
    def _():
        spread_h = ((iota((LANES, H), 0) & (H - 1)) == iota((LANES, H), 1)).astype(BF16)
        spread_p = ((iota((STATE_COLS, P), 0) & (P - 1)) == iota((STATE_COLS, P), 1)).astype(BF16)
        same_hp = (iota((LANES, STATE_COLS), 0) >> h_bits) == (iota((LANES, STATE_COLS), 1) >> p_bits)
        same_ph = (iota((STATE_COLS, LANES), 0) >> p_bits) == (iota((STATE_COLS, LANES), 1) >> h_bits)

        def expand(spread, coeff, same):
            full = jnp.dot(spread, coeff.astype(BF16), preferred_element_type=F32)
            return jnp.where(same, full, 0.0).astype(BF16)

        rows = lambda j: slice(j * LANES, (j + 1) * LANES)
        state_out = lambda d: jnp.concatenate(
            [expand(spread_p, cl_ref[d, ri], same_ph) for ri in range(2)], axis=0)
        for j in range(T):
            ws_ref[rows(j), :] = jnp.concatenate(
                [expand(spread_h, lb_ref[T - 1 - j, ri], same_hp) for ri in range(2)], axis=1)
        for i in range(T):
            wo_ref[:, rows(i)] = state_out(i + 1)
        c_out = state_out(0)
        for d in range(T):
            block = jnp.dot(ws_ref[rows(T - 1 - d), :], c_out,
                            preferred_element_type=F32).astype(BF16)
            for j in range(T - d):
                m_ref[rows(j), rows(j + d)] = block
        for j in range(T):
            for i in range(j):
                m_ref[rows(j), rows(i)] = jnp.zeros((LANES, LANES), BF16)

    pieces = [u_ref[pl.ds(b * seq_len + i, n_chunks, stride=T), :]
              for b in range(n_seq) for i in range(T)]
    x = jnp.concatenate(
        [jnp.concatenate(pieces[b * T:(b + 1) * T], axis=1) for b in range(n_seq)], axis=0)
    x_lo = x.astype(BF16)
    groups = [slice(c, c + MXU_COLS) for c in range(0, STATE_COLS, MXU_COLS)]
    im_of = lambda cols: slice(STATE_COLS + cols.start, STATE_COLS + cols.stop)
    increments = [(jnp.dot(x_lo, ws_ref[:, cols], preferred_element_type=F32),
                   jnp.dot(x_lo, ws_ref[:, im_of(cols)], preferred_element_type=F32))
                  for cols in groups]
    y = jnp.concatenate(
        [jnp.dot(x_lo[:, :hi], m_ref[:hi, hi - MXU_COLS:hi], preferred_element_type=F32)
         for hi in range(MXU_COLS, T * LANES + 1, MXU_COLS)], axis=1)
    y = y + x * jnp.concatenate([d_ref[...]] * T, axis=1)

    sub = F32_SUBLANES
    first_row = iota((sub, MXU_COLS), 0) == 0
    cmul = lambda a_re, a_im, b_re, b_im: (a_re * b_re - a_im * b_im, a_re * b_im + a_im * b_re)
    tiles_per_seq = n_chunks // sub

    for cols, (re, im) in zip(groups, increments):
        log_mul = [(alog_ref[0, k, 0, :, cols], alog_ref[0, k, 1, :, cols])
                   for k in range(sub.bit_length() - 1)]
        lin_mul = (alin_ref[0, 0, :, cols], alin_ref[0, 1, :, cols])
        prev_re, prev_im = [], []
        for b in range(n_seq):
            carry = (jnp.zeros((1, MXU_COLS), F32),) * 2
            for g in range(tiles_per_seq):
                rows = slice((b * tiles_per_seq + g) * sub, (b * tiles_per_seq + g + 1) * sub)
                t_re, t_im = re[rows], im[rows]
                for k, mul in enumerate(log_mul):
                    d_re, d_im = cmul(*mul, pltpu.roll(t_re, 1 << k, axis=0),
                                      pltpu.roll(t_im, 1 << k, axis=0))
                    t_re, t_im = t_re + d_re, t_im + d_im
                if g > 0:
                    d_re, d_im = cmul(*lin_mul, *carry)
                    t_re, t_im = t_re + d_re, t_im + d_im
                prev_re.append(jnp.where(first_row, carry[0], pltpu.roll(t_re, 1, axis=0)))
                prev_im.append(jnp.where(first_row, carry[1], pltpu.roll(t_im, 1, axis=0)))
                carry = (t_re[sub - 1:sub], t_im[sub - 1:sub])
        prev_re, prev_im = (jnp.concatenate(a, axis=0).astype(BF16) for a in (prev_re, prev_im))
        y = (y + jnp.dot(prev_re, wo_ref[cols, :], preferred_element_type=F32)
             + jnp.dot(prev_im, wo_ref[im_of(cols), :], preferred_element_type=F32))
    for b in range(n_seq):
        for i in range(T):
            y_ref[pl.ds(b * seq_len + i, n_chunks, stride=T), :] = (
                y[b * n_chunks:(b + 1) * n_chunks, i * LANES:(i + 1) * LANES])


def _s5_ssm(u, tables, d_skip, seq_len):
    lb_t, cl_t, a_log, a_lin = tables
    T, H, P = SSM_CHUNK, SSM_GROUP_DIM, SSM_STATE
    assert seq_len % (T * F32_SUBLANES) == 0
    bsz = u.shape[0] // seq_len
    n_seq = max(n for n in (4, 2, 1) if bsz % n == 0)
    io_spec = pl.BlockSpec((n_seq * seq_len, LANES), lambda q, b: (b, q))
    square = pltpu.VMEM((T * LANES, T * LANES), BF16)
    return pl.pallas_call(
        functools.partial(_ssm_kernel, seq_len=seq_len, n_seq=n_seq),
        grid=(SSM_TILES, bsz // n_seq),
        in_specs=[io_spec,
                  pl.BlockSpec((T, 2, H, STATE_COLS), lambda q, b: (0, 0, 0, q)),
                  pl.BlockSpec((T + 1, 2, P, LANES), lambda q, b: (0, 0, 0, q)),
                  pl.BlockSpec((1,) + a_log.shape[1:], lambda q, b: (q, 0, 0, 0, 0)),
                  pl.BlockSpec((1,) + a_lin.shape[1:], lambda q, b: (q, 0, 0, 0)),
                  pl.BlockSpec((1, LANES), lambda q, b: (0, q))],
        out_specs=io_spec,
        out_shape=jax.ShapeDtypeStruct(u.shape, F32),
        scratch_shapes=[square, square, square],
        compiler_params=pltpu.CompilerParams(dimension_semantics=("arbitrary", "arbitrary"),
                                             vmem_limit_bytes=VMEM_LIMIT_BYTES),
        name="s5_ssm",
    )(u, lb_t, cl_t, a_log, a_lin, d_skip)


def _out_proj_kernel(x_ref, ma_ref, y_ref, zs_ref, wg32_ref, bg_ref, w32_ref, g_ref, o_ref,
                     wg_ref, w_ref, *, final_norm):
    @pl.when(pl.program_id(0) == 0)
    def _():
        wg_ref[...] = wg32_ref[...].astype(BF16)
        w_ref[...] = w32_ref[...].astype(BF16)

    half_plus = lambda t: 1.0 + jnp.tanh(0.5 * t)
    for c in range(0, x_ref.shape[0], OUT_PROJ_ROWS):
        rows = slice(c, c + OUT_PROJ_ROWS)
        y = y_ref[rows, :]
        y = y * (1.0 + jnp.tanh(math.sqrt(2.0 / math.pi) * (y + 0.044715 * (y * y * y))))
        gate = jnp.dot((0.5 * y).astype(BF16), wg_ref[...], preferred_element_type=F32) + bg_ref[...]
        z = zs_ref[rows, :].astype(F32)
        mixed_ssm = (0.125 * y) * half_plus(gate) * (z * half_plus(z))
        mixed = jnp.concatenate([ma_ref[rows, :], mixed_ssm.astype(BF16)], axis=1)
        r = x_ref[rows, :] + jnp.dot(mixed, w_ref[...], preferred_element_type=F32)
        if final_norm:
            ms = jnp.mean(r * r, axis=-1, keepdims=True)
            r = r * lax.rsqrt(ms + NORM_EPS) * g_ref[...]
        o_ref[rows, :] = r


def _out_proj(x2, mixed_attn, y_ssm, z_ssm, w_glu, b_glu, w_out, gain, tm, final_norm):
    rows = x2.shape[0]
    row_spec = lambda width: pl.BlockSpec((tm, width), lambda i: (i, 0))
    return pl.pallas_call(
        functools.partial(_out_proj_kernel, final_norm=final_norm),
        grid=(rows // tm,),
        in_specs=[row_spec(D_MODEL), row_spec(ATTN_WIDTH), row_spec(SSM_WIDTH), row_spec(SSM_WIDTH),
                  _resident((SSM_WIDTH, SSM_WIDTH)), _resident((1, SSM_WIDTH)),
                  _resident((ATTN_WIDTH + SSM_WIDTH, D_MODEL)), _resident((1, D_MODEL))],
        out_specs=row_spec(D_MODEL),
        out_shape=jax.ShapeDtypeStruct((rows, D_MODEL), F32),
        scratch_shapes=[pltpu.VMEM((SSM_WIDTH, SSM_WIDTH), BF16),
                        pltpu.VMEM((ATTN_WIDTH + SSM_WIDTH, D_MODEL), BF16)],
        compiler_params=pltpu.CompilerParams(dimension_semantics=("arbitrary",),
                                             vmem_limit_bytes=VMEM_LIMIT_BYTES),
        name="out_proj",
    )(x2, mixed_attn, y_ssm, z_ssm, w_glu, b_glu, w_out, gain)


def _rotary_tables(seq_len):
    half = HEAD_DIM // 2
    inv_freq = 1.0 / (ROPE_THETA ** (jnp.arange(half, dtype=F32) / half))
    ang = jnp.arange(seq_len, dtype=F32)[:, None] * inv_freq[None, :]
    cos, sin, zero = jnp.cos(ang), jnp.sin(ang), jnp.zeros_like(ang)
    reps = LANES // HEAD_DIM
    cos_t = jnp.tile(jnp.concatenate([cos, cos], axis=1), (1, reps))
    sin_hi = jnp.tile(jnp.concatenate([zero, sin], axis=1), (1, reps))
    sin_lo = jnp.tile(jnp.concatenate([-sin, zero], axis=1), (1, reps))
    return cos_t, sin_hi, sin_lo


def kernel(x, norm_gain, w_in, w_out, lam_re, lam_im, b_re, b_im, c_re, c_im,
           d_skip, log_dt, w_glu, b_glu, final_gain):
    bsz, seq_len, _ = x.shape
    depth = norm_gain.shape[0]
    assert seq_len % MOBA_BLOCK == 0 and seq_len // MOBA_BLOCK > MOBA_TOP_K
    assert seq_len % SSM_CHUNK == 0
    tm = min(1024, seq_len)
    cos, sin_hi, sin_lo = _rotary_tables(seq_len)

    x2 = x.reshape(bsz * seq_len, D_MODEL)
    for layer in range(depth):
        tables = _ssm_tables(lam_re[layer], lam_im[layer], b_re[layer], b_im[layer],
                             c_re[layer], c_im[layer], log_dt[layer])
        q, k, v, z_attn, u, z_ssm = _in_proj(
            x2, norm_gain[layer].reshape(1, D_MODEL), w_in[layer], cos, sin_hi, sin_lo, seq_len, tm)
        to_seq = lambda t: t.reshape(bsz, seq_len, ATTN_WIDTH)
        mixed_attn = _moba_attention(to_seq(q), to_seq(k), to_seq(v), to_seq(z_attn))
        y_ssm = _s5_ssm(u, tables, d_skip[layer].reshape(1, SSM_WIDTH), seq_len)
        x2 = _out_proj(x2, mixed_attn.reshape(bsz * seq_len, ATTN_WIDTH), y_ssm, z_ssm,
                       w_glu[layer], b_glu[layer].reshape(1, SSM_WIDTH),
                       w_out[layer], final_gain.reshape(1, D_MODEL), tm,
                       final_norm=layer == depth - 1)
    return x2.reshape(bsz, seq_len, D_MODEL)
```

```python
import functools
import math

import jax
import jax.numpy as jnp
from jax import lax
from jax.experimental import pallas as pl
from jax.experimental.pallas import tpu as pltpu

F32 = jnp.float32
BF16 = jnp.bfloat16

D_MODEL = 1024
HEAD_DIM = 64
ATTN_HEADS = 8
ATTN_WIDTH = ATTN_HEADS * HEAD_DIM
MOBA_BLOCK = 256
MOBA_TOP_K = 3
ROPE_THETA = 10000.0
SSM_GROUP_DIM = 16
SSM_GROUPS = 32
SSM_WIDTH = SSM_GROUPS * SSM_GROUP_DIM
SSM_STATE = 64
IN_PROJ_WIDTH = 4 * ATTN_WIDTH + 2 * SSM_WIDTH
NORM_EPS = 1e-6
NEG_INF = -1e30

LANES = 128
SSM_CHUNK = 8
GROUPS_PER_TILE = LANES // SSM_GROUP_DIM
SSM_TILES = SSM_WIDTH // LANES
TILE_COLS = SSM_CHUNK * LANES
STATE_COLS = GROUPS_PER_TILE * SSM_STATE
VMEM_LIMIT_BYTES = 56 * 1024 * 1024
MXU_COLS = 256
OUT_PROJ_ROWS = 256
F32_SUBLANES = 8
BF16_SUBLANES = 16
VALUE_ROWS = HEAD_DIM + BF16_SUBLANES
QK_SCALE = math.log2(math.e) / math.sqrt(HEAD_DIM)


def _resident(shape):
    zeros = (0,) * len(shape)
    return pl.BlockSpec(shape, lambda *_: zeros, pipeline_mode=pl.Buffered(1))


def _in_proj_kernel(x_ref, g_ref, w32_ref, cos_ref, sin_hi_ref, sin_lo_ref,
                    q_ref, k_ref, v_ref, za_ref, u_ref, zs_ref, w_ref):
    @pl.when(pl.program_id(0) == 0)
    def _():
        for c in range(0, IN_PROJ_WIDTH, ATTN_WIDTH):
            w_ref[:, c:c + ATTN_WIDTH] = w32_ref[:, c:c + ATTN_WIDTH].astype(BF16)

    x = x_ref[...]
    ms = jnp.mean(x * x, axis=-1, keepdims=True)
    h = (x * lax.rsqrt(ms + NORM_EPS) * g_ref[...]).astype(BF16)

    def section(idx):
        return jnp.dot(h, w_ref[:, idx * ATTN_WIDTH:(idx + 1) * ATTN_WIDTH],
                       preferred_element_type=F32)

    cos = cos_ref[...]
    sin_hi = sin_hi_ref[...]
    sin_lo = sin_lo_ref[...]

    def rotary(t):
        return (t * cos + pltpu.roll(t, HEAD_DIM // 2, axis=1) * sin_hi
                + pltpu.roll(t, LANES - HEAD_DIM // 2, axis=1) * sin_lo)

    q = section(0)
    k = section(1)
    for c in range(ATTN_WIDTH // LANES):
        sl = slice(c * LANES, (c + 1) * LANES)
        q_ref[:, sl] = (rotary(q[:, sl]) * QK_SCALE).astype(BF16)
        k_ref[:, sl] = rotary(k[:, sl]).astype(BF16)
    v_ref[...] = section(2).astype(BF16)
    za_ref[...] = section(3).astype(BF16)
    u_ref[...] = section(4)
    zs_ref[...] = section(5).astype(BF16)


def _in_proj(x2, gain, w_in, cos, sin_hi, sin_lo, seq_len, tm):
    rows = x2.shape[0]
    pos_blocks = seq_len // tm
    row_spec = lambda width: pl.BlockSpec((tm, width), lambda i: (i, 0))
    tab_spec = pl.BlockSpec((tm, LANES), lambda i: (i % pos_blocks, 0))
    out_bf16 = jax.ShapeDtypeStruct((rows, ATTN_WIDTH), BF16)
    out_f32 = jax.ShapeDtypeStruct((rows, ATTN_WIDTH), F32)
    return pl.pallas_call(
        _in_proj_kernel,
        grid=(rows // tm,),
        in_specs=[row_spec(D_MODEL), _resident((1, D_MODEL)), _resident((D_MODEL, IN_PROJ_WIDTH)),
                  tab_spec, tab_spec, tab_spec],
        out_specs=[row_spec(ATTN_WIDTH)] * 6,
        out_shape=[out_bf16, out_bf16, out_bf16, out_bf16, out_f32, out_bf16],
        scratch_shapes=[pltpu.VMEM((D_MODEL, IN_PROJ_WIDTH), BF16)],
        compiler_params=pltpu.CompilerParams(dimension_semantics=("arbitrary",),
                                             vmem_limit_bytes=VMEM_LIMIT_BYTES),
        name="in_proj",
    )(x2, gain, w_in, cos, sin_hi, sin_lo)


def _attn_kernel(q0_ref, k0_ref, v0_ref, qn_ref, kn_ref, vn_ref, z_ref, o_ref,
                 kx_ref, vx_ref, qx_ref, sa_ref, sb_ref, m_ref, acc_ref, *, n_blocks):
    unit = pl.program_id(0)
    blk = MOBA_BLOCK
    heads = LANES // HEAD_DIM
    n_pairs = n_blocks // 2
    cur = lax.rem(unit, 2)
    nxt = 1 - cur

    head_rows = lambda h: slice(h * HEAD_DIM, (h + 1) * HEAD_DIM)
    spare_base = lambda h: ((h + 1) % heads) * HEAD_DIM
    iota = lambda shape, d: lax.broadcasted_iota(jnp.int32, shape, d)
    key = iota((blk, blk), 0)
    query = iota((blk, blk), 1)
    col_max = lambda s: jnp.max(s, axis=0, keepdims=True)

    def prepare_items(q_ref, k_ref, v_ref, slot):
        kmean = {}

        def key_block(j):
            lane = iota((blk, LANES), 1)
            sum_rows = (iota((VALUE_ROWS - HEAD_DIM, blk), 0) == 0).astype(F32)
            rows = slice(j * blk, (j + 1) * blk)
            kj = k_ref[0, rows, :]
            kmean[j] = jnp.mean(kj.astype(F32), axis=0, keepdims=True)
            vt = v_ref[0, rows, :].astype(F32).T
            for h in range(heads):
                in_head = (lane >= h * HEAD_DIM) & (lane < (h + 1) * HEAD_DIM)
                tag = jnp.where(lane == spare_base(h) + j, 1.0, 0.0).astype(BF16)
                kx_ref[slot, h, j] = jnp.where(in_head, kj, tag)
                vx_ref[slot, j, h] = jnp.concatenate([vt[head_rows(h)], sum_rows],
                                                     axis=0).astype(BF16)

        def query_block(qi):
            if "all" not in kmean:
                kmean["all"] = jnp.concatenate([kmean[j] for j in range(n_blocks)],
                                               axis=0).astype(BF16)
            blk_row = iota((n_blocks, blk), 0)
            zero_rows = lambda n: jnp.zeros((n, blk), F32)
            in_head_order = lambda h, own, other: [own, other] if h == 0 else [other, own]
            qt = q_ref[0, qi * blk:(qi + 1) * blk, :].astype(F32).T
            past = blk_row < qi
            for h in range(heads):
                q_rows = qt[head_rows(h)]
                q_only = jnp.concatenate(in_head_order(h, q_rows, zero_rows(HEAD_DIM)), axis=0)
                gate = jnp.dot(kmean["all"], q_only.astype(BF16), preferred_element_type=F32)
                gate = jnp.where(past, gate, NEG_INF)
                beaten_by = jnp.zeros((n_blocks, blk), jnp.int32)
                for j in range(n_blocks):
                    gj = gate[j:j + 1, :]
                    wins = (gj > gate) | ((gj == gate) & (j < blk_row))
                    beaten_by = beaten_by + wins.astype(jnp.int32)
                keep = (past & (beaten_by < MOBA_TOP_K)) | (blk_row == qi)
                bias = jnp.where(keep, 0.0, NEG_INF)
                spare = jnp.concatenate([bias, zero_rows(HEAD_DIM - n_blocks)], axis=0)
                qx_ref[slot, qi, h] = jnp.concatenate(in_head_order(h, q_rows, spare),
                                                      axis=0).astype(BF16)

        return ([functools.partial(key_block, j) for j in range(n_blocks)]
                + [functools.partial(query_block, qi) for qi in range(n_blocks)])

    def visible_blocks(pair):
        tiles = (pair, n_blocks - 1 - pair)
        return ([(t, tiles[t], True) for t in range(2)]
                + [(t, j, False) for t in range(2) for j in range(tiles[t])])

    traced = lambda t: t + jnp.minimum(unit, 0)

    def stage_items(pair, s_ref, m, slot):
        def block(t, j, own, h):
            qi = (pair, n_blocks - 1 - pair)[t]
            s = jnp.dot(kx_ref[slot, h, j], qx_ref[slot, qi, h], preferred_element_type=F32)
            if own:
                s = jnp.where(key <= query, s, NEG_INF)
            s_ref[traced(t), h, j] = s
            m[t, h] = col_max(s) if own else jnp.maximum(m[t, h], col_max(s))
        return [functools.partial(block, *b, h) for b in visible_blocks(pair)
                for h in range(heads)]

    def finish_items(pair, s_ref, m, acc_ref, slot):
        def block(t, j, own, h):
            p = jnp.exp2(s_ref[traced(t), h, j] - m[t, h]).astype(BF16)
            pv = jnp.dot(vx_ref[slot, j, h], p, preferred_element_type=F32)
            if own:
                acc_ref[traced(t), h] = pv
            else:
                acc_ref[traced(t), h] += pv

        def write_rows(t):
            qi = (pair, n_blocks - 1 - pair)[t]
            rows = slice(qi * blk, (qi + 1) * blk)
            acc = [acc_ref[traced(t), h] for h in range(heads)]
            o_t = jnp.concatenate([a[:HEAD_DIM] / a[HEAD_DIM:HEAD_DIM + 1] for a in acc], axis=0)
            z = z_ref[0, rows, :].astype(F32)
            silu = 0.5 * z * (1.0 + jnp.tanh(0.5 * z))
            o_ref[0, rows, :] = (o_t.T * silu).astype(BF16)

        items = []
        blocks = visible_blocks(pair)
        for i, b in enumerate(blocks):
            items += [functools.partial(block, *b, h) for h in range(heads)]
            if i >= 2 and (i + 1 == len(blocks) or blocks[i + 1][0] != b[0]):
                items.append(functools.partial(write_rows, b[0]))
        tiles_done = {b[0] for b in blocks[2:]}
        return items + [functools.partial(write_rows, t) for t in range(2) if t not in tiles_done]

    def alternate(*item_lists):
        for group in zip(*item_lists):
            for item in group:
                item()
        for items in item_lists:
            for item in items[min(map(len, item_lists)):]:
                item()

    m_keys = [(t, h) for t in range(2) for h in range(heads)]

    @pl.when(unit == 0)
    def _():
        m0 = {}
        alternate(prepare_items(q0_ref, k0_ref, v0_ref, 0))
        alternate(stage_items(0, sa_ref, m0, 0))
        for t, h in m_keys:
            m_ref[t, h] = m0[t, h]

    m = {key_: m_ref[key_] for key_ in m_keys}
    for pair in range(n_pairs):
        s_ref, s_next = (sa_ref, sb_ref) if pair % 2 == 0 else (sb_ref, sa_ref)
        m_next = {}
        lists = [finish_items(pair, s_ref, m, acc_ref.at[pair % 2], cur)]
        if pair + 1 < n_pairs:
            lists.append(stage_items(pair + 1, s_next, m_next, cur))
        else:
            lists.append(stage_items(0, s_next, m_next, nxt))
        if pair == n_pairs - 2:
            lists.append(prepare_items(qn_ref, kn_ref, vn_ref, nxt))
        alternate(lists[1], lists[0], *lists[2:])
        m = m_next
    for t, h in m_keys:
        m_ref[t, h] = m[t, h]


def _moba_attention(q, k, v, z_attn):
    bsz, seq_len, _ = q.shape
    n_blocks = seq_len // MOBA_BLOCK
    assert n_blocks % 4 == 0 and n_blocks <= HEAD_DIM
    head_pairs = ATTN_WIDTH // LANES
    heads = LANES // HEAD_DIM
    assert heads == 2, "the block bias rides in the other head's half of the 128 lanes"
    n_units = bsz * head_pairs
    block = (1, seq_len, LANES)
    unit_index = lambda u: (u // head_pairs, 0, u % head_pairs)
    first_spec = pl.BlockSpec(block, lambda u: (0, 0, 0))
    next_spec = pl.BlockSpec(block, lambda u: unit_index(jnp.minimum(u + 1, n_units - 1)))
    unit_spec = pl.BlockSpec(block, unit_index)
    scores = pltpu.VMEM((2, heads, n_blocks, MOBA_BLOCK, MOBA_BLOCK), F32)
    return pl.pallas_call(
        functools.partial(_attn_kernel, n_blocks=n_blocks),
        grid=(n_units,),
        in_specs=[first_spec] * 3 + [next_spec] * 3 + [unit_spec],
        out_specs=unit_spec,
        out_shape=jax.ShapeDtypeStruct((bsz, seq_len, ATTN_WIDTH), BF16),
        scratch_shapes=[pltpu.VMEM((2, heads, n_blocks, MOBA_BLOCK, LANES), BF16),
                        pltpu.VMEM((2, n_blocks, heads, VALUE_ROWS, MOBA_BLOCK), BF16),
                        pltpu.VMEM((2, n_blocks, heads, LANES, MOBA_BLOCK), BF16),
                        scores, scores,
                        pltpu.VMEM((2, heads, 1, MOBA_BLOCK), F32),
                        pltpu.VMEM((2, 2, heads, VALUE_ROWS, MOBA_BLOCK), F32)],
        compiler_params=pltpu.CompilerParams(dimension_semantics=("arbitrary",),
                                             vmem_limit_bytes=VMEM_LIMIT_BYTES),
        name="moba_attn",
    )(q, k, v, q, k, v, z_attn)


def _ssm_tables(lam_re, lam_im, b_re, b_im, c_re, c_im, log_dt):
    T, G, P, H = SSM_CHUNK, SSM_GROUPS, SSM_STATE, SSM_GROUP_DIM
    dt = jnp.exp(log_dt.astype(F32))[:, None]
    lam_r, lam_i = lam_re.astype(F32), lam_im.astype(F32)
    mag = jnp.exp(lam_r * dt)
    bar_r, bar_i = mag * jnp.cos(lam_i * dt), mag * jnp.sin(lam_i * dt)
    den = lam_r * lam_r + lam_i * lam_i
    f_r = ((bar_r - 1.0) * lam_r + bar_i * lam_i) / den
    f_i = (bar_i * lam_r - (bar_r - 1.0) * lam_i) / den
    bb_r = f_r[..., None] * b_re - f_i[..., None] * b_im
    bb_i = f_r[..., None] * b_im + f_i[..., None] * b_re

    def powers(base_r, base_i, count):
        out_r, out_i = [base_r], [base_i]
        for _ in range(count - 1):
            r, i = out_r[-1], out_i[-1]
            out_r.append(r * base_r - i * base_i)
            out_i.append(r * base_i + i * base_r)
        return out_r, out_i

    pw_r, pw_i = powers(bar_r, bar_i, T)
    pw_r = jnp.stack([jnp.ones_like(bar_r)] + pw_r)
    pw_i = jnp.stack([jnp.zeros_like(bar_i)] + pw_i)

    lb_r = pw_r[:T, :, :, None] * bb_r - pw_i[:T, :, :, None] * bb_i
    lb_i = pw_r[:T, :, :, None] * bb_i + pw_i[:T, :, :, None] * bb_r
    cl_r = c_re[None] * pw_r[:, :, None, :] - c_im[None] * pw_i[:, :, None, :]
    cl_i = c_re[None] * pw_i[:, :, None, :] + c_im[None] * pw_r[:, :, None, :]
    lb_t = jnp.stack([lb_r, lb_i], axis=1).transpose(0, 1, 4, 2, 3).reshape(T, 2, H, G * P)
    cl_t = jnp.stack([cl_r, -cl_i], axis=1).transpose(0, 1, 4, 2, 3).reshape(T + 1, 2, P, G * H)

    a_r, a_i = powers(pw_r[T], pw_i[T], F32_SUBLANES)
    tiled = lambda a: jnp.stack(a).reshape(len(a), SSM_TILES, STATE_COLS)
    a_lin = jnp.stack([tiled(a_r), tiled(a_i)]).transpose(2, 0, 1, 3)
    shifts = [1 << k for k in range(F32_SUBLANES.bit_length() - 1)]
    a_log = jnp.stack([tiled([a_r[s - 1] for s in shifts]),
                       tiled([a_i[s - 1] for s in shifts])]).transpose(2, 1, 0, 3)
    row = jnp.arange(F32_SUBLANES)[None, None, None, :, None]
    keep = row >= jnp.asarray(shifts)[None, :, None, None, None]
    a_log = jnp.where(keep, a_log[:, :, :, None, :], 0.0)
    return lb_t, cl_t, a_log, a_lin


def _ssm_kernel(u_ref, lb_ref, cl_ref, alog_ref, alin_ref, d_ref, y_ref, m_ref, ws_ref, wo_ref, *,
                seq_len, n_seq):
    T, H, P = SSM_CHUNK, SSM_GROUP_DIM, SSM_STATE
    n_chunks = seq_len // T
    iota = lambda shape, d: lax.broadcasted_iota(jnp.int32, shape, d)
    h_bits, p_bits = H.bit_length() - 1, P.bit_length() - 1

    @pl.when(pl.program_id(1) == 0)
    def _():
        spread_h = ((iota((LANES, H), 0) & (H - 1)) == iota((LANES, H), 1)).astype(BF16)
        spread_p = ((iota((STATE_COLS, P), 0) & (P - 1)) == iota((STATE_COLS, P), 1)).astype(BF16)
        same_hp = (iota((LANES, STATE_COLS), 0) >> h_bits) == (iota((LANES, STATE_COLS), 1) >> p_bits)
        same_ph = (iota((STATE_COLS, LANES), 0) >> p_bits) == (iota((STATE_COLS, LANES), 1) >> h_bits)

        def expand(spread, coeff, same):
            full = jnp.dot(spread, coeff.astype(BF16), preferred_element_type=F32)
            return jnp.where(same, full, 0.0).astype(BF16)

        rows = lambda j: slice(j * LANES, (j + 1) * LANES)
        state_out = lambda d: jnp.concatenate(
            [expand(spread_p, cl_ref[d, ri], same_ph) for ri in range(2)], axis=0)
        for j in range(T):
            ws_ref[rows(j), :] = jnp.concatenate(
                [expand(spread_h, lb_ref[T - 1 - j, ri], same_hp) for ri in range(2)], axis=1)
        for i in range(T):
            wo_ref[:, rows(i)] = state_out(i + 1)
        c_out = state_out(0)
        for d in range(T):
            block = jnp.dot(ws_ref[rows(T - 1 - d), :], c_out,
                            preferred_element_type=F32).astype(BF16)
            for j in range(T - d):
                m_ref[rows(j), rows(j + d)] = block
        for j in range(T):
            for i in range(j):
                m_ref[rows(j), rows(i)] = jnp.zeros((LANES, LANES), BF16)

    pieces = [u_ref[pl.ds(b * seq_len + i, n_chunks, stride=T), :]
              for b in range(n_seq) for i in range(T)]
    x = jnp.concatenate(
        [jnp.concatenate(pieces[b * T:(b + 1) * T], axis=1) for b in range(n_seq)], axis=0)
    x_lo = x.astype(BF16)
    groups = [slice(c, c + MXU_COLS) for c in range(0, STATE_COLS, MXU_COLS)]
    im_of = lambda cols: slice(STATE_COLS + cols.start, STATE_COLS + cols.stop)
    increments = [(jnp.dot(x_lo, ws_ref[:, cols], preferred_element_type=F32),
                   jnp.dot(x_lo, ws_ref[:, im_of(cols)], preferred_element_type=F32))
                  for cols in groups]
    y = jnp.concatenate(
        [jnp.dot(x_lo[:, :hi], m_ref[:hi, hi - MXU_COLS:hi], preferred_element_type=F32)
         for hi in range(MXU_COLS, T * LANES + 1, MXU_COLS)], axis=1)
    y = y + x * jnp.concatenate([d_ref[...]] * T, axis=1)

    sub = F32_SUBLANES
    first_row = iota((sub, MXU_COLS), 0) == 0
    cmul = lambda a_re, a_im, b_re, b_im: (a_re * b_re - a_im * b_im, a_re * b_im + a_im * b_re)
    tiles_per_seq = n_chunks // sub

    for cols, (re, im) in zip(groups, increments):
        log_mul = [(alog_ref[0, k, 0, :, cols], alog_ref[0, k, 1, :, cols])
                   for k in range(sub.bit_length() - 1)]
        lin_mul = (alin_ref[0, 0, :, cols], alin_ref[0, 1, :, cols])
        prev_re, prev_im = [], []
        for b in range(n_seq):
            carry = (jnp.zeros((1, MXU_COLS), F32),) * 2
            for g in range(tiles_per_seq):
                rows = slice((b * tiles_per_seq + g) * sub, (b * tiles_per_seq + g + 1) * sub)
                t_re, t_im = re[rows], im[rows]
                for k, mul in enumerate(log_mul):
                    d_re, d_im = cmul(*mul, pltpu.roll(t_re, 1 << k, axis=0),
                                      pltpu.roll(t_im, 1 << k, axis=0))
                    t_re, t_im = t_re + d_re, t_im + d_im
                if g > 0:
                    d_re, d_im = cmul(*lin_mul, *carry)
                    t_re, t_im = t_re + d_re, t_im + d_im
                prev_re.append(jnp.where(first_row, carry[0], pltpu.roll(t_re, 1, axis=0)))
                prev_im.append(jnp.where(first_row, carry[1], pltpu.roll(t_im, 1, axis=0)))
                carry = (t_re[sub - 1:sub], t_im[sub - 1:sub])
        prev_re, prev_im = (jnp.concatenate(a, axis=0).astype(BF16) for a in (prev_re, prev_im))
        y = (y + jnp.dot(prev_re, wo_ref[cols, :], preferred_element_type=F32)
             + jnp.dot(prev_im, wo_ref[im_of(cols), :], preferred_element_type=F32))
    for b in range(n_seq):
        for i in range(T):
            y_ref[pl.ds(b * seq_len + i, n_chunks, stride=T), :] = (
                y[b * n_chunks:(b + 1) * n_chunks, i * LANES:(i + 1) * LANES])


def _s5_ssm(u, tables, d_skip, seq_len):
    lb_t, cl_t, a_log, a_lin = tables
    T, H, P = SSM_CHUNK, SSM_GROUP_DIM, SSM_STATE
    assert seq_len % (T * F32_SUBLANES) == 0
    bsz = u.shape[0] // seq_len
    n_seq = max(n for n in (4, 2, 1) if bsz % n == 0)
    io_spec = pl.BlockSpec((n_seq * seq_len, LANES), lambda q, b: (b, q))
    square = pltpu.VMEM((T * LANES, T * LANES), BF16)
    return pl.pallas_call(
        functools.partial(_ssm_kernel, seq_len=seq_len, n_seq=n_seq),
        grid=(SSM_TILES, bsz // n_seq),
        in_specs=[io_spec,
                  pl.BlockSpec((T, 2, H, STATE_COLS), lambda q, b: (0, 0, 0, q)),
                  pl.BlockSpec((T + 1, 2, P, LANES), lambda q, b: (0, 0, 0, q)),
                  pl.BlockSpec((1,) + a_log.shape[1:], lambda q, b: (q, 0, 0, 0, 0)),
                  pl.BlockSpec((1,) + a_lin.shape[1:], lambda q, b: (q, 0, 0, 0)),
                  pl.BlockSpec((1, LANES), lambda q, b: (0, q))],
        out_specs=io_spec,
        out_shape=jax.ShapeDtypeStruct(u.shape, F32),
        scratch_shapes=[square, square, square],
        compiler_params=pltpu.CompilerParams(dimension_semantics=("arbitrary", "arbitrary"),
                                             vmem_limit_bytes=VMEM_LIMIT_BYTES),
        name="s5_ssm",
    )(u, lb_t, cl_t, a_log, a_lin, d_skip)


def _out_proj_kernel(x_ref, ma_ref, y_ref, zs_ref, wg32_ref, bg_ref, w32_ref, g_ref, o_ref,
                     wg_ref, w_ref, *, final_norm):
    @pl.when(pl.program_id(0) == 0)
    def _():
        wg_ref[...] = wg32_ref[...].astype(BF16)
        w_ref[...] = w32_ref[...].astype(BF16)

    half_plus = lambda t: 1.0 + jnp.tanh(0.5 * t)
    for c in range(0, x_ref.shape[0], OUT_PROJ_ROWS):
        rows = slice(c, c + OUT_PROJ_ROWS)
        y = y_ref[rows, :]
        y = y * (1.0 + jnp.tanh(math.sqrt(2.0 / math.pi) * (y + 0.044715 * (y * y * y))))
        gate = jnp.dot((0.5 * y).astype(BF16), wg_ref[...], preferred_element_type=F32) + bg_ref[...]
        z = zs_ref[rows, :].astype(F32)
        mixed_ssm = (0.125 * y) * half_plus(gate) * (z * half_plus(z))
        mixed = jnp.concatenate([ma_ref[rows, :], mixed_ssm.astype(BF16)], axis=1)
        r = x_ref[rows, :] + jnp.dot(mixed, w_ref[...], preferred_element_type=F32)
        if final_norm:
            ms = jnp.mean(r * r, axis=-1, keepdims=True)
            r = r * lax.rsqrt(ms + NORM_EPS) * g_ref[...]
        o_ref[rows, :] = r


def _out_proj(x2, mixed_attn, y_ssm, z_ssm, w_glu, b_glu, w_out, gain, tm, final_norm):
    rows = x2.shape[0]
    row_spec = lambda width: pl.BlockSpec((tm, width), lambda i: (i, 0))
    return pl.pallas_call(
        functools.partial(_out_proj_kernel, final_norm=final_norm),
        grid=(rows // tm,),
        in_specs=[row_spec(D_MODEL), row_spec(ATTN_WIDTH), row_spec(SSM_WIDTH), row_spec(SSM_WIDTH),
                  _resident((SSM_WIDTH, SSM_WIDTH)), _resident((1, SSM_WIDTH)),
                  _resident((ATTN_WIDTH + SSM_WIDTH, D_MODEL)), _resident((1, D_MODEL))],
        out_specs=row_spec(D_MODEL),
        out_shape=jax.ShapeDtypeStruct((rows, D_MODEL), F32),
        scratch_shapes=[pltpu.VMEM((SSM_WIDTH, SSM_WIDTH), BF16),
                        pltpu.VMEM((ATTN_WIDTH + SSM_WIDTH, D_MODEL), BF16)],
        compiler_params=pltpu.CompilerParams(dimension_semantics=("arbitrary",),
                                             vmem_limit_bytes=VMEM_LIMIT_BYTES),
        name="out_proj",
    )(x2, mixed_attn, y_ssm, z_ssm, w_glu, b_glu, w_out, gain)


def _rotary_tables(seq_len):
    half = HEAD_DIM // 2
    inv_freq = 1.0 / (ROPE_THETA ** (jnp.arange(half, dtype=F32) / half))
    ang = jnp.arange(seq_len, dtype=F32)[:, None] * inv_freq[None, :]
    cos, sin, zero = jnp.cos(ang), jnp.sin(ang), jnp.zeros_like(ang)
    reps = LANES // HEAD_DIM
    cos_t = jnp.tile(jnp.concatenate([cos, cos], axis=1), (1, reps))
    sin_hi = jnp.tile(jnp.concatenate([zero, sin], axis=1), (1, reps))
    sin_lo = jnp.tile(jnp.concatenate([-sin, zero], axis=1), (1, reps))
    return cos_t, sin_hi, sin_lo


def kernel(x, norm_gain, w_in, w_out, lam_re, lam_im, b_re, b_im, c_re, c_im,
           d_skip, log_dt, w_glu, b_glu, final_gain):
    bsz, seq_len, _ = x.shape
    depth = norm_gain.shape[0]
    assert seq_len % MOBA_BLOCK == 0 and seq_len // MOBA_BLOCK > MOBA_TOP_K
    assert seq_len % SSM_CHUNK == 0
    tm = min(1024, seq_len)
    cos, sin_hi, sin_lo = _rotary_tables(seq_len)

    x2 = x.reshape(bsz * seq_len, D_MODEL)
    for layer in range(depth):
        tables = _ssm_tables(lam_re[layer], lam_im[layer], b_re[layer], b_im[layer],
                             c_re[layer], c_im[layer], log_dt[layer])
        q, k, v, z_attn, u, z_ssm = _in_proj(
            x2, norm_gain[layer].reshape(1, D_MODEL), w_in[layer], cos, sin_hi, sin_lo, seq_len, tm)
        to_seq = lambda t: t.reshape(bsz, seq_len, ATTN_WIDTH)
        mixed_attn = _moba_attention(to_seq(q), to_seq(k), to_seq(v), to_seq(z_attn))
        y_ssm = _s5_ssm(u, tables, d_skip[layer].reshape(1, SSM_WIDTH), seq_len)
        x2 = _out_proj(x2, mixed_attn.reshape(bsz * seq_len, ATTN_WIDTH), y_ssm, z_ssm,
                       w_glu[layer], b_glu[layer].reshape(1, SSM_WIDTH),
                       w_out[layer], final_gain.reshape(1, D_MODEL), tm,
                       final_norm=layer == depth - 1)
    return x2.reshape(bsz, seq_len, D_MODEL)
```

```python
import functools
import math

import jax
import jax.numpy as jnp
from jax import lax
from jax.experimental import pallas as pl
from jax.experimental.pallas import tpu as pltpu

F32 = jnp.float32
BF16 = jnp.bfloat16

D_MODEL = 1024
HEAD_DIM = 64
ATTN_HEADS = 8
ATTN_WIDTH = ATTN_HEADS * HEAD_DIM
MOBA_BLOCK = 256
MOBA_TOP_K = 3
ROPE_THETA = 10000.0
SSM_GROUP_DIM = 16
SSM_GROUPS = 32
SSM_WIDTH = SSM_GROUPS * SSM_GROUP_DIM
SSM_STATE = 64
IN_PROJ_WIDTH = 4 * ATTN_WIDTH + 2 * SSM_WIDTH
NORM_EPS = 1e-6
NEG_INF = -1e30

LANES = 128
SSM_CHUNK = 8
GROUPS_PER_TILE = LANES // SSM_GROUP_DIM
SSM_TILES = SSM_WIDTH // LANES
TILE_COLS = SSM_CHUNK * LANES
STATE_COLS = GROUPS_PER_TILE * SSM_STATE
VMEM_LIMIT_BYTES = 56 * 1024 * 1024
MXU_COLS = 256
OUT_PROJ_ROWS = 256
F32_SUBLANES = 8
BF16_SUBLANES = 16
VALUE_ROWS = HEAD_DIM + BF16_SUBLANES
QK_SCALE = math.log2(math.e) / math.sqrt(HEAD_DIM)


def _resident(shape):
    zeros = (0,) * len(shape)
    return pl.BlockSpec(shape, lambda *_: zeros, pipeline_mode=pl.Buffered(1))


def _in_proj_kernel(x_ref, g_ref, w32_ref, cos_ref, sin_hi_ref, sin_lo_ref,
                    q_ref, k_ref, v_ref, za_ref, u_ref, zs_ref, w_ref):
    @pl.when(pl.program_id(0) == 0)
    def _():
        for c in range(0, IN_PROJ_WIDTH, ATTN_WIDTH):
            w_ref[:, c:c + ATTN_WIDTH] = w32_ref[:, c:c + ATTN_WIDTH].astype(BF16)

    x = x_ref[...]
    ms = jnp.mean(x * x, axis=-1, keepdims=True)
    h = (x * lax.rsqrt(ms + NORM_EPS) * g_ref[...]).astype(BF16)

    def section(idx):
        return jnp.dot(h, w_ref[:, idx * ATTN_WIDTH:(idx + 1) * ATTN_WIDTH],
                       preferred_element_type=F32)

    cos = cos_ref[...]
    sin_hi = sin_hi_ref[...]
    sin_lo = sin_lo_ref[...]

    def rotary(t):
        return (t * cos + pltpu.roll(t, HEAD_DIM // 2, axis=1) * sin_hi
                + pltpu.roll(t, LANES - HEAD_DIM // 2, axis=1) * sin_lo)

    q = section(0)
    k = section(1)
    for c in range(ATTN_WIDTH // LANES):
        sl = slice(c * LANES, (c + 1) * LANES)
        q_ref[:, sl] = (rotary(q[:, sl]) * QK_SCALE).astype(BF16)
        k_ref[:, sl] = rotary(k[:, sl]).astype(BF16)
    v_ref[...] = section(2).astype(BF16)
    za_ref[...] = section(3).astype(BF16)
    u_ref[...] = section(4)
    zs_ref[...] = section(5).astype(BF16)


def _in_proj(x2, gain, w_in, cos, sin_hi, sin_lo, seq_len, tm):
    rows = x2.shape[0]
    pos_blocks = seq_len // tm
    row_spec = lambda width: pl.BlockSpec((tm, width), lambda i: (i, 0))
    tab_spec = pl.BlockSpec((tm, LANES), lambda i: (i % pos_blocks, 0))
    out_bf16 = jax.ShapeDtypeStruct((rows, ATTN_WIDTH), BF16)
    out_f32 = jax.ShapeDtypeStruct((rows, ATTN_WIDTH), F32)
    return pl.pallas_call(
        _in_proj_kernel,
        grid=(rows // tm,),
        in_specs=[row_spec(D_MODEL), _resident((1, D_MODEL)), _resident((D_MODEL, IN_PROJ_WIDTH)),
                  tab_spec, tab_spec, tab_spec],
        out_specs=[row_spec(ATTN_WIDTH)] * 6,
        out_shape=[out_bf16, out_bf16, out_bf16, out_bf16, out_f32, out_bf16],
        scratch_shapes=[pltpu.VMEM((D_MODEL, IN_PROJ_WIDTH), BF16)],
        compiler_params=pltpu.CompilerParams(dimension_semantics=("arbitrary",),
                                             vmem_limit_bytes=VMEM_LIMIT_BYTES),
        name="in_proj",
    )(x2, gain, w_in, cos, sin_hi, sin_lo)


def _attn_kernel(q0_ref, k0_ref, v0_ref, qn_ref, kn_ref, vn_ref, z_ref, o_ref,
                 kx_ref, vx_ref, qx_ref, sa_ref, sb_ref, m_ref, acc_ref, *, n_blocks):
    unit = pl.program_id(0)
    blk = MOBA_BLOCK
    heads = LANES // HEAD_DIM
    n_pairs = n_blocks // 2
    cur = lax.rem(unit, 2)
    nxt = 1 - cur

    head_rows = lambda h: slice(h * HEAD_DIM, (h + 1) * HEAD_DIM)
    spare_base = lambda h: ((h + 1) % heads) * HEAD_DIM
    iota = lambda shape, d: lax.broadcasted_iota(jnp.int32, shape, d)
    key = iota((blk, blk), 0)
    query = iota((blk, blk), 1)
    col_max = lambda s: jnp.max(s, axis=0, keepdims=True)

    def prepare_items(q_ref, k_ref, v_ref, slot):
        kmean = {}

        def key_block(j):
            lane = iota((blk, LANES), 1)
            sum_rows = (iota((VALUE_ROWS - HEAD_DIM, blk), 0) == 0).astype(F32)
            rows = slice(j * blk, (j + 1) * blk)
            kj = k_ref[0, rows, :]
            kmean[j] = jnp.mean(kj.astype(F32), axis=0, keepdims=True)
            vt = v_ref[0, rows, :].astype(F32).T
            for h in range(heads):
                in_head = (lane >= h * HEAD_DIM) & (lane < (h + 1) * HEAD_DIM)
                tag = jnp.where(lane == spare_base(h) + j, 1.0, 0.0).astype(BF16)
                kx_ref[slot, h, j] = jnp.where(in_head, kj, tag)
                vx_ref[slot, j, h] = jnp.concatenate([vt[head_rows(h)], sum_rows],
                                                     axis=0).astype(BF16)

        def query_block(qi):
            if "all" not in kmean:
                kmean["all"] = jnp.concatenate([kmean[j] for j in range(n_blocks)],
                                               axis=0).astype(BF16)
            blk_row = iota((n_blocks, blk), 0)
            zero_rows = lambda n: jnp.zeros((n, blk), F32)
            in_head_order = lambda h, own, other: [own, other] if h == 0 else [other, own]
            qt = q_ref[0, qi * blk:(qi + 1) * blk, :].astype(F32).T
            past = blk_row < qi
            for h in range(heads):
                q_rows = qt[head_rows(h)]
                q_only = jnp.concatenate(in_head_order(h, q_rows, zero_rows(HEAD_DIM)), axis=0)
                gate = jnp.dot(kmean["all"], q_only.astype(BF16), preferred_element_type=F32)
                gate = jnp.where(past, gate, NEG_INF)
                beaten_by = jnp.zeros((n_blocks, blk), jnp.int32)
                for j in range(n_blocks):
                    gj = gate[j:j + 1, :]
                    wins = (gj > gate) | ((gj == gate) & (j < blk_row))
                    beaten_by = beaten_by + wins.astype(jnp.int32)
                keep = (past & (beaten_by < MOBA_TOP_K)) | (blk_row == qi)
                bias = jnp.where(keep, 0.0, NEG_INF)
                spare = jnp.concatenate([bias, zero_rows(HEAD_DIM - n_blocks)], axis=0)
                qx_ref[slot, qi, h] = jnp.concatenate(in_head_order(h, q_rows, spare),
                                                      axis=0).astype(BF16)

        return ([functools.partial(key_block, j) for j in range(n_blocks)]
                + [functools.partial(query_block, qi) for qi in range(n_blocks)])

    def visible_blocks(pair):
        tiles = (pair, n_blocks - 1 - pair)
        return ([(t, tiles[t], True) for t in range(2)]
                + [(t, j, False) for t in range(2) for j in range(tiles[t])])

    traced = lambda t: t + jnp.minimum(unit, 0)

    def stage_items(pair, s_ref, m, slot):
        def block(t, j, own, h):
            qi = (pair, n_blocks - 1 - pair)[t]
            s = jnp.dot(kx_ref[slot, h, j], qx_ref[slot, qi, h], preferred_element_type=F32)
            if own:
                s = jnp.where(key <= query, s, NEG_INF)
            s_ref[traced(t), h, j] = s
            m[t, h] = col_max(s) if own else jnp.maximum(m[t, h], col_max(s))
        return [functools.partial(block, *b, h) for b in visible_blocks(pair)
                for h in range(heads)]

    def finish_items(pair, s_ref, m, acc_ref, slot):
        def block(t, j, own, h):
            p = jnp.exp2(s_ref[traced(t), h, j] - m[t, h]).astype(BF16)
            pv = jnp.dot(vx_ref[slot, j, h], p, preferred_element_type=F32)
            if own:
                acc_ref[traced(t), h] = pv
            else:
                acc_ref[traced(t), h] += pv

        def write_rows():
            for t, qi in enumerate((pair, n_blocks - 1 - pair)):
                rows = slice(qi * blk, (qi + 1) * blk)
                acc = [acc_ref[t, h] for h in range(heads)]
                o_t = jnp.concatenate([a[:HEAD_DIM] / a[HEAD_DIM:HEAD_DIM + 1] for a in acc],
                                      axis=0)
                z = z_ref[0, rows, :].astype(F32)
                silu = 0.5 * z * (1.0 + jnp.tanh(0.5 * z))
                o_ref[0, rows, :] = (o_t.T * silu).astype(BF16)

        return ([functools.partial(block, *b, h) for b in visible_blocks(pair)
                 for h in range(heads)] + [write_rows])

    def alternate(*item_lists):
        for group in zip(*item_lists):
            for item in group:
                item()
        for items in item_lists:
            for item in items[min(map(len, item_lists)):]:
                item()

    m_keys = [(t, h) for t in range(2) for h in range(heads)]

    @pl.when(unit == 0)
    def _():
        m0 = {}
        alternate(prepare_items(q0_ref, k0_ref, v0_ref, 0))
        alternate(stage_items(0, sa_ref, m0, 0))
        for t, h in m_keys:
            m_ref[t, h] = m0[t, h]

    m = {key_: m_ref[key_] for key_ in m_keys}
    for pair in range(n_pairs):
        s_ref, s_next = (sa_ref, sb_ref) if pair % 2 == 0 else (sb_ref, sa_ref)
        m_next = {}
        lists = [finish_items(pair, s_ref, m, acc_ref.at[pair % 2], cur)]
        if pair + 1 < n_pairs:
            lists.append(stage_items(pair + 1, s_next, m_next, cur))
        else:
            lists.append(stage_items(0, s_next, m_next, nxt))
        if pair == n_pairs - 2:
            lists.append(prepare_items(qn_ref, kn_ref, vn_ref, nxt))
        alternate(lists[1], lists[0], *lists[2:])
        m = m_next
    for t, h in m_keys:
        m_ref[t, h] = m[t, h]


def _moba_attention(q, k, v, z_attn):
    bsz, seq_len, _ = q.shape
    n_blocks = seq_len // MOBA_BLOCK
    assert n_blocks % 4 == 0 and n_blocks <= HEAD_DIM
    head_pairs = ATTN_WIDTH // LANES
    heads = LANES // HEAD_DIM
    assert heads == 2, "the block bias rides in the other head's half of the 128 lanes"
    n_units = bsz * head_pairs
    block = (1, seq_len, LANES)
    unit_index = lambda u: (u // head_pairs, 0, u % head_pairs)
    first_spec = pl.BlockSpec(block, lambda u: (0, 0, 0))
    next_spec = pl.BlockSpec(block, lambda u: unit_index(jnp.minimum(u + 1, n_units - 1)))
    unit_spec = pl.BlockSpec(block, unit_index)
    scores = pltpu.VMEM((2, heads, n_blocks, MOBA_BLOCK, MOBA_BLOCK), F32)
    return pl.pallas_call(
        functools.partial(_attn_kernel, n_blocks=n_blocks),
        grid=(n_units,),
        in_specs=[first_spec] * 3 + [next_spec] * 3 + [unit_spec],
        out_specs=unit_spec,
        out_shape=jax.ShapeDtypeStruct((bsz, seq_len, ATTN_WIDTH), BF16),
        scratch_shapes=[pltpu.VMEM((2, heads, n_blocks, MOBA_BLOCK, LANES), BF16),
                        pltpu.VMEM((2, n_blocks, heads, VALUE_ROWS, MOBA_BLOCK), BF16),
                        pltpu.VMEM((2, n_blocks, heads, LANES, MOBA_BLOCK), BF16),
                        scores, scores,
                        pltpu.VMEM((2, heads, 1, MOBA_BLOCK), F32),
                        pltpu.VMEM((2, 2, heads, VALUE_ROWS, MOBA_BLOCK), F32)],
        compiler_params=pltpu.CompilerParams(dimension_semantics=("arbitrary",),
                                             vmem_limit_bytes=VMEM_LIMIT_BYTES),
        name="moba_attn",
    )(q, k, v, q, k, v, z_attn)


def _ssm_tables(lam_re, lam_im, b_re, b_im, c_re, c_im, log_dt):
    T, G, P, H = SSM_CHUNK, SSM_GROUPS, SSM_STATE, SSM_GROUP_DIM
    dt = jnp.exp(log_dt.astype(F32))[:, None]
    lam_r, lam_i = lam_re.astype(F32), lam_im.astype(F32)
    mag = jnp.exp(lam_r * dt)
    bar_r, bar_i = mag * jnp.cos(lam_i * dt), mag * jnp.sin(lam_i * dt)
    den = lam_r * lam_r + lam_i * lam_i
    f_r = ((bar_r - 1.0) * lam_r + bar_i * lam_i) / den
    f_i = (bar_i * lam_r - (bar_r - 1.0) * lam_i) / den
    bb_r = f_r[..., None] * b_re - f_i[..., None] * b_im
    bb_i = f_r[..., None] * b_im + f_i[..., None] * b_re

    def powers(base_r, base_i, count):
        out_r, out_i = [base_r], [base_i]
        for _ in range(count - 1):
            r, i = out_r[-1], out_i[-1]
            out_r.append(r * base_r - i * base_i)
            out_i.append(r * base_i + i * base_r)
        return out_r, out_i

    pw_r, pw_i = powers(bar_r, bar_i, T)
    pw_r = jnp.stack([jnp.ones_like(bar_r)] + pw_r)
    pw_i = jnp.stack([jnp.zeros_like(bar_i)] + pw_i)

    lb_r = pw_r[:T, :, :, None] * bb_r - pw_i[:T, :, :, None] * bb_i
    lb_i = pw_r[:T, :, :, None] * bb_i + pw_i[:T, :, :, None] * bb_r
    cl_r = c_re[None] * pw_r[:, :, None, :] - c_im[None] * pw_i[:, :, None, :]
    cl_i = c_re[None] * pw_i[:, :, None, :] + c_im[None] * pw_r[:, :, None, :]
    lb_t = jnp.stack([lb_r, lb_i], axis=1).transpose(0, 1, 4, 2, 3).reshape(T, 2, H, G * P)
    cl_t = jnp.stack([cl_r, -cl_i], axis=1).transpose(0, 1, 4, 2, 3).reshape(T + 1, 2, P, G * H)

    a_r, a_i = powers(pw_r[T], pw_i[T], F32_SUBLANES)
    tiled = lambda a: jnp.stack(a).reshape(len(a), SSM_TILES, STATE_COLS)
    a_lin = jnp.stack([tiled(a_r), tiled(a_i)]).transpose(2, 0, 1, 3)
    shifts = [1 << k for k in range(F32_SUBLANES.bit_length() - 1)]
    a_log = jnp.stack([tiled([a_r[s - 1] for s in shifts]),
                       tiled([a_i[s - 1] for s in shifts])]).transpose(2, 1, 0, 3)
    row = jnp.arange(F32_SUBLANES)[None, None, None, :, None]
    keep = row >= jnp.asarray(shifts)[None, :, None, None, None]
    a_log = jnp.where(keep, a_log[:, :, :, None, :], 0.0)
    return lb_t, cl_t, a_log, a_lin


def _ssm_kernel(u_ref, lb_ref, cl_ref, alog_ref, alin_ref, d_ref, y_ref, m_ref, ws_ref, wo_ref,
                work_ref, prev_ref, *, seq_len, n_seq):
    T, H, P = SSM_CHUNK, SSM_GROUP_DIM, SSM_STATE
    n_chunks = seq_len // T
    iota = lambda shape, d: lax.broadcasted_iota(jnp.int32, shape, d)
    h_bits, p_bits = H.bit_length() - 1, P.bit_length() - 1

    @pl.when(pl.program_id(1) == 0)
    def _():
        spread_h = ((iota((LANES, H), 0) & (H - 1)) == iota((LANES, H), 1)).astype(BF16)
        spread_p = ((iota((STATE_COLS, P), 0) & (P - 1)) == iota((STATE_COLS, P), 1)).astype(BF16)
        same_hp = (iota((LANES, STATE_COLS), 0) >> h_bits) == (iota((LANES, STATE_COLS), 1) >> p_bits)
        same_ph = (iota((STATE_COLS, LANES), 0) >> p_bits) == (iota((STATE_COLS, LANES), 1) >> h_bits)

        def expand(spread, coeff, same):
            full = jnp.dot(spread, coeff.astype(BF16), preferred_element_type=F32)
            return jnp.where(same, full, 0.0).astype(BF16)

        rows = lambda j: slice(j * LANES, (j + 1) * LANES)
        state_out = lambda d: jnp.concatenate(
            [expand(spread_p, cl_ref[d, ri], same_ph) for ri in range(2)], axis=0)
        for j in range(T):
            ws_ref[rows(j), :] = jnp.concatenate(
                [expand(spread_h, lb_ref[T - 1 - j, ri], same_hp) for ri in range(2)], axis=1)
        for i in range(T):
            wo_ref[:, rows(i)] = state_out(i + 1)
        c_out = state_out(0)
        for d in range(T):
            block = jnp.dot(ws_ref[rows(T - 1 - d), :], c_out,
                            preferred_element_type=F32).astype(BF16)
            for j in range(T - d):
                m_ref[rows(j), rows(j + d)] = block
        for j in range(T):
            for i in range(j):
                m_ref[rows(j), rows(i)] = jnp.zeros((LANES, LANES), BF16)

    pieces = [u_ref[pl.ds(b * seq_len + i, n_chunks, stride=T), :]
              for b in range(n_seq) for i in range(T)]
    x = jnp.concatenate(
        [jnp.concatenate(pieces[b * T:(b + 1) * T], axis=1) for b in range(n_seq)], axis=0)
    x_lo = x.astype(BF16)
    groups = [slice(c, c + MXU_COLS) for c in range(0, STATE_COLS, MXU_COLS)]
    im_of = lambda cols: slice(STATE_COLS + cols.start, STATE_COLS + cols.stop)
    traced = lambda k: k + jnp.minimum(pl.program_id(1), 0)
    n_groups = len(groups)
    y_region = 2 * n_groups

    def increment(g, part):
        cols = groups[g] if part == 0 else im_of(groups[g])
        work_ref[traced(2 * g + part)] = jnp.dot(x_lo, ws_ref[:, cols], preferred_element_type=F32)

    def intra_chunk(n):
        hi = (n + 1) * MXU_COLS
        work_ref[traced(y_region + n)] = jnp.dot(x_lo[:, :hi], m_ref[:hi, hi - MXU_COLS:hi],
                                                 preferred_element_type=F32)

    sub = F32_SUBLANES
    first_row = iota((sub, MXU_COLS), 0) == 0
    cmul = lambda a_re, a_im, b_re, b_im: (a_re * b_re - a_im * b_im, a_re * b_im + a_im * b_re)
    tiles_per_seq = n_chunks // sub
    pack = BF16_SUBLANES // sub

    def scan_items(g):
        cols = groups[g]
        log_mul = [(alog_ref[0, k, 0, :, cols], alog_ref[0, k, 1, :, cols])
                   for k in range(sub.bit_length() - 1)]
        lin_mul = (alin_ref[0, 0, :, cols], alin_ref[0, 1, :, cols])

        def sequence(b):
            carry = (jnp.zeros((1, MXU_COLS), F32),) * 2
            pending = []
            for t in range(tiles_per_seq):
                rows = pl.ds((b * tiles_per_seq + t) * sub, sub)
                t_re = work_ref[traced(2 * g), rows, :]
                t_im = work_ref[traced(2 * g + 1), rows, :]
                for k, mul in enumerate(log_mul):
                    d_re, d_im = cmul(*mul, pltpu.roll(t_re, 1 << k, axis=0),
                                      pltpu.roll(t_im, 1 << k, axis=0))
                    t_re, t_im = t_re + d_re, t_im + d_im
                if t > 0:
                    d_re, d_im = cmul(*lin_mul, *carry)
                    t_re, t_im = t_re + d_re, t_im + d_im
                pending.append((jnp.where(first_row, carry[0], pltpu.roll(t_re, 1, axis=0)),
                                jnp.where(first_row, carry[1], pltpu.roll(t_im, 1, axis=0))))
                carry = (t_re[sub - 1:sub], t_im[sub - 1:sub])
                if len(pending) == pack:
                    out = pl.ds((b * tiles_per_seq + t + 1 - pack) * sub, pack * sub)
                    for part, at in enumerate((cols, im_of(cols))):
                        prev_ref[out, at] = jnp.concatenate(
                            [p[part] for p in pending], axis=0).astype(BF16)
                    pending = []
        return [functools.partial(sequence, b) for b in range(n_seq)]

    def alternate(*item_lists):
        for group in zip(*item_lists):
            for item in group:
                item()
        for items in item_lists:
            for item in items[min(map(len, item_lists)):]:
                item()

    increment(0, 0)
    increment(0, 1)
    matmuls = ([functools.partial(increment, g, part) for g in range(1, n_groups)
                for part in range(2)]
               + [functools.partial(intra_chunk, n) for n in range(T * LANES // MXU_COLS)])
    per_scan = -(-len(matmuls) // n_groups)
    for g in range(n_groups):
        alternate(scan_items(g), matmuls[g * per_scan:(g + 1) * per_scan])
    per_tile = MXU_COLS // LANES
    for n in range(T * LANES // MXU_COLS):
        cols = slice(n * MXU_COLS, (n + 1) * MXU_COLS)
        y = (work_ref[y_region + n] + x[:, cols] * jnp.concatenate([d_ref[...]] * per_tile, axis=1)
             + jnp.dot(prev_ref[...], wo_ref[:, cols], preferred_element_type=F32))
        for b in range(n_seq):
            for i in range(per_tile):
                y_ref[pl.ds(b * seq_len + n * per_tile + i, n_chunks, stride=T), :] = (
                    y[b * n_chunks:(b + 1) * n_chunks, i * LANES:(i + 1) * LANES])


def _s5_ssm(u, tables, d_skip, seq_len):
    lb_t, cl_t, a_log, a_lin = tables
    T, H, P = SSM_CHUNK, SSM_GROUP_DIM, SSM_STATE
    assert seq_len % (T * F32_SUBLANES) == 0
    bsz = u.shape[0] // seq_len
    n_seq = max(n for n in (4, 2, 1) if bsz % n == 0)
    io_spec = pl.BlockSpec((n_seq * seq_len, LANES), lambda q, b: (b, q))
    square = pltpu.VMEM((T * LANES, T * LANES), BF16)
    n_groups = STATE_COLS // MXU_COLS
    chunk_rows = n_seq * seq_len // T
    return pl.pallas_call(
        functools.partial(_ssm_kernel, seq_len=seq_len, n_seq=n_seq),
        grid=(SSM_TILES, bsz // n_seq),
        in_specs=[io_spec,
                  pl.BlockSpec((T, 2, H, STATE_COLS), lambda q, b: (0, 0, 0, q)),
                  pl.BlockSpec((T + 1, 2, P, LANES), lambda q, b: (0, 0, 0, q)),
                  pl.BlockSpec((1,) + a_log.shape[1:], lambda q, b: (q, 0, 0, 0, 0)),
                  pl.BlockSpec((1,) + a_lin.shape[1:], lambda q, b: (q, 0, 0, 0)),
                  pl.BlockSpec((1, LANES), lambda q, b: (0, q))],
        out_specs=io_spec,
        out_shape=jax.ShapeDtypeStruct(u.shape, F32),
        scratch_shapes=[square, square, square,
                        pltpu.VMEM((2 * n_groups + T * LANES // MXU_COLS, chunk_rows, MXU_COLS), F32),
                        pltpu.VMEM((chunk_rows, 2 * STATE_COLS), BF16)],
        compiler_params=pltpu.CompilerParams(dimension_semantics=("arbitrary", "arbitrary"),
                                             vmem_limit_bytes=VMEM_LIMIT_BYTES),
        name="s5_ssm",
    )(u, lb_t, cl_t, a_log, a_lin, d_skip)


def _out_proj_kernel(x_ref, ma_ref, y_ref, zs_ref, wg32_ref, bg_ref, w32_ref, g_ref, o_ref,
                     wg_ref, w_ref, *, final_norm):
    @pl.when(pl.program_id(0) == 0)
    def _():
        wg_ref[...] = wg32_ref[...].astype(BF16)
        w_ref[...] = w32_ref[...].astype(BF16)

    half_plus = lambda t: 1.0 + jnp.tanh(0.5 * t)
    for c in range(0, x_ref.shape[0], OUT_PROJ_ROWS):
        rows = slice(c, c + OUT_PROJ_ROWS)
        y = y_ref[rows, :]
        y = y * (1.0 + jnp.tanh(math.sqrt(2.0 / math.pi) * (y + 0.044715 * (y * y * y))))
        gate = jnp.dot((0.5 * y).astype(BF16), wg_ref[...], preferred_element_type=F32) + bg_ref[...]
        z = zs_ref[rows, :].astype(F32)
        mixed_ssm = (0.125 * y) * half_plus(gate) * (z * half_plus(z))
        mixed = jnp.concatenate([ma_ref[rows, :], mixed_ssm.astype(BF16)], axis=1)
        r = x_ref[rows, :] + jnp.dot(mixed, w_ref[...], preferred_element_type=F32)
        if final_norm:
            ms = jnp.mean(r * r, axis=-1, keepdims=True)
            r = r * lax.rsqrt(ms + NORM_EPS) * g_ref[...]
        o_ref[rows, :] = r


def _out_proj(x2, mixed_attn, y_ssm, z_ssm, w_glu, b_glu, w_out, gain, tm, final_norm):
    rows = x2.shape[0]
    row_spec = lambda width: pl.BlockSpec((tm, width), lambda i: (i, 0))
    return pl.pallas_call(
        functools.partial(_out_proj_kernel, final_norm=final_norm),
        grid=(rows // tm,),
        in_specs=[row_spec(D_MODEL), row_spec(ATTN_WIDTH), row_spec(SSM_WIDTH), row_spec(SSM_WIDTH),
                  _resident((SSM_WIDTH, SSM_WIDTH)), _resident((1, SSM_WIDTH)),
                  _resident((ATTN_WIDTH + SSM_WIDTH, D_MODEL)), _resident((1, D_MODEL))],
        out_specs=row_spec(D_MODEL),
        out_shape=jax.ShapeDtypeStruct((rows, D_MODEL), F32),
        scratch_shapes=[pltpu.VMEM((SSM_WIDTH, SSM_WIDTH), BF16),
                        pltpu.VMEM((ATTN_WIDTH + SSM_WIDTH, D_MODEL), BF16)],
        compiler_params=pltpu.CompilerParams(dimension_semantics=("arbitrary",),
                                             vmem_limit_bytes=VMEM_LIMIT_BYTES),
        name="out_proj",
    )(x2, mixed_attn, y_ssm, z_ssm, w_glu, b_glu, w_out, gain)


def _rotary_tables(seq_len):
    half = HEAD_DIM // 2
    inv_freq = 1.0 / (ROPE_THETA ** (jnp.arange(half, dtype=F32) / half))
    ang = jnp.arange(seq_len, dtype=F32)[:, None] * inv_freq[None, :]
    cos, sin, zero = jnp.cos(ang), jnp.sin(ang), jnp.zeros_like(ang)
    reps = LANES // HEAD_DIM
    cos_t = jnp.tile(jnp.concatenate([cos, cos], axis=1), (1, reps))
    sin_hi = jnp.tile(jnp.concatenate([zero, sin], axis=1), (1, reps))
    sin_lo = jnp.tile(jnp.concatenate([-sin, zero], axis=1), (1, reps))
    return cos_t, sin_hi, sin_lo


def kernel(x, norm_gain, w_in, w_out, lam_re, lam_im, b_re, b_im, c_re, c_im,
           d_skip, log_dt, w_glu, b_glu, final_gain):
    bsz, seq_len, _ = x.shape
    depth = norm_gain.shape[0]
    assert seq_len % MOBA_BLOCK == 0 and seq_len // MOBA_BLOCK > MOBA_TOP_K
    assert seq_len % SSM_CHUNK == 0
    tm = min(1024, seq_len)
    cos, sin_hi, sin_lo = _rotary_tables(seq_len)

    x2 = x.reshape(bsz * seq_len, D_MODEL)
    for layer in range(depth):
        tables = _ssm_tables(lam_re[layer], lam_im[layer], b_re[layer], b_im[layer],
                             c_re[layer], c_im[layer], log_dt[layer])
        q, k, v, z_attn, u, z_ssm = _in_proj(
            x2, norm_gain[layer].reshape(1, D_MODEL), w_in[layer], cos, sin_hi, sin_lo, seq_len, tm)
        to_seq = lambda t: t.reshape(bsz, seq_len, ATTN_WIDTH)
        mixed_attn = _moba_attention(to_seq(q), to_seq(k), to_seq(v), to_seq(z_attn))
        y_ssm = _s5_ssm(u, tables, d_skip[layer].reshape(1, SSM_WIDTH), seq_len)
        x2 = _out_proj(x2, mixed_attn.reshape(bsz * seq_len, ATTN_WIDTH), y_ssm, z_ssm,
                       w_glu[layer], b_glu[layer].reshape(1, SSM_WIDTH),
                       w_out[layer], final_gain.reshape(1, D_MODEL), tm,
                       final_norm=layer == depth - 1)
    return x2.reshape(bsz, seq_len, D_MODEL)
```

```python
import functools
import math

import jax
import jax.numpy as jnp
from jax import lax
from jax.experimental import pallas as pl
from jax.experimental.pallas import tpu as pltpu

F32 = jnp.float32
BF16 = jnp.bfloat16

D_MODEL = 1024
HEAD_DIM = 64
ATTN_HEADS = 8
ATTN_WIDTH = ATTN_HEADS * HEAD_DIM
MOBA_BLOCK = 256
MOBA_TOP_K = 3
ROPE_THETA = 10000.0
SSM_GROUP_DIM = 16
SSM_GROUPS = 32
SSM_WIDTH = SSM_GROUPS * SSM_GROUP_DIM
SSM_STATE = 64
IN_PROJ_WIDTH = 4 * ATTN_WIDTH + 2 * SSM_WIDTH
NORM_EPS = 1e-6
NEG_INF = -1e30

LANES = 128
SSM_CHUNK = 8
GROUPS_PER_TILE = LANES // SSM_GROUP_DIM
SSM_TILES = SSM_WIDTH // LANES
TILE_COLS = SSM_CHUNK * LANES
STATE_COLS = GROUPS_PER_TILE * SSM_STATE
VMEM_LIMIT_BYTES = 56 * 1024 * 1024
MXU_COLS = 256
OUT_PROJ_ROWS = 256
PROJ_TILE_ROWS = 1024
SSM_SEQS_PER_STEP = 4
F32_SUBLANES = 8
BF16_SUBLANES = 16
VALUE_ROWS = HEAD_DIM + BF16_SUBLANES
QK_SCALE = math.log2(math.e) / math.sqrt(HEAD_DIM)


def _resident(shape):
    zeros = (0,) * len(shape)
    return pl.BlockSpec(shape, lambda *_: zeros, pipeline_mode=pl.Buffered(1))


def _in_proj_kernel(x_ref, g_ref, w32_ref, cos_ref, sin_hi_ref, sin_lo_ref,
                    q_ref, k_ref, v_ref, za_ref, u_ref, zs_ref, w_ref):
    @pl.when(pl.program_id(0) == 0)
    def _():
        for c in range(0, IN_PROJ_WIDTH, ATTN_WIDTH):
            w_ref[:, c:c + ATTN_WIDTH] = w32_ref[:, c:c + ATTN_WIDTH].astype(BF16)

    x = x_ref[...]
    ms = jnp.mean(x * x, axis=-1, keepdims=True)
    h = (x * lax.rsqrt(ms + NORM_EPS) * g_ref[...]).astype(BF16)

    def section(idx):
        return jnp.dot(h, w_ref[:, idx * ATTN_WIDTH:(idx + 1) * ATTN_WIDTH],
                       preferred_element_type=F32)

    cos = cos_ref[...]
    sin_hi = sin_hi_ref[...]
    sin_lo = sin_lo_ref[...]

    def rotary(t):
        return (t * cos + pltpu.roll(t, HEAD_DIM // 2, axis=1) * sin_hi
                + pltpu.roll(t, LANES - HEAD_DIM // 2, axis=1) * sin_lo)

    q = section(0)
    k = section(1)
    for c in range(ATTN_WIDTH // LANES):
        sl = slice(c * LANES, (c + 1) * LANES)
        q_ref[:, sl] = (rotary(q[:, sl]) * QK_SCALE).astype(BF16)
        k_ref[:, sl] = rotary(k[:, sl]).astype(BF16)
    v_ref[...] = section(2).astype(BF16)
    za_ref[...] = section(3).astype(BF16)
    u_ref[...] = section(4)
    zs_ref[...] = section(5).astype(BF16)


def _in_proj(x2, gain, w_in, cos, sin_hi, sin_lo, seq_len, tm):
    rows = x2.shape[0]
    pos_blocks = seq_len // tm
    row_spec = lambda width: pl.BlockSpec((tm, width), lambda i: (i, 0))
    tab_spec = pl.BlockSpec((tm, LANES), lambda i: (i % pos_blocks, 0))
    out_bf16 = jax.ShapeDtypeStruct((rows, ATTN_WIDTH), BF16)
    out_f32 = jax.ShapeDtypeStruct((rows, ATTN_WIDTH), F32)
    return pl.pallas_call(
        _in_proj_kernel,
        grid=(rows // tm,),
        in_specs=[row_spec(D_MODEL), _resident((1, D_MODEL)), _resident((D_MODEL, IN_PROJ_WIDTH)),
                  tab_spec, tab_spec, tab_spec],
        out_specs=[row_spec(ATTN_WIDTH)] * 6,
        out_shape=[out_bf16, out_bf16, out_bf16, out_bf16, out_f32, out_bf16],
        scratch_shapes=[pltpu.VMEM((D_MODEL, IN_PROJ_WIDTH), BF16)],
        compiler_params=pltpu.CompilerParams(dimension_semantics=("arbitrary",),
                                             vmem_limit_bytes=VMEM_LIMIT_BYTES),
        name="in_proj",
    )(x2, gain, w_in, cos, sin_hi, sin_lo)


def _attn_kernel(q0_ref, k0_ref, v0_ref, qn_ref, kn_ref, vn_ref, z_ref, o_ref,
                 kx_ref, vx_ref, qx_ref, sa_ref, sb_ref, m_ref, acc_ref, *, n_blocks):
    unit = pl.program_id(0)
    blk = MOBA_BLOCK
    heads = LANES // HEAD_DIM
    n_pairs = n_blocks // 2
    cur = lax.rem(unit, 2)
    nxt = 1 - cur

    head_rows = lambda h: slice(h * HEAD_DIM, (h + 1) * HEAD_DIM)
    spare_base = lambda h: ((h + 1) % heads) * HEAD_DIM
    iota = lambda shape, d: lax.broadcasted_iota(jnp.int32, shape, d)
    key = iota((blk, blk), 0)
    query = iota((blk, blk), 1)
    col_max = lambda s: jnp.max(s, axis=0, keepdims=True)

    def prepare_items(q_ref, k_ref, v_ref, slot):
        kmean = {}

        def key_block(j):
            lane = iota((blk, LANES), 1)
            sum_rows = (iota((VALUE_ROWS - HEAD_DIM, blk), 0) == 0).astype(F32)
            rows = slice(j * blk, (j + 1) * blk)
            kj = k_ref[0, rows, :]
            kmean[j] = jnp.mean(kj.astype(F32), axis=0, keepdims=True)
            vt = v_ref[0, rows, :].astype(F32).T
            for h in range(heads):
                in_head = (lane >= h * HEAD_DIM) & (lane < (h + 1) * HEAD_DIM)
                tag = jnp.where(lane == spare_base(h) + j, 1.0, 0.0).astype(BF16)
                kx_ref[slot, h, j] = jnp.where(in_head, kj, tag)
                vx_ref[slot, j, h] = jnp.concatenate([vt[head_rows(h)], sum_rows],
                                                     axis=0).astype(BF16)

        def query_block(qi):
            if "all" not in kmean:
                kmean["all"] = jnp.concatenate([kmean[j] for j in range(n_blocks)],
                                               axis=0).astype(BF16)
            blk_row = iota((n_blocks, blk), 0)
            zero_rows = lambda n: jnp.zeros((n, blk), F32)
            in_head_order = lambda h, own, other: [own, other] if h == 0 else [other, own]
            qt = q_ref[0, qi * blk:(qi + 1) * blk, :].astype(F32).T
            past = blk_row < qi
            for h in range(heads):
                q_rows = qt[head_rows(h)]
                q_only = jnp.concatenate(in_head_order(h, q_rows, zero_rows(HEAD_DIM)), axis=0)
                gate = jnp.dot(kmean["all"], q_only.astype(BF16), preferred_element_type=F32)
                gate = jnp.where(past, gate, NEG_INF)
                beaten_by = jnp.zeros((n_blocks, blk), jnp.int32)
                for j in range(n_blocks):
                    gj = gate[j:j + 1, :]
                    wins = (gj > gate) | ((gj == gate) & (j < blk_row))
                    beaten_by = beaten_by + wins.astype(jnp.int32)
                keep = (past & (beaten_by < MOBA_TOP_K)) | (blk_row == qi)
                bias = jnp.where(keep, 0.0, NEG_INF)
                spare = jnp.concatenate([bias, zero_rows(HEAD_DIM - n_blocks)], axis=0)
                qx_ref[slot, qi, h] = jnp.concatenate(in_head_order(h, q_rows, spare),
                                                      axis=0).astype(BF16)

        return ([functools.partial(key_block, j) for j in range(n_blocks)]
                + [functools.partial(query_block, qi) for qi in range(n_blocks)])

    def visible_blocks(pair):
        tiles = (pair, n_blocks - 1 - pair)
        return ([(t, tiles[t], True) for t in range(2)]
                + [(t, j, False) for t in range(2) for j in range(tiles[t])])

    traced = lambda t: t + jnp.minimum(unit, 0)

    def stage_items(pair, s_ref, m, slot):
        def block(t, j, own, h):
            qi = (pair, n_blocks - 1 - pair)[t]
            s = jnp.dot(kx_ref[slot, h, j], qx_ref[slot, qi, h], preferred_element_type=F32)
            if own:
                s = jnp.where(key <= query, s, NEG_INF)
            s_ref[traced(t), h, j] = s
            m[t, h] = col_max(s) if own else jnp.maximum(m[t, h], col_max(s))
        return [functools.partial(block, *b, h) for b in visible_blocks(pair)
                for h in range(heads)]

    def finish_items(pair, s_ref, m, acc_ref, slot):
        def block(t, j, own, h):
            p = jnp.exp2(s_ref[traced(t), h, j] - m[t, h]).astype(BF16)
            pv = jnp.dot(vx_ref[slot, j, h], p, preferred_element_type=F32)
            if own:
                acc_ref[traced(t), h] = pv
            else:
                acc_ref[traced(t), h] += pv

        def write_rows():
            for t, qi in enumerate((pair, n_blocks - 1 - pair)):
                rows = slice(qi * blk, (qi + 1) * blk)
                acc = [acc_ref[t, h] for h in range(heads)]
                o_t = jnp.concatenate([a[:HEAD_DIM] / a[HEAD_DIM:HEAD_DIM + 1] for a in acc],
                                      axis=0)
                z = z_ref[0, rows, :].astype(F32)
                silu = 0.5 * z * (1.0 + jnp.tanh(0.5 * z))
                o_ref[0, rows, :] = (o_t.T * silu).astype(BF16)

        return ([functools.partial(block, *b, h) for b in visible_blocks(pair)
                 for h in range(heads)] + [write_rows])

    def alternate(*item_lists):
        for group in zip(*item_lists):
            for item in group:
                item()
        for items in item_lists:
            for item in items[min(map(len, item_lists)):]:
                item()

    m_keys = [(t, h) for t in range(2) for h in range(heads)]

    @pl.when(unit == 0)
    def _():
        m0 = {}
        alternate(prepare_items(q0_ref, k0_ref, v0_ref, 0))
        alternate(stage_items(0, sa_ref, m0, 0))
        for t, h in m_keys:
            m_ref[t, h] = m0[t, h]

    m = {key_: m_ref[key_] for key_ in m_keys}
    for pair in range(n_pairs):
        s_ref, s_next = (sa_ref, sb_ref) if pair % 2 == 0 else (sb_ref, sa_ref)
        m_next = {}
        lists = [finish_items(pair, s_ref, m, acc_ref.at[pair % 2], cur)]
        if pair + 1 < n_pairs:
            lists.append(stage_items(pair + 1, s_next, m_next, cur))
        else:
            lists.append(stage_items(0, s_next, m_next, nxt))
        if pair == n_pairs - 2:
            lists.append(prepare_items(qn_ref, kn_ref, vn_ref, nxt))
        alternate(lists[1], lists[0], *lists[2:])
        m = m_next
    for t, h in m_keys:
        m_ref[t, h] = m[t, h]


def _moba_attention(q, k, v, z_attn):
    bsz, seq_len, _ = q.shape
    n_blocks = seq_len // MOBA_BLOCK
    assert n_blocks % 4 == 0 and n_blocks <= HEAD_DIM
    head_pairs = ATTN_WIDTH // LANES
    heads = LANES // HEAD_DIM
    assert heads == 2, "the block bias rides in the other head's half of the 128 lanes"
    n_units = bsz * head_pairs
    block = (1, seq_len, LANES)
    unit_index = lambda u: (u // head_pairs, 0, u % head_pairs)
    first_spec = pl.BlockSpec(block, lambda u: (0, 0, 0))
    next_spec = pl.BlockSpec(block, lambda u: unit_index(jnp.minimum(u + 1, n_units - 1)))
    unit_spec = pl.BlockSpec(block, unit_index)
    scores = pltpu.VMEM((2, heads, n_blocks, MOBA_BLOCK, MOBA_BLOCK), F32)
    return pl.pallas_call(
        functools.partial(_attn_kernel, n_blocks=n_blocks),
        grid=(n_units,),
        in_specs=[first_spec] * 3 + [next_spec] * 3 + [unit_spec],
        out_specs=unit_spec,
        out_shape=jax.ShapeDtypeStruct((bsz, seq_len, ATTN_WIDTH), BF16),
        scratch_shapes=[pltpu.VMEM((2, heads, n_blocks, MOBA_BLOCK, LANES), BF16),
                        pltpu.VMEM((2, n_blocks, heads, VALUE_ROWS, MOBA_BLOCK), BF16),
                        pltpu.VMEM((2, n_blocks, heads, LANES, MOBA_BLOCK), BF16),
                        scores, scores,
                        pltpu.VMEM((2, heads, 1, MOBA_BLOCK), F32),
                        pltpu.VMEM((2, 2, heads, VALUE_ROWS, MOBA_BLOCK), F32)],
        compiler_params=pltpu.CompilerParams(dimension_semantics=("arbitrary",),
                                             vmem_limit_bytes=VMEM_LIMIT_BYTES),
        name="moba_attn",
    )(q, k, v, q, k, v, z_attn)


def _ssm_tables(lam_re, lam_im, b_re, b_im, c_re, c_im, log_dt):
    T, G, P, H = SSM_CHUNK, SSM_GROUPS, SSM_STATE, SSM_GROUP_DIM
    dt = jnp.exp(log_dt.astype(F32))[:, None]
    lam_r, lam_i = lam_re.astype(F32), lam_im.astype(F32)
    mag = jnp.exp(lam_r * dt)
    bar_r, bar_i = mag * jnp.cos(lam_i * dt), mag * jnp.sin(lam_i * dt)
    den = lam_r * lam_r + lam_i * lam_i
    f_r = ((bar_r - 1.0) * lam_r + bar_i * lam_i) / den
    f_i = (bar_i * lam_r - (bar_r - 1.0) * lam_i) / den
    bb_r = f_r[..., None] * b_re - f_i[..., None] * b_im
    bb_i = f_r[..., None] * b_im + f_i[..., None] * b_re

    def powers(base_r, base_i, count):
        out_r, out_i = [base_r], [base_i]
        for _ in range(count - 1):
            r, i = out_r[-1], out_i[-1]
            out_r.append(r * base_r - i * base_i)
            out_i.append(r * base_i + i * base_r)
        return out_r, out_i

    pw_r, pw_i = powers(bar_r, bar_i, T)
    pw_r = jnp.stack([jnp.ones_like(bar_r)] + pw_r)
    pw_i = jnp.stack([jnp.zeros_like(bar_i)] + pw_i)

    lb_r = pw_r[:T, :, :, None] * bb_r - pw_i[:T, :, :, None] * bb_i
    lb_i = pw_r[:T, :, :, None] * bb_i + pw_i[:T, :, :, None] * bb_r
    cl_r = c_re[None] * pw_r[:, :, None, :] - c_im[None] * pw_i[:, :, None, :]
    cl_i = c_re[None] * pw_i[:, :, None, :] + c_im[None] * pw_r[:, :, None, :]
    lb_t = jnp.stack([lb_r, lb_i], axis=1).transpose(0, 1, 4, 2, 3).reshape(T, 2, H, G * P)
    cl_t = jnp.stack([cl_r, -cl_i], axis=1).transpose(0, 1, 4, 2, 3).reshape(T + 1, 2, P, G * H)

    a_r, a_i = powers(pw_r[T], pw_i[T], F32_SUBLANES)
    tiled = lambda a: jnp.stack(a).reshape(len(a), SSM_TILES, STATE_COLS)
    a_lin = jnp.stack([tiled(a_r), tiled(a_i)]).transpose(2, 0, 1, 3)
    shifts = [1 << k for k in range(F32_SUBLANES.bit_length() - 1)]
    a_log = jnp.stack([tiled([a_r[s - 1] for s in shifts]),
                       tiled([a_i[s - 1] for s in shifts])]).transpose(2, 1, 0, 3)
    row = jnp.arange(F32_SUBLANES)[None, None, None, :, None]
    keep = row >= jnp.asarray(shifts)[None, :, None, None, None]
    a_log = jnp.where(keep, a_log[:, :, :, None, :], 0.0)
    return lb_t, cl_t, a_log, a_lin


def _ssm_kernel(u_ref, lb_ref, cl_ref, alog_ref, alin_ref, d_ref, y_ref, m_ref, ws_ref, wo_ref,
                work_ref, prev_ref, *, seq_len, n_seq):
    T, H, P = SSM_CHUNK, SSM_GROUP_DIM, SSM_STATE
    n_chunks = seq_len // T
    iota = lambda shape, d: lax.broadcasted_iota(jnp.int32, shape, d)
    h_bits, p_bits = H.bit_length() - 1, P.bit_length() - 1

    @pl.when(pl.program_id(1) == 0)
    def _():
        spread_h = ((iota((LANES, H), 0) & (H - 1)) == iota((LANES, H), 1)).astype(BF16)
        spread_p = ((iota((STATE_COLS, P), 0) & (P - 1)) == iota((STATE_COLS, P), 1)).astype(BF16)
        same_hp = (iota((LANES, STATE_COLS), 0) >> h_bits) == (iota((LANES, STATE_COLS), 1) >> p_bits)
        same_ph = (iota((STATE_COLS, LANES), 0) >> p_bits) == (iota((STATE_COLS, LANES), 1) >> h_bits)

        def expand(spread, coeff, same):
            full = jnp.dot(spread, coeff.astype(BF16), preferred_element_type=F32)
            return jnp.where(same, full, 0.0).astype(BF16)

        rows = lambda j: slice(j * LANES, (j + 1) * LANES)
        state_out = lambda d: jnp.concatenate(
            [expand(spread_p, cl_ref[d, ri], same_ph) for ri in range(2)], axis=0)
        for j in range(T):
            ws_ref[rows(j), :] = jnp.concatenate(
                [expand(spread_h, lb_ref[T - 1 - j, ri], same_hp) for ri in range(2)], axis=1)
        for i in range(T):
            wo_ref[:, rows(i)] = state_out(i + 1)
        c_out = state_out(0)
        for d in range(T):
            block = jnp.dot(ws_ref[rows(T - 1 - d), :], c_out,
                            preferred_element_type=F32).astype(BF16)
            for j in range(T - d):
                m_ref[rows(j), rows(j + d)] = block
        for j in range(T):
            for i in range(j):
                m_ref[rows(j), rows(i)] = jnp.zeros((LANES, LANES), BF16)

    pieces = [u_ref[pl.ds(b * seq_len + i, n_chunks, stride=T), :]
              for b in range(n_seq) for i in range(T)]
    x = jnp.concatenate(
        [jnp.concatenate(pieces[b * T:(b + 1) * T], axis=1) for b in range(n_seq)], axis=0)
    x_lo = x.astype(BF16)
    groups = [slice(c, c + MXU_COLS) for c in range(0, STATE_COLS, MXU_COLS)]
    im_of = lambda cols: slice(STATE_COLS + cols.start, STATE_COLS + cols.stop)
    traced = lambda k: k + jnp.minimum(pl.program_id(1), 0)
    n_groups = len(groups)
    y_region = 2 * n_groups

    def increment(g, part):
        cols = groups[g] if part == 0 else im_of(groups[g])
        work_ref[traced(2 * g + part)] = jnp.dot(x_lo, ws_ref[:, cols], preferred_element_type=F32)

    def intra_chunk(n):
        hi = (n + 1) * MXU_COLS
        work_ref[traced(y_region + n)] = jnp.dot(x_lo[:, :hi], m_ref[:hi, hi - MXU_COLS:hi],
                                                 preferred_element_type=F32)

    sub = F32_SUBLANES
    first_row = iota((sub, MXU_COLS), 0) == 0
    cmul = lambda a_re, a_im, b_re, b_im: (a_re * b_re - a_im * b_im, a_re * b_im + a_im * b_re)
    tiles_per_seq = n_chunks // sub
    pack = BF16_SUBLANES // sub

    def scan_items(g):
        cols = groups[g]
        log_mul = [(alog_ref[0, k, 0, :, cols], alog_ref[0, k, 1, :, cols])
                   for k in range(sub.bit_length() - 1)]
        lin_mul = (alin_ref[0, 0, :, cols], alin_ref[0, 1, :, cols])

        def sequence(b):
            carry = (jnp.zeros((1, MXU_COLS), F32),) * 2
            pending = []
            for t in range(tiles_per_seq):
                rows = pl.ds((b * tiles_per_seq + t) * sub, sub)
                t_re = work_ref[traced(2 * g), rows, :]
                t_im = work_ref[traced(2 * g + 1), rows, :]
                for k, mul in enumerate(log_mul):
                    d_re, d_im = cmul(*mul, pltpu.roll(t_re, 1 << k, axis=0),
                                      pltpu.roll(t_im, 1 << k, axis=0))
                    t_re, t_im = t_re + d_re, t_im + d_im
                if t > 0:
                    d_re, d_im = cmul(*lin_mul, *carry)
                    t_re, t_im = t_re + d_re, t_im + d_im
                pending.append((jnp.where(first_row, carry[0], pltpu.roll(t_re, 1, axis=0)),
                                jnp.where(first_row, carry[1], pltpu.roll(t_im, 1, axis=0))))
                carry = (t_re[sub - 1:sub], t_im[sub - 1:sub])
                if len(pending) == pack:
                    out = pl.ds((b * tiles_per_seq + t + 1 - pack) * sub, pack * sub)
                    for part, at in enumerate((cols, im_of(cols))):
                        prev_ref[out, at] = jnp.concatenate(
                            [p[part] for p in pending], axis=0).astype(BF16)
                    pending = []
        return [functools.partial(sequence, b) for b in range(n_seq)]

    def alternate(*item_lists):
        for group in zip(*item_lists):
            for item in group:
                item()
        for items in item_lists:
            for item in items[min(map(len, item_lists)):]:
                item()

    increment(0, 0)
    increment(0, 1)
    matmuls = ([functools.partial(increment, g, part) for g in range(1, n_groups)
                for part in range(2)]
               + [functools.partial(intra_chunk, n) for n in range(T * LANES // MXU_COLS)])
    per_scan = -(-len(matmuls) // n_groups)
    for g in range(n_groups):
        alternate(scan_items(g), matmuls[g * per_scan:(g + 1) * per_scan])
    per_tile = MXU_COLS // LANES
    for n in range(T * LANES // MXU_COLS):
        cols = slice(n * MXU_COLS, (n + 1) * MXU_COLS)
        y = (work_ref[y_region + n] + x[:, cols] * jnp.concatenate([d_ref[...]] * per_tile, axis=1)
             + jnp.dot(prev_ref[...], wo_ref[:, cols], preferred_element_type=F32))
        for b in range(n_seq):
            for i in range(per_tile):
                y_ref[pl.ds(b * seq_len + n * per_tile + i, n_chunks, stride=T), :] = (
                    y[b * n_chunks:(b + 1) * n_chunks, i * LANES:(i + 1) * LANES])


def _s5_ssm(u, tables, d_skip, seq_len):
    lb_t, cl_t, a_log, a_lin = tables
    T, H, P = SSM_CHUNK, SSM_GROUP_DIM, SSM_STATE
    assert seq_len % (T * F32_SUBLANES) == 0
    bsz = u.shape[0] // seq_len
    n_seq = max(n for n in range(1, SSM_SEQS_PER_STEP + 1) if bsz % n == 0)
    io_spec = pl.BlockSpec((n_seq * seq_len, LANES), lambda q, b: (b, q))
    square = pltpu.VMEM((T * LANES, T * LANES), BF16)
    n_groups = STATE_COLS // MXU_COLS
    chunk_rows = n_seq * seq_len // T
    return pl.pallas_call(
        functools.partial(_ssm_kernel, seq_len=seq_len, n_seq=n_seq),
        grid=(SSM_TILES, bsz // n_seq),
        in_specs=[io_spec,
                  pl.BlockSpec((T, 2, H, STATE_COLS), lambda q, b: (0, 0, 0, q)),
                  pl.BlockSpec((T + 1, 2, P, LANES), lambda q, b: (0, 0, 0, q)),
                  pl.BlockSpec((1,) + a_log.shape[1:], lambda q, b: (q, 0, 0, 0, 0)),
                  pl.BlockSpec((1,) + a_lin.shape[1:], lambda q, b: (q, 0, 0, 0)),
                  pl.BlockSpec((1, LANES), lambda q, b: (0, q))],
        out_specs=io_spec,
        out_shape=jax.ShapeDtypeStruct(u.shape, F32),
        scratch_shapes=[square, square, square,
                        pltpu.VMEM((2 * n_groups + T * LANES // MXU_COLS, chunk_rows, MXU_COLS), F32),
                        pltpu.VMEM((chunk_rows, 2 * STATE_COLS), BF16)],
        compiler_params=pltpu.CompilerParams(dimension_semantics=("arbitrary", "arbitrary"),
                                             vmem_limit_bytes=VMEM_LIMIT_BYTES),
        name="s5_ssm",
    )(u, lb_t, cl_t, a_log, a_lin, d_skip)


def _out_proj_kernel(x_ref, ma_ref, y_ref, zs_ref, wg32_ref, bg_ref, w32_ref, g_ref, o_ref,
                     wg_ref, w_ref, *, final_norm):
    @pl.when(pl.program_id(0) == 0)
    def _():
        wg_ref[...] = wg32_ref[...].astype(BF16)
        w_ref[...] = w32_ref[...].astype(BF16)

    half_plus = lambda t: 1.0 + jnp.tanh(0.5 * t)
    for c in range(0, x_ref.shape[0], OUT_PROJ_ROWS):
        rows = slice(c, c + OUT_PROJ_ROWS)
        y = y_ref[rows, :]
        y = y * (1.0 + jnp.tanh(math.sqrt(2.0 / math.pi) * (y + 0.044715 * (y * y * y))))
        gate = jnp.dot((0.5 * y).astype(BF16), wg_ref[...], preferred_element_type=F32) + bg_ref[...]
        z = zs_ref[rows, :].astype(F32)
        mixed_ssm = (0.125 * y) * half_plus(gate) * (z * half_plus(z))
        mixed = jnp.concatenate([ma_ref[rows, :], mixed_ssm.astype(BF16)], axis=1)
        r = x_ref[rows, :] + jnp.dot(mixed, w_ref[...], preferred_element_type=F32)
        if final_norm:
            ms = jnp.mean(r * r, axis=-1, keepdims=True)
            r = r * lax.rsqrt(ms + NORM_EPS) * g_ref[...]
        o_ref[rows, :] = r


def _out_proj(x2, mixed_attn, y_ssm, z_ssm, w_glu, b_glu, w_out, gain, tm, final_norm):
    rows = x2.shape[0]
    row_spec = lambda width: pl.BlockSpec((tm, width), lambda i: (i, 0))
    return pl.pallas_call(
        functools.partial(_out_proj_kernel, final_norm=final_norm),
        grid=(rows // tm,),
        in_specs=[row_spec(D_MODEL), row_spec(ATTN_WIDTH), row_spec(SSM_WIDTH), row_spec(SSM_WIDTH),
                  _resident((SSM_WIDTH, SSM_WIDTH)), _resident((1, SSM_WIDTH)),
                  _resident((ATTN_WIDTH + SSM_WIDTH, D_MODEL)), _resident((1, D_MODEL))],
        out_specs=row_spec(D_MODEL),
        out_shape=jax.ShapeDtypeStruct((rows, D_MODEL), F32),
        scratch_shapes=[pltpu.VMEM((SSM_WIDTH, SSM_WIDTH), BF16),
                        pltpu.VMEM((ATTN_WIDTH + SSM_WIDTH, D_MODEL), BF16)],
        compiler_params=pltpu.CompilerParams(dimension_semantics=("arbitrary",),
                                             vmem_limit_bytes=VMEM_LIMIT_BYTES),
        name="out_proj",
    )(x2, mixed_attn, y_ssm, z_ssm, w_glu, b_glu, w_out, gain)


def _rotary_tables(seq_len):
    half = HEAD_DIM // 2
    inv_freq = 1.0 / (ROPE_THETA ** (jnp.arange(half, dtype=F32) / half))
    ang = jnp.arange(seq_len, dtype=F32)[:, None] * inv_freq[None, :]
    cos, sin, zero = jnp.cos(ang), jnp.sin(ang), jnp.zeros_like(ang)
    reps = LANES // HEAD_DIM
    cos_t = jnp.tile(jnp.concatenate([cos, cos], axis=1), (1, reps))
    sin_hi = jnp.tile(jnp.concatenate([zero, sin], axis=1), (1, reps))
    sin_lo = jnp.tile(jnp.concatenate([-sin, zero], axis=1), (1, reps))
    return cos_t, sin_hi, sin_lo


def kernel(x, norm_gain, w_in, w_out, lam_re, lam_im, b_re, b_im, c_re, c_im,
           d_skip, log_dt, w_glu, b_glu, final_gain):
    bsz, seq_len, _ = x.shape
    depth = norm_gain.shape[0]
    assert seq_len % MOBA_BLOCK == 0 and seq_len // MOBA_BLOCK > MOBA_TOP_K
    assert seq_len % SSM_CHUNK == 0
    tm = min(PROJ_TILE_ROWS, seq_len)
    cos, sin_hi, sin_lo = _rotary_tables(seq_len)

    x2 = x.reshape(bsz * seq_len, D_MODEL)
    for layer in range(depth):
        tables = _ssm_tables(lam_re[layer], lam_im[layer], b_re[layer], b_im[layer],
                             c_re[layer], c_im[layer], log_dt[layer])
        q, k, v, z_attn, u, z_ssm = _in_proj(
            x2, norm_gain[layer].reshape(1, D_MODEL), w_in[layer], cos, sin_hi, sin_lo, seq_len, tm)
        to_seq = lambda t: t.reshape(bsz, seq_len, ATTN_WIDTH)
        mixed_attn = _moba_attention(to_seq(q), to_seq(k), to_seq(v), to_seq(z_attn))
        y_ssm = _s5_ssm(u, tables, d_skip[layer].reshape(1, SSM_WIDTH), seq_len)
        x2 = _out_proj(x2, mixed_attn.reshape(bsz * seq_len, ATTN_WIDTH), y_ssm, z_ssm,
                       w_glu[layer], b_glu[layer].reshape(1, SSM_WIDTH),
                       w_out[layer], final_gain.reshape(1, D_MODEL), tm,
                       final_norm=layer == depth - 1)
    return x2.reshape(bsz, seq_len, D_MODEL)
```

```python
import functools
import math

import jax
import jax.numpy as jnp
from jax import lax
from jax.experimental import pallas as pl
from jax.experimental.pallas import tpu as pltpu

F32 = jnp.float32
BF16 = jnp.bfloat16

D_MODEL = 1024
HEAD_DIM = 64
ATTN_HEADS = 8
ATTN_WIDTH = ATTN_HEADS * HEAD_DIM
MOBA_BLOCK = 256
MOBA_TOP_K = 3
ROPE_THETA = 10000.0
SSM_GROUP_DIM = 16
SSM_GROUPS = 32
SSM_WIDTH = SSM_GROUPS * SSM_GROUP_DIM
SSM_STATE = 64
IN_PROJ_WIDTH = 4 * ATTN_WIDTH + 2 * SSM_WIDTH
NORM_EPS = 1e-6
NEG_INF = -1e30

LANES = 128
SSM_CHUNK = 8
GROUPS_PER_TILE = LANES // SSM_GROUP_DIM
SSM_TILES = SSM_WIDTH // LANES
TILE_COLS = SSM_CHUNK * LANES
STATE_COLS = GROUPS_PER_TILE * SSM_STATE
VMEM_LIMIT_BYTES = 56 * 1024 * 1024
MXU_COLS = 256
OUT_PROJ_ROWS = 256
PROJ_TILE_ROWS = 1024
SSM_SEQS_PER_STEP = 4
F32_SUBLANES = 8
BF16_SUBLANES = 16
VALUE_ROWS = HEAD_DIM + BF16_SUBLANES
QK_SCALE = math.log2(math.e) / math.sqrt(HEAD_DIM)


def _resident(shape):
    zeros = (0,) * len(shape)
    return pl.BlockSpec(shape, lambda *_: zeros, pipeline_mode=pl.Buffered(1))


def _in_proj_kernel(x_ref, g_ref, w32_ref, cos_ref, sin_hi_ref, sin_lo_ref,
                    q_ref, k_ref, v_ref, za_ref, u_ref, zs_ref, w_ref):
    @pl.when(pl.program_id(0) == 0)
    def _():
        for c in range(0, IN_PROJ_WIDTH, ATTN_WIDTH):
            w_ref[:, c:c + ATTN_WIDTH] = w32_ref[:, c:c + ATTN_WIDTH].astype(BF16)

    x = x_ref[...]
    ms = jnp.mean(x * x, axis=-1, keepdims=True)
    h = (x * lax.rsqrt(ms + NORM_EPS) * g_ref[...]).astype(BF16)

    def section(idx):
        return jnp.dot(h, w_ref[:, idx * ATTN_WIDTH:(idx + 1) * ATTN_WIDTH],
                       preferred_element_type=F32)

    cos = cos_ref[...]
    sin_hi = sin_hi_ref[...]
    sin_lo = sin_lo_ref[...]

    def rotary(t):
        return (t * cos + pltpu.roll(t, HEAD_DIM // 2, axis=1) * sin_hi
                + pltpu.roll(t, LANES - HEAD_DIM // 2, axis=1) * sin_lo)

    q = section(0)
    k = section(1)
    for c in range(ATTN_WIDTH // LANES):
        sl = slice(c * LANES, (c + 1) * LANES)
        q_ref[:, sl] = (rotary(q[:, sl]) * QK_SCALE).astype(BF16)
        k_ref[:, sl] = rotary(k[:, sl]).astype(BF16)
    v_ref[...] = section(2).astype(BF16)
    za_ref[...] = section(3).astype(BF16)
    u = section(4)
    for tile in range(SSM_TILES):
        u_ref[tile] = u[:, tile * LANES:(tile + 1) * LANES]
    zs_ref[...] = section(5).astype(BF16)


def _in_proj(x2, gain, w_in, cos, sin_hi, sin_lo, seq_len, tm):
    rows = x2.shape[0]
    pos_blocks = seq_len // tm
    row_spec = lambda width: pl.BlockSpec((tm, width), lambda i: (i, 0))
    tab_spec = pl.BlockSpec((tm, LANES), lambda i: (i % pos_blocks, 0))
    out_bf16 = jax.ShapeDtypeStruct((rows, ATTN_WIDTH), BF16)
    out_tiles = jax.ShapeDtypeStruct((SSM_TILES, rows, LANES), F32)
    tiles_spec = pl.BlockSpec((SSM_TILES, tm, LANES), lambda i: (0, i, 0))
    return pl.pallas_call(
        _in_proj_kernel,
        grid=(rows // tm,),
        in_specs=[row_spec(D_MODEL), _resident((1, D_MODEL)), _resident((D_MODEL, IN_PROJ_WIDTH)),
                  tab_spec, tab_spec, tab_spec],
        out_specs=[row_spec(ATTN_WIDTH)] * 4 + [tiles_spec, row_spec(SSM_WIDTH)],
        out_shape=[out_bf16, out_bf16, out_bf16, out_bf16, out_tiles, out_bf16],
        scratch_shapes=[pltpu.VMEM((D_MODEL, IN_PROJ_WIDTH), BF16)],
        compiler_params=pltpu.CompilerParams(dimension_semantics=("arbitrary",),
                                             vmem_limit_bytes=VMEM_LIMIT_BYTES),
        name="in_proj",
    )(x2, gain, w_in, cos, sin_hi, sin_lo)


def _attn_kernel(q0_ref, k0_ref, v0_ref, qn_ref, kn_ref, vn_ref, z_ref, o_ref,
                 kx_ref, vx_ref, qx_ref, sa_ref, sb_ref, m_ref, acc_ref, *, n_blocks):
    unit = pl.program_id(0)
    blk = MOBA_BLOCK
    heads = LANES // HEAD_DIM
    n_pairs = n_blocks // 2
    cur = lax.rem(unit, 2)
    nxt = 1 - cur

    head_rows = lambda h: slice(h * HEAD_DIM, (h + 1) * HEAD_DIM)
    spare_base = lambda h: ((h + 1) % heads) * HEAD_DIM
    iota = lambda shape, d: lax.broadcasted_iota(jnp.int32, shape, d)
    key = iota((blk, blk), 0)
    query = iota((blk, blk), 1)
    col_max = lambda s: jnp.max(s, axis=0, keepdims=True)

    def prepare_items(q_ref, k_ref, v_ref, slot):
        kmean = {}

        def key_block(j):
            lane = iota((blk, LANES), 1)
            sum_rows = (iota((VALUE_ROWS - HEAD_DIM, blk), 0) == 0).astype(F32)
            rows = slice(j * blk, (j + 1) * blk)
            kj = k_ref[0, rows, :]
            kmean[j] = jnp.mean(kj.astype(F32), axis=0, keepdims=True)
            vt = v_ref[0, rows, :].astype(F32).T
            for h in range(heads):
                in_head = (lane >= h * HEAD_DIM) & (lane < (h + 1) * HEAD_DIM)
                tag = jnp.where(lane == spare_base(h) + j, 1.0, 0.0).astype(BF16)
                kx_ref[slot, h, j] = jnp.where(in_head, kj, tag)
                vx_ref[slot, j, h] = jnp.concatenate([vt[head_rows(h)], sum_rows],
                                                     axis=0).astype(BF16)

        def query_block(qi):
            if "all" not in kmean:
                kmean["all"] = jnp.concatenate([kmean[j] for j in range(n_blocks)],
                                               axis=0).astype(BF16)
            blk_row = iota((n_blocks, blk), 0)
            zero_rows = lambda n: jnp.zeros((n, blk), F32)
            in_head_order = lambda h, own, other: [own, other] if h == 0 else [other, own]
            qt = q_ref[0, qi * blk:(qi + 1) * blk, :].astype(F32).T
            past = blk_row < qi
            for h in range(heads):
                q_rows = qt[head_rows(h)]
                q_only = jnp.concatenate(in_head_order(h, q_rows, zero_rows(HEAD_DIM)), axis=0)
                gate = jnp.dot(kmean["all"], q_only.astype(BF16), preferred_element_type=F32)
                gate = jnp.where(past, gate, NEG_INF)
                beaten_by = jnp.zeros((n_blocks, blk), jnp.int32)
                for j in range(n_blocks):
                    gj = gate[j:j + 1, :]
                    wins = (gj > gate) | ((gj == gate) & (j < blk_row))
                    beaten_by = beaten_by + wins.astype(jnp.int32)
                keep = (past & (beaten_by < MOBA_TOP_K)) | (blk_row == qi)
                bias = jnp.where(keep, 0.0, NEG_INF)
                spare = jnp.concatenate([bias, zero_rows(HEAD_DIM - n_blocks)], axis=0)
                qx_ref[slot, qi, h] = jnp.concatenate(in_head_order(h, q_rows, spare),
                                                      axis=0).astype(BF16)

        return ([functools.partial(key_block, j) for j in range(n_blocks)]
                + [functools.partial(query_block, qi) for qi in range(n_blocks)])

    def visible_blocks(pair):
        tiles = (pair, n_blocks - 1 - pair)
        return ([(t, tiles[t], True) for t in range(2)]
                + [(t, j, False) for t in range(2) for j in range(tiles[t])])

    traced = lambda t: t + jnp.minimum(unit, 0)

    def stage_items(pair, s_ref, m, slot):
        def block(t, j, own, h):
            qi = (pair, n_blocks - 1 - pair)[t]
            s = jnp.dot(kx_ref[slot, h, j], qx_ref[slot, qi, h], preferred_element_type=F32)
            if own:
                s = jnp.where(key <= query, s, NEG_INF)
            s_ref[traced(t), h, j] = s
            m[t, h] = col_max(s) if own else jnp.maximum(m[t, h], col_max(s))
        return [functools.partial(block, *b, h) for b in visible_blocks(pair)
                for h in range(heads)]

    def finish_items(pair, s_ref, m, acc_ref, slot):
        def block(t, j, own, h):
            p = jnp.exp2(s_ref[traced(t), h, j] - m[t, h]).astype(BF16)
            pv = jnp.dot(vx_ref[slot, j, h], p, preferred_element_type=F32)
            if own:
                acc_ref[traced(t), h] = pv
            else:
                acc_ref[traced(t), h] += pv

        def write_rows():
            for t, qi in enumerate((pair, n_blocks - 1 - pair)):
                rows = slice(qi * blk, (qi + 1) * blk)
                acc = [acc_ref[t, h] for h in range(heads)]
                o_t = jnp.concatenate([a[:HEAD_DIM] / a[HEAD_DIM:HEAD_DIM + 1] for a in acc],
                                      axis=0)
                z = z_ref[0, rows, :].astype(F32)
                silu = 0.5 * z * (1.0 + jnp.tanh(0.5 * z))
                o_ref[0, rows, :] = (o_t.T * silu).astype(BF16)

        return ([functools.partial(block, *b, h) for b in visible_blocks(pair)
                 for h in range(heads)] + [write_rows])

    def alternate(*item_lists):
        for group in zip(*item_lists):
            for item in group:
                item()
        for items in item_lists:
            for item in items[min(map(len, item_lists)):]:
                item()

    m_keys = [(t, h) for t in range(2) for h in range(heads)]

    @pl.when(unit == 0)
    def _():
        m0 = {}
        alternate(prepare_items(q0_ref, k0_ref, v0_ref, 0))
        alternate(stage_items(0, sa_ref, m0, 0))
        for t, h in m_keys:
            m_ref[t, h] = m0[t, h]

    m = {key_: m_ref[key_] for key_ in m_keys}
    for pair in range(n_pairs):
        s_ref, s_next = (sa_ref, sb_ref) if pair % 2 == 0 else (sb_ref, sa_ref)
        m_next = {}
        lists = [finish_items(pair, s_ref, m, acc_ref.at[pair % 2], cur)]
        if pair + 1 < n_pairs:
            lists.append(stage_items(pair + 1, s_next, m_next, cur))
        else:
            lists.append(stage_items(0, s_next, m_next, nxt))
        if pair == n_pairs - 2:
            lists.append(prepare_items(qn_ref, kn_ref, vn_ref, nxt))
        alternate(lists[1], lists[0], *lists[2:])
        m = m_next
    for t, h in m_keys:
        m_ref[t, h] = m[t, h]


def _moba_attention(q, k, v, z_attn):
    bsz, seq_len, _ = q.shape
    n_blocks = seq_len // MOBA_BLOCK
    assert n_blocks % 4 == 0 and n_blocks <= HEAD_DIM
    head_pairs = ATTN_WIDTH // LANES
    heads = LANES // HEAD_DIM
    assert heads == 2, "the block bias rides in the other head's half of the 128 lanes"
    n_units = bsz * head_pairs
    block = (1, seq_len, LANES)
    unit_index = lambda u: (u // head_pairs, 0, u % head_pairs)
    first_spec = pl.BlockSpec(block, lambda u: (0, 0, 0))
    next_spec = pl.BlockSpec(block, lambda u: unit_index(jnp.minimum(u + 1, n_units - 1)))
    unit_spec = pl.BlockSpec(block, unit_index)
    scores = pltpu.VMEM((2, heads, n_blocks, MOBA_BLOCK, MOBA_BLOCK), F32)
    return pl.pallas_call(
        functools.partial(_attn_kernel, n_blocks=n_blocks),
        grid=(n_units,),
        in_specs=[first_spec] * 3 + [next_spec] * 3 + [unit_spec],
        out_specs=unit_spec,
        out_shape=jax.ShapeDtypeStruct((bsz, seq_len, ATTN_WIDTH), BF16),
        scratch_shapes=[pltpu.VMEM((2, heads, n_blocks, MOBA_BLOCK, LANES), BF16),
                        pltpu.VMEM((2, n_blocks, heads, VALUE_ROWS, MOBA_BLOCK), BF16),
                        pltpu.VMEM((2, n_blocks, heads, LANES, MOBA_BLOCK), BF16),
                        scores, scores,
                        pltpu.VMEM((2, heads, 1, MOBA_BLOCK), F32),
                        pltpu.VMEM((2, 2, heads, VALUE_ROWS, MOBA_BLOCK), F32)],
        compiler_params=pltpu.CompilerParams(dimension_semantics=("arbitrary",),
                                             vmem_limit_bytes=VMEM_LIMIT_BYTES),
        name="moba_attn",
    )(q, k, v, q, k, v, z_attn)


def _ssm_tables(lam_re, lam_im, b_re, b_im, c_re, c_im, log_dt):
    T, G, P, H = SSM_CHUNK, SSM_GROUPS, SSM_STATE, SSM_GROUP_DIM
    dt = jnp.exp(log_dt.astype(F32))[:, None]
    lam_r, lam_i = lam_re.astype(F32), lam_im.astype(F32)
    mag = jnp.exp(lam_r * dt)
    bar_r, bar_i = mag * jnp.cos(lam_i * dt), mag * jnp.sin(lam_i * dt)
    den = lam_r * lam_r + lam_i * lam_i
    f_r = ((bar_r - 1.0) * lam_r + bar_i * lam_i) / den
    f_i = (bar_i * lam_r - (bar_r - 1.0) * lam_i) / den
    bb_r = f_r[..., None] * b_re - f_i[..., None] * b_im
    bb_i = f_r[..., None] * b_im + f_i[..., None] * b_re

    def powers(base_r, base_i, count):
        out_r, out_i = [base_r], [base_i]
        for _ in range(count - 1):
            r, i = out_r[-1], out_i[-1]
            out_r.append(r * base_r - i * base_i)
            out_i.append(r * base_i + i * base_r)
        return out_r, out_i

    pw_r, pw_i = powers(bar_r, bar_i, T)
    pw_r = jnp.stack([jnp.ones_like(bar_r)] + pw_r)
    pw_i = jnp.stack([jnp.zeros_like(bar_i)] + pw_i)

    lb_r = pw_r[:T, :, :, None] * bb_r - pw_i[:T, :, :, None] * bb_i
    lb_i = pw_r[:T, :, :, None] * bb_i + pw_i[:T, :, :, None] * bb_r
    cl_r = c_re[None] * pw_r[:, :, None, :] - c_im[None] * pw_i[:, :, None, :]
    cl_i = c_re[None] * pw_i[:, :, None, :] + c_im[None] * pw_r[:, :, None, :]
    lb_t = jnp.stack([lb_r, lb_i], axis=1).transpose(0, 1, 4, 2, 3).reshape(T, 2, H, G * P)
    cl_t = jnp.stack([cl_r, -cl_i], axis=1).transpose(0, 1, 4, 2, 3).reshape(T + 1, 2, P, G * H)

    a_r, a_i = powers(pw_r[T], pw_i[T], F32_SUBLANES)
    tiled = lambda a: jnp.stack(a).reshape(len(a), SSM_TILES, STATE_COLS)
    a_lin = jnp.stack([tiled(a_r), tiled(a_i)]).transpose(2, 0, 1, 3)
    shifts = [1 << k for k in range(F32_SUBLANES.bit_length() - 1)]
    a_log = jnp.stack([tiled([a_r[s - 1] for s in shifts]),
                       tiled([a_i[s - 1] for s in shifts])]).transpose(2, 1, 0, 3)
    row = jnp.arange(F32_SUBLANES)[None, None, None, :, None]
    keep = row >= jnp.asarray(shifts)[None, :, None, None, None]
    a_log = jnp.where(keep, a_log[:, :, :, None, :], 0.0)
    return lb_t, cl_t, a_log, a_lin


def _ssm_kernel(u_ref, lb_ref, cl_ref, alog_ref, alin_ref, d_ref, y_ref, m_ref, ws_ref, wo_ref,
                work_ref, prev_ref, *, seq_len, n_seq):
    T, H, P = SSM_CHUNK, SSM_GROUP_DIM, SSM_STATE
    n_chunks = seq_len // T
    iota = lambda shape, d: lax.broadcasted_iota(jnp.int32, shape, d)
    h_bits, p_bits = H.bit_length() - 1, P.bit_length() - 1

    @pl.when(pl.program_id(1) == 0)
    def _():
        spread_h = ((iota((LANES, H), 0) & (H - 1)) == iota((LANES, H), 1)).astype(BF16)
        spread_p = ((iota((STATE_COLS, P), 0) & (P - 1)) == iota((STATE_COLS, P), 1)).astype(BF16)
        same_hp = (iota((LANES, STATE_COLS), 0) >> h_bits) == (iota((LANES, STATE_COLS), 1) >> p_bits)
        same_ph = (iota((STATE_COLS, LANES), 0) >> p_bits) == (iota((STATE_COLS, LANES), 1) >> h_bits)

        def expand(spread, coeff, same):
            full = jnp.dot(spread, coeff.astype(BF16), preferred_element_type=F32)
            return jnp.where(same, full, 0.0).astype(BF16)

        rows = lambda j: slice(j * LANES, (j + 1) * LANES)
        state_out = lambda d: jnp.concatenate(
            [expand(spread_p, cl_ref[d, ri], same_ph) for ri in range(2)], axis=0)
        for j in range(T):
            ws_ref[rows(j), :] = jnp.concatenate(
                [expand(spread_h, lb_ref[T - 1 - j, ri], same_hp) for ri in range(2)], axis=1)
        for i in range(T):
            wo_ref[:, rows(i)] = state_out(i + 1)
        c_out = state_out(0)
        for d in range(T):
            block = jnp.dot(ws_ref[rows(T - 1 - d), :], c_out,
                            preferred_element_type=F32).astype(BF16)
            for j in range(T - d):
                m_ref[rows(j), rows(j + d)] = block
        for j in range(T):
            for i in range(j):
                m_ref[rows(j), rows(i)] = jnp.zeros((LANES, LANES), BF16)

    pieces = [u_ref[pl.ds(b * seq_len + i, n_chunks, stride=T), :]
              for b in range(n_seq) for i in range(T)]
    x = jnp.concatenate(
        [jnp.concatenate(pieces[b * T:(b + 1) * T], axis=1) for b in range(n_seq)], axis=0)
    x_lo = x.astype(BF16)
    groups = [slice(c, c + MXU_COLS) for c in range(0, STATE_COLS, MXU_COLS)]
    im_of = lambda cols: slice(STATE_COLS + cols.start, STATE_COLS + cols.stop)
    traced = lambda k: k + jnp.minimum(pl.program_id(1), 0)
    n_groups = len(groups)
    y_region = 2 * n_groups

    def increment(g, part):
        cols = groups[g] if part == 0 else im_of(groups[g])
        work_ref[traced(2 * g + part)] = jnp.dot(x_lo, ws_ref[:, cols], preferred_element_type=F32)

    def intra_chunk(n):
        hi = (n + 1) * MXU_COLS
        work_ref[traced(y_region + n)] = jnp.dot(x_lo[:, :hi], m_ref[:hi, hi - MXU_COLS:hi],
                                                 preferred_element_type=F32)

    sub = F32_SUBLANES
    first_row = iota((sub, MXU_COLS), 0) == 0
    cmul = lambda a_re, a_im, b_re, b_im: (a_re * b_re - a_im * b_im, a_re * b_im + a_im * b_re)
    tiles_per_seq = n_chunks // sub
    pack = BF16_SUBLANES // sub

    def scan_items(g):
        cols = groups[g]
        log_mul = [(alog_ref[0, k, 0, :, cols], alog_ref[0, k, 1, :, cols])
                   for k in range(sub.bit_length() - 1)]
        lin_mul = (alin_ref[0, 0, :, cols], alin_ref[0, 1, :, cols])

        def sequence(b):
            carry = (jnp.zeros((1, MXU_COLS), F32),) * 2
            pending = []
            for t in range(tiles_per_seq):
                rows = pl.ds((b * tiles_per_seq + t) * sub, sub)
                t_re = work_ref[traced(2 * g), rows, :]
                t_im = work_ref[traced(2 * g + 1), rows, :]
                for k, mul in enumerate(log_mul):
                    d_re, d_im = cmul(*mul, pltpu.roll(t_re, 1 << k, axis=0),
                                      pltpu.roll(t_im, 1 << k, axis=0))
                    t_re, t_im = t_re + d_re, t_im + d_im
                if t > 0:
                    d_re, d_im = cmul(*lin_mul, *carry)
                    t_re, t_im = t_re + d_re, t_im + d_im
                pending.append((jnp.where(first_row, carry[0], pltpu.roll(t_re, 1, axis=0)),
                                jnp.where(first_row, carry[1], pltpu.roll(t_im, 1, axis=0))))
                carry = (t_re[sub - 1:sub], t_im[sub - 1:sub])
                if len(pending) == pack:
                    out = pl.ds((b * tiles_per_seq + t + 1 - pack) * sub, pack * sub)
                    for part, at in enumerate((cols, im_of(cols))):
                        prev_ref[out, at] = jnp.concatenate(
                            [p[part] for p in pending], axis=0).astype(BF16)
                    pending = []
        return [functools.partial(sequence, b) for b in range(n_seq)]

    def alternate(*item_lists):
        for group in zip(*item_lists):
            for item in group:
                item()
        for items in item_lists:
            for item in items[min(map(len, item_lists)):]:
                item()

    increment(0, 0)
    increment(0, 1)
    matmuls = ([functools.partial(increment, g, part) for g in range(1, n_groups)
                for part in range(2)]
               + [functools.partial(intra_chunk, n) for n in range(T * LANES // MXU_COLS)])
    per_scan = -(-len(matmuls) // n_groups)
    for g in range(n_groups):
        alternate(scan_items(g), matmuls[g * per_scan:(g + 1) * per_scan])
    per_tile = MXU_COLS // LANES
    for n in range(T * LANES // MXU_COLS):
        cols = slice(n * MXU_COLS, (n + 1) * MXU_COLS)
        y = (work_ref[y_region + n] + x[:, cols] * jnp.concatenate([d_ref[...]] * per_tile, axis=1)
             + jnp.dot(prev_ref[...], wo_ref[:, cols], preferred_element_type=F32))
        for b in range(n_seq):
            for i in range(per_tile):
                y_ref[pl.ds(b * seq_len + n * per_tile + i, n_chunks, stride=T), :] = (
                    y[b * n_chunks:(b + 1) * n_chunks, i * LANES:(i + 1) * LANES])


def _s5_ssm(u, tables, d_skip, seq_len):
    lb_t, cl_t, a_log, a_lin = tables
    T, H, P = SSM_CHUNK, SSM_GROUP_DIM, SSM_STATE
    assert seq_len % (T * F32_SUBLANES) == 0
    bsz = u.shape[1] // seq_len
    n_seq = max(n for n in range(1, SSM_SEQS_PER_STEP + 1) if bsz % n == 0)
    io_spec = pl.BlockSpec((None, n_seq * seq_len, LANES), lambda q, b: (q, b, 0))
    square = pltpu.VMEM((T * LANES, T * LANES), BF16)
    n_groups = STATE_COLS // MXU_COLS
    chunk_rows = n_seq * seq_len // T
    return pl.pallas_call(
        functools.partial(_ssm_kernel, seq_len=seq_len, n_seq=n_seq),
        grid=(SSM_TILES, bsz // n_seq),
        in_specs=[io_spec,
                  pl.BlockSpec((T, 2, H, STATE_COLS), lambda q, b: (0, 0, 0, q)),
                  pl.BlockSpec((T + 1, 2, P, LANES), lambda q, b: (0, 0, 0, q)),
                  pl.BlockSpec((1,) + a_log.shape[1:], lambda q, b: (q, 0, 0, 0, 0)),
                  pl.BlockSpec((1,) + a_lin.shape[1:], lambda q, b: (q, 0, 0, 0)),
                  pl.BlockSpec((1, LANES), lambda q, b: (0, q))],
        out_specs=io_spec,
        out_shape=jax.ShapeDtypeStruct(u.shape, F32),
        scratch_shapes=[square, square, square,
                        pltpu.VMEM((2 * n_groups + T * LANES // MXU_COLS, chunk_rows, MXU_COLS), F32),
                        pltpu.VMEM((chunk_rows, 2 * STATE_COLS), BF16)],
        compiler_params=pltpu.CompilerParams(dimension_semantics=("arbitrary", "arbitrary"),
                                             vmem_limit_bytes=VMEM_LIMIT_BYTES),
        name="s5_ssm",
    )(u, lb_t, cl_t, a_log, a_lin, d_skip)


def _out_proj_kernel(x_ref, ma_ref, y_ref, zs_ref, wg32_ref, bg_ref, w32_ref, g_ref, o_ref,
                     wg_ref, w_ref, *, final_norm):
    @pl.when(pl.program_id(0) == 0)
    def _():
        wg_ref[...] = wg32_ref[...].astype(BF16)
        w_ref[...] = w32_ref[...].astype(BF16)

    half_plus = lambda t: 1.0 + jnp.tanh(0.5 * t)
    for c in range(0, x_ref.shape[0], OUT_PROJ_ROWS):
        rows = slice(c, c + OUT_PROJ_ROWS)
        y = jnp.concatenate([y_ref[tile, rows, :] for tile in range(SSM_TILES)], axis=1)
        y = y * (1.0 + jnp.tanh(math.sqrt(2.0 / math.pi) * (y + 0.044715 * (y * y * y))))
        gate = jnp.dot((0.5 * y).astype(BF16), wg_ref[...], preferred_element_type=F32) + bg_ref[...]
        z = zs_ref[rows, :].astype(F32)
        mixed_ssm = (0.125 * y) * half_plus(gate) * (z * half_plus(z))
        mixed = jnp.concatenate([ma_ref[rows, :], mixed_ssm.astype(BF16)], axis=1)
        r = x_ref[rows, :] + jnp.dot(mixed, w_ref[...], preferred_element_type=F32)
        if final_norm:
            ms = jnp.mean(r * r, axis=-1, keepdims=True)
            r = r * lax.rsqrt(ms + NORM_EPS) * g_ref[...]
        o_ref[rows, :] = r


def _out_proj(x2, mixed_attn, y_ssm, z_ssm, w_glu, b_glu, w_out, gain, tm, final_norm):
    rows = x2.shape[0]
    row_spec = lambda width: pl.BlockSpec((tm, width), lambda i: (i, 0))
    return pl.pallas_call(
        functools.partial(_out_proj_kernel, final_norm=final_norm),
        grid=(rows // tm,),
        in_specs=[row_spec(D_MODEL), row_spec(ATTN_WIDTH),
                  pl.BlockSpec((SSM_TILES, tm, LANES), lambda i: (0, i, 0)), row_spec(SSM_WIDTH),
                  _resident((SSM_WIDTH, SSM_WIDTH)), _resident((1, SSM_WIDTH)),
                  _resident((ATTN_WIDTH + SSM_WIDTH, D_MODEL)), _resident((1, D_MODEL))],
        out_specs=row_spec(D_MODEL),
        out_shape=jax.ShapeDtypeStruct((rows, D_MODEL), F32),
        scratch_shapes=[pltpu.VMEM((SSM_WIDTH, SSM_WIDTH), BF16),
                        pltpu.VMEM((ATTN_WIDTH + SSM_WIDTH, D_MODEL), BF16)],
        compiler_params=pltpu.CompilerParams(dimension_semantics=("arbitrary",),
                                             vmem_limit_bytes=VMEM_LIMIT_BYTES),
        name="out_proj",
    )(x2, mixed_attn, y_ssm, z_ssm, w_glu, b_glu, w_out, gain)


def _rotary_tables(seq_len):
    half = HEAD_DIM // 2
    inv_freq = 1.0 / (ROPE_THETA ** (jnp.arange(half, dtype=F32) / half))
    ang = jnp.arange(seq_len, dtype=F32)[:, None] * inv_freq[None, :]
    cos, sin, zero = jnp.cos(ang), jnp.sin(ang), jnp.zeros_like(ang)
    reps = LANES // HEAD_DIM
    cos_t = jnp.tile(jnp.concatenate([cos, cos], axis=1), (1, reps))
    sin_hi = jnp.tile(jnp.concatenate([zero, sin], axis=1), (1, reps))
    sin_lo = jnp.tile(jnp.concatenate([-sin, zero], axis=1), (1, reps))
    return cos_t, sin_hi, sin_lo


def kernel(x, norm_gain, w_in, w_out, lam_re, lam_im, b_re, b_im, c_re, c_im,
           d_skip, log_dt, w_glu, b_glu, final_gain):
    bsz, seq_len, _ = x.shape
    depth = norm_gain.shape[0]
    assert seq_len % MOBA_BLOCK == 0 and seq_len // MOBA_BLOCK > MOBA_TOP_K
    assert seq_len % SSM_CHUNK == 0
    tm = min(PROJ_TILE_ROWS, seq_len)
    cos, sin_hi, sin_lo = _rotary_tables(seq_len)

    x2 = x.reshape(bsz * seq_len, D_MODEL)
    for layer in range(depth):
        tables = _ssm_tables(lam_re[layer], lam_im[layer], b_re[layer], b_im[layer],
                             c_re[layer], c_im[layer], log_dt[layer])
        q, k, v, z_attn, u, z_ssm = _in_proj(
            x2, norm_gain[layer].reshape(1, D_MODEL), w_in[layer], cos, sin_hi, sin_lo, seq_len, tm)
        to_seq = lambda t: t.reshape(bsz, seq_len, ATTN_WIDTH)
        mixed_attn = _moba_attention(to_seq(q), to_seq(k), to_seq(v), to_seq(z_attn))
        y_ssm = _s5_ssm(u, tables, d_skip[layer].reshape(1, SSM_WIDTH), seq_len)
        x2 = _out_proj(x2, mixed_attn.reshape(bsz * seq_len, ATTN_WIDTH), y_ssm, z_ssm,
                       w_glu[layer], b_glu[layer].reshape(1, SSM_WIDTH),
                       w_out[layer], final_gain.reshape(1, D_MODEL), tm,
                       final_norm=layer == depth - 1)
    return x2.reshape(bsz, seq_len, D_MODEL)
```

```python
import functools
import math

import jax
import jax.numpy as jnp
from jax import lax
from jax.experimental import pallas as pl
from jax.experimental.pallas import tpu as pltpu

F32 = jnp.float32
BF16 = jnp.bfloat16

D_MODEL = 1024
HEAD_DIM = 64
ATTN_HEADS = 8
ATTN_WIDTH = ATTN_HEADS * HEAD_DIM
MOBA_BLOCK = 256
MOBA_TOP_K = 3
ROPE_THETA = 10000.0
SSM_GROUP_DIM = 16
SSM_GROUPS = 32
SSM_WIDTH = SSM_GROUPS * SSM_GROUP_DIM
SSM_STATE = 64
IN_PROJ_WIDTH = 4 * ATTN_WIDTH + 2 * SSM_WIDTH
NORM_EPS = 1e-6
NEG_INF = -1e30

LANES = 128
SSM_CHUNK = 8
GROUPS_PER_TILE = LANES // SSM_GROUP_DIM
SSM_TILES = SSM_WIDTH // LANES
TILE_COLS = SSM_CHUNK * LANES
STATE_COLS = GROUPS_PER_TILE * SSM_STATE
VMEM_LIMIT_BYTES = 56 * 1024 * 1024
MXU_COLS = 256
OUT_PROJ_ROWS = 256
PROJ_TILE_ROWS = 1024
SSM_SEQS_PER_STEP = 4
F32_SUBLANES = 8
BF16_SUBLANES = 16
VALUE_ROWS = HEAD_DIM + BF16_SUBLANES
QK_SCALE = math.log2(math.e) / math.sqrt(HEAD_DIM)


def _resident(shape):
    zeros = (0,) * len(shape)
    return pl.BlockSpec(shape, lambda *_: zeros, pipeline_mode=pl.Buffered(1))


def _in_proj_kernel(x_ref, g_ref, w32_ref, cos_ref, sin_hi_ref, sin_lo_ref,
                    q_ref, k_ref, v_ref, za_ref, u_ref, zs_ref, w_ref):
    @pl.when(pl.program_id(0) == 0)
    def _():
        for c in range(0, IN_PROJ_WIDTH, ATTN_WIDTH):
            w_ref[:, c:c + ATTN_WIDTH] = w32_ref[:, c:c + ATTN_WIDTH].astype(BF16)

    x = x_ref[...]
    ms = jnp.mean(x * x, axis=-1, keepdims=True)
    h = (x * lax.rsqrt(ms + NORM_EPS) * g_ref[...]).astype(BF16)

    def section(idx):
        return jnp.dot(h, w_ref[:, idx * ATTN_WIDTH:(idx + 1) * ATTN_WIDTH],
                       preferred_element_type=F32)

    cos = cos_ref[...]
    sin_hi = sin_hi_ref[...]
    sin_lo = sin_lo_ref[...]

    def rotary(t):
        return (t * cos + pltpu.roll(t, HEAD_DIM // 2, axis=1) * sin_hi
                + pltpu.roll(t, LANES - HEAD_DIM // 2, axis=1) * sin_lo)

    q = section(0)
    k = section(1)
    for c in range(ATTN_WIDTH // LANES):
        sl = slice(c * LANES, (c + 1) * LANES)
        q_ref[:, sl] = (rotary(q[:, sl]) * QK_SCALE).astype(BF16)
        k_ref[:, sl] = rotary(k[:, sl]).astype(BF16)
    v_ref[...] = section(2).astype(BF16)
    za_ref[...] = section(3).astype(BF16)
    u_ref[...] = section(4)
    zs_ref[...] = section(5).astype(BF16)


def _in_proj(x2, gain, w_in, cos, sin_hi, sin_lo, seq_len, tm):
    rows = x2.shape[0]
    pos_blocks = seq_len // tm
    row_spec = lambda width: pl.BlockSpec((tm, width), lambda i: (i, 0))
    tab_spec = pl.BlockSpec((tm, LANES), lambda i: (i % pos_blocks, 0))
    out_bf16 = jax.ShapeDtypeStruct((rows, ATTN_WIDTH), BF16)
    out_f32 = jax.ShapeDtypeStruct((rows, ATTN_WIDTH), F32)
    return pl.pallas_call(
        _in_proj_kernel,
        grid=(rows // tm,),
        in_specs=[row_spec(D_MODEL), _resident((1, D_MODEL)), _resident((D_MODEL, IN_PROJ_WIDTH)),
                  tab_spec, tab_spec, tab_spec],
        out_specs=[row_spec(ATTN_WIDTH)] * 6,
        out_shape=[out_bf16, out_bf16, out_bf16, out_bf16, out_f32, out_bf16],
        scratch_shapes=[pltpu.VMEM((D_MODEL, IN_PROJ_WIDTH), BF16)],
        compiler_params=pltpu.CompilerParams(dimension_semantics=("arbitrary",),
                                             vmem_limit_bytes=VMEM_LIMIT_BYTES),
        name="in_proj",
    )(x2, gain, w_in, cos, sin_hi, sin_lo)


def _attn_kernel(q0_ref, k0_ref, v0_ref, qn_ref, kn_ref, vn_ref, z_ref, o_ref,
                 kx_ref, vx_ref, qx_ref, sa_ref, sb_ref, m_ref, acc_ref, *, n_blocks):
    unit = pl.program_id(0)
    blk = MOBA_BLOCK
    heads = LANES // HEAD_DIM
    n_pairs = n_blocks // 2
    cur = lax.rem(unit, 2)
    nxt = 1 - cur

    head_rows = lambda h: slice(h * HEAD_DIM, (h + 1) * HEAD_DIM)
    spare_base = lambda h: ((h + 1) % heads) * HEAD_DIM
    iota = lambda shape, d: lax.broadcasted_iota(jnp.int32, shape, d)
    key = iota((blk, blk), 0)
    query = iota((blk, blk), 1)
    col_max = lambda s: jnp.max(s, axis=0, keepdims=True)

    def prepare_items(q_ref, k_ref, v_ref, slot):
        kmean = {}

        def key_block(j):
            lane = iota((blk, LANES), 1)
            sum_rows = (iota((VALUE_ROWS - HEAD_DIM, blk), 0) == 0).astype(F32)
            rows = slice(j * blk, (j + 1) * blk)
            kj = k_ref[0, rows, :]
            kmean[j] = jnp.mean(kj.astype(F32), axis=0, keepdims=True)
            vt = v_ref[0, rows, :].astype(F32).T
            for h in range(heads):
                in_head = (lane >= h * HEAD_DIM) & (lane < (h + 1) * HEAD_DIM)
                tag = jnp.where(lane == spare_base(h) + j, 1.0, 0.0).astype(BF16)
                kx_ref[slot, h, j] = jnp.where(in_head, kj, tag)
                vx_ref[slot, j, h] = jnp.concatenate([vt[head_rows(h)], sum_rows],
                                                     axis=0).astype(BF16)

        def query_block(qi):
            if "all" not in kmean:
                kmean["all"] = jnp.concatenate([kmean[j] for j in range(n_blocks)],
                                               axis=0).astype(BF16)
            blk_row = iota((n_blocks, blk), 0)
            zero_rows = lambda n: jnp.zeros((n, blk), F32)
            in_head_order = lambda h, own, other: [own, other] if h == 0 else [other, own]
            qt = q_ref[0, qi * blk:(qi + 1) * blk, :].astype(F32).T
            past = blk_row < qi
            for h in range(heads):
                q_rows = qt[head_rows(h)]
                q_only = jnp.concatenate(in_head_order(h, q_rows, zero_rows(HEAD_DIM)), axis=0)
                gate = jnp.dot(kmean["all"], q_only.astype(BF16), preferred_element_type=F32)
                gate = jnp.where(past, gate, NEG_INF)
                beaten_by = jnp.zeros((n_blocks, blk), jnp.int32)
                for j in range(n_blocks):
                    gj = gate[j:j + 1, :]
                    wins = (gj > gate) | ((gj == gate) & (j < blk_row))
                    beaten_by = beaten_by + wins.astype(jnp.int32)
                keep = (past & (beaten_by < MOBA_TOP_K)) | (blk_row == qi)
                bias = jnp.where(keep, 0.0, NEG_INF)
                spare = jnp.concatenate([bias, zero_rows(HEAD_DIM - n_blocks)], axis=0)
                qx_ref[slot, qi, h] = jnp.concatenate(in_head_order(h, q_rows, spare),
                                                      axis=0).astype(BF16)

        return ([functools.partial(key_block, j) for j in range(n_blocks)]
                + [functools.partial(query_block, qi) for qi in range(n_blocks)])

    def visible_blocks(pair):
        tiles = (pair, n_blocks - 1 - pair)
        return ([(t, tiles[t], True) for t in range(2)]
                + [(t, j, False) for t in range(2) for j in range(tiles[t])])

    traced = lambda t: t + jnp.minimum(unit, 0)

    def stage_items(pair, s_ref, m, slot):
        def block(t, j, own, h):
            qi = (pair, n_blocks - 1 - pair)[t]
            s = jnp.dot(kx_ref[slot, h, j], qx_ref[slot, qi, h], preferred_element_type=F32)
            if own:
                s = jnp.where(key <= query, s, NEG_INF)
            s_ref[traced(t), h, j] = s
            m[t, h] = col_max(s) if own else jnp.maximum(m[t, h], col_max(s))
        return [functools.partial(block, *b, h) for b in visible_blocks(pair)
                for h in range(heads)]

    def finish_items(pair, s_ref, m, acc_ref, slot):
        def block(t, j, own, h):
            p = jnp.exp2(s_ref[traced(t), h, j] - m[t, h]).astype(BF16)
            pv = jnp.dot(vx_ref[slot, j, h], p, preferred_element_type=F32)
            if own:
                acc_ref[traced(t), h] = pv
            else:
                acc_ref[traced(t), h] += pv

        def write_rows():
            for t, qi in enumerate((pair, n_blocks - 1 - pair)):
                rows = slice(qi * blk, (qi + 1) * blk)
                acc = [acc_ref[t, h] for h in range(heads)]
                o_t = jnp.concatenate([a[:HEAD_DIM] / a[HEAD_DIM:HEAD_DIM + 1] for a in acc],
                                      axis=0)
                z = z_ref[0, rows, :].astype(F32)
                silu = 0.5 * z * (1.0 + jnp.tanh(0.5 * z))
                o_ref[0, rows, :] = (o_t.T * silu).astype(BF16)

        return ([functools.partial(block, *b, h) for b in visible_blocks(pair)
                 for h in range(heads)] + [write_rows])

    def alternate(*item_lists):
        for group in zip(*item_lists):
            for item in group:
                item()
        for items in item_lists:
            for item in items[min(map(len, item_lists)):]:
                item()

    m_keys = [(t, h) for t in range(2) for h in range(heads)]

    @pl.when(unit == 0)
    def _():
        m0 = {}
        alternate(prepare_items(q0_ref, k0_ref, v0_ref, 0))
        alternate(stage_items(0, sa_ref, m0, 0))
        for t, h in m_keys:
            m_ref[t, h] = m0[t, h]

    m = {key_: m_ref[key_] for key_ in m_keys}
    for pair in range(n_pairs):
        s_ref, s_next = (sa_ref, sb_ref) if pair % 2 == 0 else (sb_ref, sa_ref)
        m_next = {}
        lists = [finish_items(pair, s_ref, m, acc_ref.at[pair % 2], cur)]
        if pair + 1 < n_pairs:
            lists.append(stage_items(pair + 1, s_next, m_next, cur))
        else:
            lists.append(stage_items(0, s_next, m_next, nxt))
        if pair == n_pairs - 2:
            lists.append(prepare_items(qn_ref, kn_ref, vn_ref, nxt))
        alternate(lists[1], lists[0], *lists[2:])
        m = m_next
    for t, h in m_keys:
        m_ref[t, h] = m[t, h]


def _moba_attention(q, k, v, z_attn):
    bsz, seq_len, _ = q.shape
    n_blocks = seq_len // MOBA_BLOCK
    assert n_blocks % 4 == 0 and n_blocks <= HEAD_DIM
    head_pairs = ATTN_WIDTH // LANES
    heads = LANES // HEAD_DIM
    assert heads == 2, "the block bias rides in the other head's half of the 128 lanes"
    n_units = bsz * head_pairs
    block = (1, seq_len, LANES)
    unit_index = lambda u: (u // head_pairs, 0, u % head_pairs)
    first_spec = pl.BlockSpec(block, lambda u: (0, 0, 0))
    next_spec = pl.BlockSpec(block, lambda u: unit_index(jnp.minimum(u + 1, n_units - 1)))
    unit_spec = pl.BlockSpec(block, unit_index)
    scores = pltpu.VMEM((2, heads, n_blocks, MOBA_BLOCK, MOBA_BLOCK), F32)
    return pl.pallas_call(
        functools.partial(_attn_kernel, n_blocks=n_blocks),
        grid=(n_units,),
        in_specs=[first_spec] * 3 + [next_spec] * 3 + [unit_spec],
        out_specs=unit_spec,
        out_shape=jax.ShapeDtypeStruct((bsz, seq_len, ATTN_WIDTH), BF16),
        scratch_shapes=[pltpu.VMEM((2, heads, n_blocks, MOBA_BLOCK, LANES), BF16),
                        pltpu.VMEM((2, n_blocks, heads, VALUE_ROWS, MOBA_BLOCK), BF16),
                        pltpu.VMEM((2, n_blocks, heads, LANES, MOBA_BLOCK), BF16),
                        scores, scores,
                        pltpu.VMEM((2, heads, 1, MOBA_BLOCK), F32),
                        pltpu.VMEM((2, 2, heads, VALUE_ROWS, MOBA_BLOCK), F32)],
        compiler_params=pltpu.CompilerParams(dimension_semantics=("arbitrary",),
                                             vmem_limit_bytes=VMEM_LIMIT_BYTES),
        name="moba_attn",
    )(q, k, v, q, k, v, z_attn)


def _ssm_tables(lam_re, lam_im, b_re, b_im, c_re, c_im, log_dt):
    T, G, P, H = SSM_CHUNK, SSM_GROUPS, SSM_STATE, SSM_GROUP_DIM
    dt = jnp.repeat(jnp.exp(log_dt.astype(F32)), P)
    lam_r, lam_i = lam_re.astype(F32).reshape(G * P), lam_im.astype(F32).reshape(G * P)
    mag = jnp.exp(lam_r * dt)
    bar_r, bar_i = mag * jnp.cos(lam_i * dt), mag * jnp.sin(lam_i * dt)
    den = lam_r * lam_r + lam_i * lam_i
    f_r = ((bar_r - 1.0) * lam_r + bar_i * lam_i) / den
    f_i = (bar_i * lam_r - (bar_r - 1.0) * lam_i) / den
    b_r, b_i = (b.astype(F32).transpose(2, 0, 1).reshape(H, G * P) for b in (b_re, b_im))
    bb_r, bb_i = f_r * b_r - f_i * b_i, f_r * b_i + f_i * b_r

    def powers(base_r, base_i, count):
        out_r, out_i = [base_r], [base_i]
        for _ in range(count - 1):
            r, i = out_r[-1], out_i[-1]
            out_r.append(r * base_r - i * base_i)
            out_i.append(r * base_i + i * base_r)
        return out_r, out_i

    pw_r, pw_i = powers(bar_r, bar_i, T)
    pw_r = jnp.stack([jnp.ones_like(bar_r)] + pw_r)
    pw_i = jnp.stack([jnp.zeros_like(bar_i)] + pw_i)

    lb_r = pw_r[:T, None, :] * bb_r - pw_i[:T, None, :] * bb_i
    lb_i = pw_r[:T, None, :] * bb_i + pw_i[:T, None, :] * bb_r
    lb_t = jnp.stack([lb_r, lb_i], axis=1)
    c_r, c_i = (c.astype(F32).transpose(2, 0, 1).reshape(P, G * H) for c in (c_re, c_im))
    per_out = lambda pw: jnp.repeat(pw.reshape(T + 1, G, P).transpose(0, 2, 1), H, axis=2)
    pwo_r, pwo_i = per_out(pw_r), per_out(pw_i)
    cl_t = jnp.stack([c_r * pwo_r - c_i * pwo_i, -(c_r * pwo_i + c_i * pwo_r)], axis=1)

    a_r, a_i = powers(pw_r[T], pw_i[T], F32_SUBLANES)
    tiled = lambda a: jnp.stack(a).reshape(len(a), SSM_TILES, STATE_COLS)
    a_lin = jnp.stack([tiled(a_r), tiled(a_i)]).transpose(2, 0, 1, 3)
    shifts = [1 << k for k in range(F32_SUBLANES.bit_length() - 1)]
    a_log = jnp.stack([tiled([a_r[s - 1] for s in shifts]),
                       tiled([a_i[s - 1] for s in shifts])]).transpose(2, 1, 0, 3)
    row = jnp.arange(F32_SUBLANES)[None, None, None, :, None]
    keep = row >= jnp.asarray(shifts)[None, :, None, None, None]
    a_log = jnp.where(keep, a_log[:, :, :, None, :], 0.0)
    return lb_t, cl_t, a_log, a_lin


def _ssm_kernel(u_ref, lb_ref, cl_ref, alog_ref, alin_ref, d_ref, y_ref, m_ref, ws_ref, wo_ref,
                work_ref, prev_ref, *, seq_len, n_seq):
    T, H, P = SSM_CHUNK, SSM_GROUP_DIM, SSM_STATE
    n_chunks = seq_len // T
    iota = lambda shape, d: lax.broadcasted_iota(jnp.int32, shape, d)
    h_bits, p_bits = H.bit_length() - 1, P.bit_length() - 1

    @pl.when(pl.program_id(1) == 0)
    def _():
        spread_h = ((iota((LANES, H), 0) & (H - 1)) == iota((LANES, H), 1)).astype(BF16)
        spread_p = ((iota((STATE_COLS, P), 0) & (P - 1)) == iota((STATE_COLS, P), 1)).astype(BF16)
        same_hp = (iota((LANES, STATE_COLS), 0) >> h_bits) == (iota((LANES, STATE_COLS), 1) >> p_bits)
        same_ph = (iota((STATE_COLS, LANES), 0) >> p_bits) == (iota((STATE_COLS, LANES), 1) >> h_bits)

        def expand(spread, coeff, same):
            full = jnp.dot(spread, coeff.astype(BF16), preferred_element_type=F32)
            return jnp.where(same, full, 0.0).astype(BF16)

        rows = lambda j: slice(j * LANES, (j + 1) * LANES)
        state_out = lambda d: jnp.concatenate(
            [expand(spread_p, cl_ref[d, ri], same_ph) for ri in range(2)], axis=0)
        for j in range(T):
            ws_ref[rows(j), :] = jnp.concatenate(
                [expand(spread_h, lb_ref[T - 1 - j, ri], same_hp) for ri in range(2)], axis=1)
        for i in range(T):
            wo_ref[:, rows(i)] = state_out(i + 1)
        c_out = state_out(0)
        for d in range(T):
            block = jnp.dot(ws_ref[rows(T - 1 - d), :], c_out,
                            preferred_element_type=F32).astype(BF16)
            for j in range(T - d):
                m_ref[rows(j), rows(j + d)] = block
        for j in range(T):
            for i in range(j):
                m_ref[rows(j), rows(i)] = jnp.zeros((LANES, LANES), BF16)

    pieces = [u_ref[pl.ds(b * seq_len + i, n_chunks, stride=T), :]
              for b in range(n_seq) for i in range(T)]
    x = jnp.concatenate(
        [jnp.concatenate(pieces[b * T:(b + 1) * T], axis=1) for b in range(n_seq)], axis=0)
    x_lo = x.astype(BF16)
    groups = [slice(c, c + MXU_COLS) for c in range(0, STATE_COLS, MXU_COLS)]
    im_of = lambda cols: slice(STATE_COLS + cols.start, STATE_COLS + cols.stop)
    traced = lambda k: k + jnp.minimum(pl.program_id(1), 0)
    n_groups = len(groups)
    y_region = 2 * n_groups

    def increment(g, part):
        cols = groups[g] if part == 0 else im_of(groups[g])
        work_ref[traced(2 * g + part)] = jnp.dot(x_lo, ws_ref[:, cols], preferred_element_type=F32)

    def intra_chunk(n):
        hi = (n + 1) * MXU_COLS
        work_ref[traced(y_region + n)] = jnp.dot(x_lo[:, :hi], m_ref[:hi, hi - MXU_COLS:hi],
                                                 preferred_element_type=F32)

    sub = F32_SUBLANES
    first_row = iota((sub, MXU_COLS), 0) == 0
    cmul = lambda a_re, a_im, b_re, b_im: (a_re * b_re - a_im * b_im, a_re * b_im + a_im * b_re)
    tiles_per_seq = n_chunks // sub
    pack = BF16_SUBLANES // sub

    def scan_items(g):
        cols = groups[g]
        log_mul = [(alog_ref[0, k, 0, :, cols], alog_ref[0, k, 1, :, cols])
                   for k in range(sub.bit_length() - 1)]
        lin_mul = (alin_ref[0, 0, :, cols], alin_ref[0, 1, :, cols])

        def sequence(b):
            carry = (jnp.zeros((1, MXU_COLS), F32),) * 2
            pending = []
            for t in range(tiles_per_seq):
                rows = pl.ds((b * tiles_per_seq + t) * sub, sub)
                t_re = work_ref[traced(2 * g), rows, :]
                t_im = work_ref[traced(2 * g + 1), rows, :]
                for k, mul in enumerate(log_mul):
                    d_re, d_im = cmul(*mul, pltpu.roll(t_re, 1 << k, axis=0),
                                      pltpu.roll(t_im, 1 << k, axis=0))
                    t_re, t_im = t_re + d_re, t_im + d_im
                if t > 0:
                    d_re, d_im = cmul(*lin_mul, *carry)
                    t_re, t_im = t_re + d_re, t_im + d_im
                pending.append((jnp.where(first_row, carry[0], pltpu.roll(t_re, 1, axis=0)),
                                jnp.where(first_row, carry[1], pltpu.roll(t_im, 1, axis=0))))
                carry = (t_re[sub - 1:sub], t_im[sub - 1:sub])
                if len(pending) == pack:
                    out = pl.ds((b * tiles_per_seq + t + 1 - pack) * sub, pack * sub)
                    for part, at in enumerate((cols, im_of(cols))):
                        prev_ref[out, at] = jnp.concatenate(
                            [p[part] for p in pending], axis=0).astype(BF16)
                    pending = []
        return [functools.partial(sequence, b) for b in range(n_seq)]

    def alternate(*item_lists):
        for group in zip(*item_lists):
            for item in group:
                item()
        for items in item_lists:
            for item in items[min(map(len, item_lists)):]:
                item()

    increment(0, 0)
    increment(0, 1)
    matmuls = ([functools.partial(increment, g, part) for g in range(1, n_groups)
                for part in range(2)]
               + [functools.partial(intra_chunk, n) for n in range(T * LANES // MXU_COLS)])
    per_scan = -(-len(matmuls) // n_groups)
    for g in range(n_groups):
        alternate(scan_items(g), matmuls[g * per_scan:(g + 1) * per_scan])
    per_tile = MXU_COLS // LANES
    for n in range(T * LANES // MXU_COLS):
        cols = slice(n * MXU_COLS, (n + 1) * MXU_COLS)
        y = (work_ref[y_region + n] + x[:, cols] * jnp.concatenate([d_ref[...]] * per_tile, axis=1)
             + jnp.dot(prev_ref[...], wo_ref[:, cols], preferred_element_type=F32))
        for b in range(n_seq):
            for i in range(per_tile):
                y_ref[pl.ds(b * seq_len + n * per_tile + i, n_chunks, stride=T), :] = (
                    y[b * n_chunks:(b + 1) * n_chunks, i * LANES:(i + 1) * LANES])


def _s5_ssm(u, tables, d_skip, seq_len):
    lb_t, cl_t, a_log, a_lin = tables
    T, H, P = SSM_CHUNK, SSM_GROUP_DIM, SSM_STATE
    assert seq_len % (T * F32_SUBLANES) == 0
    bsz = u.shape[0] // seq_len
    n_seq = max(n for n in range(1, SSM_SEQS_PER_STEP + 1) if bsz % n == 0)
    io_spec = pl.BlockSpec((n_seq * seq_len, LANES), lambda q, b: (b, q))
    square = pltpu.VMEM((T * LANES, T * LANES), BF16)
    n_groups = STATE_COLS // MXU_COLS
    chunk_rows = n_seq * seq_len // T
    return pl.pallas_call(
        functools.partial(_ssm_kernel, seq_len=seq_len, n_seq=n_seq),
        grid=(SSM_TILES, bsz // n_seq),
        in_specs=[io_spec,
                  pl.BlockSpec((T, 2, H, STATE_COLS), lambda q, b: (0, 0, 0, q)),
                  pl.BlockSpec((T + 1, 2, P, LANES), lambda q, b: (0, 0, 0, q)),
                  pl.BlockSpec((1,) + a_log.shape[1:], lambda q, b: (q, 0, 0, 0, 0)),
                  pl.BlockSpec((1,) + a_lin.shape[1:], lambda q, b: (q, 0, 0, 0)),
                  pl.BlockSpec((1, LANES), lambda q, b: (0, q))],
        out_specs=io_spec,
        out_shape=jax.ShapeDtypeStruct(u.shape, F32),
        scratch_shapes=[square, square, square,
                        pltpu.VMEM((2 * n_groups + T * LANES // MXU_COLS, chunk_rows, MXU_COLS), F32),
                        pltpu.VMEM((chunk_rows, 2 * STATE_COLS), BF16)],
        compiler_params=pltpu.CompilerParams(dimension_semantics=("arbitrary", "arbitrary"),
                                             vmem_limit_bytes=VMEM_LIMIT_BYTES),
        name="s5_ssm",
    )(u, lb_t, cl_t, a_log, a_lin, d_skip)


def _out_proj_kernel(x_ref, ma_ref, y_ref, zs_ref, wg32_ref, bg_ref, w32_ref, g_ref, o_ref,
                     wg_ref, w_ref, *, final_norm):
    @pl.when(pl.program_id(0) == 0)
    def _():
        wg_ref[...] = wg32_ref[...].astype(BF16)
        w_ref[...] = w32_ref[...].astype(BF16)

    half_plus = lambda t: 1.0 + jnp.tanh(0.5 * t)
    for c in range(0, x_ref.shape[0], OUT_PROJ_ROWS):
        rows = slice(c, c + OUT_PROJ_ROWS)
        y = y_ref[rows, :]
        y = y * (1.0 + jnp.tanh(math.sqrt(2.0 / math.pi) * (y + 0.044715 * (y * y * y))))
        gate = jnp.dot((0.5 * y).astype(BF16), wg_ref[...], preferred_element_type=F32) + bg_ref[...]
        z = zs_ref[rows, :].astype(F32)
        mixed_ssm = (0.125 * y) * half_plus(gate) * (z * half_plus(z))
        mixed = jnp.concatenate([ma_ref[rows, :], mixed_ssm.astype(BF16)], axis=1)
        r = x_ref[rows, :] + jnp.dot(mixed, w_ref[...], preferred_element_type=F32)
        if final_norm:
            ms = jnp.mean(r * r, axis=-1, keepdims=True)
            r = r * lax.rsqrt(ms + NORM_EPS) * g_ref[...]
        o_ref[rows, :] = r


def _out_proj(x2, mixed_attn, y_ssm, z_ssm, w_glu, b_glu, w_out, gain, tm, final_norm):
    rows = x2.shape[0]
    row_spec = lambda width: pl.BlockSpec((tm, width), lambda i: (i, 0))
    return pl.pallas_call(
        functools.partial(_out_proj_kernel, final_norm=final_norm),
        grid=(rows // tm,),
        in_specs=[row_spec(D_MODEL), row_spec(ATTN_WIDTH), row_spec(SSM_WIDTH), row_spec(SSM_WIDTH),
                  _resident((SSM_WIDTH, SSM_WIDTH)), _resident((1, SSM_WIDTH)),
                  _resident((ATTN_WIDTH + SSM_WIDTH, D_MODEL)), _resident((1, D_MODEL))],
        out_specs=row_spec(D_MODEL),
        out_shape=jax.ShapeDtypeStruct((rows, D_MODEL), F32),
        scratch_shapes=[pltpu.VMEM((SSM_WIDTH, SSM_WIDTH), BF16),
                        pltpu.VMEM((ATTN_WIDTH + SSM_WIDTH, D_MODEL), BF16)],
        compiler_params=pltpu.CompilerParams(dimension_semantics=("arbitrary",),
                                             vmem_limit_bytes=VMEM_LIMIT_BYTES),
        name="out_proj",
    )(x2, mixed_attn, y_ssm, z_ssm, w_glu, b_glu, w_out, gain)


def _rotary_tables(seq_len):
    half = HEAD_DIM // 2
    inv_freq = 1.0 / (ROPE_THETA ** (jnp.arange(half, dtype=F32) / half))
    ang = jnp.arange(seq_len, dtype=F32)[:, None] * inv_freq[None, :]
    cos, sin, zero = jnp.cos(ang), jnp.sin(ang), jnp.zeros_like(ang)
    reps = LANES // HEAD_DIM
    cos_t = jnp.tile(jnp.concatenate([cos, cos], axis=1), (1, reps))
    sin_hi = jnp.tile(jnp.concatenate([zero, sin], axis=1), (1, reps))
    sin_lo = jnp.tile(jnp.concatenate([-sin, zero], axis=1), (1, reps))
    return cos_t, sin_hi, sin_lo


def kernel(x, norm_gain, w_in, w_out, lam_re, lam_im, b_re, b_im, c_re, c_im,
           d_skip, log_dt, w_glu, b_glu, final_gain):
    bsz, seq_len, _ = x.shape
    depth = norm_gain.shape[0]
    assert seq_len % MOBA_BLOCK == 0 and seq_len // MOBA_BLOCK > MOBA_TOP_K
    assert seq_len % SSM_CHUNK == 0
    tm = min(PROJ_TILE_ROWS, seq_len)
    cos, sin_hi, sin_lo = _rotary_tables(seq_len)

    x2 = x.reshape(bsz * seq_len, D_MODEL)
    for layer in range(depth):
        tables = _ssm_tables(lam_re[layer], lam_im[layer], b_re[layer], b_im[layer],
                             c_re[layer], c_im[layer], log_dt[layer])
        q, k, v, z_attn, u, z_ssm = _in_proj(
            x2, norm_gain[layer].reshape(1, D_MODEL), w_in[layer], cos, sin_hi, sin_lo, seq_len, tm)
        to_seq = lambda t: t.reshape(bsz, seq_len, ATTN_WIDTH)
        mixed_attn = _moba_attention(to_seq(q), to_seq(k), to_seq(v), to_seq(z_attn))
        y_ssm = _s5_ssm(u, tables, d_skip[layer].reshape(1, SSM_WIDTH), seq_len)
        x2 = _out_proj(x2, mixed_attn.reshape(bsz * seq_len, ATTN_WIDTH), y_ssm, z_ssm,
                       w_glu[layer], b_glu[layer].reshape(1, SSM_WIDTH),
                       w_out[layer], final_gain.reshape(1, D_MODEL), tm,
                       final_norm=layer == depth - 1)
    return x2.reshape(bsz, seq_len, D_MODEL)
```

```python
import functools
import math

import jax
import jax.numpy as jnp
from jax import lax
from jax.experimental import pallas as pl
from jax.experimental.pallas import tpu as pltpu

F32 = jnp.float32
BF16 = jnp.bfloat16

D_MODEL = 1024
HEAD_DIM = 64
ATTN_HEADS = 8
ATTN_WIDTH = ATTN_HEADS * HEAD_DIM
MOBA_BLOCK = 256
MOBA_TOP_K = 3
ROPE_THETA = 10000.0
SSM_GROUP_DIM = 16
SSM_GROUPS = 32
SSM_WIDTH = SSM_GROUPS * SSM_GROUP_DIM
SSM_STATE = 64
IN_PROJ_WIDTH = 4 * ATTN_WIDTH + 2 * SSM_WIDTH
NORM_EPS = 1e-6
NEG_INF = -1e30

LANES = 128
SSM_CHUNK = 8
GROUPS_PER_TILE = LANES // SSM_GROUP_DIM
SSM_TILES = SSM_WIDTH // LANES
TILE_COLS = SSM_CHUNK * LANES
STATE_COLS = GROUPS_PER_TILE * SSM_STATE
VMEM_LIMIT_BYTES = 56 * 1024 * 1024
MXU_COLS = 256
OUT_PROJ_ROWS = 256
PROJ_TILE_ROWS = 1024
SSM_SEQS_PER_STEP = 4
F32_SUBLANES = 8
BF16_SUBLANES = 16
VALUE_ROWS = HEAD_DIM + BF16_SUBLANES
QK_SCALE = math.log2(math.e) / math.sqrt(HEAD_DIM)


def _resident(shape):
    zeros = (0,) * len(shape)
    return pl.BlockSpec(shape, lambda *_: zeros, pipeline_mode=pl.Buffered(1))


def _in_proj_kernel(x_ref, g_ref, w32_ref, cos_ref, sin_hi_ref, sin_lo_ref,
                    q_ref, k_ref, v_ref, za_ref, u_ref, zs_ref, w_ref):
    @pl.when(pl.program_id(0) == 0)
    def _():
        for c in range(0, IN_PROJ_WIDTH, ATTN_WIDTH):
            w_ref[:, c:c + ATTN_WIDTH] = w32_ref[:, c:c + ATTN_WIDTH].astype(BF16)

    x = x_ref[...]
    ms = jnp.mean(x * x, axis=-1, keepdims=True)
    h = (x * lax.rsqrt(ms + NORM_EPS) * g_ref[...]).astype(BF16)

    def section(idx):
        return jnp.dot(h, w_ref[:, idx * ATTN_WIDTH:(idx + 1) * ATTN_WIDTH],
                       preferred_element_type=F32)

    cos = cos_ref[...]
    sin_hi = sin_hi_ref[...]
    sin_lo = sin_lo_ref[...]

    def rotary(t):
        return (t * cos + pltpu.roll(t, HEAD_DIM // 2, axis=1) * sin_hi
                + pltpu.roll(t, LANES - HEAD_DIM // 2, axis=1) * sin_lo)

    q = section(0)
    k = section(1)
    for c in range(ATTN_WIDTH // LANES):
        sl = slice(c * LANES, (c + 1) * LANES)
        q_ref[:, sl] = (rotary(q[:, sl]) * QK_SCALE).astype(BF16)
        k_ref[:, sl] = rotary(k[:, sl]).astype(BF16)
    v_ref[...] = section(2).astype(BF16)
    za_ref[...] = section(3).astype(BF16)
    u_ref[...] = section(4)
    zs_ref[...] = section(5).astype(BF16)


def _in_proj(x2, gain, w_in, cos, sin_hi, sin_lo, seq_len, tm):
    rows = x2.shape[0]
    pos_blocks = seq_len // tm
    row_spec = lambda width: pl.BlockSpec((tm, width), lambda i: (i, 0))
    tab_spec = pl.BlockSpec((tm, LANES), lambda i: (i % pos_blocks, 0))
    out_bf16 = jax.ShapeDtypeStruct((rows, ATTN_WIDTH), BF16)
    out_f32 = jax.ShapeDtypeStruct((rows, ATTN_WIDTH), F32)
    return pl.pallas_call(
        _in_proj_kernel,
        grid=(rows // tm,),
        in_specs=[row_spec(D_MODEL), _resident((1, D_MODEL)), _resident((D_MODEL, IN_PROJ_WIDTH)),
                  tab_spec, tab_spec, tab_spec],
        out_specs=[row_spec(ATTN_WIDTH)] * 6,
        out_shape=[out_bf16, out_bf16, out_bf16, out_bf16, out_f32, out_bf16],
        scratch_shapes=[pltpu.VMEM((D_MODEL, IN_PROJ_WIDTH), BF16)],
        compiler_params=pltpu.CompilerParams(dimension_semantics=("arbitrary",),
                                             vmem_limit_bytes=VMEM_LIMIT_BYTES),
        name="in_proj",
    )(x2, gain, w_in, cos, sin_hi, sin_lo)


def _attn_kernel(q0_ref, k0_ref, v0_ref, qn_ref, kn_ref, vn_ref, z_ref, o_ref,
                 kx_ref, vx_ref, qx_ref, sa_ref, sb_ref, m_ref, acc_ref, *, n_blocks):
    unit = pl.program_id(0)
    blk = MOBA_BLOCK
    heads = LANES // HEAD_DIM
    n_pairs = n_blocks // 2
    cur = lax.rem(unit, 2)
    nxt = 1 - cur

    head_rows = lambda h: slice(h * HEAD_DIM, (h + 1) * HEAD_DIM)
    spare_base = lambda h: ((h + 1) % heads) * HEAD_DIM
    iota = lambda shape, d: lax.broadcasted_iota(jnp.int32, shape, d)
    key = iota((blk, blk), 0)
    query = iota((blk, blk), 1)
    col_max = lambda s: jnp.max(s, axis=0, keepdims=True)

    def prepare_items(q_ref, k_ref, v_ref, slot):
        kmean = {}

        def key_block(j):
            lane = iota((blk, LANES), 1)
            sum_rows = (iota((VALUE_ROWS - HEAD_DIM, blk), 0) == 0).astype(F32)
            rows = slice(j * blk, (j + 1) * blk)
            kj = k_ref[0, rows, :]
            kmean[j] = jnp.mean(kj.astype(F32), axis=0, keepdims=True)
            vt = v_ref[0, rows, :].astype(F32).T
            for h in range(heads):
                in_head = (lane >= h * HEAD_DIM) & (lane < (h + 1) * HEAD_DIM)
                tag = jnp.where(lane == spare_base(h) + j, 1.0, 0.0).astype(BF16)
                kx_ref[slot, h, j] = jnp.where(in_head, kj, tag)
                vx_ref[slot, j, h] = jnp.concatenate([vt[head_rows(h)], sum_rows],
                                                     axis=0).astype(BF16)

        def query_block(qi):
            if "all" not in kmean:
                kmean["all"] = jnp.concatenate([kmean[j] for j in range(n_blocks)],
                                               axis=0).astype(BF16)
            blk_row = iota((n_blocks, blk), 0)
            zero_rows = lambda n: jnp.zeros((n, blk), F32)
            in_head_order = lambda h, own, other: [own, other] if h == 0 else [other, own]
            qt = q_ref[0, qi * blk:(qi + 1) * blk, :].astype(F32).T
            past = blk_row < qi
            for h in range(heads):
                q_rows = qt[head_rows(h)]
                q_only = jnp.concatenate(in_head_order(h, q_rows, zero_rows(HEAD_DIM)), axis=0)
                gate = jnp.dot(kmean["all"], q_only.astype(BF16), preferred_element_type=F32)
                gate = jnp.where(past, gate, NEG_INF)
                beaten_by = jnp.zeros((n_blocks, blk), jnp.int32)
                for j in range(n_blocks):
                    gj = gate[j:j + 1, :]
                    wins = (gj > gate) | ((gj == gate) & (j < blk_row))
                    beaten_by = beaten_by + wins.astype(jnp.int32)
                keep = (past & (beaten_by < MOBA_TOP_K)) | (blk_row == qi)
                bias = jnp.where(keep, 0.0, NEG_INF)
                spare = jnp.concatenate([bias, zero_rows(HEAD_DIM - n_blocks)], axis=0)
                qx_ref[slot, qi, h] = jnp.concatenate(in_head_order(h, q_rows, spare),
                                                      axis=0).astype(BF16)

        return ([functools.partial(key_block, j) for j in range(n_blocks)]
                + [functools.partial(query_block, qi) for qi in range(n_blocks)])

    def visible_blocks(pair):
        tiles = (pair, n_blocks - 1 - pair)
        return ([(t, tiles[t], True) for t in range(2)]
                + [(t, j, False) for t in range(2) for j in range(tiles[t])])

    traced = lambda t: t + jnp.minimum(unit, 0)

    def stage_items(pair, s_ref, m, slot):
        def block(t, j, own, h):
            qi = (pair, n_blocks - 1 - pair)[t]
            s = jnp.dot(kx_ref[slot, h, j], qx_ref[slot, qi, h], preferred_element_type=F32)
            if own:
                s = jnp.where(key <= query, s, NEG_INF)
            s_ref[traced(t), h, j] = s
            m[t, h] = col_max(s) if own else jnp.maximum(m[t, h], col_max(s))
        return [functools.partial(block, *b, h) for b in visible_blocks(pair)
                for h in range(heads)]

    def finish_items(pair, s_ref, m, acc_ref, slot):
        def block(t, j, own, h):
            p = jnp.exp2(s_ref[traced(t), h, j] - m[t, h]).astype(BF16)
            pv = jnp.dot(vx_ref[slot, j, h], p, preferred_element_type=F32)
            if own:
                acc_ref[traced(t), h] = pv
            else:
                acc_ref[traced(t), h] += pv

        def write_rows():
            for t, qi in enumerate((pair, n_blocks - 1 - pair)):
                rows = slice(qi * blk, (qi + 1) * blk)
                acc = [acc_ref[t, h] for h in range(heads)]
                o_t = jnp.concatenate([a[:HEAD_DIM] / a[HEAD_DIM:HEAD_DIM + 1] for a in acc],
                                      axis=0)
                z = z_ref[0, rows, :].astype(F32)
                silu = 0.5 * z * (1.0 + jnp.tanh(0.5 * z))
                o_ref[0, rows, :] = (o_t.T * silu).astype(BF16)

        return ([functools.partial(block, *b, h) for b in visible_blocks(pair)
                 for h in range(heads)] + [write_rows])

    def alternate(*item_lists):
        for group in zip(*item_lists):
            for item in group:
                item()
        for items in item_lists:
            for item in items[min(map(len, item_lists)):]:
                item()

    m_keys = [(t, h) for t in range(2) for h in range(heads)]

    @pl.when(unit == 0)
    def _():
        m0 = {}
        alternate(prepare_items(q0_ref, k0_ref, v0_ref, 0))
        alternate(stage_items(0, sa_ref, m0, 0))
        for t, h in m_keys:
            m_ref[t, h] = m0[t, h]

    m = {key_: m_ref[key_] for key_ in m_keys}
    for pair in range(n_pairs):
        s_ref, s_next = (sa_ref, sb_ref) if pair % 2 == 0 else (sb_ref, sa_ref)
        m_next = {}
        lists = [finish_items(pair, s_ref, m, acc_ref.at[pair % 2], cur)]
        if pair + 1 < n_pairs:
            lists.append(stage_items(pair + 1, s_next, m_next, cur))
        else:
            lists.append(stage_items(0, s_next, m_next, nxt))
        if pair == n_pairs - 2:
            lists.append(prepare_items(qn_ref, kn_ref, vn_ref, nxt))
        alternate(lists[1], lists[0], *lists[2:])
        m = m_next
    for t, h in m_keys:
        m_ref[t, h] = m[t, h]


def _moba_attention(q, k, v, z_attn):
    bsz, seq_len, _ = q.shape
    n_blocks = seq_len // MOBA_BLOCK
    assert n_blocks % 4 == 0 and n_blocks <= HEAD_DIM
    head_pairs = ATTN_WIDTH // LANES
    heads = LANES // HEAD_DIM
    assert heads == 2, "the block bias rides in the other head's half of the 128 lanes"
    n_units = bsz * head_pairs
    block = (1, seq_len, LANES)
    unit_index = lambda u: (u // head_pairs, 0, u % head_pairs)
    first_spec = pl.BlockSpec(block, lambda u: (0, 0, 0))
    next_spec = pl.BlockSpec(block, lambda u: unit_index(jnp.minimum(u + 1, n_units - 1)))
    unit_spec = pl.BlockSpec(block, unit_index)
    scores = pltpu.VMEM((2, heads, n_blocks, MOBA_BLOCK, MOBA_BLOCK), F32)
    return pl.pallas_call(
        functools.partial(_attn_kernel, n_blocks=n_blocks),
        grid=(n_units,),
        in_specs=[first_spec] * 3 + [next_spec] * 3 + [unit_spec],
        out_specs=unit_spec,
        out_shape=jax.ShapeDtypeStruct((bsz, seq_len, ATTN_WIDTH), BF16),
        scratch_shapes=[pltpu.VMEM((2, heads, n_blocks, MOBA_BLOCK, LANES), BF16),
                        pltpu.VMEM((2, n_blocks, heads, VALUE_ROWS, MOBA_BLOCK), BF16),
                        pltpu.VMEM((2, n_blocks, heads, LANES, MOBA_BLOCK), BF16),
                        scores, scores,
                        pltpu.VMEM((2, heads, 1, MOBA_BLOCK), F32),
                        pltpu.VMEM((2, 2, heads, VALUE_ROWS, MOBA_BLOCK), F32)],
        compiler_params=pltpu.CompilerParams(dimension_semantics=("arbitrary",),
                                             vmem_limit_bytes=VMEM_LIMIT_BYTES),
        name="moba_attn",
    )(q, k, v, q, k, v, z_attn)


def _ssm_tables(lam_re, lam_im, b_re, b_im, c_re, c_im, log_dt):
    T, G, P, H = SSM_CHUNK, SSM_GROUPS, SSM_STATE, SSM_GROUP_DIM
    dt = jnp.repeat(jnp.exp(log_dt.astype(F32)), P)
    lam_r, lam_i = lam_re.astype(F32).reshape(G * P), lam_im.astype(F32).reshape(G * P)
    mag = jnp.exp(lam_r * dt)
    bar_r, bar_i = mag * jnp.cos(lam_i * dt), mag * jnp.sin(lam_i * dt)
    den = lam_r * lam_r + lam_i * lam_i
    f_r = ((bar_r - 1.0) * lam_r + bar_i * lam_i) / den
    f_i = (bar_i * lam_r - (bar_r - 1.0) * lam_i) / den
    b_r, b_i = (b.astype(F32).transpose(2, 0, 1).reshape(H, G * P) for b in (b_re, b_im))
    bb_r, bb_i = f_r * b_r - f_i * b_i, f_r * b_i + f_i * b_r

    def powers(base_r, base_i, count):
        out_r, out_i = [base_r], [base_i]
        for _ in range(count - 1):
            r, i = out_r[-1], out_i[-1]
            out_r.append(r * base_r - i * base_i)
            out_i.append(r * base_i + i * base_r)
        return out_r, out_i

    pw_r, pw_i = powers(bar_r, bar_i, T)
    pw_r = jnp.stack([jnp.ones_like(bar_r)] + pw_r)
    pw_i = jnp.stack([jnp.zeros_like(bar_i)] + pw_i)

    lb_r = pw_r[:T, None, :] * bb_r - pw_i[:T, None, :] * bb_i
    lb_i = pw_r[:T, None, :] * bb_i + pw_i[:T, None, :] * bb_r
    lb_t = jnp.stack([lb_r, lb_i], axis=1)
    c_r, c_i = (c.astype(F32).transpose(2, 0, 1).reshape(P, G * H) for c in (c_re, c_im))
    per_out = lambda pw: jnp.repeat(pw.reshape(T + 1, G, P).transpose(0, 2, 1), H, axis=2)
    pwo_r, pwo_i = per_out(pw_r), per_out(pw_i)
    cl_t = jnp.stack([c_r * pwo_r - c_i * pwo_i, -(c_r * pwo_i + c_i * pwo_r)], axis=1)

    a_r, a_i = powers(pw_r[T], pw_i[T], F32_SUBLANES)
    tiled = lambda a: jnp.stack(a).reshape(len(a), SSM_TILES, STATE_COLS)
    a_lin = jnp.stack([tiled(a_r), tiled(a_i)]).transpose(2, 0, 1, 3)
    shifts = [1 << k for k in range(F32_SUBLANES.bit_length() - 1)]
    a_log = jnp.stack([tiled([a_r[s - 1] for s in shifts]),
                       tiled([a_i[s - 1] for s in shifts])]).transpose(2, 1, 0, 3)
    row = jnp.arange(F32_SUBLANES)[None, None, None, :, None]
    keep = row >= jnp.asarray(shifts)[None, :, None, None, None]
    a_log = jnp.where(keep, a_log[:, :, :, None, :], 0.0)
    return lb_t, cl_t, a_log, a_lin


def _ssm_kernel(u_ref, lb_ref, cl_ref, alog_ref, alin_ref, d_ref, y_ref, m_ref, ws_ref, wo_ref,
                work_ref, prev_ref, *, seq_len, n_seq):
    T, H, P = SSM_CHUNK, SSM_GROUP_DIM, SSM_STATE
    n_chunks = seq_len // T
    iota = lambda shape, d: lax.broadcasted_iota(jnp.int32, shape, d)
    h_bits, p_bits = H.bit_length() - 1, P.bit_length() - 1

    @pl.when(pl.program_id(1) == 0)
    def _():
        spread_h = ((iota((LANES, H), 0) & (H - 1)) == iota((LANES, H), 1)).astype(BF16)
        spread_p = ((iota((STATE_COLS, P), 0) & (P - 1)) == iota((STATE_COLS, P), 1)).astype(BF16)
        same_hp = (iota((LANES, STATE_COLS), 0) >> h_bits) == (iota((LANES, STATE_COLS), 1) >> p_bits)
        same_ph = (iota((STATE_COLS, LANES), 0) >> p_bits) == (iota((STATE_COLS, LANES), 1) >> h_bits)

        def expand(spread, coeff, same):
            full = jnp.dot(spread, coeff.astype(BF16), preferred_element_type=F32)
            return jnp.where(same, full, 0.0).astype(BF16)

        rows = lambda j: slice(j * LANES, (j + 1) * LANES)
        state_out = lambda d: jnp.concatenate(
            [expand(spread_p, cl_ref[d, ri], same_ph) for ri in range(2)], axis=0)
        for j in range(T):
            ws_ref[rows(j), :] = jnp.concatenate(
                [expand(spread_h, lb_ref[T - 1 - j, ri], same_hp) for ri in range(2)], axis=1)
        for i in range(T):
            wo_ref[:, rows(i)] = state_out(i + 1)
        c_out = state_out(0)
        for d in range(T):
            block = jnp.dot(ws_ref[rows(T - 1 - d), :], c_out,
                            preferred_element_type=F32).astype(BF16)
            for j in range(T - d):
                m_ref[rows(j), rows(j + d)] = block
        for j in range(T):
            for i in range(j):
                m_ref[rows(j), rows(i)] = jnp.zeros((LANES, LANES), BF16)

    pieces = [u_ref[pl.ds(b * seq_len + i, n_chunks, stride=T), :]
              for b in range(n_seq) for i in range(T)]
    x = jnp.concatenate(
        [jnp.concatenate(pieces[b * T:(b + 1) * T], axis=1) for b in range(n_seq)], axis=0)
    x_lo = x.astype(BF16)
    groups = [slice(c, c + MXU_COLS) for c in range(0, STATE_COLS, MXU_COLS)]
    im_of = lambda cols: slice(STATE_COLS + cols.start, STATE_COLS + cols.stop)
    traced = lambda k: k + jnp.minimum(pl.program_id(1), 0)
    n_groups = len(groups)
    y_region = 2 * n_groups

    def increment(g, part):
        cols = groups[g] if part == 0 else im_of(groups[g])
        work_ref[traced(2 * g + part)] = jnp.dot(x_lo, ws_ref[:, cols], preferred_element_type=F32)

    def intra_chunk(n):
        hi = (n + 1) * MXU_COLS
        work_ref[traced(y_region + n)] = jnp.dot(x_lo[:, :hi], m_ref[:hi, hi - MXU_COLS:hi],
                                                 preferred_element_type=F32)

    sub = F32_SUBLANES
    first_row = iota((sub, MXU_COLS), 0) == 0
    cmul = lambda a_re, a_im, b_re, b_im: (a_re * b_re - a_im * b_im, a_re * b_im + a_im * b_re)
    tiles_per_seq = n_chunks // sub
    pack = BF16_SUBLANES // sub

    def scan_items(g):
        cols = groups[g]
        log_mul = [(alog_ref[0, k, 0, :, cols], alog_ref[0, k, 1, :, cols])
                   for k in range(sub.bit_length() - 1)]
        lin_mul = (alin_ref[0, 0, :, cols], alin_ref[0, 1, :, cols])

        def sequence(b):
            carry = (jnp.zeros((1, MXU_COLS), F32),) * 2
            pending = []
            for t in range(tiles_per_seq):
                rows = pl.ds((b * tiles_per_seq + t) * sub, sub)
                t_re = work_ref[traced(2 * g), rows, :]
                t_im = work_ref[traced(2 * g + 1), rows, :]
                for k, mul in enumerate(log_mul):
                    d_re, d_im = cmul(*mul, pltpu.roll(t_re, 1 << k, axis=0),
                                      pltpu.roll(t_im, 1 << k, axis=0))
                    t_re, t_im = t_re + d_re, t_im + d_im
                if t > 0:
                    d_re, d_im = cmul(*lin_mul, *carry)
                    t_re, t_im = t_re + d_re, t_im + d_im
                pending.append((jnp.where(first_row, carry[0], pltpu.roll(t_re, 1, axis=0)),
                                jnp.where(first_row, carry[1], pltpu.roll(t_im, 1, axis=0))))
                carry = (t_re[sub - 1:sub], t_im[sub - 1:sub])
                if len(pending) == pack:
                    out = pl.ds((b * tiles_per_seq + t + 1 - pack) * sub, pack * sub)
                    for part, at in enumerate((cols, im_of(cols))):
                        prev_ref[out, at] = jnp.concatenate(
                            [p[part] for p in pending], axis=0).astype(BF16)
                    pending = []
        return [functools.partial(sequence, b) for b in range(n_seq)]

    def alternate(*item_lists):
        for group in zip(*item_lists):
            for item in group:
                item()
        for items in item_lists:
            for item in items[min(map(len, item_lists)):]:
                item()

    increment(0, 0)
    increment(0, 1)
    matmuls = ([functools.partial(increment, g, part) for g in range(1, n_groups)
                for part in range(2)]
               + [functools.partial(intra_chunk, n) for n in range(T * LANES // MXU_COLS)])
    per_scan = -(-len(matmuls) // n_groups)
    for g in range(n_groups):
        alternate(scan_items(g), matmuls[g * per_scan:(g + 1) * per_scan])
    per_tile = MXU_COLS // LANES
    for n in range(T * LANES // MXU_COLS):
        cols = slice(n * MXU_COLS, (n + 1) * MXU_COLS)
        y = (work_ref[y_region + n] + x[:, cols] * jnp.concatenate([d_ref[...]] * per_tile, axis=1)
             + jnp.dot(prev_ref[...], wo_ref[:, cols], preferred_element_type=F32))
        for b in range(n_seq):
            for i in range(per_tile):
                y_ref[pl.ds(b * seq_len + n * per_tile + i, n_chunks, stride=T), :] = (
                    y[b * n_chunks:(b + 1) * n_chunks, i * LANES:(i + 1) * LANES])


def _s5_ssm(u, tables, d_skip, seq_len):
    lb_t, cl_t, a_log, a_lin = tables
    T, H, P = SSM_CHUNK, SSM_GROUP_DIM, SSM_STATE
    assert seq_len % (T * F32_SUBLANES) == 0
    bsz = u.shape[0] // seq_len
    n_seq = max(n for n in range(1, SSM_SEQS_PER_STEP + 1) if bsz % n == 0)
    io_spec = pl.BlockSpec((n_seq * seq_len, LANES), lambda q, b: (b, q))
    square = pltpu.VMEM((T * LANES, T * LANES), BF16)
    n_groups = STATE_COLS // MXU_COLS
    chunk_rows = n_seq * seq_len // T
    return pl.pallas_call(
        functools.partial(_ssm_kernel, seq_len=seq_len, n_seq=n_seq),
        grid=(SSM_TILES, bsz // n_seq),
        in_specs=[io_spec,
                  pl.BlockSpec((T, 2, H, STATE_COLS), lambda q, b: (0, 0, 0, q)),
                  pl.BlockSpec((T + 1, 2, P, LANES), lambda q, b: (0, 0, 0, q)),
                  pl.BlockSpec((1,) + a_log.shape[1:], lambda q, b: (q, 0, 0, 0, 0)),
                  pl.BlockSpec((1,) + a_lin.shape[1:], lambda q, b: (q, 0, 0, 0)),
                  pl.BlockSpec((1, LANES), lambda q, b: (0, q))],
        out_specs=io_spec,
        out_shape=jax.ShapeDtypeStruct(u.shape, F32),
        scratch_shapes=[square, square, square,
                        pltpu.VMEM((2 * n_groups + T * LANES // MXU_COLS, chunk_rows, MXU_COLS), F32),
                        pltpu.VMEM((chunk_rows, 2 * STATE_COLS), BF16)],
        compiler_params=pltpu.CompilerParams(dimension_semantics=("arbitrary", "arbitrary"),
                                             vmem_limit_bytes=VMEM_LIMIT_BYTES),
        name="s5_ssm",
    )(u, lb_t, cl_t, a_log, a_lin, d_skip)


def _out_proj_kernel(x_ref, ma_ref, y_ref, zs_ref, wg32_ref, bg_ref, w32_ref, g_ref, o_ref,
                     wg_ref, w_ref, *, final_norm):
    @pl.when(pl.program_id(0) == 0)
    def _():
        wg_ref[...] = wg32_ref[...].astype(BF16)
        w_ref[...] = w32_ref[...].astype(BF16)

    half_plus = lambda t: 1.0 + jnp.tanh(0.5 * t)
    for c in range(0, x_ref.shape[0], OUT_PROJ_ROWS):
        rows = slice(c, c + OUT_PROJ_ROWS)
        y = y_ref[rows, :]
        y = y * (1.0 + jnp.tanh(math.sqrt(2.0 / math.pi) * (y + 0.044715 * (y * y * y))))
        gate = jnp.dot((0.5 * y).astype(BF16), wg_ref[...], preferred_element_type=F32) + bg_ref[...]
        z = zs_ref[rows, :].astype(F32)
        mixed_ssm = (0.125 * y) * half_plus(gate) * (z * half_plus(z))
        mixed = jnp.concatenate([ma_ref[rows, :], mixed_ssm.astype(BF16)], axis=1)
        r = x_ref[rows, :] + jnp.dot(mixed, w_ref[...], preferred_element_type=F32)
        if final_norm:
            ms = jnp.mean(r * r, axis=-1, keepdims=True)
            r = r * lax.rsqrt(ms + NORM_EPS) * g_ref[...]
        o_ref[rows, :] = r


def _out_proj(x2, mixed_attn, y_ssm, z_ssm, w_glu, b_glu, w_out, gain, tm, final_norm):
    rows = x2.shape[0]
    row_spec = lambda width: pl.BlockSpec((tm, width), lambda i: (i, 0))
    return pl.pallas_call(
        functools.partial(_out_proj_kernel, final_norm=final_norm),
        grid=(rows // tm,),
        in_specs=[row_spec(D_MODEL), row_spec(ATTN_WIDTH), row_spec(SSM_WIDTH), row_spec(SSM_WIDTH),
                  _resident((SSM_WIDTH, SSM_WIDTH)), _resident((1, SSM_WIDTH)),
                  _resident((ATTN_WIDTH + SSM_WIDTH, D_MODEL)), _resident((1, D_MODEL))],
        out_specs=row_spec(D_MODEL),
        out_shape=jax.ShapeDtypeStruct((rows, D_MODEL), F32),
        scratch_shapes=[pltpu.VMEM((SSM_WIDTH, SSM_WIDTH), BF16),
                        pltpu.VMEM((ATTN_WIDTH + SSM_WIDTH, D_MODEL), BF16)],
        compiler_params=pltpu.CompilerParams(dimension_semantics=("arbitrary",),
                                             vmem_limit_bytes=VMEM_LIMIT_BYTES),
        name="out_proj",
    )(x2, mixed_attn, y_ssm, z_ssm, w_glu, b_glu, w_out, gain)


def _rotary_tables(seq_len):
    half = HEAD_DIM // 2
    lane = jnp.arange(LANES)
    inv_freq = 1.0 / (ROPE_THETA ** ((lane % half).astype(F32) / half))
    ang = jnp.arange(seq_len, dtype=F32)[:, None] * inv_freq[None, :]
    cos, sin = jnp.cos(ang), jnp.sin(ang)
    upper = (lane % HEAD_DIM >= half)[None, :]
    sin_hi = jnp.where(upper, sin, 0.0)
    sin_lo = jnp.where(upper, 0.0, -sin)
    return cos, sin_hi, sin_lo


def kernel(x, norm_gain, w_in, w_out, lam_re, lam_im, b_re, b_im, c_re, c_im,
           d_skip, log_dt, w_glu, b_glu, final_gain):
    bsz, seq_len, _ = x.shape
    depth = norm_gain.shape[0]
    assert seq_len % MOBA_BLOCK == 0 and seq_len // MOBA_BLOCK > MOBA_TOP_K
    assert seq_len % SSM_CHUNK == 0
    tm = min(PROJ_TILE_ROWS, seq_len)
    cos, sin_hi, sin_lo = _rotary_tables(seq_len)

    x2 = x.reshape(bsz * seq_len, D_MODEL)
    for layer in range(depth):
        tables = _ssm_tables(lam_re[layer], lam_im[layer], b_re[layer], b_im[layer],
                             c_re[layer], c_im[layer], log_dt[layer])
        q, k, v, z_attn, u, z_ssm = _in_proj(
            x2, norm_gain[layer].reshape(1, D_MODEL), w_in[layer], cos, sin_hi, sin_lo, seq_len, tm)
        to_seq = lambda t: t.reshape(bsz, seq_len, ATTN_WIDTH)
        mixed_attn = _moba_attention(to_seq(q), to_seq(k), to_seq(v), to_seq(z_attn))
        y_ssm = _s5_ssm(u, tables, d_skip[layer].reshape(1, SSM_WIDTH), seq_len)
        x2 = _out_proj(x2, mixed_attn.reshape(bsz * seq_len, ATTN_WIDTH), y_ssm, z_ssm,
                       w_glu[layer], b_glu[layer].reshape(1, SSM_WIDTH),
                       w_out[layer], final_gain.reshape(1, D_MODEL), tm,
                       final_norm=layer == depth - 1)
    return x2.reshape(bsz, seq_len, D_MODEL)
```

```python
import functools
import math

import jax
import jax.numpy as jnp
from jax import lax
from jax.experimental import pallas as pl
from jax.experimental.pallas import tpu as pltpu

F32 = jnp.float32
BF16 = jnp.bfloat16

D_MODEL = 1024
HEAD_DIM = 64
ATTN_HEADS = 8
ATTN_WIDTH = ATTN_HEADS * HEAD_DIM
MOBA_BLOCK = 256
MOBA_TOP_K = 3
ROPE_THETA = 10000.0
SSM_GROUP_DIM = 16
SSM_GROUPS = 32
SSM_WIDTH = SSM_GROUPS * SSM_GROUP_DIM
SSM_STATE = 64
IN_PROJ_WIDTH = 4 * ATTN_WIDTH + 2 * SSM_WIDTH
NORM_EPS = 1e-6
NEG_INF = -1e30

LANES = 128
SSM_CHUNK = 8
GROUPS_PER_TILE = LANES // SSM_GROUP_DIM
SSM_TILES = SSM_WIDTH // LANES
TILE_COLS = SSM_CHUNK * LANES
STATE_COLS = GROUPS_PER_TILE * SSM_STATE
VMEM_LIMIT_BYTES = 56 * 1024 * 1024
MXU_COLS = 256
OUT_PROJ_ROWS = 256
PROJ_TILE_ROWS = 1024
SSM_SEQS_PER_STEP = 4
F32_SUBLANES = 8
BF16_SUBLANES = 16
VALUE_ROWS = HEAD_DIM + BF16_SUBLANES
QK_SCALE = math.log2(math.e) / math.sqrt(HEAD_DIM)


def _resident(shape):
    zeros = (0,) * len(shape)
    return pl.BlockSpec(shape, lambda *_: zeros, pipeline_mode=pl.Buffered(1))


def _in_proj_kernel(x_ref, g_ref, w32_ref, cos_ref, sin_hi_ref, sin_lo_ref,
                    q_ref, k_ref, v_ref, za_ref, u_ref, zs_ref, w_ref, stage_ref):
    @pl.when(pl.program_id(0) == 0)
    def _():
        for c in range(0, IN_PROJ_WIDTH, ATTN_WIDTH):
            w_ref[:, c:c + ATTN_WIDTH] = w32_ref[:, c:c + ATTN_WIDTH].astype(BF16)

    x = x_ref[...]
    ms = jnp.mean(x * x, axis=-1, keepdims=True)
    h = (x * lax.rsqrt(ms + NORM_EPS) * g_ref[...]).astype(BF16)

    def section(idx):
        return jnp.dot(h, w_ref[:, idx * ATTN_WIDTH:(idx + 1) * ATTN_WIDTH],
                       preferred_element_type=F32)

    cos = cos_ref[...]
    sin_hi = sin_hi_ref[...]
    sin_lo = sin_lo_ref[...]

    def rotary(t):
        return (t * cos + pltpu.roll(t, HEAD_DIM // 2, axis=1) * sin_hi
                + pltpu.roll(t, LANES - HEAD_DIM // 2, axis=1) * sin_lo)

    q = section(0)
    k = section(1)
    for c in range(ATTN_WIDTH // LANES):
        sl = slice(c * LANES, (c + 1) * LANES)
        q_ref[:, sl] = (rotary(q[:, sl]) * QK_SCALE).astype(BF16)
        k_ref[:, sl] = rotary(k[:, sl]).astype(BF16)
    v_ref[...] = section(2).astype(BF16)
    za_ref[...] = section(3).astype(BF16)
    u = section(4)
    for t in range(SSM_TILES):
        stage_ref[t] = u[:, t * LANES:(t + 1) * LANES]
    chunk_rows = u.shape[0] // SSM_CHUNK
    for t in range(SSM_TILES):
        for i in range(SSM_CHUNK):
            u_ref[t, :, i * LANES:(i + 1) * LANES] = (
                stage_ref[t, pl.ds(i, chunk_rows, stride=SSM_CHUNK), :])
    zs_ref[...] = section(5).astype(BF16)


def _in_proj(x2, gain, w_in, cos, sin_hi, sin_lo, seq_len, tm):
    rows = x2.shape[0]
    pos_blocks = seq_len // tm
    row_spec = lambda width: pl.BlockSpec((tm, width), lambda i: (i, 0))
    tab_spec = pl.BlockSpec((tm, LANES), lambda i: (i % pos_blocks, 0))
    out_bf16 = jax.ShapeDtypeStruct((rows, ATTN_WIDTH), BF16)
    out_chunks = jax.ShapeDtypeStruct((SSM_TILES, rows // SSM_CHUNK, TILE_COLS), F32)
    chunk_spec = pl.BlockSpec((SSM_TILES, tm // SSM_CHUNK, TILE_COLS), lambda i: (0, i, 0))
    return pl.pallas_call(
        _in_proj_kernel,
        grid=(rows // tm,),
        in_specs=[row_spec(D_MODEL), _resident((1, D_MODEL)), _resident((D_MODEL, IN_PROJ_WIDTH)),
                  tab_spec, tab_spec, tab_spec],
        out_specs=[row_spec(ATTN_WIDTH)] * 4 + [chunk_spec, row_spec(ATTN_WIDTH)],
        out_shape=[out_bf16, out_bf16, out_bf16, out_bf16, out_chunks, out_bf16],
        scratch_shapes=[pltpu.VMEM((D_MODEL, IN_PROJ_WIDTH), BF16),
                        pltpu.VMEM((SSM_TILES, tm, LANES), F32)],
        compiler_params=pltpu.CompilerParams(dimension_semantics=("arbitrary",),
                                             vmem_limit_bytes=VMEM_LIMIT_BYTES),
        name="in_proj",
    )(x2, gain, w_in, cos, sin_hi, sin_lo)


def _attn_kernel(q0_ref, k0_ref, v0_ref, qn_ref, kn_ref, vn_ref, z_ref, o_ref,
                 kx_ref, vx_ref, qx_ref, sa_ref, sb_ref, m_ref, acc_ref, *, n_blocks):
    unit = pl.program_id(0)
    blk = MOBA_BLOCK
    heads = LANES // HEAD_DIM
    n_pairs = n_blocks // 2
    cur = lax.rem(unit, 2)
    nxt = 1 - cur

    head_rows = lambda h: slice(h * HEAD_DIM, (h + 1) * HEAD_DIM)
    spare_base = lambda h: ((h + 1) % heads) * HEAD_DIM
    iota = lambda shape, d: lax.broadcasted_iota(jnp.int32, shape, d)
    key = iota((blk, blk), 0)
    query = iota((blk, blk), 1)
    col_max = lambda s: jnp.max(s, axis=0, keepdims=True)

    def prepare_items(q_ref, k_ref, v_ref, slot):
        kmean = {}

        def key_block(j):
            lane = iota((blk, LANES), 1)
            sum_rows = (iota((VALUE_ROWS - HEAD_DIM, blk), 0) == 0).astype(F32)
            rows = slice(j * blk, (j + 1) * blk)
            kj = k_ref[0, rows, :]
            kmean[j] = jnp.mean(kj.astype(F32), axis=0, keepdims=True)
            vt = v_ref[0, rows, :].astype(F32).T
            for h in range(heads):
                in_head = (lane >= h * HEAD_DIM) & (lane < (h + 1) * HEAD_DIM)
                tag = jnp.where(lane == spare_base(h) + j, 1.0, 0.0).astype(BF16)
                kx_ref[slot, h, j] = jnp.where(in_head, kj, tag)
                vx_ref[slot, j, h] = jnp.concatenate([vt[head_rows(h)], sum_rows],
                                                     axis=0).astype(BF16)

        def query_block(qi):
            if "all" not in kmean:
                kmean["all"] = jnp.concatenate([kmean[j] for j in range(n_blocks)],
                                               axis=0).astype(BF16)
            blk_row = iota((n_blocks, blk), 0)
            zero_rows = lambda n: jnp.zeros((n, blk), F32)
            in_head_order = lambda h, own, other: [own, other] if h == 0 else [other, own]
            qt = q_ref[0, qi * blk:(qi + 1) * blk, :].astype(F32).T
            past = blk_row < qi
            for h in range(heads):
                q_rows = qt[head_rows(h)]
                q_only = jnp.concatenate(in_head_order(h, q_rows, zero_rows(HEAD_DIM)), axis=0)
                gate = jnp.dot(kmean["all"], q_only.astype(BF16), preferred_element_type=F32)
                gate = jnp.where(past, gate, NEG_INF)
                beaten_by = jnp.zeros((n_blocks, blk), jnp.int32)
                for j in range(n_blocks):
                    gj = gate[j:j + 1, :]
                    wins = (gj > gate) | ((gj == gate) & (j < blk_row))
                    beaten_by = beaten_by + wins.astype(jnp.int32)
                keep = (past & (beaten_by < MOBA_TOP_K)) | (blk_row == qi)
                bias = jnp.where(keep, 0.0, NEG_INF)
                spare = jnp.concatenate([bias, zero_rows(HEAD_DIM - n_blocks)], axis=0)
                qx_ref[slot, qi, h] = jnp.concatenate(in_head_order(h, q_rows, spare),
                                                      axis=0).astype(BF16)

        return ([functools.partial(key_block, j) for j in range(n_blocks)]
                + [functools.partial(query_block, qi) for qi in range(n_blocks)])

    def visible_blocks(pair):
        tiles = (pair, n_blocks - 1 - pair)
        return ([(t, tiles[t], True) for t in range(2)]
                + [(t, j, False) for t in range(2) for j in range(tiles[t])])

    traced = lambda t: t + jnp.minimum(unit, 0)

    def stage_items(pair, s_ref, m, slot):
        def block(t, j, own, h):
            qi = (pair, n_blocks - 1 - pair)[t]
            s = jnp.dot(kx_ref[slot, h, j], qx_ref[slot, qi, h], preferred_element_type=F32)
            if own:
                s = jnp.where(key <= query, s, NEG_INF)
            s_ref[traced(t), h, j] = s
            m[t, h] = col_max(s) if own else jnp.maximum(m[t, h], col_max(s))
        return [functools.partial(block, *b, h) for b in visible_blocks(pair)
                for h in range(heads)]

    def finish_items(pair, s_ref, m, acc_ref, slot):
        def block(t, j, own, h):
            p = jnp.exp2(s_ref[traced(t), h, j] - m[t, h]).astype(BF16)
            pv = jnp.dot(vx_ref[slot, j, h], p, preferred_element_type=F32)
            if own:
                acc_ref[traced(t), h] = pv
            else:
                acc_ref[traced(t), h] += pv

        def write_rows():
            for t, qi in enumerate((pair, n_blocks - 1 - pair)):
                rows = slice(qi * blk, (qi + 1) * blk)
                acc = [acc_ref[t, h] for h in range(heads)]
                o_t = jnp.concatenate([a[:HEAD_DIM] / a[HEAD_DIM:HEAD_DIM + 1] for a in acc],
                                      axis=0)
                z = z_ref[0, rows, :].astype(F32)
                silu = 0.5 * z * (1.0 + jnp.tanh(0.5 * z))
                o_ref[0, rows, :] = (o_t.T * silu).astype(BF16)

        return ([functools.partial(block, *b, h) for b in visible_blocks(pair)
                 for h in range(heads)] + [write_rows])

    def alternate(*item_lists):
        for group in zip(*item_lists):
            for item in group:
                item()
        for items in item_lists:
            for item in items[min(map(len, item_lists)):]:
                item()

    m_keys = [(t, h) for t in range(2) for h in range(heads)]

    @pl.when(unit == 0)
    def _():
        m0 = {}
        alternate(prepare_items(q0_ref, k0_ref, v0_ref, 0))
        alternate(stage_items(0, sa_ref, m0, 0))
        for t, h in m_keys:
            m_ref[t, h] = m0[t, h]

    m = {key_: m_ref[key_] for key_ in m_keys}
    for pair in range(n_pairs):
        s_ref, s_next = (sa_ref, sb_ref) if pair % 2 == 0 else (sb_ref, sa_ref)
        m_next = {}
        lists = [finish_items(pair, s_ref, m, acc_ref.at[pair % 2], cur)]
        if pair + 1 < n_pairs:
            lists.append(stage_items(pair + 1, s_next, m_next, cur))
        else:
            lists.append(stage_items(0, s_next, m_next, nxt))
        if pair == n_pairs - 2:
            lists.append(prepare_items(qn_ref, kn_ref, vn_ref, nxt))
        alternate(lists[1], lists[0], *lists[2:])
        m = m_next
    for t, h in m_keys:
        m_ref[t, h] = m[t, h]


def _moba_attention(q, k, v, z_attn):
    bsz, seq_len, _ = q.shape
    n_blocks = seq_len // MOBA_BLOCK
    assert n_blocks % 4 == 0 and n_blocks <= HEAD_DIM
    head_pairs = ATTN_WIDTH // LANES
    heads = LANES // HEAD_DIM
    assert heads == 2, "the block bias rides in the other head's half of the 128 lanes"
    n_units = bsz * head_pairs
    block = (1, seq_len, LANES)
    unit_index = lambda u: (u // head_pairs, 0, u % head_pairs)
    first_spec = pl.BlockSpec(block, lambda u: (0, 0, 0))
    next_spec = pl.BlockSpec(block, lambda u: unit_index(jnp.minimum(u + 1, n_units - 1)))
    unit_spec = pl.BlockSpec(block, unit_index)
    scores = pltpu.VMEM((2, heads, n_blocks, MOBA_BLOCK, MOBA_BLOCK), F32)
    return pl.pallas_call(
        functools.partial(_attn_kernel, n_blocks=n_blocks),
        grid=(n_units,),
        in_specs=[first_spec] * 3 + [next_spec] * 3 + [unit_spec],
        out_specs=unit_spec,
        out_shape=jax.ShapeDtypeStruct((bsz, seq_len, ATTN_WIDTH), BF16),
        scratch_shapes=[pltpu.VMEM((2, heads, n_blocks, MOBA_BLOCK, LANES), BF16),
                        pltpu.VMEM((2, n_blocks, heads, VALUE_ROWS, MOBA_BLOCK), BF16),
                        pltpu.VMEM((2, n_blocks, heads, LANES, MOBA_BLOCK), BF16),
                        scores, scores,
                        pltpu.VMEM((2, heads, 1, MOBA_BLOCK), F32),
                        pltpu.VMEM((2, 2, heads, VALUE_ROWS, MOBA_BLOCK), F32)],
        compiler_params=pltpu.CompilerParams(dimension_semantics=("arbitrary",),
                                             vmem_limit_bytes=VMEM_LIMIT_BYTES),
        name="moba_attn",
    )(q, k, v, q, k, v, z_attn)


def _ssm_tables(lam_re, lam_im, b_re, b_im, c_re, c_im, log_dt):
    T, G, P, H = SSM_CHUNK, SSM_GROUPS, SSM_STATE, SSM_GROUP_DIM
    dt = jnp.repeat(jnp.exp(log_dt.astype(F32)), P)
    lam_r, lam_i = lam_re.astype(F32).reshape(G * P), lam_im.astype(F32).reshape(G * P)
    mag = jnp.exp(lam_r * dt)
    bar_r, bar_i = mag * jnp.cos(lam_i * dt), mag * jnp.sin(lam_i * dt)
    den = lam_r * lam_r + lam_i * lam_i
    f_r = ((bar_r - 1.0) * lam_r + bar_i * lam_i) / den
    f_i = (bar_i * lam_r - (bar_r - 1.0) * lam_i) / den
    b_r, b_i = (b.astype(F32).transpose(2, 0, 1).reshape(H, G * P) for b in (b_re, b_im))
    bb_r, bb_i = f_r * b_r - f_i * b_i, f_r * b_i + f_i * b_r

    def powers(base_r, base_i, count):
        out_r, out_i = [base_r], [base_i]
        for _ in range(count - 1):
            r, i = out_r[-1], out_i[-1]
            out_r.append(r * base_r - i * base_i)
            out_i.append(r * base_i + i * base_r)
        return out_r, out_i

    pw_r, pw_i = powers(bar_r, bar_i, T)
    pw_r = jnp.stack([jnp.ones_like(bar_r)] + pw_r)
    pw_i = jnp.stack([jnp.zeros_like(bar_i)] + pw_i)

    lb_r = pw_r[:T, None, :] * bb_r - pw_i[:T, None, :] * bb_i
    lb_i = pw_r[:T, None, :] * bb_i + pw_i[:T, None, :] * bb_r
    lb_t = jnp.stack([lb_r, lb_i], axis=1)
    c_r, c_i = (c.astype(F32).transpose(2, 0, 1).reshape(P, G * H) for c in (c_re, c_im))
    per_out = lambda pw: jnp.repeat(pw.reshape(T + 1, G, P).transpose(0, 2, 1), H, axis=2)
    pwo_r, pwo_i = per_out(pw_r), per_out(pw_i)
    cl_t = jnp.stack([c_r * pwo_r - c_i * pwo_i, -(c_r * pwo_i + c_i * pwo_r)], axis=1)

    a_r, a_i = powers(pw_r[T], pw_i[T], F32_SUBLANES)
    tiled = lambda a: jnp.stack(a).reshape(len(a), SSM_TILES, STATE_COLS)
    a_lin = jnp.stack([tiled(a_r), tiled(a_i)]).transpose(2, 0, 1, 3)
    shifts = [1 << k for k in range(F32_SUBLANES.bit_length() - 1)]
    a_log = jnp.stack([tiled([a_r[s - 1] for s in shifts]),
                       tiled([a_i[s - 1] for s in shifts])]).transpose(2, 1, 0, 3)
    row = jnp.arange(F32_SUBLANES)[None, None, None, :, None]
    keep = row >= jnp.asarray(shifts)[None, :, None, None, None]
    a_log = jnp.where(keep, a_log[:, :, :, None, :], 0.0)
    return lb_t, cl_t, a_log, a_lin


def _ssm_kernel(u_ref, lb_ref, cl_ref, alog_ref, alin_ref, d_ref, y_ref, m_ref, ws_ref, wo_ref,
                work_ref, prev_ref, *, seq_len, n_seq):
    T, H, P = SSM_CHUNK, SSM_GROUP_DIM, SSM_STATE
    n_chunks = seq_len // T
    iota = lambda shape, d: lax.broadcasted_iota(jnp.int32, shape, d)
    h_bits, p_bits = H.bit_length() - 1, P.bit_length() - 1

    @pl.when(pl.program_id(1) == 0)
    def _():
        spread_h = ((iota((LANES, H), 0) & (H - 1)) == iota((LANES, H), 1)).astype(BF16)
        spread_p = ((iota((STATE_COLS, P), 0) & (P - 1)) == iota((STATE_COLS, P), 1)).astype(BF16)
        same_hp = (iota((LANES, STATE_COLS), 0) >> h_bits) == (iota((LANES, STATE_COLS), 1) >> p_bits)
        same_ph = (iota((STATE_COLS, LANES), 0) >> p_bits) == (iota((STATE_COLS, LANES), 1) >> h_bits)

        def expand(spread, coeff, same):
            full = jnp.dot(spread, coeff.astype(BF16), preferred_element_type=F32)
            return jnp.where(same, full, 0.0).astype(BF16)

        rows = lambda j: slice(j * LANES, (j + 1) * LANES)
        state_out = lambda d: jnp.concatenate(
            [expand(spread_p, cl_ref[d, ri], same_ph) for ri in range(2)], axis=0)
        for j in range(T):
            ws_ref[rows(j), :] = jnp.concatenate(
                [expand(spread_h, lb_ref[T - 1 - j, ri], same_hp) for ri in range(2)], axis=1)
        for i in range(T):
            wo_ref[:, rows(i)] = state_out(i + 1)
        c_out = state_out(0)
        for d in range(T):
            block = jnp.dot(ws_ref[rows(T - 1 - d), :], c_out,
                            preferred_element_type=F32).astype(BF16)
            for j in range(T - d):
                m_ref[rows(j), rows(j + d)] = block
        for j in range(T):
            for i in range(j):
                m_ref[rows(j), rows(i)] = jnp.zeros((LANES, LANES), BF16)

    x = u_ref[...]
    x_lo = x.astype(BF16)
    groups = [slice(c, c + MXU_COLS) for c in range(0, STATE_COLS, MXU_COLS)]
    im_of = lambda cols: slice(STATE_COLS + cols.start, STATE_COLS + cols.stop)
    traced = lambda k: k + jnp.minimum(pl.program_id(1), 0)
    n_groups = len(groups)
    y_region = 2 * n_groups

    def increment(g, part):
        cols = groups[g] if part == 0 else im_of(groups[g])
        work_ref[traced(2 * g + part)] = jnp.dot(x_lo, ws_ref[:, cols], preferred_element_type=F32)

    def intra_chunk(n):
        hi = (n + 1) * MXU_COLS
        work_ref[traced(y_region + n)] = jnp.dot(x_lo[:, :hi], m_ref[:hi, hi - MXU_COLS:hi],
                                                 preferred_element_type=F32)

    sub = F32_SUBLANES
    first_row = iota((sub, MXU_COLS), 0) == 0
    cmul = lambda a_re, a_im, b_re, b_im: (a_re * b_re - a_im * b_im, a_re * b_im + a_im * b_re)
    tiles_per_seq = n_chunks // sub
    pack = BF16_SUBLANES // sub

    def scan_items(g):
        cols = groups[g]
        log_mul = [(alog_ref[0, k, 0, :, cols], alog_ref[0, k, 1, :, cols])
                   for k in range(sub.bit_length() - 1)]
        lin_mul = (alin_ref[0, 0, :, cols], alin_ref[0, 1, :, cols])

        def sequence(b):
            carry = (jnp.zeros((1, MXU_COLS), F32),) * 2
            pending = []
            for t in range(tiles_per_seq):
                rows = pl.ds((b * tiles_per_seq + t) * sub, sub)
                t_re = work_ref[traced(2 * g), rows, :]
                t_im = work_ref[traced(2 * g + 1), rows, :]
                for k, mul in enumerate(log_mul):
                    d_re, d_im = cmul(*mul, pltpu.roll(t_re, 1 << k, axis=0),
                                      pltpu.roll(t_im, 1 << k, axis=0))
                    t_re, t_im = t_re + d_re, t_im + d_im
                if t > 0:
                    d_re, d_im = cmul(*lin_mul, *carry)
                    t_re, t_im = t_re + d_re, t_im + d_im
                pending.append((jnp.where(first_row, carry[0], pltpu.roll(t_re, 1, axis=0)),
                                jnp.where(first_row, carry[1], pltpu.roll(t_im, 1, axis=0))))
                carry = (t_re[sub - 1:sub], t_im[sub - 1:sub])
                if len(pending) == pack:
                    out = pl.ds((b * tiles_per_seq + t + 1 - pack) * sub, pack * sub)
                    for part, at in enumerate((cols, im_of(cols))):
                        prev_ref[out, at] = jnp.concatenate(
                            [p[part] for p in pending], axis=0).astype(BF16)
                    pending = []
        return [functools.partial(sequence, b) for b in range(n_seq)]

    def alternate(*item_lists):
        for group in zip(*item_lists):
            for item in group:
                item()
        for items in item_lists:
            for item in items[min(map(len, item_lists)):]:
                item()

    increment(0, 0)
    increment(0, 1)
    matmuls = ([functools.partial(increment, g, part) for g in range(1, n_groups)
                for part in range(2)]
               + [functools.partial(intra_chunk, n) for n in range(T * LANES // MXU_COLS)])
    per_scan = -(-len(matmuls) // n_groups)
    for g in range(n_groups):
        alternate(scan_items(g), matmuls[g * per_scan:(g + 1) * per_scan])
    per_tile = MXU_COLS // LANES
    for n in range(T * LANES // MXU_COLS):
        cols = slice(n * MXU_COLS, (n + 1) * MXU_COLS)
        y_ref[:, cols] = (
            work_ref[y_region + n] + x[:, cols] * jnp.concatenate([d_ref[...]] * per_tile, axis=1)
            + jnp.dot(prev_ref[...], wo_ref[:, cols], preferred_element_type=F32))


def _s5_ssm(u, tables, d_skip, seq_len):
    lb_t, cl_t, a_log, a_lin = tables
    T, H, P = SSM_CHUNK, SSM_GROUP_DIM, SSM_STATE
    assert seq_len % (T * F32_SUBLANES) == 0
    bsz = u.shape[1] * T // seq_len
    n_seq = max(n for n in range(1, SSM_SEQS_PER_STEP + 1) if bsz % n == 0)
    square = pltpu.VMEM((T * LANES, T * LANES), BF16)
    n_groups = STATE_COLS // MXU_COLS
    chunk_rows = n_seq * seq_len // T
    io_spec = pl.BlockSpec((None, chunk_rows, T * LANES), lambda q, b: (q, b, 0))
    return pl.pallas_call(
        functools.partial(_ssm_kernel, seq_len=seq_len, n_seq=n_seq),
        grid=(SSM_TILES, bsz // n_seq),
        in_specs=[io_spec,
                  pl.BlockSpec((T, 2, H, STATE_COLS), lambda q, b: (0, 0, 0, q)),
                  pl.BlockSpec((T + 1, 2, P, LANES), lambda q, b: (0, 0, 0, q)),
                  pl.BlockSpec((1,) + a_log.shape[1:], lambda q, b: (q, 0, 0, 0, 0)),
                  pl.BlockSpec((1,) + a_lin.shape[1:], lambda q, b: (q, 0, 0, 0)),
                  pl.BlockSpec((1, LANES), lambda q, b: (0, q))],
        out_specs=io_spec,
        out_shape=jax.ShapeDtypeStruct(u.shape, F32),
        scratch_shapes=[square, square, square,
                        pltpu.VMEM((2 * n_groups + T * LANES // MXU_COLS, chunk_rows, MXU_COLS), F32),
                        pltpu.VMEM((chunk_rows, 2 * STATE_COLS), BF16)],
        compiler_params=pltpu.CompilerParams(dimension_semantics=("arbitrary", "arbitrary"),
                                             vmem_limit_bytes=VMEM_LIMIT_BYTES),
        name="s5_ssm",
    )(u, lb_t, cl_t, a_log, a_lin, d_skip)


def _out_proj_kernel(x_ref, ma_ref, y_ref, zs_ref, wg32_ref, bg_ref, w32_ref, g_ref, o_ref,
                     wg_ref, w_ref, stage_ref, *, final_norm):
    @pl.when(pl.program_id(0) == 0)
    def _():
        wg_ref[...] = wg32_ref[...].astype(BF16)
        w_ref[...] = w32_ref[...].astype(BF16)

    chunk_rows = x_ref.shape[0] // SSM_CHUNK
    for t in range(SSM_TILES):
        for i in range(SSM_CHUNK):
            stage_ref[t, pl.ds(i, chunk_rows, stride=SSM_CHUNK), :] = (
                y_ref[t, :, i * LANES:(i + 1) * LANES])

    half_plus = lambda t: 1.0 + jnp.tanh(0.5 * t)
    for c in range(0, x_ref.shape[0], OUT_PROJ_ROWS):
        rows = slice(c, c + OUT_PROJ_ROWS)
        y = jnp.concatenate([stage_ref[t, rows, :] for t in range(SSM_TILES)], axis=1)
        y = y * (1.0 + jnp.tanh(math.sqrt(2.0 / math.pi) * (y + 0.044715 * (y * y * y))))
        gate = jnp.dot((0.5 * y).astype(BF16), wg_ref[...], preferred_element_type=F32) + bg_ref[...]
        z = zs_ref[rows, :].astype(F32)
        mixed_ssm = (0.125 * y) * half_plus(gate) * (z * half_plus(z))
        mixed = jnp.concatenate([ma_ref[rows, :], mixed_ssm.astype(BF16)], axis=1)
        r = x_ref[rows, :] + jnp.dot(mixed, w_ref[...], preferred_element_type=F32)
        if final_norm:
            ms = jnp.mean(r * r, axis=-1, keepdims=True)
            r = r * lax.rsqrt(ms + NORM_EPS) * g_ref[...]
        o_ref[rows, :] = r


def _out_proj(x2, mixed_attn, y_ssm, z_ssm, w_glu, b_glu, w_out, gain, tm, final_norm):
    rows = x2.shape[0]
    row_spec = lambda width: pl.BlockSpec((tm, width), lambda i: (i, 0))
    return pl.pallas_call(
        functools.partial(_out_proj_kernel, final_norm=final_norm),
        grid=(rows // tm,),
        in_specs=[row_spec(D_MODEL), row_spec(ATTN_WIDTH),
                  pl.BlockSpec((SSM_TILES, tm // SSM_CHUNK, TILE_COLS), lambda i: (0, i, 0)),
                  row_spec(SSM_WIDTH),
                  _resident((SSM_WIDTH, SSM_WIDTH)), _resident((1, SSM_WIDTH)),
                  _resident((ATTN_WIDTH + SSM_WIDTH, D_MODEL)), _resident((1, D_MODEL))],
        out_specs=row_spec(D_MODEL),
        out_shape=jax.ShapeDtypeStruct((rows, D_MODEL), F32),
        scratch_shapes=[pltpu.VMEM((SSM_WIDTH, SSM_WIDTH), BF16),
                        pltpu.VMEM((ATTN_WIDTH + SSM_WIDTH, D_MODEL), BF16),
                        pltpu.VMEM((SSM_TILES, tm, LANES), F32)],
        compiler_params=pltpu.CompilerParams(dimension_semantics=("arbitrary",),
                                             vmem_limit_bytes=VMEM_LIMIT_BYTES),
        name="out_proj",
    )(x2, mixed_attn, y_ssm, z_ssm, w_glu, b_glu, w_out, gain)


def _rotary_tables(seq_len):
    half = HEAD_DIM // 2
    lane = jnp.arange(LANES)
    inv_freq = 1.0 / (ROPE_THETA ** ((lane % half).astype(F32) / half))
    ang = jnp.arange(seq_len, dtype=F32)[:, None] * inv_freq[None, :]
    cos, sin = jnp.cos(ang), jnp.sin(ang)
    upper = (lane % HEAD_DIM >= half)[None, :]
    sin_hi = jnp.where(upper, sin, 0.0)
    sin_lo = jnp.where(upper, 0.0, -sin)
    return cos, sin_hi, sin_lo


def kernel(x, norm_gain, w_in, w_out, lam_re, lam_im, b_re, b_im, c_re, c_im,
           d_skip, log_dt, w_glu, b_glu, final_gain):
    bsz, seq_len, _ = x.shape
    depth = norm_gain.shape[0]
    assert seq_len % MOBA_BLOCK == 0 and seq_len // MOBA_BLOCK > MOBA_TOP_K
    assert seq_len % SSM_CHUNK == 0
    tm = min(PROJ_TILE_ROWS, seq_len)
    cos, sin_hi, sin_lo = _rotary_tables(seq_len)

    x2 = x.reshape(bsz * seq_len, D_MODEL)
    for layer in range(depth):
        tables = _ssm_tables(lam_re[layer], lam_im[layer], b_re[layer], b_im[layer],
                             c_re[layer], c_im[layer], log_dt[layer])
        q, k, v, z_attn, u, z_ssm = _in_proj(
            x2, norm_gain[layer].reshape(1, D_MODEL), w_in[layer], cos, sin_hi, sin_lo, seq_len, tm)
        to_seq = lambda t: t.reshape(bsz, seq_len, ATTN_WIDTH)
        mixed_attn = _moba_attention(to_seq(q), to_seq(k), to_seq(v), to_seq(z_attn))
        y_ssm = _s5_ssm(u, tables, d_skip[layer].reshape(1, SSM_WIDTH), seq_len)
        x2 = _out_proj(x2, mixed_attn.reshape(bsz * seq_len, ATTN_WIDTH), y_ssm, z_ssm,
                       w_glu[layer], b_glu[layer].reshape(1, SSM_WIDTH),
                       w_out[layer], final_gain.reshape(1, D_MODEL), tm,
                       final_norm=layer == depth - 1)
    return x2.reshape(bsz, seq_len, D_MODEL)
```

```python
import functools
import math

import jax
import jax.numpy as jnp
from jax import lax
from jax.experimental import pallas as pl
from jax.experimental.pallas import tpu as pltpu

F32 = jnp.float32
BF16 = jnp.bfloat16

D_MODEL = 1024
HEAD_DIM = 64
ATTN_HEADS = 8
ATTN_WIDTH = ATTN_HEADS * HEAD_DIM
MOBA_BLOCK = 256
MOBA_TOP_K = 3
ROPE_THETA = 10000.0
ROTARY_SPLIT = 64
SSM_GROUP_DIM = 16
SSM_GROUPS = 32
SSM_WIDTH = SSM_GROUPS * SSM_GROUP_DIM
SSM_STATE = 64
IN_PROJ_WIDTH = 4 * ATTN_WIDTH + 2 * SSM_WIDTH
NORM_EPS = 1e-6
NEG_INF = -1e30

LANES = 128
SSM_CHUNK = 8
GROUPS_PER_TILE = LANES // SSM_GROUP_DIM
SSM_TILES = SSM_WIDTH // LANES
TILE_COLS = SSM_CHUNK * LANES
STATE_COLS = GROUPS_PER_TILE * SSM_STATE
VMEM_LIMIT_BYTES = 56 * 1024 * 1024
MXU_COLS = 256
OUT_PROJ_ROWS = 256
PROJ_TILE_ROWS = 1024
SSM_SEQS_PER_STEP = 4
F32_SUBLANES = 8
BF16_SUBLANES = 16
VALUE_ROWS = HEAD_DIM + BF16_SUBLANES
QK_SCALE = math.log2(math.e) / math.sqrt(HEAD_DIM)


def _resident(shape):
    zeros = (0,) * len(shape)
    return pl.BlockSpec(shape, lambda *_: zeros, pipeline_mode=pl.Buffered(1))


def _in_proj_kernel(x_ref, g_ref, w32_ref, cos_ref, sin_hi_ref, sin_lo_ref,
                    q_ref, k_ref, v_ref, za_ref, u_ref, zs_ref, w_ref):
    @pl.when(pl.program_id(0) == 0)
    def _():
        for c in range(0, IN_PROJ_WIDTH, ATTN_WIDTH):
            w_ref[:, c:c + ATTN_WIDTH] = w32_ref[:, c:c + ATTN_WIDTH].astype(BF16)

    x = x_ref[...]
    ms = jnp.mean(x * x, axis=-1, keepdims=True)
    h = (x * lax.rsqrt(ms + NORM_EPS) * g_ref[...]).astype(BF16)

    def section(idx):
        return jnp.dot(h, w_ref[:, idx * ATTN_WIDTH:(idx + 1) * ATTN_WIDTH],
                       preferred_element_type=F32)

    cos = cos_ref[...]
    sin_hi = sin_hi_ref[...]
    sin_lo = sin_lo_ref[...]

    def rotary(t):
        return (t * cos + pltpu.roll(t, HEAD_DIM // 2, axis=1) * sin_hi
                + pltpu.roll(t, LANES - HEAD_DIM // 2, axis=1) * sin_lo)

    q = section(0)
    k = section(1)
    for c in range(ATTN_WIDTH // LANES):
        sl = slice(c * LANES, (c + 1) * LANES)
        q_ref[:, sl] = (rotary(q[:, sl]) * QK_SCALE).astype(BF16)
        k_ref[:, sl] = rotary(k[:, sl]).astype(BF16)
    v_ref[...] = section(2).astype(BF16)
    za_ref[...] = section(3).astype(BF16)
    u_ref[...] = section(4)
    zs_ref[...] = section(5).astype(BF16)


def _in_proj(x2, gain, w_in, cos, sin_hi, sin_lo, seq_len, tm):
    rows = x2.shape[0]
    pos_blocks = seq_len // tm
    row_spec = lambda width: pl.BlockSpec((tm, width), lambda i: (i, 0))
    tab_spec = pl.BlockSpec((tm, LANES), lambda i: (i % pos_blocks, 0))
    out_bf16 = jax.ShapeDtypeStruct((rows, ATTN_WIDTH), BF16)
    out_f32 = jax.ShapeDtypeStruct((rows, ATTN_WIDTH), F32)
    return pl.pallas_call(
        _in_proj_kernel,
        grid=(rows // tm,),
        in_specs=[row_spec(D_MODEL), _resident((1, D_MODEL)), _resident((D_MODEL, IN_PROJ_WIDTH)),
                  tab_spec, tab_spec, tab_spec],
        out_specs=[row_spec(ATTN_WIDTH)] * 6,
        out_shape=[out_bf16, out_bf16, out_bf16, out_bf16, out_f32, out_bf16],
        scratch_shapes=[pltpu.VMEM((D_MODEL, IN_PROJ_WIDTH), BF16)],
        compiler_params=pltpu.CompilerParams(dimension_semantics=("arbitrary",),
                                             vmem_limit_bytes=VMEM_LIMIT_BYTES),
        name="in_proj",
    )(x2, gain, w_in, cos, sin_hi, sin_lo)


def _attn_kernel(q0_ref, k0_ref, v0_ref, qn_ref, kn_ref, vn_ref, z_ref, o_ref,
                 kx_ref, vx_ref, qx_ref, sa_ref, sb_ref, m_ref, acc_ref, *, n_blocks):
    unit = pl.program_id(0)
    blk = MOBA_BLOCK
    heads = LANES // HEAD_DIM
    n_pairs = n_blocks // 2
    cur = lax.rem(unit, 2)
    nxt = 1 - cur

    head_rows = lambda h: slice(h * HEAD_DIM, (h + 1) * HEAD_DIM)
    spare_base = lambda h: ((h + 1) % heads) * HEAD_DIM
    iota = lambda shape, d: lax.broadcasted_iota(jnp.int32, shape, d)
    key = iota((blk, blk), 0)
    query = iota((blk, blk), 1)
    col_max = lambda s: jnp.max(s, axis=0, keepdims=True)

    def prepare_items(q_ref, k_ref, v_ref, slot):
        kmean = {}

        def key_block(j):
            lane = iota((blk, LANES), 1)
            sum_rows = (iota((VALUE_ROWS - HEAD_DIM, blk), 0) == 0).astype(F32)
            rows = slice(j * blk, (j + 1) * blk)
            kj = k_ref[0, rows, :]
            kmean[j] = jnp.mean(kj.astype(F32), axis=0, keepdims=True)
            vt = v_ref[0, rows, :].astype(F32).T
            for h in range(heads):
                in_head = (lane >= h * HEAD_DIM) & (lane < (h + 1) * HEAD_DIM)
                tag = jnp.where(lane == spare_base(h) + j, 1.0, 0.0).astype(BF16)
                kx_ref[slot, h, j] = jnp.where(in_head, kj, tag)
                vx_ref[slot, j, h] = jnp.concatenate([vt[head_rows(h)], sum_rows],
                                                     axis=0).astype(BF16)

        def query_block(qi):
            if "all" not in kmean:
                kmean["all"] = jnp.concatenate([kmean[j] for j in range(n_blocks)],
                                               axis=0).astype(BF16)
            blk_row = iota((n_blocks, blk), 0)
            zero_rows = lambda n: jnp.zeros((n, blk), F32)
            in_head_order = lambda h, own, other: [own, other] if h == 0 else [other, own]
            qt = q_ref[0, qi * blk:(qi + 1) * blk, :].astype(F32).T
            past = blk_row < qi
            for h in range(heads):
                q_rows = qt[head_rows(h)]
                q_only = jnp.concatenate(in_head_order(h, q_rows, zero_rows(HEAD_DIM)), axis=0)
                gate = jnp.dot(kmean["all"], q_only.astype(BF16), preferred_element_type=F32)
                gate = jnp.where(past, gate, NEG_INF)
                beaten_by = jnp.zeros((n_blocks, blk), jnp.int32)
                for j in range(n_blocks):
                    gj = gate[j:j + 1, :]
                    wins = (gj > gate) | ((gj == gate) & (j < blk_row))
                    beaten_by = beaten_by + wins.astype(jnp.int32)
                keep = (past & (beaten_by < MOBA_TOP_K)) | (blk_row == qi)
                bias = jnp.where(keep, 0.0, NEG_INF)
                spare = jnp.concatenate([bias, zero_rows(HEAD_DIM - n_blocks)], axis=0)
                qx_ref[slot, qi, h] = jnp.concatenate(in_head_order(h, q_rows, spare),
                                                      axis=0).astype(BF16)

        return ([functools.partial(key_block, j) for j in range(n_blocks)]
                + [functools.partial(query_block, qi) for qi in range(n_blocks)])

    def visible_blocks(pair):
        tiles = (pair, n_blocks - 1 - pair)
        return ([(t, tiles[t], True) for t in range(2)]
                + [(t, j, False) for t in range(2) for j in range(tiles[t])])

    traced = lambda t: t + jnp.minimum(unit, 0)

    def stage_items(pair, s_ref, m, slot):
        def block(t, j, own, h):
            qi = (pair, n_blocks - 1 - pair)[t]
            s = jnp.dot(kx_ref[slot, h, j], qx_ref[slot, qi, h], preferred_element_type=F32)
            if own:
                s = jnp.where(key <= query, s, NEG_INF)
            s_ref[traced(t), h, j] = s
            m[t, h] = col_max(s) if own else jnp.maximum(m[t, h], col_max(s))
        return [functools.partial(block, *b, h) for b in visible_blocks(pair)
                for h in range(heads)]

    def finish_items(pair, s_ref, m, acc_ref, slot):
        def block(t, j, own, h):
            p = jnp.exp2(s_ref[traced(t), h, j] - m[t, h]).astype(BF16)
            pv = jnp.dot(vx_ref[slot, j, h], p, preferred_element_type=F32)
            if own:
                acc_ref[traced(t), h] = pv
            else:
                acc_ref[traced(t), h] += pv

        def write_rows():
            for t, qi in enumerate((pair, n_blocks - 1 - pair)):
                rows = slice(qi * blk, (qi + 1) * blk)
                acc = [acc_ref[t, h] for h in range(heads)]
                o_t = jnp.concatenate([a[:HEAD_DIM] / a[HEAD_DIM:HEAD_DIM + 1] for a in acc],
                                      axis=0)
                z = z_ref[0, rows, :].astype(F32)
                silu = 0.5 * z * (1.0 + jnp.tanh(0.5 * z))
                o_ref[0, rows, :] = (o_t.T * silu).astype(BF16)

        return ([functools.partial(block, *b, h) for b in visible_blocks(pair)
                 for h in range(heads)] + [write_rows])

    def alternate(*item_lists):
        for group in zip(*item_lists):
            for item in group:
                item()
        for items in item_lists:
            for item in items[min(map(len, item_lists)):]:
                item()

    m_keys = [(t, h) for t in range(2) for h in range(heads)]

    @pl.when(unit == 0)
    def _():
        m0 = {}
        alternate(prepare_items(q0_ref, k0_ref, v0_ref, 0))
        alternate(stage_items(0, sa_ref, m0, 0))
        for t, h in m_keys:
            m_ref[t, h] = m0[t, h]

    m = {key_: m_ref[key_] for key_ in m_keys}
    for pair in range(n_pairs):
        s_ref, s_next = (sa_ref, sb_ref) if pair % 2 == 0 else (sb_ref, sa_ref)
        m_next = {}
        lists = [finish_items(pair, s_ref, m, acc_ref.at[pair % 2], cur)]
        if pair + 1 < n_pairs:
            lists.append(stage_items(pair + 1, s_next, m_next, cur))
        else:
            lists.append(stage_items(0, s_next, m_next, nxt))
        if pair == n_pairs - 2:
            lists.append(prepare_items(qn_ref, kn_ref, vn_ref, nxt))
        alternate(lists[1], lists[0], *lists[2:])
        m = m_next
    for t, h in m_keys:
        m_ref[t, h] = m[t, h]


def _moba_attention(q, k, v, z_attn):
    bsz, seq_len, _ = q.shape
    n_blocks = seq_len // MOBA_BLOCK
    assert n_blocks % 4 == 0 and n_blocks <= HEAD_DIM
    head_pairs = ATTN_WIDTH // LANES
    heads = LANES // HEAD_DIM
    assert heads == 2, "the block bias rides in the other head's half of the 128 lanes"
    n_units = bsz * head_pairs
    block = (1, seq_len, LANES)
    unit_index = lambda u: (u // head_pairs, 0, u % head_pairs)
    first_spec = pl.BlockSpec(block, lambda u: (0, 0, 0))
    next_spec = pl.BlockSpec(block, lambda u: unit_index(jnp.minimum(u + 1, n_units - 1)))
    unit_spec = pl.BlockSpec(block, unit_index)
    scores = pltpu.VMEM((2, heads, n_blocks, MOBA_BLOCK, MOBA_BLOCK), F32)
    return pl.pallas_call(
        functools.partial(_attn_kernel, n_blocks=n_blocks),
        grid=(n_units,),
        in_specs=[first_spec] * 3 + [next_spec] * 3 + [unit_spec],
        out_specs=unit_spec,
        out_shape=jax.ShapeDtypeStruct((bsz, seq_len, ATTN_WIDTH), BF16),
        scratch_shapes=[pltpu.VMEM((2, heads, n_blocks, MOBA_BLOCK, LANES), BF16),
                        pltpu.VMEM((2, n_blocks, heads, VALUE_ROWS, MOBA_BLOCK), BF16),
                        pltpu.VMEM((2, n_blocks, heads, LANES, MOBA_BLOCK), BF16),
                        scores, scores,
                        pltpu.VMEM((2, heads, 1, MOBA_BLOCK), F32),
                        pltpu.VMEM((2, 2, heads, VALUE_ROWS, MOBA_BLOCK), F32)],
        compiler_params=pltpu.CompilerParams(dimension_semantics=("arbitrary",),
                                             vmem_limit_bytes=VMEM_LIMIT_BYTES),
        name="moba_attn",
    )(q, k, v, q, k, v, z_attn)


def _ssm_tables(lam_re, lam_im, b_re, b_im, c_re, c_im, log_dt):
    T, G, P, H = SSM_CHUNK, SSM_GROUPS, SSM_STATE, SSM_GROUP_DIM
    dt = jnp.repeat(jnp.exp(log_dt.astype(F32)), P)
    lam_r, lam_i = lam_re.astype(F32).reshape(G * P), lam_im.astype(F32).reshape(G * P)
    mag = jnp.exp(lam_r * dt)
    bar_r, bar_i = mag * jnp.cos(lam_i * dt), mag * jnp.sin(lam_i * dt)
    den = lam_r * lam_r + lam_i * lam_i
    f_r = ((bar_r - 1.0) * lam_r + bar_i * lam_i) / den
    f_i = (bar_i * lam_r - (bar_r - 1.0) * lam_i) / den
    b_r, b_i = (b.astype(F32).transpose(2, 0, 1).reshape(H, G * P) for b in (b_re, b_im))
    bb_r, bb_i = f_r * b_r - f_i * b_i, f_r * b_i + f_i * b_r

    def powers(base_r, base_i, count):
        out_r, out_i = [base_r], [base_i]
        for _ in range(count - 1):
            r, i = out_r[-1], out_i[-1]
            out_r.append(r * base_r - i * base_i)
            out_i.append(r * base_i + i * base_r)
        return out_r, out_i

    pw_r, pw_i = powers(bar_r, bar_i, T)
    pw_r = jnp.stack([jnp.ones_like(bar_r)] + pw_r)
    pw_i = jnp.stack([jnp.zeros_like(bar_i)] + pw_i)

    lb_r = pw_r[:T, None, :] * bb_r - pw_i[:T, None, :] * bb_i
    lb_i = pw_r[:T, None, :] * bb_i + pw_i[:T, None, :] * bb_r
    lb_t = jnp.stack([lb_r, lb_i], axis=1)
    c_r, c_i = (c.astype(F32).transpose(2, 0, 1).reshape(P, G * H) for c in (c_re, c_im))
    per_out = lambda pw: jnp.repeat(pw.reshape(T + 1, G, P).transpose(0, 2, 1), H, axis=2)
    pwo_r, pwo_i = per_out(pw_r), per_out(pw_i)
    cl_t = jnp.stack([c_r * pwo_r - c_i * pwo_i, -(c_r * pwo_i + c_i * pwo_r)], axis=1)

    a_r, a_i = powers(pw_r[T], pw_i[T], F32_SUBLANES)
    tiled = lambda a: jnp.stack(a).reshape(len(a), SSM_TILES, STATE_COLS)
    a_lin = jnp.stack([tiled(a_r), tiled(a_i)]).transpose(2, 0, 1, 3)
    shifts = [1 << k for k in range(F32_SUBLANES.bit_length() - 1)]
    a_log = jnp.stack([tiled([a_r[s - 1] for s in shifts]),
                       tiled([a_i[s - 1] for s in shifts])]).transpose(2, 1, 0, 3)
    row = jnp.arange(F32_SUBLANES)[None, None, None, :, None]
    keep = row >= jnp.asarray(shifts)[None, :, None, None, None]
    a_log = jnp.where(keep, a_log[:, :, :, None, :], 0.0)
    return lb_t, cl_t, a_log, a_lin


def _ssm_kernel(u_ref, lb_ref, cl_ref, alog_ref, alin_ref, d_ref, y_ref, m_ref, ws_ref, wo_ref,
                work_ref, prev_ref, *, seq_len, n_seq):
    T, H, P = SSM_CHUNK, SSM_GROUP_DIM, SSM_STATE
    n_chunks = seq_len // T
    iota = lambda shape, d: lax.broadcasted_iota(jnp.int32, shape, d)
    h_bits, p_bits = H.bit_length() - 1, P.bit_length() - 1

    @pl.when(pl.program_id(1) == 0)
    def _():
        spread_h = ((iota((LANES, H), 0) & (H - 1)) == iota((LANES, H), 1)).astype(BF16)
        spread_p = ((iota((STATE_COLS, P), 0) & (P - 1)) == iota((STATE_COLS, P), 1)).astype(BF16)
        same_hp = (iota((LANES, STATE_COLS), 0) >> h_bits) == (iota((LANES, STATE_COLS), 1) >> p_bits)
        same_ph = (iota((STATE_COLS, LANES), 0) >> p_bits) == (iota((STATE_COLS, LANES), 1) >> h_bits)

        def expand(spread, coeff, same):
            full = jnp.dot(spread, coeff.astype(BF16), preferred_element_type=F32)
            return jnp.where(same, full, 0.0).astype(BF16)

        rows = lambda j: slice(j * LANES, (j + 1) * LANES)
        state_out = lambda d: jnp.concatenate(
            [expand(spread_p, cl_ref[d, ri], same_ph) for ri in range(2)], axis=0)
        for j in range(T):
            ws_ref[rows(j), :] = jnp.concatenate(
                [expand(spread_h, lb_ref[T - 1 - j, ri], same_hp) for ri in range(2)], axis=1)
        for i in range(T):
            wo_ref[:, rows(i)] = state_out(i + 1)
        c_out = state_out(0)
        for d in range(T):
            block = jnp.dot(ws_ref[rows(T - 1 - d), :], c_out,
                            preferred_element_type=F32).astype(BF16)
            for j in range(T - d):
                m_ref[rows(j), rows(j + d)] = block
        for j in range(T):
            for i in range(j):
                m_ref[rows(j), rows(i)] = jnp.zeros((LANES, LANES), BF16)

    pieces = [u_ref[pl.ds(b * seq_len + i, n_chunks, stride=T), :]
              for b in range(n_seq) for i in range(T)]
    x = jnp.concatenate(
        [jnp.concatenate(pieces[b * T:(b + 1) * T], axis=1) for b in range(n_seq)], axis=0)
    x_lo = x.astype(BF16)
    groups = [slice(c, c + MXU_COLS) for c in range(0, STATE_COLS, MXU_COLS)]
    im_of = lambda cols: slice(STATE_COLS + cols.start, STATE_COLS + cols.stop)
    traced = lambda k: k + jnp.minimum(pl.program_id(1), 0)
    n_groups = len(groups)
    y_region = 2 * n_groups

    def increment(g, part):
        cols = groups[g] if part == 0 else im_of(groups[g])
        work_ref[traced(2 * g + part)] = jnp.dot(x_lo, ws_ref[:, cols], preferred_element_type=F32)

    def intra_chunk(n):
        hi = (n + 1) * MXU_COLS
        work_ref[traced(y_region + n)] = jnp.dot(x_lo[:, :hi], m_ref[:hi, hi - MXU_COLS:hi],
                                                 preferred_element_type=F32)

    sub = F32_SUBLANES
    first_row = iota((sub, MXU_COLS), 0) == 0
    cmul = lambda a_re, a_im, b_re, b_im: (a_re * b_re - a_im * b_im, a_re * b_im + a_im * b_re)
    tiles_per_seq = n_chunks // sub
    pack = BF16_SUBLANES // sub

    def scan_items(g):
        cols = groups[g]
        log_mul = [(alog_ref[0, k, 0, :, cols], alog_ref[0, k, 1, :, cols])
                   for k in range(sub.bit_length() - 1)]
        lin_mul = (alin_ref[0, 0, :, cols], alin_ref[0, 1, :, cols])

        def sequence(b):
            carry = (jnp.zeros((1, MXU_COLS), F32),) * 2
            pending = []
            for t in range(tiles_per_seq):
                rows = pl.ds((b * tiles_per_seq + t) * sub, sub)
                t_re = work_ref[traced(2 * g), rows, :]
                t_im = work_ref[traced(2 * g + 1), rows, :]
                for k, mul in enumerate(log_mul):
                    d_re, d_im = cmul(*mul, pltpu.roll(t_re, 1 << k, axis=0),
                                      pltpu.roll(t_im, 1 << k, axis=0))
                    t_re, t_im = t_re + d_re, t_im + d_im
                if t > 0:
                    d_re, d_im = cmul(*lin_mul, *carry)
                    t_re, t_im = t_re + d_re, t_im + d_im
                pending.append((jnp.where(first_row, carry[0], pltpu.roll(t_re, 1, axis=0)),
                                jnp.where(first_row, carry[1], pltpu.roll(t_im, 1, axis=0))))
                carry = (t_re[sub - 1:sub], t_im[sub - 1:sub])
                if len(pending) == pack:
                    out = pl.ds((b * tiles_per_seq + t + 1 - pack) * sub, pack * sub)
                    for part, at in enumerate((cols, im_of(cols))):
                        prev_ref[out, at] = jnp.concatenate(
                            [p[part] for p in pending], axis=0).astype(BF16)
                    pending = []
        return [functools.partial(sequence, b) for b in range(n_seq)]

    def alternate(*item_lists):
        for group in zip(*item_lists):
            for item in group:
                item()
        for items in item_lists:
            for item in items[min(map(len, item_lists)):]:
                item()

    increment(0, 0)
    increment(0, 1)
    matmuls = ([functools.partial(increment, g, part) for g in range(1, n_groups)
                for part in range(2)]
               + [functools.partial(intra_chunk, n) for n in range(T * LANES // MXU_COLS)])
    per_scan = -(-len(matmuls) // n_groups)
    for g in range(n_groups):
        alternate(scan_items(g), matmuls[g * per_scan:(g + 1) * per_scan])
    per_tile = MXU_COLS // LANES
    for n in range(T * LANES // MXU_COLS):
        cols = slice(n * MXU_COLS, (n + 1) * MXU_COLS)
        y = (work_ref[y_region + n] + x[:, cols] * jnp.concatenate([d_ref[...]] * per_tile, axis=1)
             + jnp.dot(prev_ref[...], wo_ref[:, cols], preferred_element_type=F32))
        for b in range(n_seq):
            for i in range(per_tile):
                y_ref[pl.ds(b * seq_len + n * per_tile + i, n_chunks, stride=T), :] = (
                    y[b * n_chunks:(b + 1) * n_chunks, i * LANES:(i + 1) * LANES])


def _s5_ssm(u, tables, d_skip, seq_len):
    lb_t, cl_t, a_log, a_lin = tables
    T, H, P = SSM_CHUNK, SSM_GROUP_DIM, SSM_STATE
    assert seq_len % (T * F32_SUBLANES) == 0
    bsz = u.shape[0] // seq_len
    n_seq = max(n for n in range(1, SSM_SEQS_PER_STEP + 1) if bsz % n == 0)
    io_spec = pl.BlockSpec((n_seq * seq_len, LANES), lambda q, b: (b, q))
    square = pltpu.VMEM((T * LANES, T * LANES), BF16)
    n_groups = STATE_COLS // MXU_COLS
    chunk_rows = n_seq * seq_len // T
    return pl.pallas_call(
        functools.partial(_ssm_kernel, seq_len=seq_len, n_seq=n_seq),
        grid=(SSM_TILES, bsz // n_seq),
        in_specs=[io_spec,
                  pl.BlockSpec((T, 2, H, STATE_COLS), lambda q, b: (0, 0, 0, q)),
                  pl.BlockSpec((T + 1, 2, P, LANES), lambda q, b: (0, 0, 0, q)),
                  pl.BlockSpec((1,) + a_log.shape[1:], lambda q, b: (q, 0, 0, 0, 0)),
                  pl.BlockSpec((1,) + a_lin.shape[1:], lambda q, b: (q, 0, 0, 0)),
                  pl.BlockSpec((1, LANES), lambda q, b: (0, q))],
        out_specs=io_spec,
        out_shape=jax.ShapeDtypeStruct(u.shape, F32),
        scratch_shapes=[square, square, square,
                        pltpu.VMEM((2 * n_groups + T * LANES // MXU_COLS, chunk_rows, MXU_COLS), F32),
                        pltpu.VMEM((chunk_rows, 2 * STATE_COLS), BF16)],
        compiler_params=pltpu.CompilerParams(dimension_semantics=("arbitrary", "arbitrary"),
                                             vmem_limit_bytes=VMEM_LIMIT_BYTES),
        name="s5_ssm",
    )(u, lb_t, cl_t, a_log, a_lin, d_skip)


def _out_proj_kernel(x_ref, ma_ref, y_ref, zs_ref, wg32_ref, bg_ref, w32_ref, g_ref, o_ref,
                     wg_ref, w_ref, *, final_norm):
    @pl.when(pl.program_id(0) == 0)
    def _():
        wg_ref[...] = wg32_ref[...].astype(BF16)
        w_ref[...] = w32_ref[...].astype(BF16)

    half_plus = lambda t: 1.0 + jnp.tanh(0.5 * t)
    for c in range(0, x_ref.shape[0], OUT_PROJ_ROWS):
        rows = slice(c, c + OUT_PROJ_ROWS)
        y = y_ref[rows, :]
        y = y * (1.0 + jnp.tanh(math.sqrt(2.0 / math.pi) * (y + 0.044715 * (y * y * y))))
        gate = jnp.dot((0.5 * y).astype(BF16), wg_ref[...], preferred_element_type=F32) + bg_ref[...]
        z = zs_ref[rows, :].astype(F32)
        mixed_ssm = (0.125 * y) * half_plus(gate) * (z * half_plus(z))
        mixed = jnp.concatenate([ma_ref[rows, :], mixed_ssm.astype(BF16)], axis=1)
        r = x_ref[rows, :] + jnp.dot(mixed, w_ref[...], preferred_element_type=F32)
        if final_norm:
            ms = jnp.mean(r * r, axis=-1, keepdims=True)
            r = r * lax.rsqrt(ms + NORM_EPS) * g_ref[...]
        o_ref[rows, :] = r


def _out_proj(x2, mixed_attn, y_ssm, z_ssm, w_glu, b_glu, w_out, gain, tm, final_norm):
    rows = x2.shape[0]
    row_spec = lambda width: pl.BlockSpec((tm, width), lambda i: (i, 0))
    return pl.pallas_call(
        functools.partial(_out_proj_kernel, final_norm=final_norm),
        grid=(rows // tm,),
        in_specs=[row_spec(D_MODEL), row_spec(ATTN_WIDTH), row_spec(SSM_WIDTH), row_spec(SSM_WIDTH),
                  _resident((SSM_WIDTH, SSM_WIDTH)), _resident((1, SSM_WIDTH)),
                  _resident((ATTN_WIDTH + SSM_WIDTH, D_MODEL)), _resident((1, D_MODEL))],
        out_specs=row_spec(D_MODEL),
        out_shape=jax.ShapeDtypeStruct((rows, D_MODEL), F32),
        scratch_shapes=[pltpu.VMEM((SSM_WIDTH, SSM_WIDTH), BF16),
                        pltpu.VMEM((ATTN_WIDTH + SSM_WIDTH, D_MODEL), BF16)],
        compiler_params=pltpu.CompilerParams(dimension_semantics=("arbitrary",),
                                             vmem_limit_bytes=VMEM_LIMIT_BYTES),
        name="out_proj",
    )(x2, mixed_attn, y_ssm, z_ssm, w_glu, b_glu, w_out, gain)


def _rotary_tables(seq_len):
    assert seq_len % ROTARY_SPLIT == 0
    half = HEAD_DIM // 2
    lane = jnp.arange(LANES)
    inv_freq = 1.0 / (ROPE_THETA ** ((lane % half).astype(F32) / half))
    hi = jnp.arange(0, seq_len, ROTARY_SPLIT, dtype=F32)[:, None, None] * inv_freq
    lo = jnp.arange(ROTARY_SPLIT, dtype=F32)[None, :, None] * inv_freq
    cos = (jnp.cos(hi) * jnp.cos(lo) - jnp.sin(hi) * jnp.sin(lo)).reshape(seq_len, LANES)
    sin = (jnp.sin(hi) * jnp.cos(lo) + jnp.cos(hi) * jnp.sin(lo)).reshape(seq_len, LANES)
    upper = (lane % HEAD_DIM >= half)[None, :]
    sin_hi = jnp.where(upper, sin, 0.0)
    sin_lo = jnp.where(upper, 0.0, -sin)
    return cos, sin_hi, sin_lo


def kernel(x, norm_gain, w_in, w_out, lam_re, lam_im, b_re, b_im, c_re, c_im,
           d_skip, log_dt, w_glu, b_glu, final_gain):
    bsz, seq_len, _ = x.shape
    depth = norm_gain.shape[0]
    assert seq_len % MOBA_BLOCK == 0 and seq_len // MOBA_BLOCK > MOBA_TOP_K
    assert seq_len % SSM_CHUNK == 0
    tm = min(PROJ_TILE_ROWS, seq_len)
    cos, sin_hi, sin_lo = _rotary_tables(seq_len)

    x2 = x.reshape(bsz * seq_len, D_MODEL)
    for layer in range(depth):
        tables = _ssm_tables(lam_re[layer], lam_im[layer], b_re[layer], b_im[layer],
                             c_re[layer], c_im[layer], log_dt[layer])
        q, k, v, z_attn, u, z_ssm = _in_proj(
            x2, norm_gain[layer].reshape(1, D_MODEL), w_in[layer], cos, sin_hi, sin_lo, seq_len, tm)
        to_seq = lambda t: t.reshape(bsz, seq_len, ATTN_WIDTH)
        mixed_attn = _moba_attention(to_seq(q), to_seq(k), to_seq(v), to_seq(z_attn))
        y_ssm = _s5_ssm(u, tables, d_skip[layer].reshape(1, SSM_WIDTH), seq_len)
        x2 = _out_proj(x2, mixed_attn.reshape(bsz * seq_len, ATTN_WIDTH), y_ssm, z_ssm,
                       w_glu[layer], b_glu[layer].reshape(1, SSM_WIDTH),
                       w_out[layer], final_gain.reshape(1, D_MODEL), tm,
                       final_norm=layer == depth - 1)
    return x2.reshape(bsz, seq_len, D_MODEL)
```

```python
import functools
import math

import jax
import jax.numpy as jnp
from jax import lax
from jax.experimental import pallas as pl
from jax.experimental.pallas import tpu as pltpu

F32 = jnp.float32
BF16 = jnp.bfloat16

D_MODEL = 1024
HEAD_DIM = 64
ATTN_HEADS = 8
ATTN_WIDTH = ATTN_HEADS * HEAD_DIM
MOBA_BLOCK = 256
MOBA_TOP_K = 3
ROPE_THETA = 10000.0
ROTARY_SPLIT = 64
SSM_GROUP_DIM = 16
SSM_GROUPS = 32
SSM_WIDTH = SSM_GROUPS * SSM_GROUP_DIM
SSM_STATE = 64
IN_PROJ_WIDTH = 4 * ATTN_WIDTH + 2 * SSM_WIDTH
NORM_EPS = 1e-6
NEG_INF = -1e30

LANES = 128
SSM_CHUNK = 8
GROUPS_PER_TILE = LANES // SSM_GROUP_DIM
SSM_TILES = SSM_WIDTH // LANES
TILE_COLS = SSM_CHUNK * LANES
STATE_COLS = GROUPS_PER_TILE * SSM_STATE
VMEM_LIMIT_BYTES = 56 * 1024 * 1024
MXU_COLS = 256
OUT_PROJ_ROWS = 256
PROJ_TILE_ROWS = 1024
SSM_SEQS_PER_STEP = 4
F32_SUBLANES = 8
BF16_SUBLANES = 16
VALUE_ROWS = HEAD_DIM + BF16_SUBLANES
QK_SCALE = math.log2(math.e) / math.sqrt(HEAD_DIM)


def _resident(shape):
    zeros = (0,) * len(shape)
    return pl.BlockSpec(shape, lambda *_: zeros, pipeline_mode=pl.Buffered(1))


def _in_proj_kernel(x_ref, g_ref, w32_ref, cos_ref, sin_ref,
                    q_ref, k_ref, v_ref, za_ref, u_ref, zs_ref, w_ref):
    @pl.when(pl.program_id(0) == 0)
    def _():
        for c in range(0, IN_PROJ_WIDTH, ATTN_WIDTH):
            w_ref[:, c:c + ATTN_WIDTH] = w32_ref[:, c:c + ATTN_WIDTH].astype(BF16)

    x = x_ref[...]
    ms = jnp.mean(x * x, axis=-1, keepdims=True)
    h = (x * lax.rsqrt(ms + NORM_EPS) * g_ref[...]).astype(BF16)

    def section(idx):
        return jnp.dot(h, w_ref[:, idx * ATTN_WIDTH:(idx + 1) * ATTN_WIDTH],
                       preferred_element_type=F32)

    cos = cos_ref[...]
    sin = sin_ref[...]
    lane = lax.broadcasted_iota(jnp.int32, cos.shape, 1)
    upper = (lane & (HEAD_DIM - 1)) >= HEAD_DIM // 2

    def rotary(t):
        partner = jnp.where(upper, pltpu.roll(t, HEAD_DIM // 2, axis=1),
                            pltpu.roll(t, LANES - HEAD_DIM // 2, axis=1))
        return t * cos + partner * sin

    q = section(0)
    k = section(1)
    for c in range(ATTN_WIDTH // LANES):
        sl = slice(c * LANES, (c + 1) * LANES)
        q_ref[:, sl] = (rotary(q[:, sl]) * QK_SCALE).astype(BF16)
        k_ref[:, sl] = rotary(k[:, sl]).astype(BF16)
    v_ref[...] = section(2).astype(BF16)
    za_ref[...] = section(3).astype(BF16)
    u_ref[...] = section(4)
    zs_ref[...] = section(5).astype(BF16)


def _in_proj(x2, gain, w_in, cos, sin, seq_len, tm):
    rows = x2.shape[0]
    pos_blocks = seq_len // tm
    row_spec = lambda width: pl.BlockSpec((tm, width), lambda i: (i, 0))
    tab_spec = pl.BlockSpec((tm, LANES), lambda i: (i % pos_blocks, 0))
    out_bf16 = jax.ShapeDtypeStruct((rows, ATTN_WIDTH), BF16)
    out_f32 = jax.ShapeDtypeStruct((rows, ATTN_WIDTH), F32)
    return pl.pallas_call(
        _in_proj_kernel,
        grid=(rows // tm,),
        in_specs=[row_spec(D_MODEL), _resident((1, D_MODEL)), _resident((D_MODEL, IN_PROJ_WIDTH)),
                  tab_spec, tab_spec],
        out_specs=[row_spec(ATTN_WIDTH)] * 6,
        out_shape=[out_bf16, out_bf16, out_bf16, out_bf16, out_f32, out_bf16],
        scratch_shapes=[pltpu.VMEM((D_MODEL, IN_PROJ_WIDTH), BF16)],
        compiler_params=pltpu.CompilerParams(dimension_semantics=("arbitrary",),
                                             vmem_limit_bytes=VMEM_LIMIT_BYTES),
        name="in_proj",
    )(x2, gain, w_in, cos, sin)


def _attn_kernel(q0_ref, k0_ref, v0_ref, qn_ref, kn_ref, vn_ref, z_ref, o_ref,
                 kx_ref, vx_ref, qx_ref, sa_ref, sb_ref, m_ref, acc_ref, *, n_blocks):
    unit = pl.program_id(0)
    blk = MOBA_BLOCK
    heads = LANES // HEAD_DIM
    n_pairs = n_blocks // 2
    cur = lax.rem(unit, 2)
    nxt = 1 - cur

    head_rows = lambda h: slice(h * HEAD_DIM, (h + 1) * HEAD_DIM)
    spare_base = lambda h: ((h + 1) % heads) * HEAD_DIM
    iota = lambda shape, d: lax.broadcasted_iota(jnp.int32, shape, d)
    key = iota((blk, blk), 0)
    query = iota((blk, blk), 1)
    col_max = lambda s: jnp.max(s, axis=0, keepdims=True)

    def prepare_items(q_ref, k_ref, v_ref, slot):
        kmean = {}

        def key_block(j):
            lane = iota((blk, LANES), 1)
            sum_rows = (iota((VALUE_ROWS - HEAD_DIM, blk), 0) == 0).astype(F32)
            rows = slice(j * blk, (j + 1) * blk)
            kj = k_ref[0, rows, :]
            kmean[j] = jnp.mean(kj.astype(F32), axis=0, keepdims=True)
            vt = v_ref[0, rows, :].astype(F32).T
            for h in range(heads):
                in_head = (lane >= h * HEAD_DIM) & (lane < (h + 1) * HEAD_DIM)
                tag = jnp.where(lane == spare_base(h) + j, 1.0, 0.0).astype(BF16)
                kx_ref[slot, h, j] = jnp.where(in_head, kj, tag)
                vx_ref[slot, j, h] = jnp.concatenate([vt[head_rows(h)], sum_rows],
                                                     axis=0).astype(BF16)

        def query_block(qi):
            if "all" not in kmean:
                kmean["all"] = jnp.concatenate([kmean[j] for j in range(n_blocks)],
                                               axis=0).astype(BF16)
            blk_row = iota((n_blocks, blk), 0)
            zero_rows = lambda n: jnp.zeros((n, blk), F32)
            in_head_order = lambda h, own, other: [own, other] if h == 0 else [other, own]
            qt = q_ref[0, qi * blk:(qi + 1) * blk, :].astype(F32).T
            past = blk_row < qi
            for h in range(heads):
                q_rows = qt[head_rows(h)]
                q_only = jnp.concatenate(in_head_order(h, q_rows, zero_rows(HEAD_DIM)), axis=0)
                gate = jnp.dot(kmean["all"], q_only.astype(BF16), preferred_element_type=F32)
                gate = jnp.where(past, gate, NEG_INF)
                beaten_by = jnp.zeros((n_blocks, blk), jnp.int32)
                for j in range(n_blocks):
                    gj = gate[j:j + 1, :]
                    wins = (gj > gate) | ((gj == gate) & (j < blk_row))
                    beaten_by = beaten_by + wins.astype(jnp.int32)
                keep = (past & (beaten_by < MOBA_TOP_K)) | (blk_row == qi)
                bias = jnp.where(keep, 0.0, NEG_INF)
                spare = jnp.concatenate([bias, zero_rows(HEAD_DIM - n_blocks)], axis=0)
                qx_ref[slot, qi, h] = jnp.concatenate(in_head_order(h, q_rows, spare),
                                                      axis=0).astype(BF16)

        return ([functools.partial(key_block, j) for j in range(n_blocks)]
                + [functools.partial(query_block, qi) for qi in range(n_blocks)])

    def visible_blocks(pair):
        tiles = (pair, n_blocks - 1 - pair)
        return ([(t, tiles[t], True) for t in range(2)]
                + [(t, j, False) for t in range(2) for j in range(tiles[t])])

    traced = lambda t: t + jnp.minimum(unit, 0)

    def stage_items(pair, s_ref, m, slot):
        def block(t, j, own, h):
            qi = (pair, n_blocks - 1 - pair)[t]
            s = jnp.dot(kx_ref[slot, h, j], qx_ref[slot, qi, h], preferred_element_type=F32)
            if own:
                s = jnp.where(key <= query, s, NEG_INF)
            s_ref[traced(t), h, j] = s
            m[t, h] = col_max(s) if own else jnp.maximum(m[t, h], col_max(s))
        return [functools.partial(block, *b, h) for b in visible_blocks(pair)
                for h in range(heads)]

    def finish_items(pair, s_ref, m, acc_ref, slot):
        def block(t, j, own, h):
            p = jnp.exp2(s_ref[traced(t), h, j] - m[t, h]).astype(BF16)
            pv = jnp.dot(vx_ref[slot, j, h], p, preferred_element_type=F32)
            if own:
                acc_ref[traced(t), h] = pv
            else:
                acc_ref[traced(t), h] += pv

        def write_rows():
            for t, qi in enumerate((pair, n_blocks - 1 - pair)):
                rows = slice(qi * blk, (qi + 1) * blk)
                acc = [acc_ref[t, h] for h in range(heads)]
                o_t = jnp.concatenate([a[:HEAD_DIM] / a[HEAD_DIM:HEAD_DIM + 1] for a in acc],
                                      axis=0)
                z = z_ref[0, rows, :].astype(F32)
                silu = 0.5 * z * (1.0 + jnp.tanh(0.5 * z))
                o_ref[0, rows, :] = (o_t.T * silu).astype(BF16)

        return ([functools.partial(block, *b, h) for b in visible_blocks(pair)
                 for h in range(heads)] + [write_rows])

    def alternate(*item_lists):
        for group in zip(*item_lists):
            for item in group:
                item()
        for items in item_lists:
            for item in items[min(map(len, item_lists)):]:
                item()

    m_keys = [(t, h) for t in range(2) for h in range(heads)]

    @pl.when(unit == 0)
    def _():
        m0 = {}
        alternate(prepare_items(q0_ref, k0_ref, v0_ref, 0))
        alternate(stage_items(0, sa_ref, m0, 0))
        for t, h in m_keys:
            m_ref[t, h] = m0[t, h]

    m = {key_: m_ref[key_] for key_ in m_keys}
    for pair in range(n_pairs):
        s_ref, s_next = (sa_ref, sb_ref) if pair % 2 == 0 else (sb_ref, sa_ref)
        m_next = {}
        lists = [finish_items(pair, s_ref, m, acc_ref.at[pair % 2], cur)]
        if pair + 1 < n_pairs:
            lists.append(stage_items(pair + 1, s_next, m_next, cur))
        else:
            lists.append(stage_items(0, s_next, m_next, nxt))
        if pair == n_pairs - 2:
            lists.append(prepare_items(qn_ref, kn_ref, vn_ref, nxt))
        alternate(lists[1], lists[0], *lists[2:])
        m = m_next
    for t, h in m_keys:
        m_ref[t, h] = m[t, h]


def _moba_attention(q, k, v, z_attn):
    bsz, seq_len, _ = q.shape
    n_blocks = seq_len // MOBA_BLOCK
    assert n_blocks % 4 == 0 and n_blocks <= HEAD_DIM
    head_pairs = ATTN_WIDTH // LANES
    heads = LANES // HEAD_DIM
    assert heads == 2, "the block bias rides in the other head's half of the 128 lanes"
    n_units = bsz * head_pairs
    block = (1, seq_len, LANES)
    unit_index = lambda u: (u // head_pairs, 0, u % head_pairs)
    first_spec = pl.BlockSpec(block, lambda u: (0, 0, 0))
    next_spec = pl.BlockSpec(block, lambda u: unit_index(jnp.minimum(u + 1, n_units - 1)))
    unit_spec = pl.BlockSpec(block, unit_index)
    scores = pltpu.VMEM((2, heads, n_blocks, MOBA_BLOCK, MOBA_BLOCK), F32)
    return pl.pallas_call(
        functools.partial(_attn_kernel, n_blocks=n_blocks),
        grid=(n_units,),
        in_specs=[first_spec] * 3 + [next_spec] * 3 + [unit_spec],
        out_specs=unit_spec,
        out_shape=jax.ShapeDtypeStruct((bsz, seq_len, ATTN_WIDTH), BF16),
        scratch_shapes=[pltpu.VMEM((2, heads, n_blocks, MOBA_BLOCK, LANES), BF16),
                        pltpu.VMEM((2, n_blocks, heads, VALUE_ROWS, MOBA_BLOCK), BF16),
                        pltpu.VMEM((2, n_blocks, heads, LANES, MOBA_BLOCK), BF16),
                        scores, scores,
                        pltpu.VMEM((2, heads, 1, MOBA_BLOCK), F32),
                        pltpu.VMEM((2, 2, heads, VALUE_ROWS, MOBA_BLOCK), F32)],
        compiler_params=pltpu.CompilerParams(dimension_semantics=("arbitrary",),
                                             vmem_limit_bytes=VMEM_LIMIT_BYTES),
        name="moba_attn",
    )(q, k, v, q, k, v, z_attn)


def _ssm_tables(lam_re, lam_im, b_re, b_im, c_re, c_im, log_dt):
    T, G, P, H = SSM_CHUNK, SSM_GROUPS, SSM_STATE, SSM_GROUP_DIM
    dt = jnp.repeat(jnp.exp(log_dt.astype(F32)), P)
    lam_r, lam_i = lam_re.astype(F32).reshape(G * P), lam_im.astype(F32).reshape(G * P)
    mag = jnp.exp(lam_r * dt)
    bar_r, bar_i = mag * jnp.cos(lam_i * dt), mag * jnp.sin(lam_i * dt)
    den = lam_r * lam_r + lam_i * lam_i
    f_r = ((bar_r - 1.0) * lam_r + bar_i * lam_i) / den
    f_i = (bar_i * lam_r - (bar_r - 1.0) * lam_i) / den
    b_r, b_i = (b.astype(F32).transpose(2, 0, 1).reshape(H, G * P) for b in (b_re, b_im))
    bb_r, bb_i = f_r * b_r - f_i * b_i, f_r * b_i + f_i * b_r

    def powers(base_r, base_i, count):
        out_r, out_i = [base_r], [base_i]
        for _ in range(count - 1):
            r, i = out_r[-1], out_i[-1]
            out_r.append(r * base_r - i * base_i)
            out_i.append(r * base_i + i * base_r)
        return out_r, out_i

    pw_r, pw_i = powers(bar_r, bar_i, T)
    pw_r = jnp.stack([jnp.ones_like(bar_r)] + pw_r)
    pw_i = jnp.stack([jnp.zeros_like(bar_i)] + pw_i)

    lb_r = pw_r[:T, None, :] * bb_r - pw_i[:T, None, :] * bb_i
    lb_i = pw_r[:T, None, :] * bb_i + pw_i[:T, None, :] * bb_r
    lb_t = jnp.stack([lb_r, lb_i], axis=1)
    c_r, c_i = (c.astype(F32).transpose(2, 0, 1).reshape(P, G * H) for c in (c_re, c_im))
    per_out = lambda pw: jnp.repeat(pw.reshape(T + 1, G, P).transpose(0, 2, 1), H, axis=2)
    pwo_r, pwo_i = per_out(pw_r), per_out(pw_i)
    cl_t = jnp.stack([c_r * pwo_r - c_i * pwo_i, -(c_r * pwo_i + c_i * pwo_r)], axis=1)

    a_r, a_i = powers(pw_r[T], pw_i[T], F32_SUBLANES)
    tiled = lambda a: jnp.stack(a).reshape(len(a), SSM_TILES, STATE_COLS)
    a_lin = jnp.stack([tiled(a_r), tiled(a_i)]).transpose(2, 0, 1, 3)
    shifts = [1 << k for k in range(F32_SUBLANES.bit_length() - 1)]
    a_log = jnp.stack([tiled([a_r[s - 1] for s in shifts]),
                       tiled([a_i[s - 1] for s in shifts])]).transpose(2, 1, 0, 3)
    row = jnp.arange(F32_SUBLANES)[None, None, None, :, None]
    keep = row >= jnp.asarray(shifts)[None, :, None, None, None]
    a_log = jnp.where(keep, a_log[:, :, :, None, :], 0.0)
    return lb_t, cl_t, a_log, a_lin


def _ssm_kernel(u_ref, lb_ref, cl_ref, alog_ref, alin_ref, d_ref, y_ref, m_ref, ws_ref, wo_ref,
                work_ref, prev_ref, *, seq_len, n_seq):
    T, H, P = SSM_CHUNK, SSM_GROUP_DIM, SSM_STATE
    n_chunks = seq_len // T
    iota = lambda shape, d: lax.broadcasted_iota(jnp.int32, shape, d)
    h_bits, p_bits = H.bit_length() - 1, P.bit_length() - 1

    @pl.when(pl.program_id(1) == 0)
    def _():
        spread_h = ((iota((LANES, H), 0) & (H - 1)) == iota((LANES, H), 1)).astype(BF16)
        spread_p = ((iota((STATE_COLS, P), 0) & (P - 1)) == iota((STATE_COLS, P), 1)).astype(BF16)
        same_hp = (iota((LANES, STATE_COLS), 0) >> h_bits) == (iota((LANES, STATE_COLS), 1) >> p_bits)
        same_ph = (iota((STATE_COLS, LANES), 0) >> p_bits) == (iota((STATE_COLS, LANES), 1) >> h_bits)

        def expand(spread, coeff, same):
            full = jnp.dot(spread, coeff.astype(BF16), preferred_element_type=F32)
            return jnp.where(same, full, 0.0).astype(BF16)

        rows = lambda j: slice(j * LANES, (j + 1) * LANES)
        state_out = lambda d: jnp.concatenate(
            [expand(spread_p, cl_ref[d, ri], same_ph) for ri in range(2)], axis=0)
        for j in range(T):
            ws_ref[rows(j), :] = jnp.concatenate(
                [expand(spread_h, lb_ref[T - 1 - j, ri], same_hp) for ri in range(2)], axis=1)
        for i in range(T):
            wo_ref[:, rows(i)] = state_out(i + 1)
        c_out = state_out(0)
        for d in range(T):
            block = jnp.dot(ws_ref[rows(T - 1 - d), :], c_out,
                            preferred_element_type=F32).astype(BF16)
            for j in range(T - d):
                m_ref[rows(j), rows(j + d)] = block
        for j in range(T):
            for i in range(j):
                m_ref[rows(j), rows(i)] = jnp.zeros((LANES, LANES), BF16)

    pieces = [u_ref[pl.ds(b * seq_len + i, n_chunks, stride=T), :]
              for b in range(n_seq) for i in range(T)]
    x = jnp.concatenate(
        [jnp.concatenate(pieces[b * T:(b + 1) * T], axis=1) for b in range(n_seq)], axis=0)
    x_lo = x.astype(BF16)
    groups = [slice(c, c + MXU_COLS) for c in range(0, STATE_COLS, MXU_COLS)]
    im_of = lambda cols: slice(STATE_COLS + cols.start, STATE_COLS + cols.stop)
    traced = lambda k: k + jnp.minimum(pl.program_id(1), 0)
    n_groups = len(groups)
    y_region = 2 * n_groups

    def increment(g, part):
        cols = groups[g] if part == 0 else im_of(groups[g])
        work_ref[traced(2 * g + part)] = jnp.dot(x_lo, ws_ref[:, cols], preferred_element_type=F32)

    def intra_chunk(n):
        hi = (n + 1) * MXU_COLS
        work_ref[traced(y_region + n)] = jnp.dot(x_lo[:, :hi], m_ref[:hi, hi - MXU_COLS:hi],
                                                 preferred_element_type=F32)

    sub = F32_SUBLANES
    first_row = iota((sub, MXU_COLS), 0) == 0
    cmul = lambda a_re, a_im, b_re, b_im: (a_re * b_re - a_im * b_im, a_re * b_im + a_im * b_re)
    tiles_per_seq = n_chunks // sub
    pack = BF16_SUBLANES // sub

    def scan_items(g):
        cols = groups[g]
        log_mul = [(alog_ref[0, k, 0, :, cols], alog_ref[0, k, 1, :, cols])
                   for k in range(sub.bit_length() - 1)]
        lin_mul = (alin_ref[0, 0, :, cols], alin_ref[0, 1, :, cols])

        def sequence(b):
            carry = (jnp.zeros((1, MXU_COLS), F32),) * 2
            pending = []
            for t in range(tiles_per_seq):
                rows = pl.ds((b * tiles_per_seq + t) * sub, sub)
                t_re = work_ref[traced(2 * g), rows, :]
                t_im = work_ref[traced(2 * g + 1), rows, :]
                for k, mul in enumerate(log_mul):
                    d_re, d_im = cmul(*mul, pltpu.roll(t_re, 1 << k, axis=0),
                                      pltpu.roll(t_im, 1 << k, axis=0))
                    t_re, t_im = t_re + d_re, t_im + d_im
                if t > 0:
                    d_re, d_im = cmul(*lin_mul, *carry)
                    t_re, t_im = t_re + d_re, t_im + d_im
                pending.append((jnp.where(first_row, carry[0], pltpu.roll(t_re, 1, axis=0)),
                                jnp.where(first_row, carry[1], pltpu.roll(t_im, 1, axis=0))))
                carry = (t_re[sub - 1:sub], t_im[sub - 1:sub])
                if len(pending) == pack:
                    out = pl.ds((b * tiles_per_seq + t + 1 - pack) * sub, pack * sub)
                    for part, at in enumerate((cols, im_of(cols))):
                        prev_ref[out, at] = jnp.concatenate(
                            [p[part] for p in pending], axis=0).astype(BF16)
                    pending = []
        return [functools.partial(sequence, b) for b in range(n_seq)]

    def alternate(*item_lists):
        for group in zip(*item_lists):
            for item in group:
                item()
        for items in item_lists:
            for item in items[min(map(len, item_lists)):]:
                item()

    increment(0, 0)
    increment(0, 1)
    matmuls = ([functools.partial(increment, g, part) for g in range(1, n_groups)
                for part in range(2)]
               + [functools.partial(intra_chunk, n) for n in range(T * LANES // MXU_COLS)])
    per_scan = -(-len(matmuls) // n_groups)
    for g in range(n_groups):
        alternate(scan_items(g), matmuls[g * per_scan:(g + 1) * per_scan])
    per_tile = MXU_COLS // LANES
    for n in range(T * LANES // MXU_COLS):
        cols = slice(n * MXU_COLS, (n + 1) * MXU_COLS)
        y = (work_ref[y_region + n] + x[:, cols] * jnp.concatenate([d_ref[...]] * per_tile, axis=1)
             + jnp.dot(prev_ref[...], wo_ref[:, cols], preferred_element_type=F32))
        for b in range(n_seq):
            for i in range(per_tile):
                y_ref[pl.ds(b * seq_len + n * per_tile + i, n_chunks, stride=T), :] = (
                    y[b * n_chunks:(b + 1) * n_chunks, i * LANES:(i + 1) * LANES])


def _s5_ssm(u, tables, d_skip, seq_len):
    lb_t, cl_t, a_log, a_lin = tables
    T, H, P = SSM_CHUNK, SSM_GROUP_DIM, SSM_STATE
    assert seq_len % (T * F32_SUBLANES) == 0
    bsz = u.shape[0] // seq_len
    n_seq = max(n for n in range(1, SSM_SEQS_PER_STEP + 1) if bsz % n == 0)
    io_spec = pl.BlockSpec((n_seq * seq_len, LANES), lambda q, b: (b, q))
    square = pltpu.VMEM((T * LANES, T * LANES), BF16)
    n_groups = STATE_COLS // MXU_COLS
    chunk_rows = n_seq * seq_len // T
    return pl.pallas_call(
        functools.partial(_ssm_kernel, seq_len=seq_len, n_seq=n_seq),
        grid=(SSM_TILES, bsz // n_seq),
        in_specs=[io_spec,
                  pl.BlockSpec((T, 2, H, STATE_COLS), lambda q, b: (0, 0, 0, q)),
                  pl.BlockSpec((T + 1, 2, P, LANES), lambda q, b: (0, 0, 0, q)),
                  pl.BlockSpec((1,) + a_log.shape[1:], lambda q, b: (q, 0, 0, 0, 0)),
                  pl.BlockSpec((1,) + a_lin.shape[1:], lambda q, b: (q, 0, 0, 0)),
                  pl.BlockSpec((1, LANES), lambda q, b: (0, q))],
        out_specs=io_spec,
        out_shape=jax.ShapeDtypeStruct(u.shape, F32),
        scratch_shapes=[square, square, square,
                        pltpu.VMEM((2 * n_groups + T * LANES // MXU_COLS, chunk_rows, MXU_COLS), F32),
                        pltpu.VMEM((chunk_rows, 2 * STATE_COLS), BF16)],
        compiler_params=pltpu.CompilerParams(dimension_semantics=("arbitrary", "arbitrary"),
                                             vmem_limit_bytes=VMEM_LIMIT_BYTES),
        name="s5_ssm",
    )(u, lb_t, cl_t, a_log, a_lin, d_skip)


def _out_proj_kernel(x_ref, ma_ref, y_ref, zs_ref, wg32_ref, bg_ref, w32_ref, g_ref, o_ref,
                     wg_ref, w_ref, *, final_norm):
    @pl.when(pl.program_id(0) == 0)
    def _():
        wg_ref[...] = wg32_ref[...].astype(BF16)
        w_ref[...] = w32_ref[...].astype(BF16)

    half_plus = lambda t: 1.0 + jnp.tanh(0.5 * t)
    for c in range(0, x_ref.shape[0], OUT_PROJ_ROWS):
        rows = slice(c, c + OUT_PROJ_ROWS)
        y = y_ref[rows, :]
        y = y * (1.0 + jnp.tanh(math.sqrt(2.0 / math.pi) * (y + 0.044715 * (y * y * y))))
        gate = jnp.dot((0.5 * y).astype(BF16), wg_ref[...], preferred_element_type=F32) + bg_ref[...]
        z = zs_ref[rows, :].astype(F32)
        mixed_ssm = (0.125 * y) * half_plus(gate) * (z * half_plus(z))
        mixed = jnp.concatenate([ma_ref[rows, :], mixed_ssm.astype(BF16)], axis=1)
        r = x_ref[rows, :] + jnp.dot(mixed, w_ref[...], preferred_element_type=F32)
        if final_norm:
            ms = jnp.mean(r * r, axis=-1, keepdims=True)
            r = r * lax.rsqrt(ms + NORM_EPS) * g_ref[...]
        o_ref[rows, :] = r


def _out_proj(x2, mixed_attn, y_ssm, z_ssm, w_glu, b_glu, w_out, gain, tm, final_norm):
    rows = x2.shape[0]
    row_spec = lambda width: pl.BlockSpec((tm, width), lambda i: (i, 0))
    return pl.pallas_call(
        functools.partial(_out_proj_kernel, final_norm=final_norm),
        grid=(rows // tm,),
        in_specs=[row_spec(D_MODEL), row_spec(ATTN_WIDTH), row_spec(SSM_WIDTH), row_spec(SSM_WIDTH),
                  _resident((SSM_WIDTH, SSM_WIDTH)), _resident((1, SSM_WIDTH)),
                  _resident((ATTN_WIDTH + SSM_WIDTH, D_MODEL)), _resident((1, D_MODEL))],
        out_specs=row_spec(D_MODEL),
        out_shape=jax.ShapeDtypeStruct((rows, D_MODEL), F32),
        scratch_shapes=[pltpu.VMEM((SSM_WIDTH, SSM_WIDTH), BF16),
                        pltpu.VMEM((ATTN_WIDTH + SSM_WIDTH, D_MODEL), BF16)],
        compiler_params=pltpu.CompilerParams(dimension_semantics=("arbitrary",),
                                             vmem_limit_bytes=VMEM_LIMIT_BYTES),
        name="out_proj",
    )(x2, mixed_attn, y_ssm, z_ssm, w_glu, b_glu, w_out, gain)


def _rotary_tables(seq_len):
    assert seq_len % ROTARY_SPLIT == 0
    half = HEAD_DIM // 2
    lane = jnp.arange(LANES)
    inv_freq = 1.0 / (ROPE_THETA ** ((lane % half).astype(F32) / half))
    hi = jnp.arange(0, seq_len, ROTARY_SPLIT, dtype=F32)[:, None, None] * inv_freq
    lo = jnp.arange(ROTARY_SPLIT, dtype=F32)[None, :, None] * inv_freq
    sign = jnp.where(lane % HEAD_DIM >= half, 1.0, -1.0).astype(F32)
    sin_hi, sin_lo = sign * jnp.sin(hi), sign * jnp.sin(lo)
    cos = (jnp.cos(hi) * jnp.cos(lo) - jnp.sin(hi) * jnp.sin(lo)).reshape(seq_len, LANES)
    sin = (sin_hi * jnp.cos(lo) + jnp.cos(hi) * sin_lo).reshape(seq_len, LANES)
    return cos, sin


def kernel(x, norm_gain, w_in, w_out, lam_re, lam_im, b_re, b_im, c_re, c_im,
           d_skip, log_dt, w_glu, b_glu, final_gain):
    bsz, seq_len, _ = x.shape
    depth = norm_gain.shape[0]
    assert seq_len % MOBA_BLOCK == 0 and seq_len // MOBA_BLOCK > MOBA_TOP_K
    assert seq_len % SSM_CHUNK == 0
    tm = min(PROJ_TILE_ROWS, seq_len)
    cos, sin = _rotary_tables(seq_len)

    x2 = x.reshape(bsz * seq_len, D_MODEL)
    for layer in range(depth):
        tables = _ssm_tables(lam_re[layer], lam_im[layer], b_re[layer], b_im[layer],
                             c_re[layer], c_im[layer], log_dt[layer])
        q, k, v, z_attn, u, z_ssm = _in_proj(
            x2, norm_gain[layer].reshape(1, D_MODEL), w_in[layer], cos, sin, seq_len, tm)
        to_seq = lambda t: t.reshape(bsz, seq_len, ATTN_WIDTH)
        mixed_attn = _moba_attention(to_seq(q), to_seq(k), to_seq(v), to_seq(z_attn))
        y_ssm = _s5_ssm(u, tables, d_skip[layer].reshape(1, SSM_WIDTH), seq_len)
        x2 = _out_proj(x2, mixed_attn.reshape(bsz * seq_len, ATTN_WIDTH), y_ssm, z_ssm,
                       w_glu[layer], b_glu[layer].reshape(1, SSM_WIDTH),
                       w_out[layer], final_gain.reshape(1, D_MODEL), tm,
                       final_norm=layer == depth - 1)
    return x2.reshape(bsz, seq_len, D_MODEL)
```

```python
import functools
import math

import jax
import jax.numpy as jnp
from jax import lax
from jax.experimental import pallas as pl
from jax.experimental.pallas import tpu as pltpu

F32 = jnp.float32
BF16 = jnp.bfloat16

D_MODEL = 1024
HEAD_DIM = 64
ATTN_HEADS = 8
ATTN_WIDTH = ATTN_HEADS * HEAD_DIM
MOBA_BLOCK = 256
MOBA_TOP_K = 3
ROPE_THETA = 10000.0
ROTARY_SPLIT = 64
SSM_GROUP_DIM = 16
SSM_GROUPS = 32
SSM_WIDTH = SSM_GROUPS * SSM_GROUP_DIM
SSM_STATE = 64
IN_PROJ_WIDTH = 4 * ATTN_WIDTH + 2 * SSM_WIDTH
NORM_EPS = 1e-6
NEG_INF = -1e30

LANES = 128
SSM_CHUNK = 8
GROUPS_PER_TILE = LANES // SSM_GROUP_DIM
SSM_TILES = SSM_WIDTH // LANES
TILE_COLS = SSM_CHUNK * LANES
STATE_COLS = GROUPS_PER_TILE * SSM_STATE
VMEM_LIMIT_BYTES = 56 * 1024 * 1024
MXU_COLS = 256
OUT_PROJ_ROWS = 256
PROJ_TILE_ROWS = 1024
SSM_SEQS_PER_STEP = 4
F32_SUBLANES = 8
BF16_SUBLANES = 16
VALUE_ROWS = HEAD_DIM + BF16_SUBLANES
QK_SCALE = math.log2(math.e) / math.sqrt(HEAD_DIM)


def _resident(shape):
    zeros = (0,) * len(shape)
    return pl.BlockSpec(shape, lambda *_: zeros, pipeline_mode=pl.Buffered(1))


def _in_proj_kernel(x_ref, g_ref, w32_ref, cos_ref, sin_ref,
                    q_ref, k_ref, v_ref, za_ref, u_ref, zs_ref, w_ref):
    @pl.when(pl.program_id(0) == 0)
    def _():
        for c in range(0, IN_PROJ_WIDTH, ATTN_WIDTH):
            w_ref[:, c:c + ATTN_WIDTH] = w32_ref[:, c:c + ATTN_WIDTH].astype(BF16)

    x = x_ref[...]
    ms = jnp.mean(x * x, axis=-1, keepdims=True)
    h = (x * lax.rsqrt(ms + NORM_EPS) * g_ref[...]).astype(BF16)

    def section(idx):
        return jnp.dot(h, w_ref[:, idx * ATTN_WIDTH:(idx + 1) * ATTN_WIDTH],
                       preferred_element_type=F32)

    cos = cos_ref[...]
    sin = sin_ref[...]
    lane = lax.broadcasted_iota(jnp.int32, cos.shape, 1)
    upper = (lane & (HEAD_DIM - 1)) >= HEAD_DIM // 2

    def rotary(t):
        partner = jnp.where(upper, pltpu.roll(t, HEAD_DIM // 2, axis=1),
                            pltpu.roll(t, LANES - HEAD_DIM // 2, axis=1))
        return t * cos + partner * sin

    q = section(0)
    k = section(1)
    for c in range(ATTN_WIDTH // LANES):
        sl = slice(c * LANES, (c + 1) * LANES)
        q_ref[:, sl] = (rotary(q[:, sl]) * QK_SCALE).astype(BF16)
        k_ref[:, sl] = rotary(k[:, sl]).astype(BF16)
    v_ref[...] = section(2).astype(BF16)
    za_ref[...] = section(3).astype(BF16)
    u_ref[...] = section(4)
    zs_ref[...] = section(5).astype(BF16)


def _in_proj(x2, gain, w_in, cos, sin, seq_len, tm):
    rows = x2.shape[0]
    pos_blocks = seq_len // tm
    row_spec = lambda width: pl.BlockSpec((tm, width), lambda i: (i, 0))
    tab_spec = pl.BlockSpec((tm, LANES), lambda i: (i % pos_blocks, 0))
    out_bf16 = jax.ShapeDtypeStruct((rows, ATTN_WIDTH), BF16)
    out_f32 = jax.ShapeDtypeStruct((rows, ATTN_WIDTH), F32)
    return pl.pallas_call(
        _in_proj_kernel,
        grid=(rows // tm,),
        in_specs=[row_spec(D_MODEL), _resident((1, D_MODEL)), _resident((D_MODEL, IN_PROJ_WIDTH)),
                  tab_spec, tab_spec],
        out_specs=[row_spec(ATTN_WIDTH)] * 6,
        out_shape=[out_bf16, out_bf16, out_bf16, out_bf16, out_f32, out_bf16],
        scratch_shapes=[pltpu.VMEM((D_MODEL, IN_PROJ_WIDTH), BF16)],
        compiler_params=pltpu.CompilerParams(dimension_semantics=("arbitrary",),
                                             vmem_limit_bytes=VMEM_LIMIT_BYTES),
        name="in_proj",
    )(x2, gain, w_in, cos, sin)


def _attn_kernel(q0_ref, k0_ref, v0_ref, qn_ref, kn_ref, vn_ref, z_ref, o_ref,
                 kx_ref, vx_ref, qx_ref, sa_ref, sb_ref, m_ref, acc_ref, *, n_blocks):
    unit = pl.program_id(0)
    blk = MOBA_BLOCK
    heads = LANES // HEAD_DIM
    n_pairs = n_blocks // 2
    cur = lax.rem(unit, 2)
    nxt = 1 - cur

    head_rows = lambda h: slice(h * HEAD_DIM, (h + 1) * HEAD_DIM)
    spare_base = lambda h: ((h + 1) % heads) * HEAD_DIM
    iota = lambda shape, d: lax.broadcasted_iota(jnp.int32, shape, d)
    key = iota((blk, blk), 0)
    query = iota((blk, blk), 1)
    col_max = lambda s: jnp.max(s, axis=0, keepdims=True)

    def prepare_items(q_ref, k_ref, v_ref, slot):
        kmean = {}

        def key_block(j):
            lane = iota((blk, LANES), 1)
            sum_rows = (iota((VALUE_ROWS - HEAD_DIM, blk), 0) == 0).astype(F32)
            rows = slice(j * blk, (j + 1) * blk)
            kj = k_ref[0, rows, :]
            kmean[j] = jnp.mean(kj.astype(F32), axis=0, keepdims=True)
            vt = v_ref[0, rows, :].astype(F32).T
            for h in range(heads):
                in_head = (lane >= h * HEAD_DIM) & (lane < (h + 1) * HEAD_DIM)
                tag = jnp.where(lane == spare_base(h) + j, 1.0, 0.0).astype(BF16)
                kx_ref[slot, h, j] = jnp.where(in_head, kj, tag)
                vx_ref[slot, j, h] = jnp.concatenate([vt[head_rows(h)], sum_rows],
                                                     axis=0).astype(BF16)

        def query_block(qi):
            if "all" not in kmean:
                kmean["all"] = jnp.concatenate([kmean[j] for j in range(n_blocks)],
                                               axis=0).astype(BF16)
            blk_row = iota((n_blocks, blk), 0)
            zero_rows = lambda n: jnp.zeros((n, blk), F32)
            in_head_order = lambda h, own, other: [own, other] if h == 0 else [other, own]
            qt = q_ref[0, qi * blk:(qi + 1) * blk, :].astype(F32).T
            past = blk_row < qi
            for h in range(heads):
                q_rows = qt[head_rows(h)]
                q_only = jnp.concatenate(in_head_order(h, q_rows, zero_rows(HEAD_DIM)), axis=0)
                gate = jnp.dot(kmean["all"], q_only.astype(BF16), preferred_element_type=F32)
                gate = jnp.where(past, gate, NEG_INF)
                beaten_by = jnp.zeros((n_blocks, blk), jnp.int32)
                for j in range(n_blocks):
                    gj = gate[j:j + 1, :]
                    wins = (gj > gate) | ((gj == gate) & (j < blk_row))
                    beaten_by = beaten_by + wins.astype(jnp.int32)
                keep = (past & (beaten_by < MOBA_TOP_K)) | (blk_row == qi)
                bias = jnp.where(keep, 0.0, NEG_INF)
                spare = jnp.concatenate([bias, zero_rows(HEAD_DIM - n_blocks)], axis=0)
                qx_ref[slot, qi, h] = jnp.concatenate(in_head_order(h, q_rows, spare),
                                                      axis=0).astype(BF16)

        return ([functools.partial(key_block, j) for j in range(n_blocks)]
                + [functools.partial(query_block, qi) for qi in range(n_blocks)])

    def visible_blocks(pair):
        tiles = (pair, n_blocks - 1 - pair)
        return ([(t, tiles[t], True) for t in range(2)]
                + [(t, j, False) for t in range(2) for j in range(tiles[t])])

    traced = lambda t: t + jnp.minimum(unit, 0)

    def stage_items(pair, s_ref, m, slot):
        def block(t, j, own, h):
            qi = (pair, n_blocks - 1 - pair)[t]
            s = jnp.dot(kx_ref[slot, h, j], qx_ref[slot, qi, h], preferred_element_type=F32)
            if own:
                s = jnp.where(key <= query, s, NEG_INF)
            s_ref[traced(t), h, j] = s
            m[t, h] = col_max(s) if own else jnp.maximum(m[t, h], col_max(s))
        return [functools.partial(block, *b, h) for b in visible_blocks(pair)
                for h in range(heads)]

    def finish_items(pair, s_ref, m, acc_ref, slot):
        def block(t, j, own, h):
            p = jnp.exp2(s_ref[traced(t), h, j] - m[t, h]).astype(BF16)
            pv = jnp.dot(vx_ref[slot, j, h], p, preferred_element_type=F32)
            if own:
                acc_ref[traced(t), h] = pv
            else:
                acc_ref[traced(t), h] += pv

        def write_rows():
            for t, qi in enumerate((pair, n_blocks - 1 - pair)):
                rows = slice(qi * blk, (qi + 1) * blk)
                acc = [acc_ref[t, h] for h in range(heads)]
                o_t = jnp.concatenate([a[:HEAD_DIM] / a[HEAD_DIM:HEAD_DIM + 1] for a in acc],
                                      axis=0)
                z = z_ref[0, rows, :].astype(F32)
                silu = 0.5 * z * (1.0 + jnp.tanh(0.5 * z))
                o_ref[0, rows, :] = (o_t.T * silu).astype(BF16)

        return ([functools.partial(block, *b, h) for b in visible_blocks(pair)
                 for h in range(heads)] + [write_rows])

    def alternate(*item_lists):
        for group in zip(*item_lists):
            for item in group:
                item()
        for items in item_lists:
            for item in items[min(map(len, item_lists)):]:
                item()

    m_keys = [(t, h) for t in range(2) for h in range(heads)]

    @pl.when(unit == 0)
    def _():
        m0 = {}
        alternate(prepare_items(q0_ref, k0_ref, v0_ref, 0))
        alternate(stage_items(0, sa_ref, m0, 0))
        for t, h in m_keys:
            m_ref[t, h] = m0[t, h]

    m = {key_: m_ref[key_] for key_ in m_keys}
    for pair in range(n_pairs):
        s_ref, s_next = (sa_ref, sb_ref) if pair % 2 == 0 else (sb_ref, sa_ref)
        m_next = {}
        lists = [finish_items(pair, s_ref, m, acc_ref.at[pair % 2], cur)]
        if pair + 1 < n_pairs:
            lists.append(stage_items(pair + 1, s_next, m_next, cur))
        else:
            lists.append(stage_items(0, s_next, m_next, nxt))
        if pair == n_pairs - 2:
            lists.append(prepare_items(qn_ref, kn_ref, vn_ref, nxt))
        alternate(lists[1], lists[0], *lists[2:])
        m = m_next
    for t, h in m_keys:
        m_ref[t, h] = m[t, h]


def _moba_attention(q, k, v, z_attn):
    bsz, seq_len, _ = q.shape
    n_blocks = seq_len // MOBA_BLOCK
    assert n_blocks % 4 == 0 and n_blocks <= HEAD_DIM
    head_pairs = ATTN_WIDTH // LANES
    heads = LANES // HEAD_DIM
    assert heads == 2, "the block bias rides in the other head's half of the 128 lanes"
    n_units = bsz * head_pairs
    block = (1, seq_len, LANES)
    unit_index = lambda u: (u // head_pairs, 0, u % head_pairs)
    first_spec = pl.BlockSpec(block, lambda u: (0, 0, 0))
    next_spec = pl.BlockSpec(block, lambda u: unit_index(jnp.minimum(u + 1, n_units - 1)))
    unit_spec = pl.BlockSpec(block, unit_index)
    scores = pltpu.VMEM((2, heads, n_blocks, MOBA_BLOCK, MOBA_BLOCK), F32)
    return pl.pallas_call(
        functools.partial(_attn_kernel, n_blocks=n_blocks),
        grid=(n_units,),
        in_specs=[first_spec] * 3 + [next_spec] * 3 + [unit_spec],
        out_specs=unit_spec,
        out_shape=jax.ShapeDtypeStruct((bsz, seq_len, ATTN_WIDTH), BF16),
        scratch_shapes=[pltpu.VMEM((2, heads, n_blocks, MOBA_BLOCK, LANES), BF16),
                        pltpu.VMEM((2, n_blocks, heads, VALUE_ROWS, MOBA_BLOCK), BF16),
                        pltpu.VMEM((2, n_blocks, heads, LANES, MOBA_BLOCK), BF16),
                        scores, scores,
                        pltpu.VMEM((2, heads, 1, MOBA_BLOCK), F32),
                        pltpu.VMEM((2, 2, heads, VALUE_ROWS, MOBA_BLOCK), F32)],
        compiler_params=pltpu.CompilerParams(dimension_semantics=("arbitrary",),
                                             vmem_limit_bytes=VMEM_LIMIT_BYTES),
        name="moba_attn",
    )(q, k, v, q, k, v, z_attn)


def _ssm_tables(lam_re, lam_im, b_re, b_im, c_re, c_im, log_dt):
    T, G, P, H = SSM_CHUNK, SSM_GROUPS, SSM_STATE, SSM_GROUP_DIM
    dt = jnp.repeat(jnp.exp(log_dt.astype(F32)), P)
    lam_r, lam_i = lam_re.astype(F32).reshape(G * P), lam_im.astype(F32).reshape(G * P)
    mag = jnp.exp(lam_r * dt)
    bar_r, bar_i = mag * jnp.cos(lam_i * dt), mag * jnp.sin(lam_i * dt)
    den = lam_r * lam_r + lam_i * lam_i
    f_r = ((bar_r - 1.0) * lam_r + bar_i * lam_i) / den
    f_i = (bar_i * lam_r - (bar_r - 1.0) * lam_i) / den
    b_r, b_i = (b.astype(F32).transpose(2, 0, 1).reshape(H, G * P) for b in (b_re, b_im))
    bb_r, bb_i = f_r * b_r - f_i * b_i, f_r * b_i + f_i * b_r

    def powers(base_r, base_i, count):
        out_r, out_i = [base_r], [base_i]
        for _ in range(count - 1):
            r, i = out_r[-1], out_i[-1]
            out_r.append(r * base_r - i * base_i)
            out_i.append(r * base_i + i * base_r)
        return out_r, out_i

    pw_r, pw_i = powers(bar_r, bar_i, T)
    pw_r = jnp.stack([jnp.ones_like(bar_r)] + pw_r)
    pw_i = jnp.stack([jnp.zeros_like(bar_i)] + pw_i)

    lb_r = pw_r[:T, None, :] * bb_r - pw_i[:T, None, :] * bb_i
    lb_i = pw_r[:T, None, :] * bb_i + pw_i[:T, None, :] * bb_r
    lb_t = (lb_r, lb_i)
    c_r, c_i = (c.astype(F32).transpose(2, 0, 1).reshape(P, G * H) for c in (c_re, c_im))
    per_out = lambda pw: jnp.repeat(pw.reshape(T + 1, G, P).transpose(0, 2, 1), H, axis=2)
    pwo_r, pwo_i = per_out(pw_r), per_out(pw_i)
    cl_t = (c_r * pwo_r - c_i * pwo_i, -(c_r * pwo_i + c_i * pwo_r))

    a_r, a_i = powers(pw_r[T], pw_i[T], F32_SUBLANES)
    tiled = lambda a: jnp.stack(a).reshape(len(a), SSM_TILES, STATE_COLS)
    a_lin = jnp.stack([tiled(a_r), tiled(a_i)]).transpose(2, 0, 1, 3)
    shifts = [1 << k for k in range(F32_SUBLANES.bit_length() - 1)]
    a_log = jnp.stack([tiled([a_r[s - 1] for s in shifts]),
                       tiled([a_i[s - 1] for s in shifts])]).transpose(2, 1, 0, 3)
    row = jnp.arange(F32_SUBLANES)[None, None, None, :, None]
    keep = row >= jnp.asarray(shifts)[None, :, None, None, None]
    a_log = jnp.where(keep, a_log[:, :, :, None, :], 0.0)
    return lb_t, cl_t, a_log, a_lin


def _ssm_kernel(u_ref, lb_re_ref, lb_im_ref, cl_re_ref, cl_im_ref, alog_ref, alin_ref, d_ref,
                y_ref, m_ref, ws_ref, wo_ref, work_ref, prev_ref, *, seq_len, n_seq):
    T, H, P = SSM_CHUNK, SSM_GROUP_DIM, SSM_STATE
    n_chunks = seq_len // T
    iota = lambda shape, d: lax.broadcasted_iota(jnp.int32, shape, d)
    h_bits, p_bits = H.bit_length() - 1, P.bit_length() - 1

    @pl.when(pl.program_id(1) == 0)
    def _():
        spread_h = ((iota((LANES, H), 0) & (H - 1)) == iota((LANES, H), 1)).astype(BF16)
        spread_p = ((iota((STATE_COLS, P), 0) & (P - 1)) == iota((STATE_COLS, P), 1)).astype(BF16)
        same_hp = (iota((LANES, STATE_COLS), 0) >> h_bits) == (iota((LANES, STATE_COLS), 1) >> p_bits)
        same_ph = (iota((STATE_COLS, LANES), 0) >> p_bits) == (iota((STATE_COLS, LANES), 1) >> h_bits)

        def expand(spread, coeff, same):
            full = jnp.dot(spread, coeff.astype(BF16), preferred_element_type=F32)
            return jnp.where(same, full, 0.0).astype(BF16)

        rows = lambda j: slice(j * LANES, (j + 1) * LANES)
        state_out = lambda d: jnp.concatenate(
            [expand(spread_p, cl_ref[d], same_ph) for cl_ref in (cl_re_ref, cl_im_ref)], axis=0)
        for j in range(T):
            ws_ref[rows(j), :] = jnp.concatenate(
                [expand(spread_h, lb_ref[T - 1 - j], same_hp) for lb_ref in (lb_re_ref, lb_im_ref)],
                axis=1)
        for i in range(T):
            wo_ref[:, rows(i)] = state_out(i + 1)
        c_out = state_out(0)
        for d in range(T):
            block = jnp.dot(ws_ref[rows(T - 1 - d), :], c_out,
                            preferred_element_type=F32).astype(BF16)
            for j in range(T - d):
                m_ref[rows(j), rows(j + d)] = block
        for j in range(T):
            for i in range(j):
                m_ref[rows(j), rows(i)] = jnp.zeros((LANES, LANES), BF16)

    pieces = [u_ref[pl.ds(b * seq_len + i, n_chunks, stride=T), :]
              for b in range(n_seq) for i in range(T)]
    x = jnp.concatenate(
        [jnp.concatenate(pieces[b * T:(b + 1) * T], axis=1) for b in range(n_seq)], axis=0)
    x_lo = x.astype(BF16)
    groups = [slice(c, c + MXU_COLS) for c in range(0, STATE_COLS, MXU_COLS)]
    im_of = lambda cols: slice(STATE_COLS + cols.start, STATE_COLS + cols.stop)
    traced = lambda k: k + jnp.minimum(pl.program_id(1), 0)
    n_groups = len(groups)
    y_region = 2 * n_groups

    def increment(g, part):
        cols = groups[g] if part == 0 else im_of(groups[g])
        work_ref[traced(2 * g + part)] = jnp.dot(x_lo, ws_ref[:, cols], preferred_element_type=F32)

    def intra_chunk(n):
        hi = (n + 1) * MXU_COLS
        work_ref[traced(y_region + n)] = jnp.dot(x_lo[:, :hi], m_ref[:hi, hi - MXU_COLS:hi],
                                                 preferred_element_type=F32)

    sub = F32_SUBLANES
    first_row = iota((sub, MXU_COLS), 0) == 0
    cmul = lambda a_re, a_im, b_re, b_im: (a_re * b_re - a_im * b_im, a_re * b_im + a_im * b_re)
    tiles_per_seq = n_chunks // sub
    pack = BF16_SUBLANES // sub

    def scan_items(g):
        cols = groups[g]
        log_mul = [(alog_ref[0, k, 0, :, cols], alog_ref[0, k, 1, :, cols])
                   for k in range(sub.bit_length() - 1)]
        lin_mul = (alin_ref[0, 0, :, cols], alin_ref[0, 1, :, cols])

        def sequence(b):
            carry = (jnp.zeros((1, MXU_COLS), F32),) * 2
            pending = []
            for t in range(tiles_per_seq):
                rows = pl.ds((b * tiles_per_seq + t) * sub, sub)
                t_re = work_ref[traced(2 * g), rows, :]
                t_im = work_ref[traced(2 * g + 1), rows, :]
                for k, mul in enumerate(log_mul):
                    d_re, d_im = cmul(*mul, pltpu.roll(t_re, 1 << k, axis=0),
                                      pltpu.roll(t_im, 1 << k, axis=0))
                    t_re, t_im = t_re + d_re, t_im + d_im
                if t > 0:
                    d_re, d_im = cmul(*lin_mul, *carry)
                    t_re, t_im = t_re + d_re, t_im + d_im
                pending.append((jnp.where(first_row, carry[0], pltpu.roll(t_re, 1, axis=0)),
                                jnp.where(first_row, carry[1], pltpu.roll(t_im, 1, axis=0))))
                carry = (t_re[sub - 1:sub], t_im[sub - 1:sub])
                if len(pending) == pack:
                    out = pl.ds((b * tiles_per_seq + t + 1 - pack) * sub, pack * sub)
                    for part, at in enumerate((cols, im_of(cols))):
                        prev_ref[out, at] = jnp.concatenate(
                            [p[part] for p in pending], axis=0).astype(BF16)
                    pending = []
        return [functools.partial(sequence, b) for b in range(n_seq)]

    def alternate(*item_lists):
        for group in zip(*item_lists):
            for item in group:
                item()
        for items in item_lists:
            for item in items[min(map(len, item_lists)):]:
                item()

    increment(0, 0)
    increment(0, 1)
    matmuls = ([functools.partial(increment, g, part) for g in range(1, n_groups)
                for part in range(2)]
               + [functools.partial(intra_chunk, n) for n in range(T * LANES // MXU_COLS)])
    per_scan = -(-len(matmuls) // n_groups)
    for g in range(n_groups):
        alternate(scan_items(g), matmuls[g * per_scan:(g + 1) * per_scan])
    per_tile = MXU_COLS // LANES
    for n in range(T * LANES // MXU_COLS):
        cols = slice(n * MXU_COLS, (n + 1) * MXU_COLS)
        y = (work_ref[y_region + n] + x[:, cols] * jnp.concatenate([d_ref[...]] * per_tile, axis=1)
             + jnp.dot(prev_ref[...], wo_ref[:, cols], preferred_element_type=F32))
        for b in range(n_seq):
            for i in range(per_tile):
                y_ref[pl.ds(b * seq_len + n * per_tile + i, n_chunks, stride=T), :] = (
                    y[b * n_chunks:(b + 1) * n_chunks, i * LANES:(i + 1) * LANES])


def _s5_ssm(u, tables, d_skip, seq_len):
    lb_t, cl_t, a_log, a_lin = tables
    T, H, P = SSM_CHUNK, SSM_GROUP_DIM, SSM_STATE
    assert seq_len % (T * F32_SUBLANES) == 0
    bsz = u.shape[0] // seq_len
    n_seq = max(n for n in range(1, SSM_SEQS_PER_STEP + 1) if bsz % n == 0)
    io_spec = pl.BlockSpec((n_seq * seq_len, LANES), lambda q, b: (b, q))
    lb_spec = pl.BlockSpec((T, H, STATE_COLS), lambda q, b: (0, 0, q))
    cl_spec = pl.BlockSpec((T + 1, P, LANES), lambda q, b: (0, 0, q))
    square = pltpu.VMEM((T * LANES, T * LANES), BF16)
    n_groups = STATE_COLS // MXU_COLS
    chunk_rows = n_seq * seq_len // T
    return pl.pallas_call(
        functools.partial(_ssm_kernel, seq_len=seq_len, n_seq=n_seq),
        grid=(SSM_TILES, bsz // n_seq),
        in_specs=[io_spec, lb_spec, lb_spec, cl_spec, cl_spec,
                  pl.BlockSpec((1,) + a_log.shape[1:], lambda q, b: (q, 0, 0, 0, 0)),
                  pl.BlockSpec((1,) + a_lin.shape[1:], lambda q, b: (q, 0, 0, 0)),
                  pl.BlockSpec((1, LANES), lambda q, b: (0, q))],
        out_specs=io_spec,
        out_shape=jax.ShapeDtypeStruct(u.shape, F32),
        scratch_shapes=[square, square, square,
                        pltpu.VMEM((2 * n_groups + T * LANES // MXU_COLS, chunk_rows, MXU_COLS), F32),
                        pltpu.VMEM((chunk_rows, 2 * STATE_COLS), BF16)],
        compiler_params=pltpu.CompilerParams(dimension_semantics=("arbitrary", "arbitrary"),
                                             vmem_limit_bytes=VMEM_LIMIT_BYTES),
        name="s5_ssm",
    )(u, *lb_t, *cl_t, a_log, a_lin, d_skip)


def _out_proj_kernel(x_ref, ma_ref, y_ref, zs_ref, wg32_ref, bg_ref, w32_ref, g_ref, o_ref,
                     wg_ref, w_ref, *, final_norm):
    @pl.when(pl.program_id(0) == 0)
    def _():
        wg_ref[...] = wg32_ref[...].astype(BF16)
        w_ref[...] = w32_ref[...].astype(BF16)

    half_plus = lambda t: 1.0 + jnp.tanh(0.5 * t)
    for c in range(0, x_ref.shape[0], OUT_PROJ_ROWS):
        rows = slice(c, c + OUT_PROJ_ROWS)
        y = y_ref[rows, :]
        y = y * (1.0 + jnp.tanh(math.sqrt(2.0 / math.pi) * (y + 0.044715 * (y * y * y))))
        gate = jnp.dot((0.5 * y).astype(BF16), wg_ref[...], preferred_element_type=F32) + bg_ref[...]
        z = zs_ref[rows, :].astype(F32)
        mixed_ssm = (0.125 * y) * half_plus(gate) * (z * half_plus(z))
        mixed = jnp.concatenate([ma_ref[rows, :], mixed_ssm.astype(BF16)], axis=1)
        r = x_ref[rows, :] + jnp.dot(mixed, w_ref[...], preferred_element_type=F32)
        if final_norm:
            ms = jnp.mean(r * r, axis=-1, keepdims=True)
            r = r * lax.rsqrt(ms + NORM_EPS) * g_ref[...]
        o_ref[rows, :] = r


def _out_proj(x2, mixed_attn, y_ssm, z_ssm, w_glu, b_glu, w_out, gain, tm, final_norm):
    rows = x2.shape[0]
    row_spec = lambda width: pl.BlockSpec((tm, width), lambda i: (i, 0))
    return pl.pallas_call(
        functools.partial(_out_proj_kernel, final_norm=final_norm),
        grid=(rows // tm,),
        in_specs=[row_spec(D_MODEL), row_spec(ATTN_WIDTH), row_spec(SSM_WIDTH), row_spec(SSM_WIDTH),
                  _resident((SSM_WIDTH, SSM_WIDTH)), _resident((1, SSM_WIDTH)),
                  _resident((ATTN_WIDTH + SSM_WIDTH, D_MODEL)), _resident((1, D_MODEL))],
        out_specs=row_spec(D_MODEL),
        out_shape=jax.ShapeDtypeStruct((rows, D_MODEL), F32),
        scratch_shapes=[pltpu.VMEM((SSM_WIDTH, SSM_WIDTH), BF16),
                        pltpu.VMEM((ATTN_WIDTH + SSM_WIDTH, D_MODEL), BF16)],
        compiler_params=pltpu.CompilerParams(dimension_semantics=("arbitrary",),
                                             vmem_limit_bytes=VMEM_LIMIT_BYTES),
        name="out_proj",
    )(x2, mixed_attn, y_ssm, z_ssm, w_glu, b_glu, w_out, gain)


def _rotary_tables(seq_len):
    assert seq_len % ROTARY_SPLIT == 0
    half = HEAD_DIM // 2
    lane = jnp.arange(LANES)
    inv_freq = 1.0 / (ROPE_THETA ** ((lane % half).astype(F32) / half))
    hi = jnp.arange(0, seq_len, ROTARY_SPLIT, dtype=F32)[:, None, None] * inv_freq
    lo = jnp.arange(ROTARY_SPLIT, dtype=F32)[None, :, None] * inv_freq
    sign = jnp.where(lane % HEAD_DIM >= half, 1.0, -1.0).astype(F32)
    sin_hi, sin_lo = sign * jnp.sin(hi), sign * jnp.sin(lo)
    cos = (jnp.cos(hi) * jnp.cos(lo) - jnp.sin(hi) * jnp.sin(lo)).reshape(seq_len, LANES)
    sin = (sin_hi * jnp.cos(lo) + jnp.cos(hi) * sin_lo).reshape(seq_len, LANES)
    return cos, sin


def kernel(x, norm_gain, w_in, w_out, lam_re, lam_im, b_re, b_im, c_re, c_im,
           d_skip, log_dt, w_glu, b_glu, final_gain):
    bsz, seq_len, _ = x.shape
    depth = norm_gain.shape[0]
    assert seq_len % MOBA_BLOCK == 0 and seq_len // MOBA_BLOCK > MOBA_TOP_K
    assert seq_len % SSM_CHUNK == 0
    tm = min(PROJ_TILE_ROWS, seq_len)
    cos, sin = _rotary_tables(seq_len)

    x2 = x.reshape(bsz * seq_len, D_MODEL)
    for layer in range(depth):
        tables = _ssm_tables(lam_re[layer], lam_im[layer], b_re[layer], b_im[layer],
                             c_re[layer], c_im[layer], log_dt[layer])
        q, k, v, z_attn, u, z_ssm = _in_proj(
            x2, norm_gain[layer].reshape(1, D_MODEL), w_in[layer], cos, sin, seq_len, tm)
        to_seq = lambda t: t.reshape(bsz, seq_len, ATTN_WIDTH)
        mixed_attn = _moba_attention(to_seq(q), to_seq(k), to_seq(v), to_seq(z_attn))
        y_ssm = _s5_ssm(u, tables, d_skip[layer].reshape(1, SSM_WIDTH), seq_len)
        x2 = _out_proj(x2, mixed_attn.reshape(bsz * seq_len, ATTN_WIDTH), y_ssm, z_ssm,
                       w_glu[layer], b_glu[layer].reshape(1, SSM_WIDTH),
                       w_out[layer], final_gain.reshape(1, D_MODEL), tm,
                       final_norm=layer == depth - 1)
    return x2.reshape(bsz, seq_len, D_MODEL)
```

```python
import functools
import math

import jax
import jax.numpy as jnp
from jax import lax
from jax.experimental import pallas as pl
from jax.experimental.pallas import tpu as pltpu

F32 = jnp.float32
BF16 = jnp.bfloat16

D_MODEL = 1024
HEAD_DIM = 64
ATTN_HEADS = 8
ATTN_WIDTH = ATTN_HEADS * HEAD_DIM
MOBA_BLOCK = 256
MOBA_TOP_K = 3
ROPE_THETA = 10000.0
ROTARY_SPLIT = 64
SSM_GROUP_DIM = 16
SSM_GROUPS = 32
SSM_WIDTH = SSM_GROUPS * SSM_GROUP_DIM
SSM_STATE = 64
IN_PROJ_WIDTH = 4 * ATTN_WIDTH + 2 * SSM_WIDTH
NORM_EPS = 1e-6
NEG_INF = -1e30

LANES = 128
SSM_CHUNK = 8
GROUPS_PER_TILE = LANES // SSM_GROUP_DIM
SSM_TILES = SSM_WIDTH // LANES
TILE_COLS = SSM_CHUNK * LANES
STATE_COLS = GROUPS_PER_TILE * SSM_STATE
VMEM_LIMIT_BYTES = 56 * 1024 * 1024
MXU_COLS = 256
OUT_PROJ_ROWS = 256
PROJ_TILE_ROWS = 1024
SSM_SEQS_PER_STEP = 4
F32_SUBLANES = 8
BF16_SUBLANES = 16
VALUE_ROWS = HEAD_DIM + BF16_SUBLANES
QK_SCALE = math.log2(math.e) / math.sqrt(HEAD_DIM)


def _resident(shape):
    zeros = (0,) * len(shape)
    return pl.BlockSpec(shape, lambda *_: zeros, pipeline_mode=pl.Buffered(1))


def _in_proj_kernel(x_ref, g_ref, w32_ref, cos_ref, sin_ref,
                    q_ref, k_ref, v_ref, za_ref, u_ref, zs_ref, w_ref):
    @pl.when(pl.program_id(0) == 0)
    def _():
        for c in range(0, IN_PROJ_WIDTH, ATTN_WIDTH):
            w_ref[:, c:c + ATTN_WIDTH] = w32_ref[:, c:c + ATTN_WIDTH].astype(BF16)

    x = x_ref[...]
    ms = jnp.mean(x * x, axis=-1, keepdims=True)
    h = (x * lax.rsqrt(ms + NORM_EPS) * g_ref[...]).astype(BF16)

    def section(idx):
        return jnp.dot(h, w_ref[:, idx * ATTN_WIDTH:(idx + 1) * ATTN_WIDTH],
                       preferred_element_type=F32)

    cos = cos_ref[...]
    sin = sin_ref[...]
    lane = lax.broadcasted_iota(jnp.int32, cos.shape, 1)
    upper = (lane & (HEAD_DIM - 1)) >= HEAD_DIM // 2

    def rotary(t):
        partner = jnp.where(upper, pltpu.roll(t, HEAD_DIM // 2, axis=1),
                            pltpu.roll(t, LANES - HEAD_DIM // 2, axis=1))
        return t * cos + partner * sin

    q = section(0)
    k = section(1)
    for c in range(ATTN_WIDTH // LANES):
        sl = slice(c * LANES, (c + 1) * LANES)
        q_ref[:, sl] = (rotary(q[:, sl]) * QK_SCALE).astype(BF16)
        k_ref[:, sl] = rotary(k[:, sl]).astype(BF16)
    v_ref[...] = section(2).astype(BF16)
    za_ref[...] = section(3).astype(BF16)
    u_ref[...] = section(4)
    zs_ref[...] = section(5).astype(BF16)


def _in_proj(x2, gain, w_in, cos, sin, seq_len, tm):
    rows = x2.shape[0]
    pos_blocks = seq_len // tm
    row_spec = lambda width: pl.BlockSpec((tm, width), lambda i: (i, 0))
    tab_spec = pl.BlockSpec((tm, LANES), lambda i: (i % pos_blocks, 0))
    out_bf16 = jax.ShapeDtypeStruct((rows, ATTN_WIDTH), BF16)
    out_f32 = jax.ShapeDtypeStruct((rows, ATTN_WIDTH), F32)
    return pl.pallas_call(
        _in_proj_kernel,
        grid=(rows // tm,),
        in_specs=[row_spec(D_MODEL), _resident((1, D_MODEL)), _resident((D_MODEL, IN_PROJ_WIDTH)),
                  tab_spec, tab_spec],
        out_specs=[row_spec(ATTN_WIDTH)] * 6,
        out_shape=[out_bf16, out_bf16, out_bf16, out_bf16, out_f32, out_bf16],
        scratch_shapes=[pltpu.VMEM((D_MODEL, IN_PROJ_WIDTH), BF16)],
        compiler_params=pltpu.CompilerParams(dimension_semantics=("arbitrary",),
                                             vmem_limit_bytes=VMEM_LIMIT_BYTES),
        name="in_proj",
    )(x2, gain, w_in, cos, sin)


def _attn_kernel(q0_ref, k0_ref, v0_ref, qn_ref, kn_ref, vn_ref, z_ref, o_ref,
                 kx_ref, vx_ref, qx_ref, sa_ref, sb_ref, m_ref, acc_ref, *, n_blocks):
    unit = pl.program_id(0)
    blk = MOBA_BLOCK
    heads = LANES // HEAD_DIM
    n_pairs = n_blocks // 2
    cur = lax.rem(unit, 2)
    nxt = 1 - cur

    head_rows = lambda h: slice(h * HEAD_DIM, (h + 1) * HEAD_DIM)
    spare_base = lambda h: ((h + 1) % heads) * HEAD_DIM
    iota = lambda shape, d: lax.broadcasted_iota(jnp.int32, shape, d)
    key = iota((blk, blk), 0)
    query = iota((blk, blk), 1)
    col_max = lambda s: jnp.max(s, axis=0, keepdims=True)

    def prepare_items(q_ref, k_ref, v_ref, slot):
        kmean = {}

        def key_block(j):
            lane = iota((blk, LANES), 1)
            sum_rows = (iota((VALUE_ROWS - HEAD_DIM, blk), 0) == 0).astype(F32)
            rows = slice(j * blk, (j + 1) * blk)
            kj = k_ref[0, rows, :]
            kmean[j] = jnp.mean(kj.astype(F32), axis=0, keepdims=True)
            vt = v_ref[0, rows, :].astype(F32).T
            for h in range(heads):
                in_head = (lane >= h * HEAD_DIM) & (lane < (h + 1) * HEAD_DIM)
                tag = jnp.where(lane == spare_base(h) + j, 1.0, 0.0).astype(BF16)
                kx_ref[slot, h, j] = jnp.where(in_head, kj, tag)
                vx_ref[slot, j, h] = jnp.concatenate([vt[head_rows(h)], sum_rows],
                                                     axis=0).astype(BF16)

        def query_block(qi):
            if "all" not in kmean:
                kmean["all"] = jnp.concatenate([kmean[j] for j in range(n_blocks)],
                                               axis=0).astype(BF16)
            blk_row = iota((n_blocks, blk), 0)
            zero_rows = lambda n: jnp.zeros((n, blk), F32)
            in_head_order = lambda h, own, other: [own, other] if h == 0 else [other, own]
            qt = q_ref[0, qi * blk:(qi + 1) * blk, :].astype(F32).T
            past = blk_row < qi
            for h in range(heads):
                q_rows = qt[head_rows(h)]
                q_only = jnp.concatenate(in_head_order(h, q_rows, zero_rows(HEAD_DIM)), axis=0)
                gate = jnp.dot(kmean["all"], q_only.astype(BF16), preferred_element_type=F32)
                gate = jnp.where(past, gate, NEG_INF)
                beaten_by = jnp.zeros((n_blocks, blk), jnp.int32)
                for j in range(n_blocks):
                    gj = gate[j:j + 1, :]
                    wins = (gj > gate) | ((gj == gate) & (j < blk_row))
                    beaten_by = beaten_by + wins.astype(jnp.int32)
                keep = (past & (beaten_by < MOBA_TOP_K)) | (blk_row == qi)
                bias = jnp.where(keep, 0.0, NEG_INF)
                spare = jnp.concatenate([bias, zero_rows(HEAD_DIM - n_blocks)], axis=0)
                qx_ref[slot, qi, h] = jnp.concatenate(in_head_order(h, q_rows, spare),
                                                      axis=0).astype(BF16)

        return ([functools.partial(key_block, j) for j in range(n_blocks)]
                + [functools.partial(query_block, qi) for qi in range(n_blocks)])

    def visible_blocks(pair):
        tiles = (pair, n_blocks - 1 - pair)
        return ([(t, tiles[t], True) for t in range(2)]
                + [(t, j, False) for t in range(2) for j in range(tiles[t])])

    traced = lambda t: t + jnp.minimum(unit, 0)

    def stage_items(pair, s_ref, m, slot):
        def block(t, j, own, h):
            qi = (pair, n_blocks - 1 - pair)[t]
            s = jnp.dot(kx_ref[slot, h, j], qx_ref[slot, qi, h], preferred_element_type=F32)
            if own:
                s = jnp.where(key <= query, s, NEG_INF)
            s_ref[traced(t), h, j] = s
            m[t, h] = col_max(s) if own else jnp.maximum(m[t, h], col_max(s))
        return [functools.partial(block, *b, h) for b in visible_blocks(pair)
                for h in range(heads)]

    def finish_items(pair, s_ref, m, acc_ref, slot):
        def block(t, j, own, h):
            p = jnp.exp2(s_ref[traced(t), h, j] - m[t, h]).astype(BF16)
            pv = jnp.dot(vx_ref[slot, j, h], p, preferred_element_type=F32)
            if own:
                acc_ref[traced(t), h] = pv
            else:
                acc_ref[traced(t), h] += pv

        def write_rows():
            for t, qi in enumerate((pair, n_blocks - 1 - pair)):
                rows = slice(qi * blk, (qi + 1) * blk)
                acc = [acc_ref[t, h] for h in range(heads)]
                o_t = jnp.concatenate([a[:HEAD_DIM] / a[HEAD_DIM:HEAD_DIM + 1] for a in acc],
                                      axis=0)
                z = z_ref[0, rows, :].astype(F32)
                silu = 0.5 * z * (1.0 + jnp.tanh(0.5 * z))
                o_ref[0, rows, :] = (o_t.T * silu).astype(BF16)

        return ([functools.partial(block, *b, h) for b in visible_blocks(pair)
                 for h in range(heads)] + [write_rows])

    def alternate(*item_lists):
        for group in zip(*item_lists):
            for item in group:
                item()
        for items in item_lists:
            for item in items[min(map(len, item_lists)):]:
                item()

    m_keys = [(t, h) for t in range(2) for h in range(heads)]

    @pl.when(unit == 0)
    def _():
        m0 = {}
        alternate(prepare_items(q0_ref, k0_ref, v0_ref, 0))
        alternate(stage_items(0, sa_ref, m0, 0))
        for t, h in m_keys:
            m_ref[t, h] = m0[t, h]

    m = {key_: m_ref[key_] for key_ in m_keys}
    for pair in range(n_pairs):
        s_ref, s_next = (sa_ref, sb_ref) if pair % 2 == 0 else (sb_ref, sa_ref)
        m_next = {}
        lists = [finish_items(pair, s_ref, m, acc_ref.at[pair % 2], cur)]
        if pair + 1 < n_pairs:
            lists.append(stage_items(pair + 1, s_next, m_next, cur))
        else:
            lists.append(stage_items(0, s_next, m_next, nxt))
        if pair == n_pairs - 2:
            lists.append(prepare_items(qn_ref, kn_ref, vn_ref, nxt))
        alternate(lists[1], lists[0], *lists[2:])
        m = m_next
    for t, h in m_keys:
        m_ref[t, h] = m[t, h]


def _moba_attention(q, k, v, z_attn):
    bsz, seq_len, _ = q.shape
    n_blocks = seq_len // MOBA_BLOCK
    assert n_blocks % 4 == 0 and n_blocks <= HEAD_DIM
    head_pairs = ATTN_WIDTH // LANES
    heads = LANES // HEAD_DIM
    assert heads == 2, "the block bias rides in the other head's half of the 128 lanes"
    n_units = bsz * head_pairs
    block = (1, seq_len, LANES)
    unit_index = lambda u: (u // head_pairs, 0, u % head_pairs)
    first_spec = pl.BlockSpec(block, lambda u: (0, 0, 0))
    next_spec = pl.BlockSpec(block, lambda u: unit_index(jnp.minimum(u + 1, n_units - 1)))
    unit_spec = pl.BlockSpec(block, unit_index)
    scores = pltpu.VMEM((2, heads, n_blocks, MOBA_BLOCK, MOBA_BLOCK), F32)
    return pl.pallas_call(
        functools.partial(_attn_kernel, n_blocks=n_blocks),
        grid=(n_units,),
        in_specs=[first_spec] * 3 + [next_spec] * 3 + [unit_spec],
        out_specs=unit_spec,
        out_shape=jax.ShapeDtypeStruct((bsz, seq_len, ATTN_WIDTH), BF16),
        scratch_shapes=[pltpu.VMEM((2, heads, n_blocks, MOBA_BLOCK, LANES), BF16),
                        pltpu.VMEM((2, n_blocks, heads, VALUE_ROWS, MOBA_BLOCK), BF16),
                        pltpu.VMEM((2, n_blocks, heads, LANES, MOBA_BLOCK), BF16),
                        scores, scores,
                        pltpu.VMEM((2, heads, 1, MOBA_BLOCK), F32),
                        pltpu.VMEM((2, 2, heads, VALUE_ROWS, MOBA_BLOCK), F32)],
        compiler_params=pltpu.CompilerParams(dimension_semantics=("arbitrary",),
                                             vmem_limit_bytes=VMEM_LIMIT_BYTES),
        name="moba_attn",
    )(q, k, v, q, k, v, z_attn)


def _ssm_tables(lam_re, lam_im, b_re, b_im, c_re, c_im, log_dt):
    T, G, P, H = SSM_CHUNK, SSM_GROUPS, SSM_STATE, SSM_GROUP_DIM
    dt = jnp.repeat(jnp.exp(log_dt.astype(F32)), P)
    lam_r, lam_i = lam_re.astype(F32).reshape(G * P), lam_im.astype(F32).reshape(G * P)
    mag = jnp.exp(lam_r * dt)
    bar_r, bar_i = mag * jnp.cos(lam_i * dt), mag * jnp.sin(lam_i * dt)
    den = lam_r * lam_r + lam_i * lam_i
    f_r = ((bar_r - 1.0) * lam_r + bar_i * lam_i) / den
    f_i = (bar_i * lam_r - (bar_r - 1.0) * lam_i) / den
    b_r, b_i = (b.astype(F32).transpose(2, 0, 1).reshape(H, G * P) for b in (b_re, b_im))
    bb_r, bb_i = f_r * b_r - f_i * b_i, f_r * b_i + f_i * b_r

    def powers(base_r, base_i, count):
        out_r, out_i = [base_r], [base_i]
        for _ in range(count - 1):
            r, i = out_r[-1], out_i[-1]
            out_r.append(r * base_r - i * base_i)
            out_i.append(r * base_i + i * base_r)
        return out_r, out_i

    pw_r, pw_i = powers(bar_r, bar_i, T)
    pw_r = jnp.stack([jnp.ones_like(bar_r)] + pw_r)
    pw_i = jnp.stack([jnp.zeros_like(bar_i)] + pw_i)

    lb_r = pw_r[:T, None, :] * bb_r - pw_i[:T, None, :] * bb_i
    lb_i = pw_r[:T, None, :] * bb_i + pw_i[:T, None, :] * bb_r
    lb_t = jnp.stack([lb_r, lb_i], axis=1)
    c_r, c_i = (c.astype(F32).transpose(2, 0, 1).reshape(P, G * H) for c in (c_re, c_im))
    per_out = lambda pw: jnp.repeat(pw.reshape(T + 1, G, P).transpose(0, 2, 1), H, axis=2)
    pwo_r, pwo_i = per_out(pw_r), per_out(pw_i)
    cl_t = jnp.stack([c_r * pwo_r - c_i * pwo_i, -(c_r * pwo_i + c_i * pwo_r)], axis=1)

    a_r, a_i = powers(pw_r[T], pw_i[T], F32_SUBLANES)
    tiled = lambda a: jnp.stack(a).reshape(len(a), SSM_TILES, STATE_COLS)
    a_lin = jnp.stack([tiled(a_r), tiled(a_i)]).transpose(2, 0, 1, 3)
    shifts = [1 << k for k in range(F32_SUBLANES.bit_length() - 1)]
    a_log = jnp.stack([tiled([a_r[s - 1] for s in shifts]),
                       tiled([a_i[s - 1] for s in shifts])]).transpose(2, 1, 0, 3)
    row = jnp.arange(F32_SUBLANES)[None, None, None, :, None]
    keep = row >= jnp.asarray(shifts)[None, :, None, None, None]
    a_log = jnp.where(keep, a_log[:, :, :, None, :], 0.0)
    return lb_t, cl_t, a_log, a_lin


def _ssm_kernel(u_ref, lb_ref, cl_ref, alog_ref, alin_ref, d_ref, y_ref, m_ref, ws_ref, wo_ref,
                work_ref, prev_ref, *, seq_len, n_seq):
    T, H, P = SSM_CHUNK, SSM_GROUP_DIM, SSM_STATE
    n_chunks = seq_len // T
    iota = lambda shape, d: lax.broadcasted_iota(jnp.int32, shape, d)
    h_bits, p_bits = H.bit_length() - 1, P.bit_length() - 1

    @pl.when(pl.program_id(1) == 0)
    def _():
        same_hp = (iota((LANES, STATE_COLS), 0) >> h_bits) == (iota((LANES, STATE_COLS), 1) >> p_bits)
        same_ph = (iota((STATE_COLS, LANES), 0) >> p_bits) == (iota((STATE_COLS, LANES), 1) >> h_bits)

        def expand(coeff, same):
            full = jnp.concatenate([coeff] * GROUPS_PER_TILE, axis=0)
            return jnp.where(same, full, 0.0).astype(BF16)

        rows = lambda j: slice(j * LANES, (j + 1) * LANES)
        state_out = lambda d: jnp.concatenate(
            [expand(cl_ref[d, ri], same_ph) for ri in range(2)], axis=0)
        for j in range(T):
            ws_ref[rows(j), :] = jnp.concatenate(
                [expand(lb_ref[T - 1 - j, ri], same_hp) for ri in range(2)], axis=1)
        for i in range(T):
            wo_ref[:, rows(i)] = state_out(i + 1)
        c_out = state_out(0)
        for d in range(T):
            block = jnp.dot(ws_ref[rows(T - 1 - d), :], c_out,
                            preferred_element_type=F32).astype(BF16)
            for j in range(T - d):
                m_ref[rows(j), rows(j + d)] = block
        for j in range(T):
            for i in range(j):
                m_ref[rows(j), rows(i)] = jnp.zeros((LANES, LANES), BF16)

    pieces = [u_ref[pl.ds(b * seq_len + i, n_chunks, stride=T), :]
              for b in range(n_seq) for i in range(T)]
    x = jnp.concatenate(
        [jnp.concatenate(pieces[b * T:(b + 1) * T], axis=1) for b in range(n_seq)], axis=0)
    x_lo = x.astype(BF16)
    groups = [slice(c, c + MXU_COLS) for c in range(0, STATE_COLS, MXU_COLS)]
    im_of = lambda cols: slice(STATE_COLS + cols.start, STATE_COLS + cols.stop)
    traced = lambda k: k + jnp.minimum(pl.program_id(1), 0)
    n_groups = len(groups)
    y_region = 2 * n_groups

    def increment(g, part):
        cols = groups[g] if part == 0 else im_of(groups[g])
        work_ref[traced(2 * g + part)] = jnp.dot(x_lo, ws_ref[:, cols], preferred_element_type=F32)

    def intra_chunk(n):
        hi = (n + 1) * MXU_COLS
        work_ref[traced(y_region + n)] = jnp.dot(x_lo[:, :hi], m_ref[:hi, hi - MXU_COLS:hi],
                                                 preferred_element_type=F32)

    sub = F32_SUBLANES
    first_row = iota((sub, MXU_COLS), 0) == 0
    cmul = lambda a_re, a_im, b_re, b_im: (a_re * b_re - a_im * b_im, a_re * b_im + a_im * b_re)
    tiles_per_seq = n_chunks // sub
    pack = BF16_SUBLANES // sub

    def scan_items(g):
        cols = groups[g]
        log_mul = [(alog_ref[0, k, 0, :, cols], alog_ref[0, k, 1, :, cols])
                   for k in range(sub.bit_length() - 1)]
        lin_mul = (alin_ref[0, 0, :, cols], alin_ref[0, 1, :, cols])

        def sequence(b):
            carry = (jnp.zeros((1, MXU_COLS), F32),) * 2
            pending = []
            for t in range(tiles_per_seq):
                rows = pl.ds((b * tiles_per_seq + t) * sub, sub)
                t_re = work_ref[traced(2 * g), rows, :]
                t_im = work_ref[traced(2 * g + 1), rows, :]
                for k, mul in enumerate(log_mul):
                    d_re, d_im = cmul(*mul, pltpu.roll(t_re, 1 << k, axis=0),
                                      pltpu.roll(t_im, 1 << k, axis=0))
                    t_re, t_im = t_re + d_re, t_im + d_im
                if t > 0:
                    d_re, d_im = cmul(*lin_mul, *carry)
                    t_re, t_im = t_re + d_re, t_im + d_im
                pending.append((jnp.where(first_row, carry[0], pltpu.roll(t_re, 1, axis=0)),
                                jnp.where(first_row, carry[1], pltpu.roll(t_im, 1, axis=0))))
                carry = (t_re[sub - 1:sub], t_im[sub - 1:sub])
                if len(pending) == pack:
                    out = pl.ds((b * tiles_per_seq + t + 1 - pack) * sub, pack * sub)
                    for part, at in enumerate((cols, im_of(cols))):
                        prev_ref[out, at] = jnp.concatenate(
                            [p[part] for p in pending], axis=0).astype(BF16)
                    pending = []
        return [functools.partial(sequence, b) for b in range(n_seq)]

    def alternate(*item_lists):
        for group in zip(*item_lists):
            for item in group:
                item()
        for items in item_lists:
            for item in items[min(map(len, item_lists)):]:
                item()

    increment(0, 0)
    increment(0, 1)
    matmuls = ([functools.partial(increment, g, part) for g in range(1, n_groups)
                for part in range(2)]
               + [functools.partial(intra_chunk, n) for n in range(T * LANES // MXU_COLS)])
    per_scan = -(-len(matmuls) // n_groups)
    for g in range(n_groups):
        alternate(scan_items(g), matmuls[g * per_scan:(g + 1) * per_scan])
    per_tile = MXU_COLS // LANES
    for n in range(T * LANES // MXU_COLS):
        cols = slice(n * MXU_COLS, (n + 1) * MXU_COLS)
        y = (work_ref[y_region + n] + x[:, cols] * jnp.concatenate([d_ref[...]] * per_tile, axis=1)
             + jnp.dot(prev_ref[...], wo_ref[:, cols], preferred_element_type=F32))
        for b in range(n_seq):
            for i in range(per_tile):
                y_ref[pl.ds(b * seq_len + n * per_tile + i, n_chunks, stride=T), :] = (
                    y[b * n_chunks:(b + 1) * n_chunks, i * LANES:(i + 1) * LANES])


def _s5_ssm(u, tables, d_skip, seq_len):
    lb_t, cl_t, a_log, a_lin = tables
    T, H, P = SSM_CHUNK, SSM_GROUP_DIM, SSM_STATE
    assert seq_len % (T * F32_SUBLANES) == 0
    bsz = u.shape[0] // seq_len
    n_seq = max(n for n in range(1, SSM_SEQS_PER_STEP + 1) if bsz % n == 0)
    io_spec = pl.BlockSpec((n_seq * seq_len, LANES), lambda q, b: (b, q))
    square = pltpu.VMEM((T * LANES, T * LANES), BF16)
    n_groups = STATE_COLS // MXU_COLS
    chunk_rows = n_seq * seq_len // T
    return pl.pallas_call(
        functools.partial(_ssm_kernel, seq_len=seq_len, n_seq=n_seq),
        grid=(SSM_TILES, bsz // n_seq),
        in_specs=[io_spec,
                  pl.BlockSpec((T, 2, H, STATE_COLS), lambda q, b: (0, 0, 0, q)),
                  pl.BlockSpec((T + 1, 2, P, LANES), lambda q, b: (0, 0, 0, q)),
                  pl.BlockSpec((1,) + a_log.shape[1:], lambda q, b: (q, 0, 0, 0, 0)),
                  pl.BlockSpec((1,) + a_lin.shape[1:], lambda q, b: (q, 0, 0, 0)),
                  pl.BlockSpec((1, LANES), lambda q, b: (0, q))],
        out_specs=io_spec,
        out_shape=jax.ShapeDtypeStruct(u.shape, F32),
        scratch_shapes=[square, square, square,
                        pltpu.VMEM((2 * n_groups + T * LANES // MXU_COLS, chunk_rows, MXU_COLS), F32),
                        pltpu.VMEM((chunk_rows, 2 * STATE_COLS), BF16)],
        compiler_params=pltpu.CompilerParams(dimension_semantics=("arbitrary", "arbitrary"),
                                             vmem_limit_bytes=VMEM_LIMIT_BYTES),
        name="s5_ssm",
    )(u, lb_t, cl_t, a_log, a_lin, d_skip)


def _out_proj_kernel(x_ref, ma_ref, y_ref, zs_ref, wg32_ref, bg_ref, w32_ref, g_ref, o_ref,
                     wg_ref, w_ref, *, final_norm):
    @pl.when(pl.program_id(0) == 0)
    def _():
        wg_ref[...] = wg32_ref[...].astype(BF16)
        w_ref[...] = w32_ref[...].astype(BF16)

    half_plus = lambda t: 1.0 + jnp.tanh(0.5 * t)
    for c in range(0, x_ref.shape[0], OUT_PROJ_ROWS):
        rows = slice(c, c + OUT_PROJ_ROWS)
        y = y_ref[rows, :]
        y = y * (1.0 + jnp.tanh(math.sqrt(2.0 / math.pi) * (y + 0.044715 * (y * y * y))))
        gate = jnp.dot((0.5 * y).astype(BF16), wg_ref[...], preferred_element_type=F32) + bg_ref[...]
        z = zs_ref[rows, :].astype(F32)
        mixed_ssm = (0.125 * y) * half_plus(gate) * (z * half_plus(z))
        mixed = jnp.concatenate([ma_ref[rows, :], mixed_ssm.astype(BF16)], axis=1)
        r = x_ref[rows, :] + jnp.dot(mixed, w_ref[...], preferred_element_type=F32)
        if final_norm:
            ms = jnp.mean(r * r, axis=-1, keepdims=True)
            r = r * lax.rsqrt(ms + NORM_EPS) * g_ref[...]
        o_ref[rows, :] = r


def _out_proj(x2, mixed_attn, y_ssm, z_ssm, w_glu, b_glu, w_out, gain, tm, final_norm):
    rows = x2.shape[0]
    row_spec = lambda width: pl.BlockSpec((tm, width), lambda i: (i, 0))
    return pl.pallas_call(
        functools.partial(_out_proj_kernel, final_norm=final_norm),
        grid=(rows // tm,),
        in_specs=[row_spec(D_MODEL), row_spec(ATTN_WIDTH), row_spec(SSM_WIDTH), row_spec(SSM_WIDTH),
                  _resident((SSM_WIDTH, SSM_WIDTH)), _resident((1, SSM_WIDTH)),
                  _resident((ATTN_WIDTH + SSM_WIDTH, D_MODEL)), _resident((1, D_MODEL))],
        out_specs=row_spec(D_MODEL),
        out_shape=jax.ShapeDtypeStruct((rows, D_MODEL), F32),
        scratch_shapes=[pltpu.VMEM((SSM_WIDTH, SSM_WIDTH), BF16),
                        pltpu.VMEM((ATTN_WIDTH + SSM_WIDTH, D_MODEL), BF16)],
        compiler_params=pltpu.CompilerParams(dimension_semantics=("arbitrary",),
                                             vmem_limit_bytes=VMEM_LIMIT_BYTES),
        name="out_proj",
    )(x2, mixed_attn, y_ssm, z_ssm, w_glu, b_glu, w_out, gain)


def _rotary_tables(seq_len):
    assert seq_len % ROTARY_SPLIT == 0
    half = HEAD_DIM // 2
    lane = jnp.arange(LANES)
    inv_freq = 1.0 / (ROPE_THETA ** ((lane % half).astype(F32) / half))
    hi = jnp.arange(0, seq_len, ROTARY_SPLIT, dtype=F32)[:, None, None] * inv_freq
    lo = jnp.arange(ROTARY_SPLIT, dtype=F32)[None, :, None] * inv_freq
    sign = jnp.where(lane % HEAD_DIM >= half, 1.0, -1.0).astype(F32)
    sin_hi, sin_lo = sign * jnp.sin(hi), sign * jnp.sin(lo)
    cos = (jnp.cos(hi) * jnp.cos(lo) - jnp.sin(hi) * jnp.sin(lo)).reshape(seq_len, LANES)
    sin = (sin_hi * jnp.cos(lo) + jnp.cos(hi) * sin_lo).reshape(seq_len, LANES)
    return cos, sin


def kernel(x, norm_gain, w_in, w_out, lam_re, lam_im, b_re, b_im, c_re, c_im,
           d_skip, log_dt, w_glu, b_glu, final_gain):
    bsz, seq_len, _ = x.shape
    depth = norm_gain.shape[0]
    assert seq_len % MOBA_BLOCK == 0 and seq_len // MOBA_BLOCK > MOBA_TOP_K
    assert seq_len % SSM_CHUNK == 0
    tm = min(PROJ_TILE_ROWS, seq_len)
    cos, sin = _rotary_tables(seq_len)

    x2 = x.reshape(bsz * seq_len, D_MODEL)
    for layer in range(depth):
        tables = _ssm_tables(lam_re[layer], lam_im[layer], b_re[layer], b_im[layer],
                             c_re[layer], c_im[layer], log_dt[layer])
        q, k, v, z_attn, u, z_ssm = _in_proj(
            x2, norm_gain[layer].reshape(1, D_MODEL), w_in[layer], cos, sin, seq_len, tm)
        to_seq = lambda t: t.reshape(bsz, seq_len, ATTN_WIDTH)
        mixed_attn = _moba_attention(to_seq(q), to_seq(k), to_seq(v), to_seq(z_attn))
        y_ssm = _s5_ssm(u, tables, d_skip[layer].reshape(1, SSM_WIDTH), seq_len)
        x2 = _out_proj(x2, mixed_attn.reshape(bsz * seq_len, ATTN_WIDTH), y_ssm, z_ssm,
                       w_glu[layer], b_glu[layer].reshape(1, SSM_WIDTH),
                       w_out[layer], final_gain.reshape(1, D_MODEL), tm,
                       final_norm=layer == depth - 1)
    return x2.reshape(bsz, seq_len, D_MODEL)
```

```python
import functools
import math

import jax
import jax.numpy as jnp
from jax import lax
from jax.experimental import pallas as pl
from jax.experimental.pallas import tpu as pltpu

F32 = jnp.float32
BF16 = jnp.bfloat16

D_MODEL = 1024
HEAD_DIM = 64
ATTN_HEADS = 8
ATTN_WIDTH = ATTN_HEADS * HEAD_DIM
MOBA_BLOCK = 256
MOBA_TOP_K = 3
ROPE_THETA = 10000.0
ROTARY_SPLIT = 64
SSM_GROUP_DIM = 16
SSM_GROUPS = 32
SSM_WIDTH = SSM_GROUPS * SSM_GROUP_DIM
SSM_STATE = 64
IN_PROJ_WIDTH = 4 * ATTN_WIDTH + 2 * SSM_WIDTH
NORM_EPS = 1e-6
NEG_INF = -1e30

LANES = 128
SSM_CHUNK = 8
GROUPS_PER_TILE = LANES // SSM_GROUP_DIM
SSM_TILES = SSM_WIDTH // LANES
TILE_COLS = SSM_CHUNK * LANES
STATE_COLS = GROUPS_PER_TILE * SSM_STATE
VMEM_LIMIT_BYTES = 56 * 1024 * 1024
MXU_COLS = 256
OUT_PROJ_ROWS = 256
PROJ_TILE_ROWS = 1024
SSM_SEQS_PER_STEP = 4
F32_SUBLANES = 8
BF16_SUBLANES = 16
VALUE_ROWS = HEAD_DIM + BF16_SUBLANES
QK_SCALE = math.log2(math.e) / math.sqrt(HEAD_DIM)


def _resident(shape):
    zeros = (0,) * len(shape)
    return pl.BlockSpec(shape, lambda *_: zeros, pipeline_mode=pl.Buffered(1))


def _in_proj_kernel(x_ref, g_ref, w32_ref, cos_ref, sin_ref,
                    q_ref, k_ref, v_ref, za_ref, u_ref, zs_ref, w_ref):
    @pl.when(pl.program_id(0) == 0)
    def _():
        for c in range(0, IN_PROJ_WIDTH, ATTN_WIDTH):
            w_ref[:, c:c + ATTN_WIDTH] = w32_ref[:, c:c + ATTN_WIDTH].astype(BF16)

    x = x_ref[...]
    ms = jnp.mean(x * x, axis=-1, keepdims=True)
    h = (x * lax.rsqrt(ms + NORM_EPS) * g_ref[...]).astype(BF16)

    def section(idx):
        return jnp.dot(h, w_ref[:, idx * ATTN_WIDTH:(idx + 1) * ATTN_WIDTH],
                       preferred_element_type=F32)

    cos = cos_ref[...]
    sin = sin_ref[...]
    lane = lax.broadcasted_iota(jnp.int32, cos.shape, 1)
    upper = (lane & (HEAD_DIM - 1)) >= HEAD_DIM // 2

    def rotary(t):
        partner = jnp.where(upper, pltpu.roll(t, HEAD_DIM // 2, axis=1),
                            pltpu.roll(t, LANES - HEAD_DIM // 2, axis=1))
        return t * cos + partner * sin

    q = section(0)
    k = section(1)
    for c in range(ATTN_WIDTH // LANES):
        sl = slice(c * LANES, (c + 1) * LANES)
        q_ref[:, sl] = (rotary(q[:, sl]) * QK_SCALE).astype(BF16)
        k_ref[:, sl] = rotary(k[:, sl]).astype(BF16)
    v_ref[...] = section(2).astype(BF16)
    za_ref[...] = section(3).astype(BF16)
    u_ref[...] = section(4)
    zs_ref[...] = section(5).astype(BF16)


def _in_proj(x2, gain, w_in, cos, sin, seq_len, tm):
    rows = x2.shape[0]
    pos_blocks = seq_len // tm
    row_spec = lambda width: pl.BlockSpec((tm, width), lambda i: (i, 0))
    tab_spec = pl.BlockSpec((tm, LANES), lambda i: (i % pos_blocks, 0))
    out_bf16 = jax.ShapeDtypeStruct((rows, ATTN_WIDTH), BF16)
    out_f32 = jax.ShapeDtypeStruct((rows, ATTN_WIDTH), F32)
    return pl.pallas_call(
        _in_proj_kernel,
        grid=(rows // tm,),
        in_specs=[row_spec(D_MODEL), _resident((1, D_MODEL)), _resident((D_MODEL, IN_PROJ_WIDTH)),
                  tab_spec, tab_spec],
        out_specs=[row_spec(ATTN_WIDTH)] * 6,
        out_shape=[out_bf16, out_bf16, out_bf16, out_bf16, out_f32, out_bf16],
        scratch_shapes=[pltpu.VMEM((D_MODEL, IN_PROJ_WIDTH), BF16)],
        compiler_params=pltpu.CompilerParams(dimension_semantics=("arbitrary",),
                                             vmem_limit_bytes=VMEM_LIMIT_BYTES),
        name="in_proj",
    )(x2, gain, w_in, cos, sin)


def _attn_kernel(q0_ref, k0_ref, v0_ref, qn_ref, kn_ref, vn_ref, z_ref, o_ref,
                 kx_ref, vx_ref, qx_ref, sa_ref, sb_ref, m_ref, acc_ref, *, n_blocks):
    unit = pl.program_id(0)
    blk = MOBA_BLOCK
    heads = LANES // HEAD_DIM
    n_pairs = n_blocks // 2
    cur = lax.rem(unit, 2)
    nxt = 1 - cur

    head_rows = lambda h: slice(h * HEAD_DIM, (h + 1) * HEAD_DIM)
    spare_base = lambda h: ((h + 1) % heads) * HEAD_DIM
    iota = lambda shape, d: lax.broadcasted_iota(jnp.int32, shape, d)
    key = iota((blk, blk), 0)
    query = iota((blk, blk), 1)
    col_max = lambda s: jnp.max(s, axis=0, keepdims=True)

    def prepare_items(q_ref, k_ref, v_ref, slot):
        kmean = {}

        def key_block(j):
            lane = iota((blk, LANES), 1)
            sum_rows = (iota((VALUE_ROWS - HEAD_DIM, blk), 0) == 0).astype(F32)
            rows = slice(j * blk, (j + 1) * blk)
            kj = k_ref[0, rows, :]
            kmean[j] = jnp.mean(kj.astype(F32), axis=0, keepdims=True)
            vt = v_ref[0, rows, :].astype(F32).T
            for h in range(heads):
                in_head = (lane >= h * HEAD_DIM) & (lane < (h + 1) * HEAD_DIM)
                tag = jnp.where(lane == spare_base(h) + j, 1.0, 0.0).astype(BF16)
                kx_ref[slot, h, j] = jnp.where(in_head, kj, tag)
                vx_ref[slot, j, h] = jnp.concatenate([vt[head_rows(h)], sum_rows],
                                                     axis=0).astype(BF16)

        def query_block(qi):
            if "all" not in kmean:
                kmean["all"] = jnp.concatenate([kmean[j] for j in range(n_blocks)],
                                               axis=0).astype(BF16)
            blk_row = iota((n_blocks, blk), 0)
            zero_rows = lambda n: jnp.zeros((n, blk), F32)
            in_head_order = lambda h, own, other: [own, other] if h == 0 else [other, own]
            qt = q_ref[0, qi * blk:(qi + 1) * blk, :].astype(F32).T
            past = blk_row < qi
            for h in range(heads):
                q_rows = qt[head_rows(h)]
                q_only = jnp.concatenate(in_head_order(h, q_rows, zero_rows(HEAD_DIM)), axis=0)
                gate = jnp.dot(kmean["all"], q_only.astype(BF16), preferred_element_type=F32)
                gate = jnp.where(past, gate, NEG_INF)
                beaten_by = jnp.zeros((n_blocks, blk), jnp.int32)
                for j in range(n_blocks):
                    gj = gate[j:j + 1, :]
                    wins = (gj > gate) | ((gj == gate) & (j < blk_row))
                    beaten_by = beaten_by + wins.astype(jnp.int32)
                keep = (past & (beaten_by < MOBA_TOP_K)) | (blk_row == qi)
                bias = jnp.where(keep, 0.0, NEG_INF)
                spare = jnp.concatenate([bias, zero_rows(HEAD_DIM - n_blocks)], axis=0)
                qx_ref[slot, qi, h] = jnp.concatenate(in_head_order(h, q_rows, spare),
                                                      axis=0).astype(BF16)

        return ([functools.partial(key_block, j) for j in range(n_blocks)]
                + [functools.partial(query_block, qi) for qi in range(n_blocks)])

    def visible_blocks(pair):
        tiles = (pair, n_blocks - 1 - pair)
        return ([(t, tiles[t], True) for t in range(2)]
                + [(t, j, False) for t in range(2) for j in range(tiles[t])])

    traced = lambda t: t + jnp.minimum(unit, 0)

    def stage_items(pair, s_ref, m, slot):
        def block(t, j, own, h):
            qi = (pair, n_blocks - 1 - pair)[t]
            s = jnp.dot(kx_ref[slot, h, j], qx_ref[slot, qi, h], preferred_element_type=F32)
            if own:
                s = jnp.where(key <= query, s, NEG_INF)
            s_ref[traced(t), h, j] = s
            m[t, h] = col_max(s) if own else jnp.maximum(m[t, h], col_max(s))
        return [functools.partial(block, *b, h) for b in visible_blocks(pair)
                for h in range(heads)]

    def finish_items(pair, s_ref, m, acc_ref, slot):
        def block(t, j, own, h):
            p = jnp.exp2(s_ref[traced(t), h, j] - m[t, h]).astype(BF16)
            pv = jnp.dot(vx_ref[slot, j, h], p, preferred_element_type=F32)
            if own:
                acc_ref[traced(t), h] = pv
            else:
                acc_ref[traced(t), h] += pv

        def write_rows():
            for t, qi in enumerate((pair, n_blocks - 1 - pair)):
                rows = slice(qi * blk, (qi + 1) * blk)
                acc = [acc_ref[t, h] for h in range(heads)]
                o_t = jnp.concatenate([a[:HEAD_DIM] / a[HEAD_DIM:HEAD_DIM + 1] for a in acc],
                                      axis=0)
                z = z_ref[0, rows, :].astype(F32)
                silu = 0.5 * z * (1.0 + jnp.tanh(0.5 * z))
                o_ref[0, rows, :] = (o_t.T * silu).astype(BF16)

        return ([functools.partial(block, *b, h) for b in visible_blocks(pair)
                 for h in range(heads)] + [write_rows])

    def alternate(*item_lists):
        for group in zip(*item_lists):
            for item in group:
                item()
        for items in item_lists:
            for item in items[min(map(len, item_lists)):]:
                item()

    m_keys = [(t, h) for t in range(2) for h in range(heads)]

    @pl.when(unit == 0)
    def _():
        m0 = {}
        alternate(prepare_items(q0_ref, k0_ref, v0_ref, 0))
        alternate(stage_items(0, sa_ref, m0, 0))
        for t, h in m_keys:
            m_ref[t, h] = m0[t, h]

    m = {key_: m_ref[key_] for key_ in m_keys}
    for pair in range(n_pairs):
        s_ref, s_next = (sa_ref, sb_ref) if pair % 2 == 0 else (sb_ref, sa_ref)
        m_next = {}
        lists = [finish_items(pair, s_ref, m, acc_ref.at[pair % 2], cur)]
        if pair + 1 < n_pairs:
            lists.append(stage_items(pair + 1, s_next, m_next, cur))
        else:
            lists.append(stage_items(0, s_next, m_next, nxt))
        if pair == n_pairs - 2:
            lists.append(prepare_items(qn_ref, kn_ref, vn_ref, nxt))
        alternate(lists[1], lists[0], *lists[2:])
        m = m_next
    for t, h in m_keys:
        m_ref[t, h] = m[t, h]


def _moba_attention(q, k, v, z_attn):
    bsz, seq_len, _ = q.shape
    n_blocks = seq_len // MOBA_BLOCK
    assert n_blocks % 4 == 0 and n_blocks <= HEAD_DIM
    head_pairs = ATTN_WIDTH // LANES
    heads = LANES // HEAD_DIM
    assert heads == 2, "the block bias rides in the other head's half of the 128 lanes"
    n_units = bsz * head_pairs
    block = (1, seq_len, LANES)
    unit_index = lambda u: (u // head_pairs, 0, u % head_pairs)
    first_spec = pl.BlockSpec(block, lambda u: (0, 0, 0))
    next_spec = pl.BlockSpec(block, lambda u: unit_index(jnp.minimum(u + 1, n_units - 1)))
    unit_spec = pl.BlockSpec(block, unit_index)
    scores = pltpu.VMEM((2, heads, n_blocks, MOBA_BLOCK, MOBA_BLOCK), F32)
    return pl.pallas_call(
        functools.partial(_attn_kernel, n_blocks=n_blocks),
        grid=(n_units,),
        in_specs=[first_spec] * 3 + [next_spec] * 3 + [unit_spec],
        out_specs=unit_spec,
        out_shape=jax.ShapeDtypeStruct((bsz, seq_len, ATTN_WIDTH), BF16),
        scratch_shapes=[pltpu.VMEM((2, heads, n_blocks, MOBA_BLOCK, LANES), BF16),
                        pltpu.VMEM((2, n_blocks, heads, VALUE_ROWS, MOBA_BLOCK), BF16),
                        pltpu.VMEM((2, n_blocks, heads, LANES, MOBA_BLOCK), BF16),
                        scores, scores,
                        pltpu.VMEM((2, heads, 1, MOBA_BLOCK), F32),
                        pltpu.VMEM((2, 2, heads, VALUE_ROWS, MOBA_BLOCK), F32)],
        compiler_params=pltpu.CompilerParams(dimension_semantics=("arbitrary",),
                                             vmem_limit_bytes=VMEM_LIMIT_BYTES),
        name="moba_attn",
    )(q, k, v, q, k, v, z_attn)


def _ssm_tables(lam_re, lam_im, b_re, b_im, c_re, c_im, log_dt):
    T, G, P, H = SSM_CHUNK, SSM_GROUPS, SSM_STATE, SSM_GROUP_DIM
    dt = jnp.repeat(jnp.exp(log_dt.astype(F32)), P)
    lam_r, lam_i = lam_re.astype(F32).reshape(G * P), lam_im.astype(F32).reshape(G * P)
    mag = jnp.exp(lam_r * dt)
    bar_r, bar_i = mag * jnp.cos(lam_i * dt), mag * jnp.sin(lam_i * dt)
    den = lam_r * lam_r + lam_i * lam_i
    f_r = ((bar_r - 1.0) * lam_r + bar_i * lam_i) / den
    f_i = (bar_i * lam_r - (bar_r - 1.0) * lam_i) / den
    b_r, b_i = (b.astype(F32).transpose(2, 0, 1).reshape(H, G * P) for b in (b_re, b_im))
    bb_r, bb_i = f_r * b_r - f_i * b_i, f_r * b_i + f_i * b_r

    def powers(base_r, base_i, count):
        out_r, out_i = [base_r], [base_i]
        for _ in range(count - 1):
            r, i = out_r[-1], out_i[-1]
            out_r.append(r * base_r - i * base_i)
            out_i.append(r * base_i + i * base_r)
        return out_r, out_i

    pw_r, pw_i = powers(bar_r, bar_i, T)
    pw_r = jnp.stack([jnp.ones_like(bar_r)] + pw_r)
    pw_i = jnp.stack([jnp.zeros_like(bar_i)] + pw_i)

    lb_r = pw_r[:T, None, :] * bb_r - pw_i[:T, None, :] * bb_i
    lb_i = pw_r[:T, None, :] * bb_i + pw_i[:T, None, :] * bb_r
    lb_t = jnp.stack([lb_r, lb_i], axis=1).astype(BF16)
    c_r, c_i = (c.astype(F32).transpose(2, 0, 1).reshape(P, G * H) for c in (c_re, c_im))
    per_out = lambda pw: jnp.repeat(pw.reshape(T + 1, G, P).transpose(0, 2, 1), H, axis=2)
    pwo_r, pwo_i = per_out(pw_r), per_out(pw_i)
    cl_t = jnp.stack([c_r * pwo_r - c_i * pwo_i, -(c_r * pwo_i + c_i * pwo_r)], axis=1).astype(BF16)

    a_r, a_i = powers(pw_r[T], pw_i[T], F32_SUBLANES)
    tiled = lambda a: jnp.stack(a).reshape(len(a), SSM_TILES, STATE_COLS)
    a_lin = jnp.stack([tiled(a_r), tiled(a_i)]).transpose(2, 0, 1, 3)
    shifts = [1 << k for k in range(F32_SUBLANES.bit_length() - 1)]
    a_log = jnp.stack([tiled([a_r[s - 1] for s in shifts]),
                       tiled([a_i[s - 1] for s in shifts])]).transpose(2, 1, 0, 3)
    row = jnp.arange(F32_SUBLANES)[None, None, None, :, None]
    keep = row >= jnp.asarray(shifts)[None, :, None, None, None]
    a_log = jnp.where(keep, a_log[:, :, :, None, :], 0.0)
    return lb_t, cl_t, a_log, a_lin


def _ssm_kernel(u_ref, lb_ref, cl_ref, alog_ref, alin_ref, d_ref, y_ref, m_ref, ws_ref, wo_ref,
                work_ref, prev_ref, *, seq_len, n_seq):
    T, H, P = SSM_CHUNK, SSM_GROUP_DIM, SSM_STATE
    n_chunks = seq_len // T
    iota = lambda shape, d: lax.broadcasted_iota(jnp.int32, shape, d)
    h_bits, p_bits = H.bit_length() - 1, P.bit_length() - 1

    @pl.when(pl.program_id(1) == 0)
    def _():
        same_hp = (iota((LANES, STATE_COLS), 0) >> h_bits) == (iota((LANES, STATE_COLS), 1) >> p_bits)
        same_ph = (iota((STATE_COLS, LANES), 0) >> p_bits) == (iota((STATE_COLS, LANES), 1) >> h_bits)

        def expand(coeff, same):
            full = jnp.concatenate([coeff] * GROUPS_PER_TILE, axis=0)
            return jnp.where(same, full, jnp.zeros_like(full))

        rows = lambda j: slice(j * LANES, (j + 1) * LANES)
        state_out = lambda d: jnp.concatenate(
            [expand(cl_ref[d, ri], same_ph) for ri in range(2)], axis=0)
        for j in range(T):
            ws_ref[rows(j), :] = jnp.concatenate(
                [expand(lb_ref[T - 1 - j, ri], same_hp) for ri in range(2)], axis=1)
        for i in range(T):
            wo_ref[:, rows(i)] = state_out(i + 1)
        c_out = state_out(0)
        for d in range(T):
            block = jnp.dot(ws_ref[rows(T - 1 - d), :], c_out,
                            preferred_element_type=F32).astype(BF16)
            for j in range(T - d):
                m_ref[rows(j), rows(j + d)] = block
        for j in range(T):
            for i in range(j):
                m_ref[rows(j), rows(i)] = jnp.zeros((LANES, LANES), BF16)

    pieces = [u_ref[pl.ds(b * seq_len + i, n_chunks, stride=T), :]
              for b in range(n_seq) for i in range(T)]
    x = jnp.concatenate(
        [jnp.concatenate(pieces[b * T:(b + 1) * T], axis=1) for b in range(n_seq)], axis=0)
    x_lo = x.astype(BF16)
    groups = [slice(c, c + MXU_COLS) for c in range(0, STATE_COLS, MXU_COLS)]
    im_of = lambda cols: slice(STATE_COLS + cols.start, STATE_COLS + cols.stop)
    traced = lambda k: k + jnp.minimum(pl.program_id(1), 0)
    n_groups = len(groups)
    y_region = 2 * n_groups

    def increment(g, part):
        cols = groups[g] if part == 0 else im_of(groups[g])
        work_ref[traced(2 * g + part)] = jnp.dot(x_lo, ws_ref[:, cols], preferred_element_type=F32)

    def intra_chunk(n):
        hi = (n + 1) * MXU_COLS
        work_ref[traced(y_region + n)] = jnp.dot(x_lo[:, :hi], m_ref[:hi, hi - MXU_COLS:hi],
                                                 preferred_element_type=F32)

    sub = F32_SUBLANES
    first_row = iota((sub, MXU_COLS), 0) == 0
    cmul = lambda a_re, a_im, b_re, b_im: (a_re * b_re - a_im * b_im, a_re * b_im + a_im * b_re)
    tiles_per_seq = n_chunks // sub
    pack = BF16_SUBLANES // sub

    def scan_items(g):
        cols = groups[g]
        log_mul = [(alog_ref[0, k, 0, :, cols], alog_ref[0, k, 1, :, cols])
                   for k in range(sub.bit_length() - 1)]
        lin_mul = (alin_ref[0, 0, :, cols], alin_ref[0, 1, :, cols])

        def sequence(b):
            carry = (jnp.zeros((1, MXU_COLS), F32),) * 2
            pending = []
            for t in range(tiles_per_seq):
                rows = pl.ds((b * tiles_per_seq + t) * sub, sub)
                t_re = work_ref[traced(2 * g), rows, :]
                t_im = work_ref[traced(2 * g + 1), rows, :]
                for k, mul in enumerate(log_mul):
                    d_re, d_im = cmul(*mul, pltpu.roll(t_re, 1 << k, axis=0),
                                      pltpu.roll(t_im, 1 << k, axis=0))
                    t_re, t_im = t_re + d_re, t_im + d_im
                if t > 0:
                    d_re, d_im = cmul(*lin_mul, *carry)
                    t_re, t_im = t_re + d_re, t_im + d_im
                pending.append((jnp.where(first_row, carry[0], pltpu.roll(t_re, 1, axis=0)),
                                jnp.where(first_row, carry[1], pltpu.roll(t_im, 1, axis=0))))
                carry = (t_re[sub - 1:sub], t_im[sub - 1:sub])
                if len(pending) == pack:
                    out = pl.ds((b * tiles_per_seq + t + 1 - pack) * sub, pack * sub)
                    for part, at in enumerate((cols, im_of(cols))):
                        prev_ref[out, at] = jnp.concatenate(
                            [p[part] for p in pending], axis=0).astype(BF16)
                    pending = []
        return [functools.partial(sequence, b) for b in range(n_seq)]

    def alternate(*item_lists):
        for group in zip(*item_lists):
            for item in group:
                item()
        for items in item_lists:
            for item in items[min(map(len, item_lists)):]:
                item()

    increment(0, 0)
    increment(0, 1)
    matmuls = ([functools.partial(increment, g, part) for g in range(1, n_groups)
                for part in range(2)]
               + [functools.partial(intra_chunk, n) for n in range(T * LANES // MXU_COLS)])
    per_scan = -(-len(matmuls) // n_groups)
    for g in range(n_groups):
        alternate(scan_items(g), matmuls[g * per_scan:(g + 1) * per_scan])
    per_tile = MXU_COLS // LANES
    for n in range(T * LANES // MXU_COLS):
        cols = slice(n * MXU_COLS, (n + 1) * MXU_COLS)
        y = (work_ref[y_region + n] + x[:, cols] * jnp.concatenate([d_ref[...]] * per_tile, axis=1)
             + jnp.dot(prev_ref[...], wo_ref[:, cols], preferred_element_type=F32))
        for b in range(n_seq):
            for i in range(per_tile):
                y_ref[pl.ds(b * seq_len + n * per_tile + i, n_chunks, stride=T), :] = (
                    y[b * n_chunks:(b + 1) * n_chunks, i * LANES:(i + 1) * LANES])


def _s5_ssm(u, tables, d_skip, seq_len):
    lb_t, cl_t, a_log, a_lin = tables
    T, H, P = SSM_CHUNK, SSM_GROUP_DIM, SSM_STATE
    assert seq_len % (T * F32_SUBLANES) == 0
    bsz = u.shape[0] // seq_len
    n_seq = max(n for n in range(1, SSM_SEQS_PER_STEP + 1) if bsz % n == 0)
    io_spec = pl.BlockSpec((n_seq * seq_len, LANES), lambda q, b: (b, q))
    square = pltpu.VMEM((T * LANES, T * LANES), BF16)
    n_groups = STATE_COLS // MXU_COLS
    chunk_rows = n_seq * seq_len // T
    return pl.pallas_call(
        functools.partial(_ssm_kernel, seq_len=seq_len, n_seq=n_seq),
        grid=(SSM_TILES, bsz // n_seq),
        in_specs=[io_spec,
                  pl.BlockSpec((T, 2, H, STATE_COLS), lambda q, b: (0, 0, 0, q)),
                  pl.BlockSpec((T + 1, 2, P, LANES), lambda q, b: (0, 0, 0, q)),
                  pl.BlockSpec((1,) + a_log.shape[1:], lambda q, b: (q, 0, 0, 0, 0)),
                  pl.BlockSpec((1,) + a_lin.shape[1:], lambda q, b: (q, 0, 0, 0)),
                  pl.BlockSpec((1, LANES), lambda q, b: (0, q))],
        out_specs=io_spec,
        out_shape=jax.ShapeDtypeStruct(u.shape, F32),
        scratch_shapes=[square, square, square,
                        pltpu.VMEM((2 * n_groups + T * LANES // MXU_COLS, chunk_rows, MXU_COLS), F32),
                        pltpu.VMEM((chunk_rows, 2 * STATE_COLS), BF16)],
        compiler_params=pltpu.CompilerParams(dimension_semantics=("arbitrary", "arbitrary"),
                                             vmem_limit_bytes=VMEM_LIMIT_BYTES),
        name="s5_ssm",
    )(u, lb_t, cl_t, a_log, a_lin, d_skip)


def _out_proj_kernel(x_ref, ma_ref, y_ref, zs_ref, wg32_ref, bg_ref, w32_ref, g_ref, o_ref,
                     wg_ref, w_ref, *, final_norm):
    @pl.when(pl.program_id(0) == 0)
    def _():
        wg_ref[...] = wg32_ref[...].astype(BF16)
        w_ref[...] = w32_ref[...].astype(BF16)

    half_plus = lambda t: 1.0 + jnp.tanh(0.5 * t)
    for c in range(0, x_ref.shape[0], OUT_PROJ_ROWS):
        rows = slice(c, c + OUT_PROJ_ROWS)
        y = y_ref[rows, :]
        y = y * (1.0 + jnp.tanh(math.sqrt(2.0 / math.pi) * (y + 0.044715 * (y * y * y))))
        gate = jnp.dot((0.5 * y).astype(BF16), wg_ref[...], preferred_element_type=F32) + bg_ref[...]
        z = zs_ref[rows, :].astype(F32)
        mixed_ssm = (0.125 * y) * half_plus(gate) * (z * half_plus(z))
        mixed = jnp.concatenate([ma_ref[rows, :], mixed_ssm.astype(BF16)], axis=1)
        r = x_ref[rows, :] + jnp.dot(mixed, w_ref[...], preferred_element_type=F32)
        if final_norm:
            ms = jnp.mean(r * r, axis=-1, keepdims=True)
            r = r * lax.rsqrt(ms + NORM_EPS) * g_ref[...]
        o_ref[rows, :] = r


def _out_proj(x2, mixed_attn, y_ssm, z_ssm, w_glu, b_glu, w_out, gain, tm, final_norm):
    rows = x2.shape[0]
    row_spec = lambda width: pl.BlockSpec((tm, width), lambda i: (i, 0))
    return pl.pallas_call(
        functools.partial(_out_proj_kernel, final_norm=final_norm),
        grid=(rows // tm,),
        in_specs=[row_spec(D_MODEL), row_spec(ATTN_WIDTH), row_spec(SSM_WIDTH), row_spec(SSM_WIDTH),
                  _resident((SSM_WIDTH, SSM_WIDTH)), _resident((1, SSM_WIDTH)),
                  _resident((ATTN_WIDTH + SSM_WIDTH, D_MODEL)), _resident((1, D_MODEL))],
        out_specs=row_spec(D_MODEL),
        out_shape=jax.ShapeDtypeStruct((rows, D_MODEL), F32),
        scratch_shapes=[pltpu.VMEM((SSM_WIDTH, SSM_WIDTH), BF16),
                        pltpu.VMEM((ATTN_WIDTH + SSM_WIDTH, D_MODEL), BF16)],
        compiler_params=pltpu.CompilerParams(dimension_semantics=("arbitrary",),
                                             vmem_limit_bytes=VMEM_LIMIT_BYTES),
        name="out_proj",
    )(x2, mixed_attn, y_ssm, z_ssm, w_glu, b_glu, w_out, gain)


def _rotary_tables(seq_len):
    assert seq_len % ROTARY_SPLIT == 0
    half = HEAD_DIM // 2
    lane = jnp.arange(LANES)
    inv_freq = 1.0 / (ROPE_THETA ** ((lane % half).astype(F32) / half))
    hi = jnp.arange(0, seq_len, ROTARY_SPLIT, dtype=F32)[:, None, None] * inv_freq
    lo = jnp.arange(ROTARY_SPLIT, dtype=F32)[None, :, None] * inv_freq
    sign = jnp.where(lane % HEAD_DIM >= half, 1.0, -1.0).astype(F32)
    sin_hi, sin_lo = sign * jnp.sin(hi), sign * jnp.sin(lo)
    cos = (jnp.cos(hi) * jnp.cos(lo) - jnp.sin(hi) * jnp.sin(lo)).reshape(seq_len, LANES)
    sin = (sin_hi * jnp.cos(lo) + jnp.cos(hi) * sin_lo).reshape(seq_len, LANES)
    return cos, sin


def kernel(x, norm_gain, w_in, w_out, lam_re, lam_im, b_re, b_im, c_re, c_im,
           d_skip, log_dt, w_glu, b_glu, final_gain):
    bsz, seq_len, _ = x.shape
    depth = norm_gain.shape[0]
    assert seq_len % MOBA_BLOCK == 0 and seq_len // MOBA_BLOCK > MOBA_TOP_K
    assert seq_len % SSM_CHUNK == 0
    tm = min(PROJ_TILE_ROWS, seq_len)
    cos, sin = _rotary_tables(seq_len)

    x2 = x.reshape(bsz * seq_len, D_MODEL)
    for layer in range(depth):
        tables = _ssm_tables(lam_re[layer], lam_im[layer], b_re[layer], b_im[layer],
                             c_re[layer], c_im[layer], log_dt[layer])
        q, k, v, z_attn, u, z_ssm = _in_proj(
            x2, norm_gain[layer].reshape(1, D_MODEL), w_in[layer], cos, sin, seq_len, tm)
        to_seq = lambda t: t.reshape(bsz, seq_len, ATTN_WIDTH)
        mixed_attn = _moba_attention(to_seq(q), to_seq(k), to_seq(v), to_seq(z_attn))
        y_ssm = _s5_ssm(u, tables, d_skip[layer].reshape(1, SSM_WIDTH), seq_len)
        x2 = _out_proj(x2, mixed_attn.reshape(bsz * seq_len, ATTN_WIDTH), y_ssm, z_ssm,
                       w_glu[layer], b_glu[layer].reshape(1, SSM_WIDTH),
                       w_out[layer], final_gain.reshape(1, D_MODEL), tm,
                       final_norm=layer == depth - 1)
    return x2.reshape(bsz, seq_len, D_MODEL)
```

```python
import functools
import math

import jax
import jax.numpy as jnp
from jax import lax
from jax.experimental import pallas as pl
from jax.experimental.pallas import tpu as pltpu

F32 = jnp.float32
BF16 = jnp.bfloat16

D_MODEL = 1024
HEAD_DIM = 64
ATTN_HEADS = 8
ATTN_WIDTH = ATTN_HEADS * HEAD_DIM
MOBA_BLOCK = 256
MOBA_TOP_K = 3
ROPE_THETA = 10000.0
ROTARY_SPLIT = 64
SSM_GROUP_DIM = 16
SSM_GROUPS = 32
SSM_WIDTH = SSM_GROUPS * SSM_GROUP_DIM
SSM_STATE = 64
IN_PROJ_WIDTH = 4 * ATTN_WIDTH + 2 * SSM_WIDTH
NORM_EPS = 1e-6
NEG_INF = -1e30

LANES = 128
SSM_CHUNK = 8
GROUPS_PER_TILE = LANES // SSM_GROUP_DIM
SSM_TILES = SSM_WIDTH // LANES
TILE_COLS = SSM_CHUNK * LANES
STATE_COLS = GROUPS_PER_TILE * SSM_STATE
VMEM_LIMIT_BYTES = 56 * 1024 * 1024
MXU_COLS = 256
OUT_PROJ_ROWS = 256
PROJ_TILE_ROWS = 1024
SSM_SEQS_PER_STEP = 4
F32_SUBLANES = 8
BF16_SUBLANES = 16
VALUE_ROWS = HEAD_DIM + BF16_SUBLANES
QK_SCALE = math.log2(math.e) / math.sqrt(HEAD_DIM)


def _resident(shape):
    zeros = (0,) * len(shape)
    return pl.BlockSpec(shape, lambda *_: zeros, pipeline_mode=pl.Buffered(1))


def _in_proj_kernel(x_ref, g_ref, w32_ref, cos_ref, sin_ref,
                    q_ref, k_ref, v_ref, za_ref, u_ref, zs_ref, w_ref):
    @pl.when(pl.program_id(0) == 0)
    def _():
        for c in range(0, IN_PROJ_WIDTH, ATTN_WIDTH):
            w_ref[:, c:c + ATTN_WIDTH] = w32_ref[:, c:c + ATTN_WIDTH].astype(BF16)

    x = x_ref[...]
    ms = jnp.mean(x * x, axis=-1, keepdims=True)
    h = (x * lax.rsqrt(ms + NORM_EPS) * g_ref[...]).astype(BF16)

    def section(idx):
        return jnp.dot(h, w_ref[:, idx * ATTN_WIDTH:(idx + 1) * ATTN_WIDTH],
                       preferred_element_type=F32)

    cos = cos_ref[...]
    sin = sin_ref[...]
    lane = lax.broadcasted_iota(jnp.int32, cos.shape, 1)
    upper = (lane & (HEAD_DIM - 1)) >= HEAD_DIM // 2

    def rotary(t):
        partner = jnp.where(upper, pltpu.roll(t, HEAD_DIM // 2, axis=1),
                            pltpu.roll(t, LANES - HEAD_DIM // 2, axis=1))
        return t * cos + partner * sin

    q = section(0)
    k = section(1)
    for c in range(ATTN_WIDTH // LANES):
        sl = slice(c * LANES, (c + 1) * LANES)
        q_ref[:, sl] = (rotary(q[:, sl]) * QK_SCALE).astype(BF16)
        k_ref[:, sl] = rotary(k[:, sl]).astype(BF16)
    v_ref[...] = section(2).astype(BF16)
    za_ref[...] = section(3).astype(BF16)
    u_ref[...] = section(4)
    zs_ref[...] = section(5).astype(BF16)


def _in_proj(x2, gain, w_in, cos, sin, seq_len, tm):
    rows = x2.shape[0]
    pos_blocks = seq_len // tm
    row_spec = lambda width: pl.BlockSpec((tm, width), lambda i: (i, 0))
    tab_spec = pl.BlockSpec((tm, LANES), lambda i: (i % pos_blocks, 0))
    out_bf16 = jax.ShapeDtypeStruct((rows, ATTN_WIDTH), BF16)
    out_f32 = jax.ShapeDtypeStruct((rows, ATTN_WIDTH), F32)
    return pl.pallas_call(
        _in_proj_kernel,
        grid=(rows // tm,),
        in_specs=[row_spec(D_MODEL), _resident((1, D_MODEL)), _resident((D_MODEL, IN_PROJ_WIDTH)),
                  tab_spec, tab_spec],
        out_specs=[row_spec(ATTN_WIDTH)] * 6,
        out_shape=[out_bf16, out_bf16, out_bf16, out_bf16, out_f32, out_bf16],
        scratch_shapes=[pltpu.VMEM((D_MODEL, IN_PROJ_WIDTH), BF16)],
        compiler_params=pltpu.CompilerParams(dimension_semantics=("arbitrary",),
                                             vmem_limit_bytes=VMEM_LIMIT_BYTES),
        name="in_proj",
    )(x2, gain, w_in, cos, sin)


def _attn_kernel(q0_ref, k0_ref, v0_ref, qn_ref, kn_ref, vn_ref, z_ref, o_ref,
                 kx_ref, vx_ref, qx_ref, sa_ref, sb_ref, m_ref, acc_ref, *, n_blocks):
    unit = pl.program_id(0)
    blk = MOBA_BLOCK
    heads = LANES // HEAD_DIM
    n_pairs = n_blocks // 2
    cur = lax.rem(unit, 2)
    nxt = 1 - cur

    head_rows = lambda h: slice(h * HEAD_DIM, (h + 1) * HEAD_DIM)
    spare_base = lambda h: ((h + 1) % heads) * HEAD_DIM
    iota = lambda shape, d: lax.broadcasted_iota(jnp.int32, shape, d)
    key = iota((blk, blk), 0)
    query = iota((blk, blk), 1)
    col_max = lambda s: jnp.max(s, axis=0, keepdims=True)

    def prepare_items(q_ref, k_ref, v_ref, slot):
        kmean = {}

        def key_block(j):
            lane = iota((blk, LANES), 1)
            sum_rows = (iota((VALUE_ROWS - HEAD_DIM, blk), 0) == 0).astype(F32)
            rows = slice(j * blk, (j + 1) * blk)
            kj = k_ref[0, rows, :]
            kmean[j] = jnp.mean(kj.astype(F32), axis=0, keepdims=True)
            vt = v_ref[0, rows, :].astype(F32).T
            for h in range(heads):
                in_head = (lane >= h * HEAD_DIM) & (lane < (h + 1) * HEAD_DIM)
                tag = jnp.where(lane == spare_base(h) + j, 1.0, 0.0).astype(BF16)
                kx_ref[slot, h, j] = jnp.where(in_head, kj, tag)
                vx_ref[slot, j, h] = jnp.concatenate([vt[head_rows(h)], sum_rows],
                                                     axis=0).astype(BF16)

        def query_block(qi):
            if "all" not in kmean:
                kmean["all"] = jnp.concatenate([kmean[j] for j in range(n_blocks)],
                                               axis=0).astype(BF16)
            blk_row = iota((n_blocks, blk), 0)
            zero_rows = lambda n: jnp.zeros((n, blk), F32)
            in_head_order = lambda h, own, other: [own, other] if h == 0 else [other, own]
            qt = q_ref[0, qi * blk:(qi + 1) * blk, :].astype(F32).T
            past = blk_row < qi
            for h in range(heads):
                q_rows = qt[head_rows(h)]
                q_only = jnp.concatenate(in_head_order(h, q_rows, zero_rows(HEAD_DIM)), axis=0)
                gate = jnp.dot(kmean["all"], q_only.astype(BF16), preferred_element_type=F32)
                gate = jnp.where(past, gate, NEG_INF)
                beaten_by = jnp.zeros((n_blocks, blk), jnp.int32)
                for j in range(n_blocks):
                    gj = gate[j:j + 1, :]
                    wins = (gj > gate) | ((gj == gate) & (j < blk_row))
                    beaten_by = beaten_by + wins.astype(jnp.int32)
                keep = (past & (beaten_by < MOBA_TOP_K)) | (blk_row == qi)
                bias = jnp.where(keep, 0.0, NEG_INF)
                spare = jnp.concatenate([bias, zero_rows(HEAD_DIM - n_blocks)], axis=0)
                qx_ref[slot, qi, h] = jnp.concatenate(in_head_order(h, q_rows, spare),
                                                      axis=0).astype(BF16)

        return ([functools.partial(key_block, j) for j in range(n_blocks)]
                + [functools.partial(query_block, qi) for qi in range(n_blocks)])

    def visible_blocks(pair):
        tiles = (pair, n_blocks - 1 - pair)
        return ([(t, tiles[t], True) for t in range(2)]
                + [(t, j, False) for t in range(2) for j in range(tiles[t])])

    traced = lambda t: t + jnp.minimum(unit, 0)

    def stage_items(pair, s_ref, m, slot):
        def block(t, j, own, h):
            qi = (pair, n_blocks - 1 - pair)[t]
            s = jnp.dot(kx_ref[slot, h, j], qx_ref[slot, qi, h], preferred_element_type=F32)
            if own:
                s = jnp.where(key <= query, s, NEG_INF)
            s_ref[traced(t), h, j] = s
            m[t, h] = col_max(s) if own else jnp.maximum(m[t, h], col_max(s))
        return [functools.partial(block, *b, h) for b in visible_blocks(pair)
                for h in range(heads)]

    def finish_items(pair, s_ref, m, acc_ref, slot):
        def block(t, j, own, h):
            p = jnp.exp2(s_ref[traced(t), h, j] - m[t, h]).astype(BF16)
            pv = jnp.dot(vx_ref[slot, j, h], p, preferred_element_type=F32)
            if own:
                acc_ref[traced(t), h] = pv
            else:
                acc_ref[traced(t), h] += pv

        def write_rows():
            for t, qi in enumerate((pair, n_blocks - 1 - pair)):
                rows = slice(qi * blk, (qi + 1) * blk)
                acc = [acc_ref[t, h] for h in range(heads)]
                o_t = jnp.concatenate([a[:HEAD_DIM] / a[HEAD_DIM:HEAD_DIM + 1] for a in acc],
                                      axis=0)
                z = z_ref[0, rows, :].astype(F32)
                silu = 0.5 * z * (1.0 + jnp.tanh(0.5 * z))
                o_ref[0, rows, :] = (o_t.T * silu).astype(BF16)

        return ([functools.partial(block, *b, h) for b in visible_blocks(pair)
                 for h in range(heads)] + [write_rows])

    def alternate(*item_lists):
        for group in zip(*item_lists):
            for item in group:
                item()
        for items in item_lists:
            for item in items[min(map(len, item_lists)):]:
                item()

    m_keys = [(t, h) for t in range(2) for h in range(heads)]

    @pl.when(unit == 0)
    def _():
        m0 = {}
        alternate(prepare_items(q0_ref, k0_ref, v0_ref, 0))
        alternate(stage_items(0, sa_ref, m0, 0))
        for t, h in m_keys:
            m_ref[t, h] = m0[t, h]

    m = {key_: m_ref[key_] for key_ in m_keys}
    for pair in range(n_pairs):
        s_ref, s_next = (sa_ref, sb_ref) if pair % 2 == 0 else (sb_ref, sa_ref)
        m_next = {}
        lists = [finish_items(pair, s_ref, m, acc_ref.at[pair % 2], cur)]
        if pair + 1 < n_pairs:
            lists.append(stage_items(pair + 1, s_next, m_next, cur))
        else:
            lists.append(stage_items(0, s_next, m_next, nxt))
        if pair == n_pairs - 2:
            lists.append(prepare_items(qn_ref, kn_ref, vn_ref, nxt))
        alternate(lists[1], lists[0], *lists[2:])
        m = m_next
    for t, h in m_keys:
        m_ref[t, h] = m[t, h]


def _moba_attention(q, k, v, z_attn):
    bsz, seq_len, _ = q.shape
    n_blocks = seq_len // MOBA_BLOCK
    assert n_blocks % 4 == 0 and n_blocks <= HEAD_DIM
    head_pairs = ATTN_WIDTH // LANES
    heads = LANES // HEAD_DIM
    assert heads == 2, "the block bias rides in the other head's half of the 128 lanes"
    n_units = bsz * head_pairs
    block = (1, seq_len, LANES)
    unit_index = lambda u: (u // head_pairs, 0, u % head_pairs)
    first_spec = pl.BlockSpec(block, lambda u: (0, 0, 0))
    next_spec = pl.BlockSpec(block, lambda u: unit_index(jnp.minimum(u + 1, n_units - 1)))
    unit_spec = pl.BlockSpec(block, unit_index)
    scores = pltpu.VMEM((2, heads, n_blocks, MOBA_BLOCK, MOBA_BLOCK), F32)
    return pl.pallas_call(
        functools.partial(_attn_kernel, n_blocks=n_blocks),
        grid=(n_units,),
        in_specs=[first_spec] * 3 + [next_spec] * 3 + [unit_spec],
        out_specs=unit_spec,
        out_shape=jax.ShapeDtypeStruct((bsz, seq_len, ATTN_WIDTH), BF16),
        scratch_shapes=[pltpu.VMEM((2, heads, n_blocks, MOBA_BLOCK, LANES), BF16),
                        pltpu.VMEM((2, n_blocks, heads, VALUE_ROWS, MOBA_BLOCK), BF16),
                        pltpu.VMEM((2, n_blocks, heads, LANES, MOBA_BLOCK), BF16),
                        scores, scores,
                        pltpu.VMEM((2, heads, 1, MOBA_BLOCK), F32),
                        pltpu.VMEM((2, 2, heads, VALUE_ROWS, MOBA_BLOCK), F32)],
        compiler_params=pltpu.CompilerParams(dimension_semantics=("arbitrary",),
                                             vmem_limit_bytes=VMEM_LIMIT_BYTES),
        name="moba_attn",
    )(q, k, v, q, k, v, z_attn)


def _ssm_tables(lam_re, lam_im, b_re, b_im, c_re, c_im, log_dt):
    T, G, P, H = SSM_CHUNK, SSM_GROUPS, SSM_STATE, SSM_GROUP_DIM
    dt = jnp.repeat(jnp.exp(log_dt.astype(F32)), P)
    lam_r, lam_i = lam_re.astype(F32).reshape(G * P), lam_im.astype(F32).reshape(G * P)
    mag = jnp.exp(lam_r * dt)
    bar_r, bar_i = mag * jnp.cos(lam_i * dt), mag * jnp.sin(lam_i * dt)
    den = lam_r * lam_r + lam_i * lam_i
    f_r = ((bar_r - 1.0) * lam_r + bar_i * lam_i) / den
    f_i = (bar_i * lam_r - (bar_r - 1.0) * lam_i) / den
    b_r, b_i = (b.astype(F32).transpose(2, 0, 1).reshape(H, G * P) for b in (b_re, b_im))
    bb_r, bb_i = f_r * b_r - f_i * b_i, f_r * b_i + f_i * b_r

    def powers(base_r, base_i, count):
        out_r, out_i = [base_r], [base_i]
        for _ in range(count - 1):
            r, i = out_r[-1], out_i[-1]
            out_r.append(r * base_r - i * base_i)
            out_i.append(r * base_i + i * base_r)
        return out_r, out_i

    pw_r, pw_i = powers(bar_r, bar_i, T)
    pw_r = jnp.stack([jnp.ones_like(bar_r)] + pw_r)
    pw_i = jnp.stack([jnp.zeros_like(bar_i)] + pw_i)

    lb_r = pw_r[:T, None, :] * bb_r - pw_i[:T, None, :] * bb_i
    lb_i = pw_r[:T, None, :] * bb_i + pw_i[:T, None, :] * bb_r
    lb_t = jnp.stack([lb_r, lb_i], axis=1).astype(BF16)
    c_r, c_i = (c.astype(F32).transpose(2, 0, 1).reshape(P, G * H) for c in (c_re, c_im))
    per_out = lambda pw: jnp.repeat(pw.reshape(T + 1, G, P).transpose(0, 2, 1), H, axis=2)
    pwo_r, pwo_i = per_out(pw_r), per_out(pw_i)
    cl_t = jnp.stack([c_r * pwo_r - c_i * pwo_i, -(c_r * pwo_i + c_i * pwo_r)], axis=1).astype(BF16)

    a_r, a_i = powers(pw_r[T], pw_i[T], F32_SUBLANES)
    tiled = lambda a: jnp.stack(a).reshape(len(a), SSM_TILES, STATE_COLS)
    a_lin = jnp.stack([tiled(a_r), tiled(a_i)]).transpose(2, 0, 1, 3)
    shifts = [1 << k for k in range(F32_SUBLANES.bit_length() - 1)]
    a_log = jnp.stack([tiled([a_r[s - 1] for s in shifts]),
                       tiled([a_i[s - 1] for s in shifts])]).transpose(2, 1, 0, 3)
    row = jnp.arange(F32_SUBLANES)[None, None, None, :, None]
    keep = row >= jnp.asarray(shifts)[None, :, None, None, None]
    a_log = jnp.where(keep, a_log[:, :, :, None, :], 0.0)
    return lb_t, cl_t, a_log, a_lin


def _ssm_kernel(u_ref, lb_ref, cl_ref, alog_ref, alin_ref, d_ref, y_ref, m_ref, ws_ref, wo_ref,
                work_ref, prev_ref, *, seq_len, n_seq):
    T, H, P = SSM_CHUNK, SSM_GROUP_DIM, SSM_STATE
    n_chunks = seq_len // T
    iota = lambda shape, d: lax.broadcasted_iota(jnp.int32, shape, d)
    h_bits, p_bits = H.bit_length() - 1, P.bit_length() - 1

    @pl.when(pl.program_id(1) == 0)
    def _():
        same_hp = (iota((LANES, STATE_COLS), 0) >> h_bits) == (iota((LANES, STATE_COLS), 1) >> p_bits)
        same_ph = (iota((STATE_COLS, LANES), 0) >> p_bits) == (iota((STATE_COLS, LANES), 1) >> h_bits)

        def expand(coeff, same):
            full = jnp.concatenate([coeff] * GROUPS_PER_TILE, axis=0)
            return jnp.where(same, full, jnp.zeros_like(full))

        rows = lambda j: slice(j * LANES, (j + 1) * LANES)
        state_out = lambda d: jnp.concatenate(
            [expand(cl_ref[d, ri], same_ph) for ri in range(2)], axis=0)
        for j in range(T):
            ws_ref[rows(j), :] = jnp.concatenate(
                [expand(lb_ref[T - 1 - j, ri], same_hp) for ri in range(2)], axis=1)
        for i in range(T):
            wo_ref[:, rows(i)] = state_out(i + 1)
        c_out = state_out(0)
        for d in range(T):
            block = jnp.dot(ws_ref[rows(T - 1 - d), :], c_out,
                            preferred_element_type=F32).astype(BF16)
            for j in range(T - d):
                m_ref[rows(j), rows(j + d)] = block
        for j in range(T):
            for i in range(j):
                m_ref[rows(j), rows(i)] = jnp.zeros((LANES, LANES), BF16)

    pieces = [u_ref[pl.ds(b * seq_len + i, n_chunks, stride=T), :]
              for b in range(n_seq) for i in range(T)]
    x = jnp.concatenate(
        [jnp.concatenate(pieces[b * T:(b + 1) * T], axis=1) for b in range(n_seq)], axis=0)
    x_lo = x.astype(BF16)
    groups = [slice(c, c + MXU_COLS) for c in range(0, STATE_COLS, MXU_COLS)]
    im_of = lambda cols: slice(STATE_COLS + cols.start, STATE_COLS + cols.stop)
    traced = lambda k: k + jnp.minimum(pl.program_id(1), 0)
    n_groups = len(groups)
    y_region = 2 * n_groups

    def increment(g, part):
        cols = groups[g] if part == 0 else im_of(groups[g])
        work_ref[traced(2 * g + part)] = jnp.dot(x_lo, ws_ref[:, cols], preferred_element_type=F32)

    def intra_chunk(n):
        hi = (n + 1) * MXU_COLS
        work_ref[traced(y_region + n)] = jnp.dot(x_lo[:, :hi], m_ref[:hi, hi - MXU_COLS:hi],
                                                 preferred_element_type=F32)

    sub = F32_SUBLANES
    first_row = iota((sub, MXU_COLS), 0) == 0
    cmul = lambda a_re, a_im, b_re, b_im: (a_re * b_re - a_im * b_im, a_re * b_im + a_im * b_re)
    tiles_per_seq = n_chunks // sub
    pack = BF16_SUBLANES // sub

    def scan_items(g):
        cols = groups[g]
        log_mul = [(alog_ref[0, k, 0, :, cols], alog_ref[0, k, 1, :, cols])
                   for k in range(sub.bit_length() - 1)]
        lin_mul = (alin_ref[0, 0, :, cols], alin_ref[0, 1, :, cols])

        def sequence(b):
            carry = (jnp.zeros((1, MXU_COLS), F32),) * 2
            pending = []
            for t in range(tiles_per_seq):
                rows = pl.ds((b * tiles_per_seq + t) * sub, sub)
                t_re = work_ref[traced(2 * g), rows, :]
                t_im = work_ref[traced(2 * g + 1), rows, :]
                for k, mul in enumerate(log_mul):
                    d_re, d_im = cmul(*mul, pltpu.roll(t_re, 1 << k, axis=0),
                                      pltpu.roll(t_im, 1 << k, axis=0))
                    t_re, t_im = t_re + d_re, t_im + d_im
                if t > 0:
                    d_re, d_im = cmul(*lin_mul, *carry)
                    t_re, t_im = t_re + d_re, t_im + d_im
                pending.append((jnp.where(first_row, carry[0], pltpu.roll(t_re, 1, axis=0)),
                                jnp.where(first_row, carry[1], pltpu.roll(t_im, 1, axis=0))))
                carry = (t_re[sub - 1:sub], t_im[sub - 1:sub])
                if len(pending) == pack:
                    out = pl.ds((b * tiles_per_seq + t + 1 - pack) * sub, pack * sub)
                    for part, at in enumerate((cols, im_of(cols))):
                        prev_ref[out, at] = jnp.concatenate(
                            [p[part] for p in pending], axis=0).astype(BF16)
                    pending = []
        return [functools.partial(sequence, b) for b in range(n_seq)]

    def alternate(*item_lists):
        for group in zip(*item_lists):
            for item in group:
                item()
        for items in item_lists:
            for item in items[min(map(len, item_lists)):]:
                item()

    increment(0, 0)
    increment(0, 1)
    matmuls = ([functools.partial(increment, g, part) for g in range(1, n_groups)
                for part in range(2)]
               + [functools.partial(intra_chunk, n) for n in range(T * LANES // MXU_COLS)])
    per_scan = -(-len(matmuls) // n_groups)
    for g in range(n_groups):
        alternate(scan_items(g), matmuls[g * per_scan:(g + 1) * per_scan])
    per_tile = MXU_COLS // LANES
    for n in range(T * LANES // MXU_COLS):
        cols = slice(n * MXU_COLS, (n + 1) * MXU_COLS)
        y = (work_ref[y_region + n] + x[:, cols] * jnp.concatenate([d_ref[...]] * per_tile, axis=1)
             + jnp.dot(prev_ref[...], wo_ref[:, cols], preferred_element_type=F32))
        for b in range(n_seq):
            for i in range(per_tile):
                y_ref[pl.ds(b * seq_len + n * per_tile + i, n_chunks, stride=T), :] = (
                    y[b * n_chunks:(b + 1) * n_chunks, i * LANES:(i + 1) * LANES])


def _s5_ssm(u, tables, d_skip, seq_len):
    lb_t, cl_t, a_log, a_lin = tables
    T, H, P = SSM_CHUNK, SSM_GROUP_DIM, SSM_STATE
    assert seq_len % (T * F32_SUBLANES) == 0
    bsz = u.shape[0] // seq_len
    n_seq = max(n for n in range(1, SSM_SEQS_PER_STEP + 1) if bsz % n == 0)
    io_spec = pl.BlockSpec((n_seq * seq_len, LANES), lambda q, b: (b, q))
    square = pltpu.VMEM((T * LANES, T * LANES), BF16)
    n_groups = STATE_COLS // MXU_COLS
    chunk_rows = n_seq * seq_len // T
    return pl.pallas_call(
        functools.partial(_ssm_kernel, seq_len=seq_len, n_seq=n_seq),
        grid=(SSM_TILES, bsz // n_seq),
        in_specs=[io_spec,
                  pl.BlockSpec((T, 2, H, STATE_COLS), lambda q, b: (0, 0, 0, q)),
                  pl.BlockSpec((T + 1, 2, P, LANES), lambda q, b: (0, 0, 0, q)),
                  pl.BlockSpec((1,) + a_log.shape[1:], lambda q, b: (q, 0, 0, 0, 0)),
                  pl.BlockSpec((1,) + a_lin.shape[1:], lambda q, b: (q, 0, 0, 0)),
                  pl.BlockSpec((1, LANES), lambda q, b: (0, q))],
        out_specs=io_spec,
        out_shape=jax.ShapeDtypeStruct(u.shape, F32),
        scratch_shapes=[square, square, square,
                        pltpu.VMEM((2 * n_groups + T * LANES // MXU_COLS, chunk_rows, MXU_COLS), F32),
                        pltpu.VMEM((chunk_rows, 2 * STATE_COLS), BF16)],
        compiler_params=pltpu.CompilerParams(dimension_semantics=("arbitrary", "arbitrary"),
                                             vmem_limit_bytes=VMEM_LIMIT_BYTES),
        name="s5_ssm",
    )(u, lb_t, cl_t, a_log, a_lin, d_skip)


def _out_proj_kernel(x_hbm, ma_hbm, y_hbm, zs_hbm, wg32_ref, bg_ref, w32_ref, g_ref, o_hbm,
                     wg_ref, w_ref, *, final_norm, tm):
    wg_ref[...] = wg32_ref[...].astype(BF16)
    w_ref[...] = w32_ref[...].astype(BF16)
    tile = lambda width, buffers: pl.BlockSpec((tm, width), lambda i: (i, 0),
                                               pipeline_mode=pl.Buffered(buffers))
    pltpu.emit_pipeline(
        functools.partial(_out_proj_tile, wg_ref, bg_ref, w_ref, g_ref, final_norm),
        grid=(x_hbm.shape[0] // tm,),
        in_specs=[tile(D_MODEL, 3), tile(ATTN_WIDTH, 3), tile(SSM_WIDTH, 3), tile(SSM_WIDTH, 3)],
        out_specs=[tile(D_MODEL, 2)],
    )(x_hbm, ma_hbm, y_hbm, zs_hbm, o_hbm)


def _out_proj_tile(wg_ref, bg_ref, w_ref, g_ref, final_norm, x_ref, ma_ref, y_ref, zs_ref, o_ref):
    half_plus = lambda t: 1.0 + jnp.tanh(0.5 * t)
    for c in range(0, x_ref.shape[0], OUT_PROJ_ROWS):
        rows = slice(c, c + OUT_PROJ_ROWS)
        y = y_ref[rows, :]
        y = y * (1.0 + jnp.tanh(math.sqrt(2.0 / math.pi) * (y + 0.044715 * (y * y * y))))
        gate = jnp.dot((0.5 * y).astype(BF16), wg_ref[...], preferred_element_type=F32) + bg_ref[...]
        z = zs_ref[rows, :].astype(F32)
        mixed_ssm = (0.125 * y) * half_plus(gate) * (z * half_plus(z))
        mixed = jnp.concatenate([ma_ref[rows, :], mixed_ssm.astype(BF16)], axis=1)
        r = x_ref[rows, :] + jnp.dot(mixed, w_ref[...], preferred_element_type=F32)
        if final_norm:
            ms = jnp.mean(r * r, axis=-1, keepdims=True)
            r = r * lax.rsqrt(ms + NORM_EPS) * g_ref[...]
        o_ref[rows, :] = r


def _out_proj(x2, mixed_attn, y_ssm, z_ssm, w_glu, b_glu, w_out, gain, tm, final_norm):
    rows = x2.shape[0]
    streamed = pl.BlockSpec(memory_space=pl.ANY)
    whole = pl.BlockSpec(memory_space=pltpu.VMEM)
    return pl.pallas_call(
        functools.partial(_out_proj_kernel, final_norm=final_norm, tm=tm),
        in_specs=[streamed] * 4 + [whole] * 4,
        out_specs=streamed,
        out_shape=jax.ShapeDtypeStruct((rows, D_MODEL), F32),
        scratch_shapes=[pltpu.VMEM((SSM_WIDTH, SSM_WIDTH), BF16),
                        pltpu.VMEM((ATTN_WIDTH + SSM_WIDTH, D_MODEL), BF16)],
        compiler_params=pltpu.CompilerParams(vmem_limit_bytes=VMEM_LIMIT_BYTES),
        name="out_proj",
    )(x2, mixed_attn, y_ssm, z_ssm, w_glu, b_glu, w_out, gain)


def _rotary_tables(seq_len):
    assert seq_len % ROTARY_SPLIT == 0
    half = HEAD_DIM // 2
    lane = jnp.arange(LANES)
    inv_freq = 1.0 / (ROPE_THETA ** ((lane % half).astype(F32) / half))
    hi = jnp.arange(0, seq_len, ROTARY_SPLIT, dtype=F32)[:, None, None] * inv_freq
    lo = jnp.arange(ROTARY_SPLIT, dtype=F32)[None, :, None] * inv_freq
    sign = jnp.where(lane % HEAD_DIM >= half, 1.0, -1.0).astype(F32)
    sin_hi, sin_lo = sign * jnp.sin(hi), sign * jnp.sin(lo)
    cos = (jnp.cos(hi) * jnp.cos(lo) - jnp.sin(hi) * jnp.sin(lo)).reshape(seq_len, LANES)
    sin = (sin_hi * jnp.cos(lo) + jnp.cos(hi) * sin_lo).reshape(seq_len, LANES)
    return cos, sin


def kernel(x, norm_gain, w_in, w_out, lam_re, lam_im, b_re, b_im, c_re, c_im,
           d_skip, log_dt, w_glu, b_glu, final_gain):
    bsz, seq_len, _ = x.shape
    depth = norm_gain.shape[0]
    assert seq_len % MOBA_BLOCK == 0 and seq_len // MOBA_BLOCK > MOBA_TOP_K
    assert seq_len % SSM_CHUNK == 0
    tm = min(PROJ_TILE_ROWS, seq_len)
    cos, sin = _rotary_tables(seq_len)

    x2 = x.reshape(bsz * seq_len, D_MODEL)
    for layer in range(depth):
        tables = _ssm_tables(lam_re[layer], lam_im[layer], b_re[layer], b_im[layer],
                             c_re[layer], c_im[layer], log_dt[layer])
        q, k, v, z_attn, u, z_ssm = _in_proj(
            x2, norm_gain[layer].reshape(1, D_MODEL), w_in[layer], cos, sin, seq_len, tm)
        to_seq = lambda t: t.reshape(bsz, seq_len, ATTN_WIDTH)
        mixed_attn = _moba_attention(to_seq(q), to_seq(k), to_seq(v), to_seq(z_attn))
        y_ssm = _s5_ssm(u, tables, d_skip[layer].reshape(1, SSM_WIDTH), seq_len)
        x2 = _out_proj(x2, mixed_attn.reshape(bsz * seq_len, ATTN_WIDTH), y_ssm, z_ssm,
                       w_glu[layer], b_glu[layer].reshape(1, SSM_WIDTH),
                       w_out[layer], final_gain.reshape(1, D_MODEL), tm,
                       final_norm=layer == depth - 1)
    return x2.reshape(bsz, seq_len, D_MODEL)
```
